```python
import math
import jax
import jax.numpy as jnp
from jax import lax
import numpy as np

D_MODEL = 2048
BATCH = 4
SEQ = 2048
DEPTH = 2
DEC_BATCH = 2
DEC_SEQ = 4096
PAST_LEN = 128

GRID_W = 64
NA_HEADS = 8
NA_HEAD_DIM = 64
NA_ROWS = 8
NA_COLS = 16
DIFF_HEADS = 4
DIFF_QK_DIM = 64
DIFF_V_DIM = 128
DIFF_QBLOCK = 128
DIL_WINDOWS = (128, 512, 2048)
DIL_DILATIONS = (1, 4, 16)
N_DIL_GROUPS = 3
DIL_HEADS = 4
DIL_HEAD_DIM = 128
SWA_Q_HEADS = 8
SWA_KV_HEADS = 2
SWA_GROUP = SWA_Q_HEADS // SWA_KV_HEADS
SWA_HEAD_DIM = 64
SWA_RADIUS = 128
N_BRANCHES = 4
BRANCH_WIDTH = 512
D_FF = ((8 * D_MODEL // 3 + 127) // 128) * 128
N_ALIBI = SWA_Q_HEADS + N_DIL_GROUPS * DIL_HEADS + DIFF_HEADS
NA_W = NA_HEADS * NA_HEAD_DIM
DIFF_QK_W = DIFF_HEADS * 2 * DIFF_QK_DIM
DIFF_V_W = DIFF_HEADS * DIFF_V_DIM
DIL_W = N_DIL_GROUPS * DIL_HEADS * DIL_HEAD_DIM
SWA_Q_W = SWA_Q_HEADS * SWA_HEAD_DIM
SWA_KV_W = SWA_KV_HEADS * SWA_HEAD_DIM
IN_SPLITS = (NA_W, NA_W, NA_W, DIFF_QK_W, DIFF_QK_W, DIFF_V_W, DIL_W, DIL_W, DIL_W, SWA_Q_W, SWA_KV_W, SWA_KV_W)
IN_WIDTH = NA_W * 3 + DIFF_QK_W * 2 + DIFF_V_W + DIL_W * 3 + SWA_Q_W + SWA_KV_W * 2
RMS_EPS = 1e-6
NEG_INF = -1e30

kernel_name = 'hybrid_bidir_gated_parallel_encoder'


def rmsnorm(x, g, eps=RMS_EPS):
    x32 = x.astype(jnp.float32)
    y = x32 * lax.rsqrt(jnp.mean(x32 * x32, axis=-1, keepdims=True) + eps)
    return (y * g.astype(jnp.float32)).astype(x.dtype)


def swiglu(x, w_in, w_out):
    gate, up = jnp.split(x @ w_in, 2, axis=-1)
    return (jax.nn.silu(gate) * up) @ w_out


def alibi_slopes(n):
    return jnp.exp2(-8.0 * jnp.arange(1, n + 1, dtype=jnp.float32) / n)


def split_cols(a, sizes):
    parts, start = [], 0
    for size in sizes:
        parts.append(a[..., start:start + size])
        start += size
    return parts


def neighborhood_attention(q, k, v, rpb):
    b, t, h, dh = q.shape
    rows = t // GRID_W
    kh = min(NA_ROWS, rows)
    ncb = GRID_W // NA_COLS
    r = jnp.arange(rows)
    row_idx = jnp.clip(r - kh // 2, 0, rows - kh)[:, None] + jnp.arange(kh)[None, :]
    cb = jnp.arange(ncb)
    col_idx = jnp.clip(cb * NA_COLS - NA_COLS // 2, 0, GRID_W - 2 * NA_COLS)[:, None] + jnp.arange(2 * NA_COLS)[None, :]
    qcol = cb[:, None] * NA_COLS + jnp.arange(NA_COLS)[None, :]
    cstart = jnp.clip(qcol - NA_COLS // 2, 0, GRID_W - NA_COLS)
    ccol = col_idx[:, None, :]
    col_ok = (ccol >= cstart[..., None]) & (ccol < cstart[..., None] + NA_COLS)
    ri = row_idx[:, None, :, None]
    ci = col_idx[None, :, None, :]
    kg = k.reshape(b, rows, GRID_W, h, dh)[:, ri, ci]
    vg = v.reshape(b, rows, GRID_W, h, dh)[:, ri, ci]
    qg = q.reshape(b, rows, ncb, NA_COLS, h, dh)
    s = jnp.einsum('brnqhd,brnkwhd->bhrnqkw', qg, kg, preferred_element_type=jnp.float32) * dh ** -0.5
    dr = row_idx - r[:, None] + NA_ROWS - 1
    dc = jnp.clip(ccol - qcol[..., None] + NA_COLS - 1, 0, 2 * NA_COLS - 2)
    bias = rpb.astype(jnp.float32)[:, dr[:, None, None, :, None], dc[None, :, :, None, :]]
    s = jnp.where(col_ok[:, :, None, :], s + bias, NEG_INF)
    p = jax.nn.softmax(s.reshape(s.shape[:-2] + (-1,)), axis=-1).reshape(s.shape)
    out = jnp.einsum('bhrnqkw,brnkwhd->brnqhd', p.astype(v.dtype), vg)
    return out.reshape(b, t, h * dh)


def diff_attention(q, k, v, lam_vecs, subln_g, slopes, lam_init):
    b, t, h, _, dk = q.shape
    nqb = t // DIFF_QBLOCK
    lv = lam_vecs.astype(jnp.float32)
    lam = jnp.exp(jnp.sum(lv[0] * lv[1])) - jnp.exp(jnp.sum(lv[2] * lv[3])) + lam_init
    kpos = jnp.arange(t)
    q_blocks = q.reshape(b, nqb, DIFF_QBLOCK, h, 2, dk).transpose(1, 0, 2, 3, 4, 5)

    def one_block(args):
        q_blk, i = args
        s = jnp.einsum('bqhmd,bkhmd->bhmqk', q_blk, k, preferred_element_type=jnp.float32) * dk ** -0.5
        qpos = i * DIFF_QBLOCK + jnp.arange(DIFF_QBLOCK)
        dist = jnp.abs(qpos[:, None] - kpos[None, :]).astype(jnp.float32)
        p = jax.nn.softmax(s - slopes[None, :, None, None, None] * dist, axis=-1)
        a = p[:, :, 0] - lam * p[:, :, 1]
        return jnp.einsum('bhqk,bkhd->bqhd', a.astype(v.dtype), v)

    o = lax.map(one_block, (q_blocks, jnp.arange(nqb)))
    o = o.transpose(1, 0, 2, 3, 4).reshape(b, t, h, v.shape[-1])
    o = rmsnorm(o, subln_g, eps=1e-5) * (1.0 - lam_init)
    return o.reshape(b, t, h * v.shape[-1])


def banded_attention(q, k, v, radius, step, slopes, sink=None):
    n, length, hk, g, dh = q.shape
    blk = radius
    nb = -(-length // blk)
    lp = nb * blk
    q = jnp.pad(q, ((0, 0), (0, lp - length), (0, 0), (0, 0), (0, 0)))
    kv_pad = ((0, 0), (blk, lp - length + blk), (0, 0), (0, 0))
    k = jnp.pad(k, kv_pad)
    v = jnp.pad(v, kv_pad)
    kidx = jnp.arange(nb)[:, None] * blk + jnp.arange(3 * blk)[None, :]
    kb = k[:, kidx]
    vb = v[:, kidx]
    qb = q.reshape(n, nb, blk, hk, g, dh)
    qpos = jnp.arange(nb)[:, None] * blk + jnp.arange(blk)[None, :]
    kpos = kidx - blk
    dist = jnp.abs(qpos[:, :, None] - kpos[:, None, :])
    valid = (dist <= radius) & (kpos[:, None, :] >= 0) & (kpos[:, None, :] < length)
    s = jnp.einsum('bnqhgd,bnkhd->bhgnqk', qb, kb, preferred_element_type=jnp.float32) * dh ** -0.5
    s = s - (slopes.astype(jnp.float32)[:, :, None, None, None] * step) * dist.astype(jnp.float32)
    s = jnp.where(valid, s, NEG_INF)
    lse = jax.nn.logsumexp(s, axis=-1)
    if sink is not None:
        lse = jnp.logaddexp(lse, sink.astype(jnp.float32)[:, :, None, None])
    p = jnp.exp(s - lse[..., None])
    out = jnp.einsum('bhgnqk,bnkhd->bnqhgd', p.astype(v.dtype), vb)
    out = out.reshape(n, lp, hk, g, dh)[:, :length]
    lse = lse.transpose(0, 3, 4, 1, 2).reshape(n, lp, hk, g)[:, :length]
    return out, lse


def dilated_attention(q, k, v, slopes):
    b, t = q.shape[:2]
    outs, lses = [], []
    for gi in range(N_DIL_GROUPS):
        dil = DIL_DILATIONS[gi]
        radius = DIL_WINDOWS[gi] // (2 * dil)
        n_sub = t // dil

        def to_sub(a, gi=gi, dil=dil, n_sub=n_sub):
            a = a[:, :, gi].reshape(b, n_sub, dil, DIL_HEADS, DIL_HEAD_DIM)
            return a.transpose(0, 2, 1, 3, 4).reshape(b * dil, n_sub, DIL_HEADS, DIL_HEAD_DIM)

        o, lse = banded_attention(to_sub(q)[:, :, :, None], to_sub(k), to_sub(v), radius, dil, slopes[gi][:, None])
        o = o[:, :, :, 0].reshape(b, dil, n_sub, DIL_HEADS, DIL_HEAD_DIM).transpose(0, 2, 1, 3, 4)
        lse = lse[:, :, :, 0].reshape(b, dil, n_sub, DIL_HEADS).transpose(0, 2, 1, 3)
        outs.append(o.reshape(b, t, DIL_HEADS, DIL_HEAD_DIM))
        lses.append(lse.reshape(b, t, DIL_HEADS))
    w = jax.nn.softmax(jnp.stack(lses, axis=-1), axis=-1).astype(q.dtype)
    out = jnp.einsum('bthgd,bthg->bthd', jnp.stack(outs, axis=3), w)
    return out.reshape(b, t, DIL_HEADS * DIL_HEAD_DIM)


def encoder(x, weights):
    (norm_ffn1, w_ffn1_in, w_ffn1_out, norm_mix, w_in, na_rpb, diff_lambda, diff_subln, swa_sink,
     w_branch, w_gate, b_gate, w_out, norm_ffn2, w_ffn2_in, w_ffn2_out, norm_final) = weights
    slopes = alibi_slopes(N_ALIBI)
    n_c = N_DIL_GROUPS * DIL_HEADS
    slopes_d = slopes[:SWA_Q_HEADS].reshape(SWA_KV_HEADS, SWA_GROUP)
    slopes_c = slopes[SWA_Q_HEADS:SWA_Q_HEADS + n_c].reshape(N_DIL_GROUPS, DIL_HEADS)
    slopes_b = slopes[SWA_Q_HEADS + n_c:]
    b, t, d = x.shape
    for l in range(DEPTH):
        x = x + 0.5 * swiglu(rmsnorm(x, norm_ffn1[l]), w_ffn1_in[l], w_ffn1_out[l])
        h = rmsnorm(x, norm_mix[l])
        (q_na, k_na, v_na, q_df, k_df, v_df, q_dl, k_dl, v_dl, q_sw, k_sw, v_sw) = split_cols(h @ w_in[l], IN_SPLITS)
        o_a = neighborhood_attention(q_na.reshape(b, t, NA_HEADS, NA_HEAD_DIM), k_na.reshape(b, t, NA_HEADS, NA_HEAD_DIM),
                                     v_na.reshape(b, t, NA_HEADS, NA_HEAD_DIM), na_rpb[l])
        lam_init = 0.8 - 0.6 * math.exp(-0.3 * l)
        o_b = diff_attention(q_df.reshape(b, t, DIFF_HEADS, 2, DIFF_QK_DIM), k_df.reshape(b, t, DIFF_HEADS, 2, DIFF_QK_DIM),
                             v_df.reshape(b, t, DIFF_HEADS, DIFF_V_DIM), diff_lambda[l], diff_subln[l], slopes_b, lam_init)
        dl_shape = (b, t, N_DIL_GROUPS, DIL_HEADS, DIL_HEAD_DIM)
        o_c = dilated_attention(q_dl.reshape(dl_shape), k_dl.reshape(dl_shape), v_dl.reshape(dl_shape), slopes_c)
        o_d, _ = banded_attention(q_sw.reshape(b, t, SWA_KV_HEADS, SWA_GROUP, SWA_HEAD_DIM),
                                  k_sw.reshape(b, t, SWA_KV_HEADS, SWA_HEAD_DIM), v_sw.reshape(b, t, SWA_KV_HEADS, SWA_HEAD_DIM),
                                  SWA_RADIUS, 1, slopes_d, swa_sink[l].reshape(SWA_KV_HEADS, SWA_GROUP))
        o_d = o_d.reshape(b, t, SWA_Q_W)
        branches = jnp.stack([o_a, o_b, o_c, o_d], axis=2)
        proj = jnp.einsum('btnc,ncd->btnd', branches, w_branch[l])
        gates = jax.nn.sigmoid(h @ w_gate[l] + b_gate[l]).reshape(b, t, N_BRANCHES, d)
        merged = jnp.einsum('btnd,btnd->btd', gates, proj)
        x = x + merged @ w_out[l]
        x = x + 0.5 * swiglu(rmsnorm(x, norm_ffn2[l]), w_ffn2_in[l], w_ffn2_out[l])
    return rmsnorm(x, norm_final)


def setup_inputs(seed: int = 0) -> dict:
    key = jax.random.key(seed)
    ks = jax.random.split(key, 20)
    f32 = jnp.float32

    def normal(k, shape, scale):
        return jax.random.normal(k, shape, f32) * scale

    def gain(k, shape):
        return 1.0 + normal(k, shape, 0.01)

    nl, d = DEPTH, D_MODEL
    return {
        'x_prompt': normal(ks[0], (BATCH, SEQ, d), 1.0),
        'x_sample': normal(ks[1], (DEC_BATCH, DEC_SEQ, d), 1.0),
        'norm_ffn1': gain(ks[2], (nl, d)),
        'w_ffn1_in': normal(ks[3], (nl, d, 2 * D_FF), d ** -0.5),
        'w_ffn1_out': normal(ks[4], (nl, D_FF, d), D_FF ** -0.5),
        'norm_mix': gain(ks[5], (nl, d)),
        'w_in': normal(ks[6], (nl, d, IN_WIDTH), d ** -0.5),
        'na_rpb': normal(ks[7], (nl, NA_HEADS, 2 * NA_ROWS - 1, 2 * NA_COLS - 1), 0.1),
        'diff_lambda': normal(ks[8], (nl, 4, DIFF_QK_DIM), 0.1),
        'diff_subln': gain(ks[9], (nl, DIFF_V_DIM)),
        'swa_sink': normal(ks[10], (nl, SWA_Q_HEADS), 0.5),
        'w_branch': normal(ks[11], (nl, N_BRANCHES, BRANCH_WIDTH, d), BRANCH_WIDTH ** -0.5),
        'w_gate': normal(ks[12], (nl, d, N_BRANCHES * d), d ** -0.5),
        'b_gate': normal(ks[13], (nl, N_BRANCHES * d), 0.02),
        'w_out': normal(ks[14], (nl, d, d), d ** -0.5),
        'norm_ffn2': gain(ks[15], (nl, d)),
        'w_ffn2_in': normal(ks[16], (nl, d, 2 * D_FF), d ** -0.5),
        'w_ffn2_out': normal(ks[17], (nl, D_FF, d), D_FF ** -0.5),
        'norm_final': gain(ks[18], (d,)),
    }


def reference(x_prompt, x_sample, norm_ffn1, w_ffn1_in, w_ffn1_out, norm_mix, w_in, na_rpb, diff_lambda, diff_subln,
              swa_sink, w_branch, w_gate, b_gate, w_out, norm_ffn2, w_ffn2_in, w_ffn2_out, norm_final):
    weights = (norm_ffn1, w_ffn1_in, w_ffn1_out, norm_mix, w_in, na_rpb, diff_lambda, diff_subln, swa_sink,
               w_branch, w_gate, b_gate, w_out, norm_ffn2, w_ffn2_in, w_ffn2_out, norm_final)
    y_prompt = encoder(x_prompt, weights)
    y_sample = encoder(x_sample, weights)
    return (y_prompt, y_sample)
```

```python
import functools
import math

import jax
import jax.numpy as jnp
import numpy as np
from jax import lax
from jax.experimental import pallas as pl
from jax.experimental.pallas import tpu as pltpu

F32 = jnp.float32
BF16 = jnp.bfloat16

D = 2048
DEPTH = 2
GRID_W = 64
NA_HEADS, NA_DH, NA_ROWS, NA_COLS = 8, 64, 8, 16
DIFF_HEADS, DIFF_DK, DIFF_DV = 4, 64, 128
DIL_WINDOWS, DIL_DILATIONS = (128, 512, 2048), (1, 4, 16)
DIL_GROUPS, DIL_HEADS, DIL_DH = 3, 4, 128
SWA_QH, SWA_KVH, SWA_DH, SWA_RADIUS = 8, 2, 64, 128
SWA_GROUP = SWA_QH // SWA_KVH
BRANCH_W = 512
D_FF = ((8 * D // 3 + 127) // 128) * 128
N_ALIBI = SWA_QH + DIL_GROUPS * DIL_HEADS + DIFF_HEADS
RMS_EPS = 1e-6
NEG_INF = -1e30

LANES = 128
IN_WIDTH = 3 * NA_HEADS * NA_DH + 2 * DIFF_HEADS * 2 * DIFF_DK + DIFF_HEADS * DIFF_DV \
    + 3 * DIL_GROUPS * DIL_HEADS * DIL_DH + SWA_QH * SWA_DH + 2 * SWA_KVH * SWA_DH
N_BLK = IN_WIDTH // LANES
BLK_NA_Q, BLK_NA_K, BLK_NA_V = 0, 4, 8
BLK_DF_Q, BLK_DF_K, BLK_DF_V = 12, 16, 20
BLK_DL_Q, BLK_DL_K, BLK_DL_V = 24, 36, 48
BLK_SW_Q, BLK_SW_K, BLK_SW_V = 60, 64, 65
COL_NA_Q, COL_DF_Q, COL_SW_Q = 0, 1536, 7680

FFN_TF = 512
FFN_FP = -(-D_FF // FFN_TF) * FFN_TF
FFN_NC1 = FFN_FP // FFN_TF
TN_OUT = 256
TM_FFN = 512
TM_PROJ = 512
TN_PROJ = 11 * LANES
TM_MERGE = 512
TC_MERGE = 512
TM_COMBINE = 1024
TQ_DIFF = 256
TQ_SWA = 256
TQ_DIL = 128
VMEM_LIMIT = 52 * 1024 * 1024

_ALIBI = [2.0 ** (-8.0 * (i + 1) / N_ALIBI) for i in range(N_ALIBI)]
SLOPES_SWA = _ALIBI[:SWA_QH]
SLOPES_DIL = _ALIBI[SWA_QH:SWA_QH + DIL_GROUPS * DIL_HEADS]
SLOPES_DIFF = _ALIBI[SWA_QH + DIL_GROUPS * DIL_HEADS:]


def _params(sem):
    return pltpu.CompilerParams(dimension_semantics=sem, vmem_limit_bytes=VMEM_LIMIT)


def _rms(x, g, eps):
    ms = jnp.mean(x * x, axis=-1, keepdims=True)
    return x * lax.rsqrt(ms + eps) * g


def _dot(a, b):
    return jnp.dot(a, b, preferred_element_type=F32)


def _dot_nt(a, b):
    return lax.dot_general(a, b, (((1,), (1,)), ((), ())), preferred_element_type=F32)


def _half_masks(rows):
    lane = lax.broadcasted_iota(jnp.int32, (rows, LANES), 1)
    return lane < (LANES // 2)


def _keep_half(x, lo, half):
    keep = lo if half == 0 else jnp.logical_not(lo)
    return jnp.where(keep, x.astype(F32), 0.0).astype(BF16)


def _ffn_kernel(x_ref, g_ref, w1_ref, w2_ref, xres_ref, o_ref, h_scr, act_scr):
    j = pl.program_id(1)

    @pl.when(j == 0)
    def _():
        h_scr[...] = _rms(x_ref[...], g_ref[...], RMS_EPS).astype(BF16)

    @pl.when(j < FFN_NC1)
    def _():
        gu = _dot(h_scr[...], w1_ref[0])
        gate = gu[:, :FFN_TF]
        up = gu[:, FFN_TF:]
        act_scr[j] = (gate * jax.nn.sigmoid(gate) * up).astype(BF16)

    @pl.when(j >= FFN_NC1)
    def _():
        acc = _dot(act_scr[0], w2_ref[0:FFN_TF, :])
        for c in range(1, FFN_NC1):
            acc = acc + _dot(act_scr[c], w2_ref[c * FFN_TF:(c + 1) * FFN_TF, :])
        o_ref[...] = xres_ref[...] + 0.5 * acc


def _ffn(x, g, w1, w2):
    m = x.shape[0]
    tm = TM_FFN
    nc2 = D // TN_OUT
    col = lambda i, j: (i, jnp.maximum(j - FFN_NC1, 0))
    return pl.pallas_call(
        _ffn_kernel,
        grid=(m // tm, FFN_NC1 + nc2),
        in_specs=[
            pl.BlockSpec((tm, D), lambda i, j: (i, 0)),
            pl.BlockSpec((1, D), lambda i, j: (0, 0)),
            pl.BlockSpec((1, D, 2 * FFN_TF), lambda i, j: (jnp.minimum(j, FFN_NC1 - 1), 0, 0)),
            pl.BlockSpec((FFN_FP, TN_OUT), lambda i, j: (0, jnp.maximum(j - FFN_NC1, 0))),
            pl.BlockSpec((tm, TN_OUT), col),
        ],
        out_specs=pl.BlockSpec((tm, TN_OUT), col),
        out_shape=jax.ShapeDtypeStruct((m, D), F32),
        scratch_shapes=[pltpu.VMEM((tm, D), BF16), pltpu.VMEM((FFN_NC1, tm, FFN_TF), BF16)],
        compiler_params=_params(("parallel", "arbitrary")),
        name="ffn",
    )(x, g, w1, w2, x)


def _proj_kernel(x_ref, g_ref, w_ref, h_ref, o_ref):
    @pl.when(pl.program_id(1) == 0)
    def _():
        h_ref[...] = _rms(x_ref[...], g_ref[...], RMS_EPS).astype(BF16)

    res = _dot(h_ref[...], w_ref[...])
    for k in range(TN_PROJ // LANES):
        o_ref[k] = res[:, k * LANES:(k + 1) * LANES].astype(BF16)


def _proj(x, g, w):
    m = x.shape[0]
    tm = TM_PROJ
    nb = TN_PROJ // LANES
    return pl.pallas_call(
        _proj_kernel,
        grid=(m // tm, IN_WIDTH // TN_PROJ),
        in_specs=[
            pl.BlockSpec((tm, D), lambda i, j: (i, 0)),
            pl.BlockSpec((1, D), lambda i, j: (0, 0)),
            pl.BlockSpec((D, TN_PROJ), lambda i, j: (0, j)),
        ],
        out_specs=[
            pl.BlockSpec((tm, D), lambda i, j: (i, 0)),
            pl.BlockSpec((nb, tm, LANES), lambda i, j: (j, i, 0)),
        ],
        out_shape=[jax.ShapeDtypeStruct((m, D), BF16),
                   jax.ShapeDtypeStruct((N_BLK, m, LANES), BF16)],
        compiler_params=_params(("parallel", "arbitrary")),
        name="proj",
    )(x, g, w)


def _na_kernel(q_ref, k_ref, v_ref, b_ref, o_ref, *, rows):
    r = pl.program_id(1)
    row0 = jnp.clip(r - NA_ROWS // 2, 0, rows - NA_ROWS)
    start = pl.multiple_of(row0 * GRID_W, GRID_W)
    nkeys = NA_ROWS * GRID_W
    lo = _half_masks(GRID_W)
    for hp in range(NA_HEADS // 2):
        q = q_ref[hp]
        k = k_ref[hp, pl.ds(start, nkeys), :]
        v = v_ref[hp, pl.ds(start, nkeys), :]
        outs = []
        for half in range(2):
            s = _dot_nt(_keep_half(q, lo, half), k) + b_ref[0, 2 * hp + half]
            m = jnp.max(s, axis=-1, keepdims=True)
            p = jnp.exp(s - m)
            l = jnp.sum(p, axis=-1, keepdims=True)
            outs.append(_dot(p.astype(BF16), v) / l)
        o_ref[:, hp * LANES:(hp + 1) * LANES] = jnp.where(lo, outs[0], outs[1]).astype(BF16)


def _na(qkv, bias, n_seq, t, row_off):
    m = qkv.shape[1]
    rows = t // GRID_W
    blk0 = row_off // GRID_W
    seq0 = row_off // t

    def bias_idx(s, r):
        return (jnp.clip(r - NA_ROWS // 2, 0, rows - NA_ROWS) - r + NA_ROWS - 1, 0, 0, 0)

    return pl.pallas_call(
        functools.partial(_na_kernel, rows=rows),
        grid=(n_seq, rows),
        in_specs=[
            pl.BlockSpec((4, GRID_W, LANES), lambda s, r: (BLK_NA_Q // 4, blk0 + s * rows + r, 0)),
            pl.BlockSpec((4, t, LANES), lambda s, r: (BLK_NA_K // 4, seq0 + s, 0)),
            pl.BlockSpec((4, t, LANES), lambda s, r: (BLK_NA_V // 4, seq0 + s, 0)),
            pl.BlockSpec((1, NA_HEADS, GRID_W, NA_ROWS * GRID_W), bias_idx),
        ],
        out_specs=pl.BlockSpec((GRID_W, BRANCH_W), lambda s, r: (s * rows + r, 0)),
        out_shape=jax.ShapeDtypeStruct((n_seq * t, BRANCH_W), BF16),
        compiler_params=_params(("parallel", "arbitrary")),
        name="mixer_a",
    )(qkv, qkv, qkv, bias)


def _na_bias_table(rpb):
    c = np.arange(GRID_W)[:, None]
    kc = np.arange(GRID_W)[None, :]
    cstart = np.clip(c - NA_COLS // 2, 0, GRID_W - NA_COLS)
    ok = (kc >= cstart) & (kc < cstart + NA_COLS)
    dc = np.clip(kc - c + NA_COLS - 1, 0, 2 * NA_COLS - 2)
    dr = np.arange(NA_ROWS)[:, None] + np.arange(NA_ROWS)[None, :]
    tab = rpb.astype(F32)[:, dr[:, None, :, None], dc[None, :, None, :]]
    tab = jnp.where(ok[None, None, :, None, :], tab, NEG_INF)
    return tab.transpose(1, 0, 2, 3, 4).reshape(NA_ROWS, NA_HEADS, GRID_W, NA_ROWS * GRID_W)


def _diff_kernel(sc_ref, q_ref, k_ref, v_ref, g_ref, o_ref, *, t, tq, out_scale):
    h = pl.program_id(1)
    qi = pl.program_id(2)
    lam = sc_ref[0]
    slope = sc_ref[1 + h]
    q = q_ref[0]
    k = k_ref[0]
    lo = _half_masks(tq)
    rel = (lax.broadcasted_iota(jnp.int32, (tq, t), 0) + qi * tq) - lax.broadcasted_iota(jnp.int32, (tq, t), 1)
    bias = slope * jnp.abs(rel).astype(F32)

    def softmax_map(half):
        s = _dot_nt(_keep_half(q, lo, half), k) - bias
        m = jnp.max(s, axis=-1, keepdims=True)
        p = jnp.exp(s - m)
        return p, jnp.sum(p, axis=-1, keepdims=True)

    p1, l1 = softmax_map(0)
    p2, l2 = softmax_map(1)
    a = p1 * (1.0 / l1) - p2 * (lam / l2)
    o = _dot(a.astype(BF16), v_ref[0])
    o_ref[...] = (_rms(o, g_ref[...], 1e-5) * out_scale).astype(BF16)


def _diff(qkv, scalars, subln_g, n_seq, t, row_off, lam_init):
    tq = TQ_DIFF
    nq = t // tq
    blk0 = row_off // tq
    seq0 = row_off // t
    return pl.pallas_call(
        functools.partial(_diff_kernel, t=t, tq=tq, out_scale=1.0 - lam_init),
        grid=(n_seq, DIFF_HEADS, nq),
        in_specs=[
            pl.BlockSpec(memory_space=pltpu.SMEM),
            pl.BlockSpec((1, tq, LANES), lambda s, h, i: (BLK_DF_Q + h, blk0 + s * nq + i, 0)),
            pl.BlockSpec((1, t, LANES), lambda s, h, i: (BLK_DF_K + h, seq0 + s, 0)),
            pl.BlockSpec((1, t, LANES), lambda s, h, i: (BLK_DF_V + h, seq0 + s, 0)),
            pl.BlockSpec((1, DIFF_DV), lambda s, h, i: (0, 0)),
        ],
        out_specs=pl.BlockSpec((tq, LANES), lambda s, h, i: (s * nq + i, h)),
        out_shape=jax.ShapeDtypeStruct((n_seq * t, BRANCH_W), BF16),
        compiler_params=_params(("parallel", "arbitrary", "arbitrary")),
        name="mixer_b",
    )(scalars, qkv, qkv, qkv, subln_g)


def _band_window(qi, tq, kw, radius, length):
    start = jnp.clip(qi * tq - radius, 0, length - kw)
    rel = (lax.broadcasted_iota(jnp.int32, (tq, kw), 0) + (qi * tq - start)) \
        - lax.broadcasted_iota(jnp.int32, (tq, kw), 1)
    dist = jnp.abs(rel)
    return pl.multiple_of(start, 64), dist.astype(F32), dist <= radius


def _dil_kernel(q_ref, k_ref, v_ref, o_ref, lse_ref, *, n_sub, tq, kw, radius, slopes, scale):
    qi = pl.program_id(2)
    start, dist, valid = _band_window(qi, tq, kw, radius, n_sub)
    for h in range(DIL_HEADS):
        k = k_ref[h, pl.ds(start, kw), :]
        v = v_ref[h, pl.ds(start, kw), :]
        s = _dot_nt(q_ref[h], k) * scale - slopes[h] * dist
        s = jnp.where(valid, s, NEG_INF)
        m = jnp.max(s, axis=-1, keepdims=True)
        p = jnp.exp(s - m)
        l = jnp.sum(p, axis=-1, keepdims=True)
        o_ref[h] = (_dot(p.astype(BF16), v) / l).astype(BF16)
        lse_ref[h] = jnp.broadcast_to(m + jnp.log(l), (tq, LANES))


def _dil_group(qkv, gi, n_seq, t, row_off):
    m = qkv.shape[1]
    dil = DIL_DILATIONS[gi]
    radius = DIL_WINDOWS[gi] // (2 * dil)
    n_sub = t // dil
    tq = min(TQ_DIL, n_sub)
    kw = min(tq + 2 * radius, n_sub)
    nq = n_sub // tq
    view = qkv.reshape(N_BLK, m // dil, dil * LANES)
    blk0 = row_off // dil // tq
    seq0 = row_off // t
    slopes = tuple(SLOPES_DIL[gi * DIL_HEADS + h] * dil for h in range(DIL_HEADS))
    rows_out = n_seq * n_sub
    q_idx = lambda s, r, i: (BLK_DL_Q // 4 + gi, blk0 + s * nq + i, r)
    o_idx = lambda s, r, i: (0, s * nq + i, r)
    o, lse = pl.pallas_call(
        functools.partial(_dil_kernel, n_sub=n_sub, tq=tq, kw=kw, radius=radius, slopes=slopes,
                          scale=DIL_DH ** -0.5),
        grid=(n_seq, dil, nq),
        in_specs=[
            pl.BlockSpec((4, tq, LANES), q_idx),
            pl.BlockSpec((4, n_sub, LANES), lambda s, r, i: (BLK_DL_K // 4 + gi, seq0 + s, r)),
            pl.BlockSpec((4, n_sub, LANES), lambda s, r, i: (BLK_DL_V // 4 + gi, seq0 + s, r)),
        ],
        out_specs=[pl.BlockSpec((4, tq, LANES), o_idx), pl.BlockSpec((4, tq, LANES), o_idx)],
        out_shape=[jax.ShapeDtypeStruct((DIL_HEADS, rows_out, dil * LANES), BF16),
                   jax.ShapeDtypeStruct((DIL_HEADS, rows_out, dil * LANES), F32)],
        compiler_params=_params(("parallel", "arbitrary", "arbitrary")),
        name=f"mixer_c{gi}",
    )(view, view, view)
    return o.reshape(DIL_HEADS, n_seq * t, LANES), lse.reshape(DIL_HEADS, n_seq * t, LANES)


def _dil_combine_kernel(o0, o1, o2, l0, l1, l2, out_ref):
    for h in range(DIL_HEADS):
        a0, a1, a2 = l0[h], l1[h], l2[h]
        mx = jnp.maximum(jnp.maximum(a0, a1), a2)
        e0, e1, e2 = jnp.exp(a0 - mx), jnp.exp(a1 - mx), jnp.exp(a2 - mx)
        num = e0 * o0[h].astype(F32) + e1 * o1[h].astype(F32) + e2 * o2[h].astype(F32)
        out_ref[:, h * LANES:(h + 1) * LANES] = (num / (e0 + e1 + e2)).astype(BF16)


def _dil_combine(outs, lses):
    m = outs[0].shape[1]
    tm = min(TM_COMBINE, m)
    spec = pl.BlockSpec((DIL_HEADS, tm, LANES), lambda i: (0, i, 0))
    return pl.pallas_call(
        _dil_combine_kernel,
        grid=(m // tm,),
        in_specs=[spec] * 6,
        out_specs=pl.BlockSpec((tm, BRANCH_W), lambda i: (i, 0)),
        out_shape=jax.ShapeDtypeStruct((m, BRANCH_W), BF16),
        compiler_params=_params(("parallel",)),
        name="mixer_c_combine",
    )(*outs, *lses)


def _swa_kernel(sink_ref, q_ref, k_ref, v_ref, o_ref, *, t, tq, kw):
    qi = pl.program_id(1)
    start, dist, valid = _band_window(qi, tq, kw, SWA_RADIUS, t)
    k = k_ref[0, pl.ds(start, kw), :]
    v = v_ref[0, pl.ds(start, kw), :]
    lo = _half_masks(tq)
    for g in range(SWA_GROUP):
        q = q_ref[g]
        outs = []
        for hk in range(SWA_KVH):
            head = hk * SWA_GROUP + g
            sink = sink_ref[head]
            s = _dot_nt(_keep_half(q, lo, hk), k) - SLOPES_SWA[head] * dist
            s = jnp.where(valid, s, NEG_INF)
            m = jnp.maximum(jnp.max(s, axis=-1, keepdims=True), sink)
            p = jnp.exp(s - m)
            l = jnp.sum(p, axis=-1, keepdims=True) + jnp.exp(sink - m)
            outs.append(_dot(p.astype(BF16), v) / l)
        o_ref[:, g * LANES:(g + 1) * LANES] = jnp.where(lo, outs[0], outs[1]).astype(BF16)


def _swa(qkv, sink, n_seq, t, row_off):
    tq = TQ_SWA
    kw = min(tq + 2 * SWA_RADIUS, t)
    nq = t // tq
    blk0 = row_off // tq
    seq0 = row_off // t
    return pl.pallas_call(
        functools.partial(_swa_kernel, t=t, tq=tq, kw=kw),
        grid=(n_seq, nq),
        in_specs=[
            pl.BlockSpec(memory_space=pltpu.SMEM),
            pl.BlockSpec((4, tq, LANES), lambda s, i: (BLK_SW_Q // 4, blk0 + s * nq + i, 0)),
            pl.BlockSpec((1, t, LANES), lambda s, i: (BLK_SW_K, seq0 + s, 0)),
            pl.BlockSpec((1, t, LANES), lambda s, i: (BLK_SW_V, seq0 + s, 0)),
        ],
        out_specs=pl.BlockSpec((tq, BRANCH_W), lambda s, i: (s * nq + i, 0)),
        out_shape=jax.ShapeDtypeStruct((n_seq * t, BRANCH_W), BF16),
        compiler_params=_params(("parallel", "arbitrary")),
        name="mixer_d",
    )(sink, qkv, qkv, qkv)


def _merge_kernel(h_ref, oa_ref, ob_ref, oc_ref, od_ref, wg0_ref, wg1_ref, wg2_ref, wg3_ref,
                  bg_ref, wb_ref, wo_ref, xres_ref, o_ref, m_scr, *, nc1, tc):
    j = pl.program_id(1)

    @pl.when(j < nc1)
    def _():
        h = h_ref[...]
        acc = None
        branches = ((oa_ref, wg0_ref), (ob_ref, wg1_ref), (oc_ref, wg2_ref), (od_ref, wg3_ref))
        for n, (b_ref, wg_ref) in enumerate(branches):
            gate = jax.nn.sigmoid(_dot(h, wg_ref[...]) + bg_ref[n:n + 1, :])
            term = gate * _dot(b_ref[...], wb_ref[n])
            acc = term if acc is None else acc + term
        m_scr[j] = acc.astype(BF16)

    @pl.when(j >= nc1)
    def _():
        acc = _dot(m_scr[0], wo_ref[0:tc, :])
        for c in range(1, nc1):
            acc = acc + _dot(m_scr[c], wo_ref[c * tc:(c + 1) * tc, :])
        o_ref[...] = xres_ref[...] + acc


def _merge(x, h, branches, w_gate, b_gate, w_branch, w_out):
    m = x.shape[0]
    tm, tc = TM_MERGE, TC_MERGE
    nc1 = D // tc
    nc2 = D // TN_OUT
    first = lambda j: jnp.minimum(j, nc1 - 1)
    col = lambda i, j: (i, jnp.maximum(j - nc1, 0))
    row_tile = lambda i, j: (i, 0)
    gate_specs = [pl.BlockSpec((D, tc), functools.partial(lambda i, j, n: (0, n * nc1 + first(j)), n=n))
                  for n in range(4)]
    return pl.pallas_call(
        functools.partial(_merge_kernel, nc1=nc1, tc=tc),
        grid=(m // tm, nc1 + nc2),
        in_specs=[pl.BlockSpec((tm, D), row_tile)]
        + [pl.BlockSpec((tm, BRANCH_W), row_tile)] * 4
        + gate_specs
        + [
            pl.BlockSpec((4, tc), lambda i, j: (0, first(j))),
            pl.BlockSpec((4, BRANCH_W, tc), lambda i, j: (0, 0, first(j))),
            pl.BlockSpec((D, TN_OUT), lambda i, j: (0, jnp.maximum(j - nc1, 0))),
            pl.BlockSpec((tm, TN_OUT), col),
        ],
        out_specs=pl.BlockSpec((tm, TN_OUT), col),
        out_shape=jax.ShapeDtypeStruct((m, D), F32),
        scratch_shapes=[pltpu.VMEM((nc1, tm, tc), BF16)],
        compiler_params=_params(("parallel", "arbitrary")),
        name="merge",
    )(h, *branches, w_gate, w_gate, w_gate, w_gate, b_gate, w_branch, w_out, x)


def _final_norm_kernel(x_ref, g_ref, o_ref):
    o_ref[...] = _rms(x_ref[...], g_ref[...], RMS_EPS)


def _final_norm(x, g):
    m = x.shape[0]
    tm = 512
    return pl.pallas_call(
        _final_norm_kernel,
        grid=(m // tm,),
        in_specs=[pl.BlockSpec((tm, D), lambda i: (i, 0)), pl.BlockSpec((1, D), lambda i: (0, 0))],
        out_specs=pl.BlockSpec((tm, D), lambda i: (i, 0)),
        out_shape=jax.ShapeDtypeStruct((m, D), F32),
        compiler_params=_params(("parallel",)),
        name="final_norm",
    )(x, g)


def _prep_ffn(w_in, w_out):
    pad = FFN_FP - D_FF
    gate = jnp.pad(w_in[:, :D_FF], ((0, 0), (0, pad))).reshape(D, FFN_NC1, 1, FFN_TF)
    up = jnp.pad(w_in[:, D_FF:], ((0, 0), (0, pad))).reshape(D, FFN_NC1, 1, FFN_TF)
    w1 = jnp.concatenate([gate, up], axis=2).transpose(1, 0, 2, 3).reshape(FFN_NC1, D, 2 * FFN_TF)
    w2 = jnp.pad(w_out, ((0, pad), (0, 0)))
    return w1.astype(BF16), w2.astype(BF16)


def _swa_head_perm():
    perm = np.empty(SWA_QH * SWA_DH, dtype=np.int32)
    for g in range(SWA_GROUP):
        for hk in range(SWA_KVH):
            dst = (g * SWA_KVH + hk) * SWA_DH
            src = (hk * SWA_GROUP + g) * SWA_DH
            perm[dst:dst + SWA_DH] = np.arange(src, src + SWA_DH)
    return perm


def _prep_w_in(w_in):
    scale = np.ones((IN_WIDTH,), np.float32)
    scale[COL_NA_Q:COL_NA_Q + NA_HEADS * NA_DH] = NA_DH ** -0.5
    scale[COL_DF_Q:COL_DF_Q + DIFF_HEADS * 2 * DIFF_DK] = DIFF_DK ** -0.5
    scale[COL_SW_Q:COL_SW_Q + SWA_QH * SWA_DH] = SWA_DH ** -0.5
    cols = np.arange(IN_WIDTH)
    cols[COL_SW_Q:COL_SW_Q + SWA_QH * SWA_DH] = COL_SW_Q + _swa_head_perm()
    return (w_in[:, cols] * scale[cols]).astype(BF16)


def _encoder_layer(x, l, seqs, p):
    x = _ffn(x, p["norm_ffn1"][l][None], *_prep_ffn(p["w_ffn1_in"][l], p["w_ffn1_out"][l]))
    h, qkv = _proj(x, p["norm_mix"][l][None], _prep_w_in(p["w_in"][l]))

    lam_init = 0.8 - 0.6 * math.exp(-0.3 * l)
    lv = p["diff_lambda"][l].astype(F32)
    lam = jnp.exp(jnp.sum(lv[0] * lv[1])) - jnp.exp(jnp.sum(lv[2] * lv[3])) + lam_init
    diff_scalars = jnp.concatenate([lam[None], jnp.asarray(SLOPES_DIFF, F32)])
    na_bias = _na_bias_table(p["na_rpb"][l])
    subln = p["diff_subln"][l].astype(F32)[None]
    sink = p["swa_sink"][l].astype(F32)

    o_a, o_b, o_d = [], [], []
    dil_o = [[] for _ in range(DIL_GROUPS)]
    dil_lse = [[] for _ in range(DIL_GROUPS)]
    for n_seq, t, row_off in seqs:
        o_a.append(_na(qkv, na_bias, n_seq, t, row_off))
        o_b.append(_diff(qkv, diff_scalars, subln, n_seq, t, row_off, lam_init))
        for gi in range(DIL_GROUPS):
            o, lse = _dil_group(qkv, gi, n_seq, t, row_off)
            dil_o[gi].append(o)
            dil_lse[gi].append(lse)
        o_d.append(_swa(qkv, sink, n_seq, t, row_off))
    o_c = _dil_combine([jnp.concatenate(o, axis=1) for o in dil_o],
                       [jnp.concatenate(s, axis=1) for s in dil_lse])
    branches = [jnp.concatenate(o_a, axis=0), jnp.concatenate(o_b, axis=0), o_c, jnp.concatenate(o_d, axis=0)]

    perm = _swa_head_perm()
    w_branch = p["w_branch"][l]
    w_branch = jnp.concatenate([w_branch[:3], w_branch[3:, perm]], axis=0).astype(BF16)
    x = _merge(x, h, branches, p["w_gate"][l].astype(BF16), p["b_gate"][l].astype(F32).reshape(4, D),
               w_branch, p["w_out"][l].astype(BF16))
    return _ffn(x, p["norm_ffn2"][l][None], *_prep_ffn(p["w_ffn2_in"][l], p["w_ffn2_out"][l]))


def kernel(x_prompt, x_sample, norm_ffn1, w_ffn1_in, w_ffn1_out, norm_mix, w_in, na_rpb, diff_lambda, diff_subln,
           swa_sink, w_branch, w_gate, b_gate, w_out, norm_ffn2, w_ffn2_in, w_ffn2_out, norm_final):
    p = dict(norm_ffn1=norm_ffn1, w_ffn1_in=w_ffn1_in, w_ffn1_out=w_ffn1_out, norm_mix=norm_mix, w_in=w_in,
             na_rpb=na_rpb, diff_lambda=diff_lambda, diff_subln=diff_subln, swa_sink=swa_sink,
             w_branch=w_branch, w_gate=w_gate, b_gate=b_gate, w_out=w_out, norm_ffn2=norm_ffn2,
             w_ffn2_in=w_ffn2_in, w_ffn2_out=w_ffn2_out)
    bp, tp, _ = x_prompt.shape
    bs, ts, _ = x_sample.shape
    mp = bp * tp
    x = jnp.concatenate([x_prompt.reshape(mp, D), x_sample.reshape(bs * ts, D)], axis=0).astype(F32)
    seqs = ((bp, tp, 0), (bs, ts, mp))
    for l in range(DEPTH):
        x = _encoder_layer(x, l, seqs, p)
    y = _final_norm(x, norm_final.astype(F32)[None])
    return y[:mp].reshape(bp, tp, D), y[mp:].reshape(bs, ts, D)
```

```python
import functools
import math

import jax
import jax.numpy as jnp
import numpy as np
from jax import lax
from jax.experimental import pallas as pl
from jax.experimental.pallas import tpu as pltpu

F32 = jnp.float32
BF16 = jnp.bfloat16

D = 2048
DEPTH = 2
GRID_W = 64
NA_HEADS, NA_DH, NA_ROWS, NA_COLS = 8, 64, 8, 16
DIFF_HEADS, DIFF_DK, DIFF_DV = 4, 64, 128
DIL_WINDOWS, DIL_DILATIONS = (128, 512, 2048), (1, 4, 16)
DIL_GROUPS, DIL_HEADS, DIL_DH = 3, 4, 128
SWA_QH, SWA_KVH, SWA_DH, SWA_RADIUS = 8, 2, 64, 128
SWA_GROUP = SWA_QH // SWA_KVH
BRANCH_W = 512
D_FF = ((8 * D // 3 + 127) // 128) * 128
N_ALIBI = SWA_QH + DIL_GROUPS * DIL_HEADS + DIFF_HEADS
RMS_EPS = 1e-6
NEG_INF = -1e30

LANES = 128
IN_WIDTH = 3 * NA_HEADS * NA_DH + 2 * DIFF_HEADS * 2 * DIFF_DK + DIFF_HEADS * DIFF_DV \
    + 3 * DIL_GROUPS * DIL_HEADS * DIL_DH + SWA_QH * SWA_DH + 2 * SWA_KVH * SWA_DH
N_BLK = IN_WIDTH // LANES
BLK_NA_Q, BLK_NA_K, BLK_NA_V = 0, 4, 8
BLK_DF_Q, BLK_DF_K, BLK_DF_V = 12, 16, 20
BLK_DL_Q, BLK_DL_K, BLK_DL_V = 24, 28, 32
BLK_SW_Q, BLK_SW_K, BLK_SW_V = 36, 40, 41
N_BLK_MAIN = 42
N_BLK_DIL = 3 * DIL_HEADS
COL_NA, COL_DF, COL_DL, COL_SW = 0, 1536, 3072, 7680
DIL_W = DIL_GROUPS * DIL_HEADS * DIL_DH

FFN_TF = 512
FFN_FP = -(-D_FF // FFN_TF) * FFN_TF
FFN_NC1 = FFN_FP // FFN_TF
TN_OUT = 256
TM_FFN = 512
TM_PROJ = 512
NB_PROJ = 6
TN_PROJ = NB_PROJ * LANES
NJ_MAIN = N_BLK_MAIN // NB_PROJ
NJ_DIL = N_BLK_DIL // NB_PROJ
TM_MERGE = 512
TC_MERGE = 512
TM_COMBINE = 1024
TQ_DIFF = 256
TQ_SWA = 256
TQ_DIL = 128
VMEM_LIMIT = 52 * 1024 * 1024

_ALIBI = [2.0 ** (-8.0 * (i + 1) / N_ALIBI) for i in range(N_ALIBI)]
SLOPES_SWA = _ALIBI[:SWA_QH]
SLOPES_DIL = _ALIBI[SWA_QH:SWA_QH + DIL_GROUPS * DIL_HEADS]
SLOPES_DIFF = _ALIBI[SWA_QH + DIL_GROUPS * DIL_HEADS:]


def _params(sem):
    return pltpu.CompilerParams(dimension_semantics=sem, vmem_limit_bytes=VMEM_LIMIT)


def _rms(x, g, eps):
    ms = jnp.mean(x * x, axis=-1, keepdims=True)
    return x * lax.rsqrt(ms + eps) * g


def _dot(a, b):
    return jnp.dot(a, b, preferred_element_type=F32)


def _dot_nt(a, b):
    return lax.dot_general(a, b, (((1,), (1,)), ((), ())), preferred_element_type=F32)


def _half_masks(rows):
    lane = lax.broadcasted_iota(jnp.int32, (rows, LANES), 1)
    return lane < (LANES // 2)


def _keep_half(x, lo, half):
    keep = lo if half == 0 else jnp.logical_not(lo)
    return jnp.where(keep, x.astype(F32), 0.0).astype(BF16)


def _ffn_kernel(x_ref, g_ref, w1_ref, w2_ref, xres_ref, o_ref, h_scr, act_scr):
    j = pl.program_id(1)

    @pl.when(j == 0)
    def _():
        h_scr[...] = _rms(x_ref[...], g_ref[...], RMS_EPS).astype(BF16)

    @pl.when(j < FFN_NC1)
    def _():
        gu = _dot(h_scr[...], w1_ref[0])
        gate = gu[:, :FFN_TF]
        up = gu[:, FFN_TF:]
        act_scr[j] = (gate * jax.nn.sigmoid(gate) * up).astype(BF16)

    @pl.when(j >= FFN_NC1)
    def _():
        acc = _dot(act_scr[0], w2_ref[0, 0:FFN_TF, :])
        for c in range(1, FFN_NC1):
            acc = acc + _dot(act_scr[c], w2_ref[0, c * FFN_TF:(c + 1) * FFN_TF, :])
        o_ref[...] = xres_ref[...] + 0.5 * acc


def _ffn(x, g, w1, w2):
    m = x.shape[0]
    tm = TM_FFN
    nc2 = D // TN_OUT
    col = lambda i, j: (i, jnp.maximum(j - FFN_NC1, 0))
    return pl.pallas_call(
        _ffn_kernel,
        grid=(m // tm, FFN_NC1 + nc2),
        in_specs=[
            pl.BlockSpec((tm, D), lambda i, j: (i, 0)),
            pl.BlockSpec((1, D), lambda i, j: (0, 0)),
            pl.BlockSpec((1, D, 2 * FFN_TF), lambda i, j: (jnp.minimum(j, FFN_NC1 - 1), 0, 0)),
            pl.BlockSpec((1, FFN_FP, TN_OUT), lambda i, j: (jnp.maximum(j - FFN_NC1, 0), 0, 0)),
            pl.BlockSpec((tm, TN_OUT), col),
        ],
        out_specs=pl.BlockSpec((tm, TN_OUT), col),
        out_shape=jax.ShapeDtypeStruct((m, D), F32),
        scratch_shapes=[pltpu.VMEM((tm, D), BF16), pltpu.VMEM((FFN_NC1, tm, FFN_TF), BF16)],
        compiler_params=_params(("parallel", "arbitrary")),
        name="ffn",
    )(x, g, w1, w2, x)


def _proj_kernel(x_ref, g_ref, w_ref, h_ref, main_ref, d4_ref, d16_ref, res_scr, *, tm):
    j = pl.program_id(1)

    @pl.when(j == 0)
    def _():
        h_ref[...] = _rms(x_ref[...], g_ref[...], RMS_EPS).astype(BF16)

    res = _dot(h_ref[...], w_ref[0])

    @pl.when(j < NJ_MAIN)
    def _():
        for k in range(NB_PROJ):
            main_ref[k] = res[:, k * LANES:(k + 1) * LANES].astype(BF16)

    @pl.when(j >= NJ_MAIN)
    def _():
        for k in range(NB_PROJ):
            res_scr[k] = res[:, k * LANES:(k + 1) * LANES]

    for out_ref, dil, first in ((d4_ref, DIL_DILATIONS[1], NJ_MAIN), (d16_ref, DIL_DILATIONS[2], NJ_MAIN + NJ_DIL)):
        @pl.when((j >= first) & (j < first + NJ_DIL))
        def _(out_ref=out_ref, dil=dil):
            for k in range(NB_PROJ):
                for r in range(dil):
                    rows = res_scr[k, pl.ds(r, tm // dil, stride=dil), :]
                    out_ref[k, :, r * LANES:(r + 1) * LANES] = rows.astype(BF16)


def _proj(x, g, w):
    m = x.shape[0]
    tm = TM_PROJ
    d4, d16 = DIL_DILATIONS[1], DIL_DILATIONS[2]
    return pl.pallas_call(
        functools.partial(_proj_kernel, tm=tm),
        grid=(m // tm, NJ_MAIN + 2 * NJ_DIL),
        in_specs=[
            pl.BlockSpec((tm, D), lambda i, j: (i, 0)),
            pl.BlockSpec((1, D), lambda i, j: (0, 0)),
            pl.BlockSpec((1, D, TN_PROJ), lambda i, j: (j, 0, 0)),
        ],
        out_specs=[
            pl.BlockSpec((tm, D), lambda i, j: (i, 0)),
            pl.BlockSpec((NB_PROJ, tm, LANES), lambda i, j: (jnp.minimum(j, NJ_MAIN - 1), i, 0)),
            pl.BlockSpec((NB_PROJ, tm // d4, d4 * LANES),
                         lambda i, j: (jnp.clip(j - NJ_MAIN, 0, NJ_DIL - 1), i, 0)),
            pl.BlockSpec((NB_PROJ, tm // d16, d16 * LANES),
                         lambda i, j: (jnp.clip(j - NJ_MAIN - NJ_DIL, 0, NJ_DIL - 1), i, 0)),
        ],
        out_shape=[jax.ShapeDtypeStruct((m, D), BF16),
                   jax.ShapeDtypeStruct((N_BLK_MAIN, m, LANES), BF16),
                   jax.ShapeDtypeStruct((N_BLK_DIL, m // d4, d4 * LANES), BF16),
                   jax.ShapeDtypeStruct((N_BLK_DIL, m // d16, d16 * LANES), BF16)],
        scratch_shapes=[pltpu.VMEM((NB_PROJ, tm, LANES), F32)],
        compiler_params=_params(("parallel", "arbitrary")),
        name="proj",
    )(x, g, w)


def _na_kernel(q_ref, k_ref, v_ref, b_ref, o_ref, *, rows):
    r = pl.program_id(1)
    row0 = jnp.clip(r - NA_ROWS // 2, 0, rows - NA_ROWS)
    start = pl.multiple_of(row0 * GRID_W, GRID_W)
    nkeys = NA_ROWS * GRID_W
    lo = _half_masks(GRID_W)
    for hp in range(NA_HEADS // 2):
        q = q_ref[hp]
        k = k_ref[hp, pl.ds(start, nkeys), :]
        v = v_ref[hp, pl.ds(start, nkeys), :]
        outs = []
        for half in range(2):
            s = _dot_nt(_keep_half(q, lo, half), k) + b_ref[0, 2 * hp + half]
            m = jnp.max(s, axis=-1, keepdims=True)
            p = jnp.exp(s - m)
            l = jnp.sum(p, axis=-1, keepdims=True)
            outs.append(_dot(p.astype(BF16), v) / l)
        o_ref[:, hp * LANES:(hp + 1) * LANES] = jnp.where(lo, outs[0], outs[1]).astype(BF16)


def _na(qkv, bias, n_seq, t, row_off):
    m = qkv.shape[1]
    rows = t // GRID_W
    blk0 = row_off // GRID_W
    seq0 = row_off // t

    def bias_idx(s, r):
        return (jnp.clip(r - NA_ROWS // 2, 0, rows - NA_ROWS) - r + NA_ROWS - 1, 0, 0, 0)

    return pl.pallas_call(
        functools.partial(_na_kernel, rows=rows),
        grid=(n_seq, rows),
        in_specs=[
            pl.BlockSpec((4, GRID_W, LANES), lambda s, r: (BLK_NA_Q // 4, blk0 + s * rows + r, 0)),
            pl.BlockSpec((4, t, LANES), lambda s, r: (BLK_NA_K // 4, seq0 + s, 0)),
            pl.BlockSpec((4, t, LANES), lambda s, r: (BLK_NA_V // 4, seq0 + s, 0)),
            pl.BlockSpec((1, NA_HEADS, GRID_W, NA_ROWS * GRID_W), bias_idx),
        ],
        out_specs=pl.BlockSpec((GRID_W, BRANCH_W), lambda s, r: (s * rows + r, 0)),
        out_shape=jax.ShapeDtypeStruct((n_seq * t, BRANCH_W), BF16),
        compiler_params=_params(("parallel", "arbitrary")),
        name="mixer_a",
    )(qkv, qkv, qkv, bias)


def _na_bias_table(rpb):
    c = np.arange(GRID_W)[:, None]
    kc = np.arange(GRID_W)[None, :]
    cstart = np.clip(c - NA_COLS // 2, 0, GRID_W - NA_COLS)
    ok = (kc >= cstart) & (kc < cstart + NA_COLS)
    dc = np.clip(kc - c + NA_COLS - 1, 0, 2 * NA_COLS - 2)
    dr = np.arange(NA_ROWS)[:, None] + np.arange(NA_ROWS)[None, :]
    tab = rpb.astype(F32)[:, dr[:, None, :, None], dc[None, :, None, :]]
    tab = jnp.where(ok[None, None, :, None, :], tab, NEG_INF)
    return tab.transpose(1, 0, 2, 3, 4).reshape(NA_ROWS, NA_HEADS, GRID_W, NA_ROWS * GRID_W)


def _diff_kernel(sc_ref, q_ref, k_ref, v_ref, g_ref, o_ref, *, t, tq, out_scale):
    h = pl.program_id(1)
    qi = pl.program_id(2)
    lam = sc_ref[0]
    slope = sc_ref[1 + h]
    q = q_ref[0]
    k = k_ref[0]
    lo = _half_masks(tq)
    rel = (lax.broadcasted_iota(jnp.int32, (tq, t), 0) + qi * tq) - lax.broadcasted_iota(jnp.int32, (tq, t), 1)
    bias = slope * jnp.abs(rel).astype(F32)

    def softmax_map(half):
        s = _dot_nt(_keep_half(q, lo, half), k) - bias
        m = jnp.max(s, axis=-1, keepdims=True)
        p = jnp.exp(s - m)
        return p, jnp.sum(p, axis=-1, keepdims=True)

    p1, l1 = softmax_map(0)
    p2, l2 = softmax_map(1)
    a = p1 * (1.0 / l1) - p2 * (lam / l2)
    o = _dot(a.astype(BF16), v_ref[0])
    o_ref[...] = (_rms(o, g_ref[...], 1e-5) * out_scale).astype(BF16)


def _diff(qkv, scalars, subln_g, n_seq, t, row_off, lam_init):
    tq = TQ_DIFF
    nq = t // tq
    blk0 = row_off // tq
    seq0 = row_off // t
    return pl.pallas_call(
        functools.partial(_diff_kernel, t=t, tq=tq, out_scale=1.0 - lam_init),
        grid=(n_seq, DIFF_HEADS, nq),
        in_specs=[
            pl.BlockSpec(memory_space=pltpu.SMEM),
            pl.BlockSpec((1, tq, LANES), lambda s, h, i: (BLK_DF_Q + h, blk0 + s * nq + i, 0)),
            pl.BlockSpec((1, t, LANES), lambda s, h, i: (BLK_DF_K + h, seq0 + s, 0)),
            pl.BlockSpec((1, t, LANES), lambda s, h, i: (BLK_DF_V + h, seq0 + s, 0)),
            pl.BlockSpec((1, DIFF_DV), lambda s, h, i: (0, 0)),
        ],
        out_specs=pl.BlockSpec((tq, LANES), lambda s, h, i: (s * nq + i, h)),
        out_shape=jax.ShapeDtypeStruct((n_seq * t, BRANCH_W), BF16),
        compiler_params=_params(("parallel", "arbitrary", "arbitrary")),
        name="mixer_b",
    )(scalars, qkv, qkv, qkv, subln_g)


def _band_window(qi, tq, kw, radius, length):
    start = jnp.clip(qi * tq - radius, 0, length - kw)
    rel = (lax.broadcasted_iota(jnp.int32, (tq, kw), 0) + (qi * tq - start)) \
        - lax.broadcasted_iota(jnp.int32, (tq, kw), 1)
    dist = jnp.abs(rel)
    return pl.multiple_of(start, 64), dist.astype(F32), dist <= radius


def _dil_kernel(q_ref, k_ref, v_ref, o_ref, lse_ref, *, n_sub, tq, kw, radius, slopes, scale):
    qi = pl.program_id(2)
    start, dist, valid = _band_window(qi, tq, kw, radius, n_sub)
    for h in range(DIL_HEADS):
        k = k_ref[h, pl.ds(start, kw), :]
        v = v_ref[h, pl.ds(start, kw), :]
        s = _dot_nt(q_ref[h], k) * scale - slopes[h] * dist
        s = jnp.where(valid, s, NEG_INF)
        m = jnp.max(s, axis=-1, keepdims=True)
        p = jnp.exp(s - m)
        l = jnp.sum(p, axis=-1, keepdims=True)
        o_ref[h] = (_dot(p.astype(BF16), v) / l).astype(BF16)
        lse_ref[h] = jnp.broadcast_to(m + jnp.log(l), (tq, LANES))


def _dil_group(view, gi, n_seq, t, row_off):
    dil = DIL_DILATIONS[gi]
    radius = DIL_WINDOWS[gi] // (2 * dil)
    n_sub = t // dil
    tq = min(TQ_DIL, n_sub)
    kw = min(tq + 2 * radius, n_sub)
    nq = n_sub // tq
    blk0 = row_off // dil // tq
    seq0 = row_off // t
    qb, kb, vb = (BLK_DL_Q // 4, BLK_DL_K // 4, BLK_DL_V // 4) if gi == 0 else (0, 1, 2)
    slopes = tuple(SLOPES_DIL[gi * DIL_HEADS + h] * dil for h in range(DIL_HEADS))
    rows_out = n_seq * n_sub
    q_idx = lambda s, r, i: (qb, blk0 + s * nq + i, r)
    o_idx = lambda s, r, i: (0, s * nq + i, r)
    o, lse = pl.pallas_call(
        functools.partial(_dil_kernel, n_sub=n_sub, tq=tq, kw=kw, radius=radius, slopes=slopes,
                          scale=DIL_DH ** -0.5),
        grid=(n_seq, dil, nq),
        in_specs=[
            pl.BlockSpec((4, tq, LANES), q_idx),
            pl.BlockSpec((4, n_sub, LANES), lambda s, r, i: (kb, seq0 + s, r)),
            pl.BlockSpec((4, n_sub, LANES), lambda s, r, i: (vb, seq0 + s, r)),
        ],
        out_specs=[pl.BlockSpec((4, tq, LANES), o_idx), pl.BlockSpec((4, tq, LANES), o_idx)],
        out_shape=[jax.ShapeDtypeStruct((DIL_HEADS, rows_out, dil * LANES), BF16),
                   jax.ShapeDtypeStruct((DIL_HEADS, rows_out, dil * LANES), F32)],
        compiler_params=_params(("parallel", "arbitrary", "arbitrary")),
        name=f"mixer_c{gi}",
    )(view, view, view)
    return o, lse


def _dil_combine_kernel(o0, o1, o2, l0, l1, l2, out_ref, o_scr, l_scr, *, tm):
    for h in range(DIL_HEADS):
        for gi, (o_ref, l_ref) in ((1, (o1, l1)), (2, (o2, l2))):
            dil = DIL_DILATIONS[gi]
            for r in range(dil):
                rows = pl.ds(r, tm // dil, stride=dil)
                o_scr[gi - 1, rows, :] = o_ref[h, :, r * LANES:(r + 1) * LANES].astype(F32)
                l_scr[gi - 1, rows, :] = l_ref[h, :, r * LANES:(r + 1) * LANES]
        a0, a1, a2 = l0[h], l_scr[0], l_scr[1]
        mx = jnp.maximum(jnp.maximum(a0, a1), a2)
        e0, e1, e2 = jnp.exp(a0 - mx), jnp.exp(a1 - mx), jnp.exp(a2 - mx)
        num = e0 * o0[h].astype(F32) + e1 * o_scr[0] + e2 * o_scr[1]
        out_ref[:, h * LANES:(h + 1) * LANES] = (num / (e0 + e1 + e2)).astype(BF16)


def _dil_combine(outs, lses):
    m = outs[0].shape[1]
    tm = min(TM_COMBINE, m)
    specs = [pl.BlockSpec((DIL_HEADS, tm // dil, dil * LANES), lambda i: (0, i, 0)) for dil in DIL_DILATIONS]
    return pl.pallas_call(
        functools.partial(_dil_combine_kernel, tm=tm),
        grid=(m // tm,),
        in_specs=specs * 2,
        out_specs=pl.BlockSpec((tm, BRANCH_W), lambda i: (i, 0)),
        out_shape=jax.ShapeDtypeStruct((m, BRANCH_W), BF16),
        scratch_shapes=[pltpu.VMEM((2, tm, LANES), F32), pltpu.VMEM((2, tm, LANES), F32)],
        compiler_params=_params(("parallel",)),
        name="mixer_c_combine",
    )(*outs, *lses)


def _swa_kernel(sink_ref, q_ref, k_ref, v_ref, o_ref, *, t, tq, kw):
    qi = pl.program_id(1)
    start, dist, valid = _band_window(qi, tq, kw, SWA_RADIUS, t)
    k = k_ref[0, pl.ds(start, kw), :]
    v = v_ref[0, pl.ds(start, kw), :]
    lo = _half_masks(tq)
    for g in range(SWA_GROUP):
        q = q_ref[g]
        outs = []
        for hk in range(SWA_KVH):
            head = hk * SWA_GROUP + g
            sink = sink_ref[head]
            s = _dot_nt(_keep_half(q, lo, hk), k) - SLOPES_SWA[head] * dist
            s = jnp.where(valid, s, NEG_INF)
            m = jnp.maximum(jnp.max(s, axis=-1, keepdims=True), sink)
            p = jnp.exp(s - m)
            l = jnp.sum(p, axis=-1, keepdims=True) + jnp.exp(sink - m)
            outs.append(_dot(p.astype(BF16), v) / l)
        o_ref[:, g * LANES:(g + 1) * LANES] = jnp.where(lo, outs[0], outs[1]).astype(BF16)


def _swa(qkv, sink, n_seq, t, row_off):
    tq = TQ_SWA
    kw = min(tq + 2 * SWA_RADIUS, t)
    nq = t // tq
    blk0 = row_off // tq
    seq0 = row_off // t
    return pl.pallas_call(
        functools.partial(_swa_kernel, t=t, tq=tq, kw=kw),
        grid=(n_seq, nq),
        in_specs=[
            pl.BlockSpec(memory_space=pltpu.SMEM),
            pl.BlockSpec((4, tq, LANES), lambda s, i: (BLK_SW_Q // 4, blk0 + s * nq + i, 0)),
            pl.BlockSpec((1, t, LANES), lambda s, i: (BLK_SW_K, seq0 + s, 0)),
            pl.BlockSpec((1, t, LANES), lambda s, i: (BLK_SW_V, seq0 + s, 0)),
        ],
        out_specs=pl.BlockSpec((tq, BRANCH_W), lambda s, i: (s * nq + i, 0)),
        out_shape=jax.ShapeDtypeStruct((n_seq * t, BRANCH_W), BF16),
        compiler_params=_params(("parallel", "arbitrary")),
        name="mixer_d",
    )(sink, qkv, qkv, qkv)


def _merge_kernel(h_ref, oa_ref, ob_ref, oc_ref, od_ref, wg_ref, bg_ref, wb_ref, wo_ref, xres_ref, o_ref, m_scr,
                  *, nc1, tc):
    j = pl.program_id(1)

    @pl.when(j < nc1)
    def _():
        h = h_ref[...]
        acc = None
        for n, b_ref in enumerate((oa_ref, ob_ref, oc_ref, od_ref)):
            gate = jax.nn.sigmoid(_dot(h, wg_ref[0, n]) + bg_ref[n:n + 1, :])
            term = gate * _dot(b_ref[...], wb_ref[0, n])
            acc = term if acc is None else acc + term
        m_scr[j] = acc.astype(BF16)

    @pl.when(j >= nc1)
    def _():
        acc = _dot(m_scr[0], wo_ref[0, 0:tc, :])
        for c in range(1, nc1):
            acc = acc + _dot(m_scr[c], wo_ref[0, c * tc:(c + 1) * tc, :])
        o_ref[...] = xres_ref[...] + acc


def _merge(x, h, branches, w_gate, b_gate, w_branch, w_out):
    m = x.shape[0]
    tm, tc = TM_MERGE, TC_MERGE
    nc1 = D // tc
    nc2 = D // TN_OUT
    first = lambda j: jnp.minimum(j, nc1 - 1)
    col = lambda i, j: (i, jnp.maximum(j - nc1, 0))
    row_tile = lambda i, j: (i, 0)
    return pl.pallas_call(
        functools.partial(_merge_kernel, nc1=nc1, tc=tc),
        grid=(m // tm, nc1 + nc2),
        in_specs=[pl.BlockSpec((tm, D), row_tile)]
        + [pl.BlockSpec((tm, BRANCH_W), row_tile)] * 4
        + [
            pl.BlockSpec((1, 4, D, tc), lambda i, j: (first(j), 0, 0, 0)),
            pl.BlockSpec((4, tc), lambda i, j: (0, first(j))),
            pl.BlockSpec((1, 4, BRANCH_W, tc), lambda i, j: (first(j), 0, 0, 0)),
            pl.BlockSpec((1, D, TN_OUT), lambda i, j: (jnp.maximum(j - nc1, 0), 0, 0)),
            pl.BlockSpec((tm, TN_OUT), col),
        ],
        out_specs=pl.BlockSpec((tm, TN_OUT), col),
        out_shape=jax.ShapeDtypeStruct((m, D), F32),
        scratch_shapes=[pltpu.VMEM((nc1, tm, tc), BF16)],
        compiler_params=_params(("parallel", "arbitrary")),
        name="merge",
    )(h, *branches, w_gate, b_gate, w_branch, w_out, x)


def _final_norm_kernel(x_ref, g_ref, o_ref):
    o_ref[...] = _rms(x_ref[...], g_ref[...], RMS_EPS)


def _final_norm(x, g):
    m = x.shape[0]
    tm = 512
    return pl.pallas_call(
        _final_norm_kernel,
        grid=(m // tm,),
        in_specs=[pl.BlockSpec((tm, D), lambda i: (i, 0)), pl.BlockSpec((1, D), lambda i: (0, 0))],
        out_specs=pl.BlockSpec((tm, D), lambda i: (i, 0)),
        out_shape=jax.ShapeDtypeStruct((m, D), F32),
        compiler_params=_params(("parallel",)),
        name="final_norm",
    )(x, g)


def _prep_ffn(w_in, w_out):
    pad = FFN_FP - D_FF
    gate = jnp.pad(w_in[:, :D_FF], ((0, 0), (0, pad))).reshape(D, FFN_NC1, 1, FFN_TF)
    up = jnp.pad(w_in[:, D_FF:], ((0, 0), (0, pad))).reshape(D, FFN_NC1, 1, FFN_TF)
    w1 = jnp.concatenate([gate, up], axis=2).transpose(1, 0, 2, 3).reshape(FFN_NC1, D, 2 * FFN_TF)
    w2 = jnp.pad(w_out, ((0, pad), (0, 0))).reshape(FFN_FP, D // TN_OUT, TN_OUT).transpose(1, 0, 2)
    return w1.astype(BF16), w2.astype(BF16)


def _swa_pair_heads(w, axis):
    shape = w.shape
    split = shape[:axis] + (SWA_KVH, SWA_GROUP, SWA_DH) + shape[axis + 1:]
    return jnp.swapaxes(w.reshape(split), axis, axis + 1).reshape(shape)


def _proj_columns(w_in):
    scale = np.ones((IN_WIDTH,), np.float32)
    scale[COL_NA:COL_NA + NA_HEADS * NA_DH] = NA_DH ** -0.5
    scale[COL_DF:COL_DF + DIFF_HEADS * 2 * DIFF_DK] = DIFF_DK ** -0.5
    scale[COL_SW:COL_SW + SWA_QH * SWA_DH] = SWA_DH ** -0.5
    w = w_in * scale
    gw = DIL_HEADS * DIL_DH

    def dil_group(gi):
        return [w[:, COL_DL + part * DIL_W + gi * gw:COL_DL + part * DIL_W + (gi + 1) * gw] for part in range(3)]

    sw_q_end = COL_SW + SWA_QH * SWA_DH
    cols = [w[:, :COL_DL]] + dil_group(0) + [_swa_pair_heads(w[:, COL_SW:sw_q_end], 1), w[:, sw_q_end:]] \
        + dil_group(1) + dil_group(2)
    return jnp.concatenate(cols, axis=1)


def _prep_w_in(w_in):
    w = _proj_columns(w_in).astype(BF16)
    return w.reshape(D, IN_WIDTH // TN_PROJ, TN_PROJ).transpose(1, 0, 2)


def _encoder_layer(x, l, seqs, p):
    x = _ffn(x, p["norm_ffn1"][l][None], *_prep_ffn(p["w_ffn1_in"][l], p["w_ffn1_out"][l]))
    h, qkv, qkv_d4, qkv_d16 = _proj(x, p["norm_mix"][l][None], _prep_w_in(p["w_in"][l]))
    dil_views = (qkv, qkv_d4, qkv_d16)

    lam_init = 0.8 - 0.6 * math.exp(-0.3 * l)
    lv = p["diff_lambda"][l].astype(F32)
    lam = jnp.exp(jnp.sum(lv[0] * lv[1])) - jnp.exp(jnp.sum(lv[2] * lv[3])) + lam_init
    diff_scalars = jnp.concatenate([lam[None], jnp.asarray(SLOPES_DIFF, F32)])
    na_bias = _na_bias_table(p["na_rpb"][l])
    subln = p["diff_subln"][l].astype(F32)[None]
    sink = p["swa_sink"][l].astype(F32)

    o_a, o_b, o_d = [], [], []
    dil_o = [[] for _ in range(DIL_GROUPS)]
    dil_lse = [[] for _ in range(DIL_GROUPS)]
    for n_seq, t, row_off in seqs:
        o_a.append(_na(qkv, na_bias, n_seq, t, row_off))
        o_b.append(_diff(qkv, diff_scalars, subln, n_seq, t, row_off, lam_init))
        for gi in range(DIL_GROUPS):
            o, lse = _dil_group(dil_views[gi], gi, n_seq, t, row_off)
            dil_o[gi].append(o)
            dil_lse[gi].append(lse)
        o_d.append(_swa(qkv, sink, n_seq, t, row_off))
    o_c = _dil_combine([jnp.concatenate(o, axis=1) for o in dil_o],
                       [jnp.concatenate(s, axis=1) for s in dil_lse])
    branches = [jnp.concatenate(o_a, axis=0), jnp.concatenate(o_b, axis=0), o_c, jnp.concatenate(o_d, axis=0)]

    nc = D // TC_MERGE
    w_branch = p["w_branch"][l]
    w_branch = jnp.concatenate([w_branch[:3], _swa_pair_heads(w_branch[3:], 1)], axis=0).astype(BF16)
    w_branch = w_branch.reshape(4, BRANCH_W, nc, TC_MERGE).transpose(2, 0, 1, 3)
    w_gate = p["w_gate"][l].astype(BF16).reshape(D, 4, nc, TC_MERGE).transpose(2, 1, 0, 3)
    w_out = p["w_out"][l].astype(BF16).reshape(D, D // TN_OUT, TN_OUT).transpose(1, 0, 2)
    x = _merge(x, h, branches, w_gate, p["b_gate"][l].astype(F32).reshape(4, D), w_branch, w_out)
    return _ffn(x, p["norm_ffn2"][l][None], *_prep_ffn(p["w_ffn2_in"][l], p["w_ffn2_out"][l]))


def kernel(x_prompt, x_sample, norm_ffn1, w_ffn1_in, w_ffn1_out, norm_mix, w_in, na_rpb, diff_lambda, diff_subln,
           swa_sink, w_branch, w_gate, b_gate, w_out, norm_ffn2, w_ffn2_in, w_ffn2_out, norm_final):
    p = dict(norm_ffn1=norm_ffn1, w_ffn1_in=w_ffn1_in, w_ffn1_out=w_ffn1_out, norm_mix=norm_mix, w_in=w_in,
             na_rpb=na_rpb, diff_lambda=diff_lambda, diff_subln=diff_subln, swa_sink=swa_sink,
             w_branch=w_branch, w_gate=w_gate, b_gate=b_gate, w_out=w_out, norm_ffn2=norm_ffn2,
             w_ffn2_in=w_ffn2_in, w_ffn2_out=w_ffn2_out)
    bp, tp, _ = x_prompt.shape
    bs, ts, _ = x_sample.shape
    mp = bp * tp
    x = jnp.concatenate([x_prompt.reshape(mp, D), x_sample.reshape(bs * ts, D)], axis=0).astype(F32)
    seqs = ((bp, tp, 0), (bs, ts, mp))
    for l in range(DEPTH):
        x = _encoder_layer(x, l, seqs, p)
    y = _final_norm(x, norm_final.astype(F32)[None])
    return y[:mp].reshape(bp, tp, D), y[mp:].reshape(bs, ts, D)
```

```python
import functools
import math

import jax
import jax.numpy as jnp
import numpy as np
from jax import lax
from jax.experimental import pallas as pl
from jax.experimental.pallas import tpu as pltpu

F32 = jnp.float32
BF16 = jnp.bfloat16

D = 2048
DEPTH = 2
GRID_W = 64
NA_HEADS, NA_DH, NA_ROWS, NA_COLS = 8, 64, 8, 16
DIFF_HEADS, DIFF_DK, DIFF_DV = 4, 64, 128
DIL_WINDOWS, DIL_DILATIONS = (128, 512, 2048), (1, 4, 16)
DIL_GROUPS, DIL_HEADS, DIL_DH = 3, 4, 128
SWA_QH, SWA_KVH, SWA_DH, SWA_RADIUS = 8, 2, 64, 128
SWA_GROUP = SWA_QH // SWA_KVH
BRANCH_W = 512
D_FF = ((8 * D // 3 + 127) // 128) * 128
N_ALIBI = SWA_QH + DIL_GROUPS * DIL_HEADS + DIFF_HEADS
RMS_EPS = 1e-6
NEG_INF = -1e30

LANES = 128
IN_WIDTH = 3 * NA_HEADS * NA_DH + 2 * DIFF_HEADS * 2 * DIFF_DK + DIFF_HEADS * DIFF_DV \
    + 3 * DIL_GROUPS * DIL_HEADS * DIL_DH + SWA_QH * SWA_DH + 2 * SWA_KVH * SWA_DH
N_BLK = IN_WIDTH // LANES
BLK_NA_Q, BLK_NA_K, BLK_NA_V = 0, 4, 8
BLK_DF_Q, BLK_DF_K, BLK_DF_V = 12, 16, 20
BLK_DL_Q, BLK_DL_K, BLK_DL_V = 24, 28, 32
BLK_SW_Q, BLK_SW_K, BLK_SW_V = 36, 40, 41
N_BLK_MAIN = 42
N_BLK_DIL = 3 * DIL_HEADS
COL_NA, COL_DF, COL_DL, COL_SW = 0, 1536, 3072, 7680
DIL_W = DIL_GROUPS * DIL_HEADS * DIL_DH

FFN_TF = 512
FFN_FP = -(-D_FF // FFN_TF) * FFN_TF
FFN_NC1 = FFN_FP // FFN_TF
TN_OUT = 256
TM_FFN = 1024
TM_PROJ = 1024
NB_PROJ = 6
TN_PROJ = NB_PROJ * LANES
NJ_MAIN = N_BLK_MAIN // NB_PROJ
NJ_DIL = N_BLK_DIL // NB_PROJ
TM_MERGE = 1024
TC_MERGE = 256
TM_COMBINE = 1024
TQ_DIFF = 256
TQ_SWA = 256
TQ_DIL = 128
VMEM_LIMIT = 60 * 1024 * 1024

_ALIBI = [2.0 ** (-8.0 * (i + 1) / N_ALIBI) for i in range(N_ALIBI)]
SLOPES_SWA = _ALIBI[:SWA_QH]
SLOPES_DIL = _ALIBI[SWA_QH:SWA_QH + DIL_GROUPS * DIL_HEADS]
SLOPES_DIFF = _ALIBI[SWA_QH + DIL_GROUPS * DIL_HEADS:]


def _params(sem):
    return pltpu.CompilerParams(dimension_semantics=sem, vmem_limit_bytes=VMEM_LIMIT)


def _rms(x, g, eps):
    ms = jnp.mean(x * x, axis=-1, keepdims=True)
    return x * lax.rsqrt(ms + eps) * g


def _dot(a, b):
    return jnp.dot(a, b, preferred_element_type=F32)


def _dot_nt(a, b):
    return lax.dot_general(a, b, (((1,), (1,)), ((), ())), preferred_element_type=F32)


def _half_masks(rows):
    lane = lax.broadcasted_iota(jnp.int32, (rows, LANES), 1)
    return lane < (LANES // 2)


def _keep_half(x, lo, half):
    keep = lo if half == 0 else jnp.logical_not(lo)
    return jnp.where(keep, x.astype(F32), 0.0).astype(BF16)


def _ffn_kernel(x_ref, g_ref, wg_ref, wu_ref, w2_ref, xres_ref, o_ref, h_scr, act_scr):
    j = pl.program_id(1)

    @pl.when(j == 0)
    def _():
        h_scr[...] = _rms(x_ref[...], g_ref[...], RMS_EPS).astype(BF16)

    @pl.when(j < FFN_NC1)
    def _():
        h = h_scr[...]
        gate = _dot(h, wg_ref[...])
        up = _dot(h, wu_ref[...])
        act_scr[j] = (gate * jax.nn.sigmoid(gate) * up).astype(BF16)

    @pl.when(j >= FFN_NC1)
    def _():
        acc = _dot(act_scr[0], w2_ref[0:FFN_TF, :])
        for c in range(1, FFN_NC1):
            acc = acc + _dot(act_scr[c], w2_ref[c * FFN_TF:(c + 1) * FFN_TF, :])
        o_ref[...] = xres_ref[...] + 0.5 * acc


def _ffn(x, g, w_gate, w_up, w2):
    m = x.shape[0]
    tm = TM_FFN
    nc2 = D // TN_OUT
    hidden = lambda i, j: (0, jnp.minimum(j, FFN_NC1 - 1))
    col = lambda i, j: (i, jnp.maximum(j - FFN_NC1, 0))
    return pl.pallas_call(
        _ffn_kernel,
        grid=(m // tm, FFN_NC1 + nc2),
        in_specs=[
            pl.BlockSpec((tm, D), lambda i, j: (i, 0)),
            pl.BlockSpec((1, D), lambda i, j: (0, 0)),
            pl.BlockSpec((D, FFN_TF), hidden),
            pl.BlockSpec((D, FFN_TF), hidden),
            pl.BlockSpec((FFN_FP, TN_OUT), lambda i, j: (0, jnp.maximum(j - FFN_NC1, 0))),
            pl.BlockSpec((tm, TN_OUT), col),
        ],
        out_specs=pl.BlockSpec((tm, TN_OUT), col),
        out_shape=jax.ShapeDtypeStruct((m, D), F32),
        scratch_shapes=[pltpu.VMEM((tm, D), BF16), pltpu.VMEM((FFN_NC1, tm, FFN_TF), BF16)],
        compiler_params=_params(("parallel", "arbitrary")),
        name="ffn",
    )(x, g, w_gate, w_up, w2, x)


def _proj_kernel(x_ref, g_ref, w_ref, h_ref, main_ref, d4_ref, d16_ref, res_scr, *, tm):
    j = pl.program_id(1)

    @pl.when(j == 0)
    def _():
        h_ref[...] = _rms(x_ref[...], g_ref[...], RMS_EPS).astype(BF16)

    res = _dot(h_ref[...], w_ref[...])

    @pl.when(j < NJ_MAIN)
    def _():
        for k in range(NB_PROJ):
            main_ref[k] = res[:, k * LANES:(k + 1) * LANES].astype(BF16)

    @pl.when(j >= NJ_MAIN)
    def _():
        for k in range(NB_PROJ):
            res_scr[k] = res[:, k * LANES:(k + 1) * LANES]

    for out_ref, dil, first in ((d4_ref, DIL_DILATIONS[1], NJ_MAIN), (d16_ref, DIL_DILATIONS[2], NJ_MAIN + NJ_DIL)):
        @pl.when((j >= first) & (j < first + NJ_DIL))
        def _(out_ref=out_ref, dil=dil):
            for k in range(NB_PROJ):
                for r in range(dil):
                    rows = res_scr[k, pl.ds(r, tm // dil, stride=dil), :]
                    out_ref[k, :, r * LANES:(r + 1) * LANES] = rows.astype(BF16)


def _proj(x, g, w):
    m = x.shape[0]
    tm = TM_PROJ
    d4, d16 = DIL_DILATIONS[1], DIL_DILATIONS[2]
    return pl.pallas_call(
        functools.partial(_proj_kernel, tm=tm),
        grid=(m // tm, NJ_MAIN + 2 * NJ_DIL),
        in_specs=[
            pl.BlockSpec((tm, D), lambda i, j: (i, 0)),
            pl.BlockSpec((1, D), lambda i, j: (0, 0)),
            pl.BlockSpec((D, TN_PROJ), lambda i, j: (0, j)),
        ],
        out_specs=[
            pl.BlockSpec((tm, D), lambda i, j: (i, 0)),
            pl.BlockSpec((NB_PROJ, tm, LANES), lambda i, j: (jnp.minimum(j, NJ_MAIN - 1), i, 0)),
            pl.BlockSpec((NB_PROJ, tm // d4, d4 * LANES),
                         lambda i, j: (jnp.clip(j - NJ_MAIN, 0, NJ_DIL - 1), i, 0)),
            pl.BlockSpec((NB_PROJ, tm // d16, d16 * LANES),
                         lambda i, j: (jnp.clip(j - NJ_MAIN - NJ_DIL, 0, NJ_DIL - 1), i, 0)),
        ],
        out_shape=[jax.ShapeDtypeStruct((m, D), BF16),
                   jax.ShapeDtypeStruct((N_BLK_MAIN, m, LANES), BF16),
                   jax.ShapeDtypeStruct((N_BLK_DIL, m // d4, d4 * LANES), BF16),
                   jax.ShapeDtypeStruct((N_BLK_DIL, m // d16, d16 * LANES), BF16)],
        scratch_shapes=[pltpu.VMEM((NB_PROJ, tm, LANES), F32)],
        compiler_params=_params(("parallel", "arbitrary")),
        name="proj",
    )(x, g, w)


def _na_kernel(q_ref, k_ref, v_ref, b_ref, o_ref, *, rows):
    r = pl.program_id(1)
    row0 = jnp.clip(r - NA_ROWS // 2, 0, rows - NA_ROWS)
    start = pl.multiple_of(row0 * GRID_W, GRID_W)
    nkeys = NA_ROWS * GRID_W
    lo = _half_masks(GRID_W)
    for hp in range(NA_HEADS // 2):
        q = q_ref[hp]
        k = k_ref[hp, pl.ds(start, nkeys), :]
        v = v_ref[hp, pl.ds(start, nkeys), :]
        outs = []
        for half in range(2):
            s = _dot_nt(_keep_half(q, lo, half), k) + b_ref[0, 2 * hp + half]
            m = jnp.max(s, axis=-1, keepdims=True)
            p = jnp.exp(s - m)
            l = jnp.sum(p, axis=-1, keepdims=True)
            outs.append(_dot(p.astype(BF16), v) / l)
        o_ref[:, hp * LANES:(hp + 1) * LANES] = jnp.where(lo, outs[0], outs[1]).astype(BF16)


def _na(qkv, bias, n_seq, t, row_off):
    m = qkv.shape[1]
    rows = t // GRID_W
    blk0 = row_off // GRID_W
    seq0 = row_off // t

    def bias_idx(s, r):
        return (jnp.clip(r - NA_ROWS // 2, 0, rows - NA_ROWS) - r + NA_ROWS - 1, 0, 0, 0)

    return pl.pallas_call(
        functools.partial(_na_kernel, rows=rows),
        grid=(n_seq, rows),
        in_specs=[
            pl.BlockSpec((4, GRID_W, LANES), lambda s, r: (BLK_NA_Q // 4, blk0 + s * rows + r, 0)),
            pl.BlockSpec((4, t, LANES), lambda s, r: (BLK_NA_K // 4, seq0 + s, 0)),
            pl.BlockSpec((4, t, LANES), lambda s, r: (BLK_NA_V // 4, seq0 + s, 0)),
            pl.BlockSpec((1, NA_HEADS, GRID_W, NA_ROWS * GRID_W), bias_idx),
        ],
        out_specs=pl.BlockSpec((GRID_W, BRANCH_W), lambda s, r: (s * rows + r, 0)),
        out_shape=jax.ShapeDtypeStruct((n_seq * t, BRANCH_W), BF16),
        compiler_params=_params(("parallel", "arbitrary")),
        name="mixer_a",
    )(qkv, qkv, qkv, bias)


def _na_bias_table(rpb):
    c = np.arange(GRID_W)[:, None]
    kc = np.arange(GRID_W)[None, :]
    cstart = np.clip(c - NA_COLS // 2, 0, GRID_W - NA_COLS)
    ok = (kc >= cstart) & (kc < cstart + NA_COLS)
    rpb = rpb.astype(F32)
    rows = jnp.stack([rpb[:, s:s + NA_ROWS] for s in range(NA_ROWS)])
    pad = GRID_W - NA_COLS
    rows = jnp.pad(rows, ((0, 0), (0, 0), (0, 0), (pad, pad)))
    tab = jnp.stack([rows[..., GRID_W - 1 - q:2 * GRID_W - 1 - q] for q in range(GRID_W)], axis=2)
    tab = jnp.where(ok[None, None, :, None, :], tab, NEG_INF)
    return tab.reshape(NA_ROWS, NA_HEADS, GRID_W, NA_ROWS * GRID_W)


def _diff_kernel(sc_ref, q_ref, k_ref, v_ref, g_ref, o_ref, *, t, tq, out_scale):
    h = pl.program_id(1)
    qi = pl.program_id(2)
    lam = sc_ref[0]
    slope = sc_ref[1 + h]
    q = q_ref[0]
    k = k_ref[0]
    lo = _half_masks(tq)
    rel = (lax.broadcasted_iota(jnp.int32, (tq, t), 0) + qi * tq) - lax.broadcasted_iota(jnp.int32, (tq, t), 1)
    bias = slope * jnp.abs(rel).astype(F32)

    def softmax_map(half):
        s = _dot_nt(_keep_half(q, lo, half), k) - bias
        m = jnp.max(s, axis=-1, keepdims=True)
        p = jnp.exp(s - m)
        return p, jnp.sum(p, axis=-1, keepdims=True)

    p1, l1 = softmax_map(0)
    p2, l2 = softmax_map(1)
    a = p1 * (1.0 / l1) - p2 * (lam / l2)
    o = _dot(a.astype(BF16), v_ref[0])
    o_ref[...] = (_rms(o, g_ref[...], 1e-5) * out_scale).astype(BF16)


def _diff(qkv, scalars, subln_g, n_seq, t, row_off, lam_init):
    tq = TQ_DIFF
    nq = t // tq
    blk0 = row_off // tq
    seq0 = row_off // t
    return pl.pallas_call(
        functools.partial(_diff_kernel, t=t, tq=tq, out_scale=1.0 - lam_init),
        grid=(n_seq, DIFF_HEADS, nq),
        in_specs=[
            pl.BlockSpec(memory_space=pltpu.SMEM),
            pl.BlockSpec((1, tq, LANES), lambda s, h, i: (BLK_DF_Q + h, blk0 + s * nq + i, 0)),
            pl.BlockSpec((1, t, LANES), lambda s, h, i: (BLK_DF_K + h, seq0 + s, 0)),
            pl.BlockSpec((1, t, LANES), lambda s, h, i: (BLK_DF_V + h, seq0 + s, 0)),
            pl.BlockSpec((1, DIFF_DV), lambda s, h, i: (0, 0)),
        ],
        out_specs=pl.BlockSpec((tq, LANES), lambda s, h, i: (s * nq + i, h)),
        out_shape=jax.ShapeDtypeStruct((n_seq * t, BRANCH_W), BF16),
        compiler_params=_params(("parallel", "arbitrary", "arbitrary")),
        name="mixer_b",
    )(scalars, qkv, qkv, qkv, subln_g)


def _band_window(qi, tq, kw, radius, length):
    start = jnp.clip(qi * tq - radius, 0, length - kw)
    rel = (lax.broadcasted_iota(jnp.int32, (tq, kw), 0) + (qi * tq - start)) \
        - lax.broadcasted_iota(jnp.int32, (tq, kw), 1)
    dist = jnp.abs(rel)
    return pl.multiple_of(start, 64), dist.astype(F32), dist <= radius


def _dil_kernel(q_ref, k_ref, v_ref, o_ref, lse_ref, *, n_sub, tq, kw, radius, slopes, scale):
    qi = pl.program_id(2)
    start, dist, valid = _band_window(qi, tq, kw, radius, n_sub)
    for h in range(DIL_HEADS):
        k = k_ref[h, pl.ds(start, kw), :]
        v = v_ref[h, pl.ds(start, kw), :]
        s = _dot_nt(q_ref[h], k) * scale - slopes[h] * dist
        s = jnp.where(valid, s, NEG_INF)
        m = jnp.max(s, axis=-1, keepdims=True)
        p = jnp.exp(s - m)
        l = jnp.sum(p, axis=-1, keepdims=True)
        o_ref[h] = (_dot(p.astype(BF16), v) / l).astype(BF16)
        lse_ref[h] = jnp.broadcast_to(m + jnp.log(l), (tq, LANES))


def _dil_group(view, gi, n_seq, t, row_off):
    dil = DIL_DILATIONS[gi]
    radius = DIL_WINDOWS[gi] // (2 * dil)
    n_sub = t // dil
    tq = min(TQ_DIL, n_sub)
    kw = min(tq + 2 * radius, n_sub)
    nq = n_sub // tq
    blk0 = row_off // dil // tq
    seq0 = row_off // t
    qb, kb, vb = (BLK_DL_Q // 4, BLK_DL_K // 4, BLK_DL_V // 4) if gi == 0 else (0, 1, 2)
    slopes = tuple(SLOPES_DIL[gi * DIL_HEADS + h] * dil for h in range(DIL_HEADS))
    rows_out = n_seq * n_sub
    q_idx = lambda s, r, i: (qb, blk0 + s * nq + i, r)
    o_idx = lambda s, r, i: (0, s * nq + i, r)
    o, lse = pl.pallas_call(
        functools.partial(_dil_kernel, n_sub=n_sub, tq=tq, kw=kw, radius=radius, slopes=slopes,
                          scale=DIL_DH ** -0.5),
        grid=(n_seq, dil, nq),
        in_specs=[
            pl.BlockSpec((4, tq, LANES), q_idx),
            pl.BlockSpec((4, n_sub, LANES), lambda s, r, i: (kb, seq0 + s, r)),
            pl.BlockSpec((4, n_sub, LANES), lambda s, r, i: (vb, seq0 + s, r)),
        ],
        out_specs=[pl.BlockSpec((4, tq, LANES), o_idx), pl.BlockSpec((4, tq, LANES), o_idx)],
        out_shape=[jax.ShapeDtypeStruct((DIL_HEADS, rows_out, dil * LANES), BF16),
                   jax.ShapeDtypeStruct((DIL_HEADS, rows_out, dil * LANES), F32)],
        compiler_params=_params(("parallel", "arbitrary", "arbitrary")),
        name=f"mixer_c{gi}",
    )(view, view, view)
    return o, lse


def _dil_combine_kernel(o0, o1, o2, l0, l1, l2, out_ref, o_scr, l_scr, *, tm):
    for h in range(DIL_HEADS):
        for gi, (o_ref, l_ref) in ((1, (o1, l1)), (2, (o2, l2))):
            dil = DIL_DILATIONS[gi]
            for r in range(dil):
                rows = pl.ds(r, tm // dil, stride=dil)
                o_scr[gi - 1, rows, :] = o_ref[h, :, r * LANES:(r + 1) * LANES].astype(F32)
                l_scr[gi - 1, rows, :] = l_ref[h, :, r * LANES:(r + 1) * LANES]
        a0, a1, a2 = l0[h], l_scr[0], l_scr[1]
        mx = jnp.maximum(jnp.maximum(a0, a1), a2)
        e0, e1, e2 = jnp.exp(a0 - mx), jnp.exp(a1 - mx), jnp.exp(a2 - mx)
        num = e0 * o0[h].astype(F32) + e1 * o_scr[0] + e2 * o_scr[1]
        out_ref[:, h * LANES:(h + 1) * LANES] = (num / (e0 + e1 + e2)).astype(BF16)


def _dil_combine(outs, lses):
    m = outs[0].shape[1]
    tm = min(TM_COMBINE, m)
    specs = [pl.BlockSpec((DIL_HEADS, tm // dil, dil * LANES), lambda i: (0, i, 0)) for dil in DIL_DILATIONS]
    return pl.pallas_call(
        functools.partial(_dil_combine_kernel, tm=tm),
        grid=(m // tm,),
        in_specs=specs * 2,
        out_specs=pl.BlockSpec((tm, BRANCH_W), lambda i: (i, 0)),
        out_shape=jax.ShapeDtypeStruct((m, BRANCH_W), BF16),
        scratch_shapes=[pltpu.VMEM((2, tm, LANES), F32), pltpu.VMEM((2, tm, LANES), F32)],
        compiler_params=_params(("parallel",)),
        name="mixer_c_combine",
    )(*outs, *lses)


def _swa_kernel(sink_ref, q_ref, k_ref, v_ref, o_ref, *, t, tq, kw):
    qi = pl.program_id(1)
    start, dist, valid = _band_window(qi, tq, kw, SWA_RADIUS, t)
    k = k_ref[0, pl.ds(start, kw), :]
    v = v_ref[0, pl.ds(start, kw), :]
    lo = _half_masks(tq)
    for g in range(SWA_GROUP):
        q = q_ref[g]
        outs = []
        for hk in range(SWA_KVH):
            head = hk * SWA_GROUP + g
            sink = sink_ref[head]
            s = _dot_nt(_keep_half(q, lo, hk), k) - SLOPES_SWA[head] * dist
            s = jnp.where(valid, s, NEG_INF)
            m = jnp.maximum(jnp.max(s, axis=-1, keepdims=True), sink)
            p = jnp.exp(s - m)
            l = jnp.sum(p, axis=-1, keepdims=True) + jnp.exp(sink - m)
            outs.append(_dot(p.astype(BF16), v) / l)
        o_ref[:, g * LANES:(g + 1) * LANES] = jnp.where(lo, outs[0], outs[1]).astype(BF16)


def _swa(qkv, sink, n_seq, t, row_off):
    tq = TQ_SWA
    kw = min(tq + 2 * SWA_RADIUS, t)
    nq = t // tq
    blk0 = row_off // tq
    seq0 = row_off // t
    return pl.pallas_call(
        functools.partial(_swa_kernel, t=t, tq=tq, kw=kw),
        grid=(n_seq, nq),
        in_specs=[
            pl.BlockSpec(memory_space=pltpu.SMEM),
            pl.BlockSpec((4, tq, LANES), lambda s, i: (BLK_SW_Q // 4, blk0 + s * nq + i, 0)),
            pl.BlockSpec((1, t, LANES), lambda s, i: (BLK_SW_K, seq0 + s, 0)),
            pl.BlockSpec((1, t, LANES), lambda s, i: (BLK_SW_V, seq0 + s, 0)),
        ],
        out_specs=pl.BlockSpec((tq, BRANCH_W), lambda s, i: (s * nq + i, 0)),
        out_shape=jax.ShapeDtypeStruct((n_seq * t, BRANCH_W), BF16),
        compiler_params=_params(("parallel", "arbitrary")),
        name="mixer_d",
    )(sink, qkv, qkv, qkv)


def _merge_kernel(h_ref, oa_ref, ob_ref, oc_ref, od_ref, wg0_ref, wg1_ref, wg2_ref, wg3_ref,
                  bg_ref, wb_ref, wo_ref, xres_ref, o_ref, m_scr, *, nc1, tc):
    j = pl.program_id(1)

    @pl.when(j < nc1)
    def _():
        h = h_ref[...]
        acc = None
        branches = ((oa_ref, wg0_ref), (ob_ref, wg1_ref), (oc_ref, wg2_ref), (od_ref, wg3_ref))
        for n, (b_ref, wg_ref) in enumerate(branches):
            gate = jax.nn.sigmoid(_dot(h, wg_ref[...]) + bg_ref[n:n + 1, :])
            term = gate * _dot(b_ref[...], wb_ref[n])
            acc = term if acc is None else acc + term
        m_scr[j] = acc.astype(BF16)

    @pl.when(j >= nc1)
    def _():
        acc = _dot(m_scr[0], wo_ref[0:tc, :])
        for c in range(1, nc1):
            acc = acc + _dot(m_scr[c], wo_ref[c * tc:(c + 1) * tc, :])
        o_ref[...] = xres_ref[...] + acc


def _merge(x, h, branches, w_gate, b_gate, w_branch, w_out):
    m = x.shape[0]
    tm, tc = TM_MERGE, TC_MERGE
    nc1 = D // tc
    nc2 = D // TN_OUT
    first = lambda j: jnp.minimum(j, nc1 - 1)
    col = lambda i, j: (i, jnp.maximum(j - nc1, 0))
    row_tile = lambda i, j: (i, 0)
    gate_specs = [pl.BlockSpec((D, tc), functools.partial(lambda i, j, n: (0, n * nc1 + first(j)), n=n))
                  for n in range(4)]
    return pl.pallas_call(
        functools.partial(_merge_kernel, nc1=nc1, tc=tc),
        grid=(m // tm, nc1 + nc2),
        in_specs=[pl.BlockSpec((tm, D), row_tile)]
        + [pl.BlockSpec((tm, BRANCH_W), row_tile)] * 4
        + gate_specs
        + [
            pl.BlockSpec((4, tc), lambda i, j: (0, first(j))),
            pl.BlockSpec((4, BRANCH_W, tc), lambda i, j: (0, 0, first(j))),
            pl.BlockSpec((D, TN_OUT), lambda i, j: (0, jnp.maximum(j - nc1, 0))),
            pl.BlockSpec((tm, TN_OUT), col),
        ],
        out_specs=pl.BlockSpec((tm, TN_OUT), col),
        out_shape=jax.ShapeDtypeStruct((m, D), F32),
        scratch_shapes=[pltpu.VMEM((nc1, tm, tc), BF16)],
        compiler_params=_params(("parallel", "arbitrary")),
        name="merge",
    )(h, *branches, w_gate, w_gate, w_gate, w_gate, b_gate, w_branch, w_out, x)


def _final_norm_kernel(x_ref, g_ref, o_ref):
    o_ref[...] = _rms(x_ref[...], g_ref[...], RMS_EPS)


def _final_norm(x, g):
    m = x.shape[0]
    tm = 512
    return pl.pallas_call(
        _final_norm_kernel,
        grid=(m // tm,),
        in_specs=[pl.BlockSpec((tm, D), lambda i: (i, 0)), pl.BlockSpec((1, D), lambda i: (0, 0))],
        out_specs=pl.BlockSpec((tm, D), lambda i: (i, 0)),
        out_shape=jax.ShapeDtypeStruct((m, D), F32),
        compiler_params=_params(("parallel",)),
        name="final_norm",
    )(x, g)


def _prep_ffn(w_in, w_out):
    pad = FFN_FP - D_FF
    gate = jnp.pad(w_in[:, :D_FF].astype(BF16), ((0, 0), (0, pad)))
    up = jnp.pad(w_in[:, D_FF:].astype(BF16), ((0, 0), (0, pad)))
    return gate, up, jnp.pad(w_out.astype(BF16), ((0, pad), (0, 0)))


def _swa_pair_heads(w, axis):
    shape = w.shape
    split = shape[:axis] + (SWA_KVH, SWA_GROUP, SWA_DH) + shape[axis + 1:]
    return jnp.swapaxes(w.reshape(split), axis, axis + 1).reshape(shape)


def _proj_columns(w_in):
    scale = np.ones((IN_WIDTH,), np.float32)
    scale[COL_NA:COL_NA + NA_HEADS * NA_DH] = NA_DH ** -0.5
    scale[COL_DF:COL_DF + DIFF_HEADS * 2 * DIFF_DK] = DIFF_DK ** -0.5
    scale[COL_SW:COL_SW + SWA_QH * SWA_DH] = SWA_DH ** -0.5
    w = w_in * scale
    gw = DIL_HEADS * DIL_DH

    def dil_group(gi):
        return [w[:, COL_DL + part * DIL_W + gi * gw:COL_DL + part * DIL_W + (gi + 1) * gw] for part in range(3)]

    sw_q_end = COL_SW + SWA_QH * SWA_DH
    cols = [w[:, :COL_DL]] + dil_group(0) + [_swa_pair_heads(w[:, COL_SW:sw_q_end], 1), w[:, sw_q_end:]] \
        + dil_group(1) + dil_group(2)
    return jnp.concatenate(cols, axis=1)


def _prep_w_in(w_in):
    return _proj_columns(w_in).astype(BF16)


def _encoder_layer(x, l, seqs, p):
    x = _ffn(x, p["norm_ffn1"][l][None], *_prep_ffn(p["w_ffn1_in"][l], p["w_ffn1_out"][l]))
    h, qkv, qkv_d4, qkv_d16 = _proj(x, p["norm_mix"][l][None], _prep_w_in(p["w_in"][l]))
    dil_views = (qkv, qkv_d4, qkv_d16)

    lam_init = 0.8 - 0.6 * math.exp(-0.3 * l)
    lv = p["diff_lambda"][l].astype(F32)
    lam = jnp.exp(jnp.sum(lv[0] * lv[1])) - jnp.exp(jnp.sum(lv[2] * lv[3])) + lam_init
    diff_scalars = jnp.concatenate([lam[None], jnp.asarray(SLOPES_DIFF, F32)])
    na_bias = _na_bias_table(p["na_rpb"][l])
    subln = p["diff_subln"][l].astype(F32)[None]
    sink = p["swa_sink"][l].astype(F32)

    o_a, o_b, o_d = [], [], []
    dil_o = [[] for _ in range(DIL_GROUPS)]
    dil_lse = [[] for _ in range(DIL_GROUPS)]
    for n_seq, t, row_off in seqs:
        o_a.append(_na(qkv, na_bias, n_seq, t, row_off))
        o_b.append(_diff(qkv, diff_scalars, subln, n_seq, t, row_off, lam_init))
        for gi in range(DIL_GROUPS):
            o, lse = _dil_group(dil_views[gi], gi, n_seq, t, row_off)
            dil_o[gi].append(o)
            dil_lse[gi].append(lse)
        o_d.append(_swa(qkv, sink, n_seq, t, row_off))
    o_c = _dil_combine([jnp.concatenate(o, axis=1) for o in dil_o],
                       [jnp.concatenate(s, axis=1) for s in dil_lse])
    branches = [jnp.concatenate(o_a, axis=0), jnp.concatenate(o_b, axis=0), o_c, jnp.concatenate(o_d, axis=0)]

    w_branch = p["w_branch"][l]
    w_branch = jnp.concatenate([w_branch[:3], _swa_pair_heads(w_branch[3:], 1)], axis=0).astype(BF16)
    x = _merge(x, h, branches, p["w_gate"][l].astype(BF16), p["b_gate"][l].astype(F32).reshape(4, D),
               w_branch, p["w_out"][l].astype(BF16))
    return _ffn(x, p["norm_ffn2"][l][None], *_prep_ffn(p["w_ffn2_in"][l], p["w_ffn2_out"][l]))


def kernel(x_prompt, x_sample, norm_ffn1, w_ffn1_in, w_ffn1_out, norm_mix, w_in, na_rpb, diff_lambda, diff_subln,
           swa_sink, w_branch, w_gate, b_gate, w_out, norm_ffn2, w_ffn2_in, w_ffn2_out, norm_final):
    p = dict(norm_ffn1=norm_ffn1, w_ffn1_in=w_ffn1_in, w_ffn1_out=w_ffn1_out, norm_mix=norm_mix, w_in=w_in,
             na_rpb=na_rpb, diff_lambda=diff_lambda, diff_subln=diff_subln, swa_sink=swa_sink,
             w_branch=w_branch, w_gate=w_gate, b_gate=b_gate, w_out=w_out, norm_ffn2=norm_ffn2,
             w_ffn2_in=w_ffn2_in, w_ffn2_out=w_ffn2_out)
    bp, tp, _ = x_prompt.shape
    bs, ts, _ = x_sample.shape
    mp = bp * tp
    x = jnp.concatenate([x_prompt.reshape(mp, D), x_sample.reshape(bs * ts, D)], axis=0).astype(F32)
    seqs = ((bp, tp, 0), (bs, ts, mp))
    for l in range(DEPTH):
        x = _encoder_layer(x, l, seqs, p)
    y = _final_norm(x, norm_final.astype(F32)[None])
    return y[:mp].reshape(bp, tp, D), y[mp:].reshape(bs, ts, D)
```

```python
import functools
import math

import jax
import jax.numpy as jnp
import numpy as np
from jax import lax
from jax.experimental import pallas as pl
from jax.experimental.pallas import tpu as pltpu

F32 = jnp.float32
BF16 = jnp.bfloat16

D = 2048
DEPTH = 2
GRID_W = 64
NA_HEADS, NA_DH, NA_ROWS, NA_COLS = 8, 64, 8, 16
DIFF_HEADS, DIFF_DK, DIFF_DV = 4, 64, 128
DIL_WINDOWS, DIL_DILATIONS = (128, 512, 2048), (1, 4, 16)
DIL_GROUPS, DIL_HEADS, DIL_DH = 3, 4, 128
SWA_QH, SWA_KVH, SWA_DH, SWA_RADIUS = 8, 2, 64, 128
SWA_GROUP = SWA_QH // SWA_KVH
BRANCH_W = 512
D_FF = ((8 * D // 3 + 127) // 128) * 128
N_ALIBI = SWA_QH + DIL_GROUPS * DIL_HEADS + DIFF_HEADS
RMS_EPS = 1e-6
NEG_INF = -1e30
LOG2E = math.log2(math.e)

LANES = 128
IN_WIDTH = 3 * NA_HEADS * NA_DH + 2 * DIFF_HEADS * 2 * DIFF_DK + DIFF_HEADS * DIFF_DV \
    + 3 * DIL_GROUPS * DIL_HEADS * DIL_DH + SWA_QH * SWA_DH + 2 * SWA_KVH * SWA_DH
N_BLK = IN_WIDTH // LANES
BLK_NA_Q, BLK_NA_K, BLK_NA_V = 0, 4, 8
BLK_DF_Q, BLK_DF_K, BLK_DF_V = 12, 16, 20
BLK_DL_Q, BLK_DL_K, BLK_DL_V = 24, 28, 32
BLK_SW_Q, BLK_SW_K, BLK_SW_V = 36, 40, 41
N_BLK_MAIN = 42
N_BLK_DIL = 3 * DIL_HEADS
COL_NA, COL_DF, COL_DL, COL_SW = 0, 1536, 3072, 7680
DIL_W = DIL_GROUPS * DIL_HEADS * DIL_DH

FFN_TF = 512
FFN_FP = -(-D_FF // FFN_TF) * FFN_TF
FFN_NC1 = FFN_FP // FFN_TF
TN_OUT = 256
TM_FFN = 1024
TM_PROJ = 1024
NB_PROJ = 6
TN_PROJ = NB_PROJ * LANES
NJ_MAIN = N_BLK_MAIN // NB_PROJ
NJ_DIL = N_BLK_DIL // NB_PROJ
TM_MERGE = 1024
TC_MERGE = 256
TM_COMBINE = 1024
RQ_NA = 4
NA_UROWS = 12
TQ_DIFF = 256
TQ_SWA = 256
TQ_DIL = 128
UNITS_DIL = 4
VMEM_LIMIT = 60 * 1024 * 1024

_ALIBI = [2.0 ** (-8.0 * (i + 1) / N_ALIBI) for i in range(N_ALIBI)]
SLOPES_SWA = _ALIBI[:SWA_QH]
SLOPES_DIL = _ALIBI[SWA_QH:SWA_QH + DIL_GROUPS * DIL_HEADS]
SLOPES_DIFF = _ALIBI[SWA_QH + DIL_GROUPS * DIL_HEADS:]


def _params(sem):
    return pltpu.CompilerParams(dimension_semantics=sem, vmem_limit_bytes=VMEM_LIMIT)


def _rms(x, g, eps):
    ms = jnp.mean(x * x, axis=-1, keepdims=True)
    return x * lax.rsqrt(ms + eps) * g


def _dot(a, b):
    return jnp.dot(a, b, preferred_element_type=F32)


def _dot_nt(a, b):
    return lax.dot_general(a, b, (((1,), (1,)), ((), ())), preferred_element_type=F32)


def _half_masks(rows):
    lane = lax.broadcasted_iota(jnp.int32, (rows, LANES), 1)
    return lane < (LANES // 2)


def _keep_half(x, lo, half):
    keep = lo if half == 0 else jnp.logical_not(lo)
    return jnp.where(keep, x.astype(F32), 0.0).astype(BF16)


def _ffn_kernel(x_ref, g_ref, wg_ref, wu_ref, w2_ref, xres_ref, o_ref, h_scr, act_scr):
    j = pl.program_id(1)

    @pl.when(j == 0)
    def _():
        h_scr[...] = _rms(x_ref[...], g_ref[...], RMS_EPS).astype(BF16)

    @pl.when(j < FFN_NC1)
    def _():
        h = h_scr[...]
        gate = _dot(h, wg_ref[...])
        up = _dot(h, wu_ref[...])
        act_scr[j] = (gate * jax.nn.sigmoid(gate) * up).astype(BF16)

    @pl.when(j >= FFN_NC1)
    def _():
        acc = _dot(act_scr[0], w2_ref[0:FFN_TF, :])
        for c in range(1, FFN_NC1):
            acc = acc + _dot(act_scr[c], w2_ref[c * FFN_TF:(c + 1) * FFN_TF, :])
        o_ref[...] = xres_ref[...] + 0.5 * acc


def _ffn(x, g, w_gate, w_up, w2):
    m = x.shape[0]
    tm = TM_FFN
    nc2 = D // TN_OUT
    hidden = lambda i, j: (0, jnp.minimum(j, FFN_NC1 - 1))
    col = lambda i, j: (i, jnp.maximum(j - FFN_NC1, 0))
    return pl.pallas_call(
        _ffn_kernel,
        grid=(m // tm, FFN_NC1 + nc2),
        in_specs=[
            pl.BlockSpec((tm, D), lambda i, j: (i, 0)),
            pl.BlockSpec((1, D), lambda i, j: (0, 0)),
            pl.BlockSpec((D, FFN_TF), hidden),
            pl.BlockSpec((D, FFN_TF), hidden),
            pl.BlockSpec((FFN_FP, TN_OUT), lambda i, j: (0, jnp.maximum(j - FFN_NC1, 0))),
            pl.BlockSpec((tm, TN_OUT), col),
        ],
        out_specs=pl.BlockSpec((tm, TN_OUT), col),
        out_shape=jax.ShapeDtypeStruct((m, D), F32),
        scratch_shapes=[pltpu.VMEM((tm, D), BF16), pltpu.VMEM((FFN_NC1, tm, FFN_TF), BF16)],
        compiler_params=_params(("parallel", "arbitrary")),
        name="ffn",
    )(x, g, w_gate, w_up, w2, x)


def _proj_kernel(x_ref, g_ref, w_ref, h_ref, main_ref, d4_ref, d16_ref, res_scr, *, tm):
    j = pl.program_id(1)

    @pl.when(j == 0)
    def _():
        h_ref[...] = _rms(x_ref[...], g_ref[...], RMS_EPS).astype(BF16)

    res = _dot(h_ref[...], w_ref[...])

    @pl.when(j < NJ_MAIN)
    def _():
        for k in range(NB_PROJ):
            main_ref[k] = res[:, k * LANES:(k + 1) * LANES].astype(BF16)

    @pl.when(j >= NJ_MAIN)
    def _():
        for k in range(NB_PROJ):
            res_scr[k] = res[:, k * LANES:(k + 1) * LANES]

    for out_ref, dil, first in ((d4_ref, DIL_DILATIONS[1], NJ_MAIN), (d16_ref, DIL_DILATIONS[2], NJ_MAIN + NJ_DIL)):
        @pl.when((j >= first) & (j < first + NJ_DIL))
        def _(out_ref=out_ref, dil=dil):
            for k in range(NB_PROJ):
                for r in range(dil):
                    rows = res_scr[k, pl.ds(r, tm // dil, stride=dil), :]
                    out_ref[k, :, r * LANES:(r + 1) * LANES] = rows.astype(BF16)


def _proj(x, g, w):
    m = x.shape[0]
    tm = TM_PROJ
    d4, d16 = DIL_DILATIONS[1], DIL_DILATIONS[2]
    return pl.pallas_call(
        functools.partial(_proj_kernel, tm=tm),
        grid=(m // tm, NJ_MAIN + 2 * NJ_DIL),
        in_specs=[
            pl.BlockSpec((tm, D), lambda i, j: (i, 0)),
            pl.BlockSpec((1, D), lambda i, j: (0, 0)),
            pl.BlockSpec((D, TN_PROJ), lambda i, j: (0, j)),
        ],
        out_specs=[
            pl.BlockSpec((tm, D), lambda i, j: (i, 0)),
            pl.BlockSpec((NB_PROJ, tm, LANES), lambda i, j: (jnp.minimum(j, NJ_MAIN - 1), i, 0)),
            pl.BlockSpec((NB_PROJ, tm // d4, d4 * LANES),
                         lambda i, j: (jnp.clip(j - NJ_MAIN, 0, NJ_DIL - 1), i, 0)),
            pl.BlockSpec((NB_PROJ, tm // d16, d16 * LANES),
                         lambda i, j: (jnp.clip(j - NJ_MAIN - NJ_DIL, 0, NJ_DIL - 1), i, 0)),
        ],
        out_shape=[jax.ShapeDtypeStruct((m, D), BF16),
                   jax.ShapeDtypeStruct((N_BLK_MAIN, m, LANES), BF16),
                   jax.ShapeDtypeStruct((N_BLK_DIL, m // d4, d4 * LANES), BF16),
                   jax.ShapeDtypeStruct((N_BLK_DIL, m // d16, d16 * LANES), BF16)],
        scratch_shapes=[pltpu.VMEM((NB_PROJ, tm, LANES), F32)],
        compiler_params=_params(("parallel", "arbitrary")),
        name="proj",
    )(x, g, w)


def _na_kernel(q_ref, k_ref, v_ref, b_ref, o_ref, *, rows):
    nq = RQ_NA * GRID_W
    nk = NA_UROWS * GRID_W
    row0 = jnp.clip(RQ_NA * pl.program_id(1) - NA_ROWS // 2, 0, rows - NA_UROWS)
    start = pl.multiple_of(row0 * GRID_W, GRID_W)
    lo = _half_masks(nq)
    for hp in range(NA_HEADS // 2):
        q = q_ref[hp]
        k = k_ref[hp, pl.ds(start, nk), :]
        v = v_ref[hp, pl.ds(start, nk), :]
        lhs = jnp.concatenate([_keep_half(q, lo, 0), _keep_half(q, lo, 1)], axis=0)
        s = _dot_nt(lhs, k) + b_ref[0, hp]
        m = jnp.max(s, axis=-1, keepdims=True)
        p = jnp.exp(s - m)
        l = jnp.sum(p, axis=-1, keepdims=True)
        o = _dot(p.astype(BF16), v) / l
        o_ref[:, hp * LANES:(hp + 1) * LANES] = jnp.where(lo, o[:nq], o[nq:]).astype(BF16)


def _na(qkv, bias, n_seq, t, row_off):
    rows = t // GRID_W
    nb = rows // RQ_NA
    nq = RQ_NA * GRID_W
    blk0 = row_off // nq
    seq0 = row_off // t
    group_type = lambda g: jnp.where(g == 0, 0, jnp.where(g == nb - 1, 2, 1))
    return pl.pallas_call(
        functools.partial(_na_kernel, rows=rows),
        grid=(n_seq, nb),
        in_specs=[
            pl.BlockSpec((4, nq, LANES), lambda s, g: (BLK_NA_Q // 4, blk0 + s * nb + g, 0)),
            pl.BlockSpec((4, t, LANES), lambda s, g: (BLK_NA_K // 4, seq0 + s, 0)),
            pl.BlockSpec((4, t, LANES), lambda s, g: (BLK_NA_V // 4, seq0 + s, 0)),
            pl.BlockSpec((1, NA_HEADS // 2, 2 * nq, NA_UROWS * GRID_W), lambda s, g: (group_type(g), 0, 0, 0)),
        ],
        out_specs=pl.BlockSpec((nq, BRANCH_W), lambda s, g: (s * nb + g, 0)),
        out_shape=jax.ShapeDtypeStruct((n_seq * t, BRANCH_W), BF16),
        compiler_params=_params(("parallel", "arbitrary")),
        name="mixer_a",
    )(qkv, qkv, qkv, bias)


def _na_bias_table(rpb):
    c = np.arange(GRID_W)[:, None]
    kc = np.arange(GRID_W)[None, :]
    cstart = np.clip(c - NA_COLS // 2, 0, GRID_W - NA_COLS)
    ok = (kc >= cstart) & (kc < cstart + NA_COLS)
    pad = GRID_W - NA_COLS
    padded = jnp.pad(rpb.astype(F32), ((0, 0), (0, 0), (pad, pad)))
    e = jnp.stack([padded[..., GRID_W - 1 - q:2 * GRID_W - 1 - q] for q in range(GRID_W)], axis=2)
    e = jnp.where(ok[None, None], e, NEG_INF)
    neg = jnp.full((NA_HEADS, GRID_W, GRID_W), NEG_INF, F32)
    half = NA_ROWS // 2
    group_types = (
        [(-i, -i) for i in range(RQ_NA)],
        [(-half - i, -half) for i in range(RQ_NA)],
        [(-(NA_UROWS - RQ_NA) - i, -(NA_ROWS - RQ_NA) - i) for i in range(RQ_NA)],
    )
    tables = []
    for rel in group_types:
        q_rows = []
        for u0, w0 in rel:
            blocks = [e[:, u0 + j + NA_ROWS - 1] if w0 <= u0 + j < w0 + NA_ROWS else neg for j in range(NA_UROWS)]
            q_rows.append(jnp.concatenate(blocks, axis=2))
        tables.append(jnp.concatenate(q_rows, axis=1))
    return jnp.stack(tables).reshape(3, NA_HEADS // 2, 2 * RQ_NA * GRID_W, NA_UROWS * GRID_W)


def _diff_kernel(lam_ref, q_ref, k_ref, v_ref, tab_ref, g_ref, o_ref, *, t, tq, out_scale):
    qi = pl.program_id(2)
    lam = lam_ref[0]
    k = k_ref[0]
    off = pl.multiple_of((t // tq - 1 - qi) * tq, tq)
    q = q_ref[0]
    bias = tab_ref[0, :, pl.ds(off, t)]
    lo = _half_masks(tq)

    def softmax_map(half):
        s = _dot_nt(_keep_half(q, lo, half), k) - bias
        m = jnp.max(s, axis=-1, keepdims=True)
        p = jnp.exp2(s - m)
        return p, jnp.sum(p, axis=-1, keepdims=True)

    p1, l1 = softmax_map(0)
    p2, l2 = softmax_map(1)
    a = p1 * (1.0 / l1) - p2 * (lam / l2)
    o = _dot(a.astype(BF16), v_ref[0])
    o_ref[...] = (_rms(o, g_ref[...], 1e-5) * out_scale).astype(BF16)


def _diff_bias_table(t, tq):
    r = lax.broadcasted_iota(jnp.int32, (tq, 2 * t - tq), 0)
    x = lax.broadcasted_iota(jnp.int32, (tq, 2 * t - tq), 1)
    dist = jnp.abs(r - x + (t - tq)).astype(F32)
    return jnp.asarray([s * LOG2E for s in SLOPES_DIFF], F32)[:, None, None] * dist[None]


def _diff(qkv, lam, subln_g, n_seq, t, row_off, lam_init):
    tq = TQ_DIFF
    nq = t // tq
    blk0 = row_off // tq
    seq0 = row_off // t
    return pl.pallas_call(
        functools.partial(_diff_kernel, t=t, tq=tq, out_scale=1.0 - lam_init),
        grid=(DIFF_HEADS, n_seq, nq),
        in_specs=[
            pl.BlockSpec(memory_space=pltpu.SMEM),
            pl.BlockSpec((1, tq, LANES), lambda h, s, i: (BLK_DF_Q + h, blk0 + s * nq + i, 0)),
            pl.BlockSpec((1, t, LANES), lambda h, s, i: (BLK_DF_K + h, seq0 + s, 0)),
            pl.BlockSpec((1, t, LANES), lambda h, s, i: (BLK_DF_V + h, seq0 + s, 0)),
            pl.BlockSpec((1, tq, 2 * t - tq), lambda h, s, i: (h, 0, 0)),
            pl.BlockSpec((1, DIFF_DV), lambda h, s, i: (0, 0)),
        ],
        out_specs=pl.BlockSpec((tq, LANES), lambda h, s, i: (s * nq + i, h)),
        out_shape=jax.ShapeDtypeStruct((n_seq * t, BRANCH_W), BF16),
        compiler_params=_params(("parallel", "arbitrary", "arbitrary")),
        name="mixer_b",
    )(lam, qkv, qkv, qkv, _diff_bias_table(t, tq), subln_g)


def _band_window(qi, tq, kw, radius, length):
    start = jnp.clip(qi * tq - radius, 0, length - kw)
    rel = (lax.broadcasted_iota(jnp.int32, (tq, kw), 0) + (qi * tq - start)) \
        - lax.broadcasted_iota(jnp.int32, (tq, kw), 1)
    dist = jnp.abs(rel)
    return pl.multiple_of(start, 64), dist.astype(F32), dist <= radius


def _dil_kernel(q_ref, k_ref, v_ref, o_ref, lse_ref, *, n_sub, tq, kw, radius, slopes, scale, nqb, nrb):
    for b in range(nqb):
        start, dist, valid = _band_window(pl.program_id(2) * nqb + b, tq, kw, radius, n_sub)
        rows = slice(b * tq, (b + 1) * tq)
        for rr in range(nrb):
            lanes = slice(rr * LANES, (rr + 1) * LANES)
            for h in range(DIL_HEADS):
                k = k_ref[h, pl.ds(start, kw), lanes]
                v = v_ref[h, pl.ds(start, kw), lanes]
                s = _dot_nt(q_ref[h, rows, lanes], k) * scale - slopes[h] * dist
                s = jnp.where(valid, s, NEG_INF)
                m = jnp.max(s, axis=-1, keepdims=True)
                p = jnp.exp(s - m)
                l = jnp.sum(p, axis=-1, keepdims=True)
                o_ref[h, rows, lanes] = (_dot(p.astype(BF16), v) / l).astype(BF16)
                lse_ref[h, rows, lanes] = jnp.broadcast_to(m + jnp.log(l), (tq, LANES))


def _dil_group(view, gi, n_seq, t, row_off):
    dil = DIL_DILATIONS[gi]
    radius = DIL_WINDOWS[gi] // (2 * dil)
    n_sub = t // dil
    tq = min(TQ_DIL, n_sub)
    kw = min(tq + 2 * radius, n_sub)
    nqb = min(n_sub // tq, UNITS_DIL)
    nrb = min(dil, UNITS_DIL // nqb)
    nq = n_sub // (tq * nqb)
    blk0 = row_off // dil // (tq * nqb)
    seq0 = row_off // t
    qb, kb, vb = (BLK_DL_Q // 4, BLK_DL_K // 4, BLK_DL_V // 4) if gi == 0 else (0, 1, 2)
    slopes = tuple(SLOPES_DIL[gi * DIL_HEADS + h] * dil for h in range(DIL_HEADS))
    rows_out = n_seq * n_sub
    q_idx = lambda s, r, i: (qb, blk0 + s * nq + i, r)
    o_idx = lambda s, r, i: (0, s * nq + i, r)
    o, lse = pl.pallas_call(
        functools.partial(_dil_kernel, n_sub=n_sub, tq=tq, kw=kw, radius=radius, slopes=slopes,
                          scale=DIL_DH ** -0.5, nqb=nqb, nrb=nrb),
        grid=(n_seq, dil // nrb, nq),
        in_specs=[
            pl.BlockSpec((4, nqb * tq, nrb * LANES), q_idx),
            pl.BlockSpec((4, n_sub, nrb * LANES), lambda s, r, i: (kb, seq0 + s, r)),
            pl.BlockSpec((4, n_sub, nrb * LANES), lambda s, r, i: (vb, seq0 + s, r)),
        ],
        out_specs=[pl.BlockSpec((4, nqb * tq, nrb * LANES), o_idx),
                   pl.BlockSpec((4, nqb * tq, nrb * LANES), o_idx)],
        out_shape=[jax.ShapeDtypeStruct((DIL_HEADS, rows_out, dil * LANES), BF16),
                   jax.ShapeDtypeStruct((DIL_HEADS, rows_out, dil * LANES), F32)],
        compiler_params=_params(("parallel", "arbitrary", "arbitrary")),
        name=f"mixer_c{gi}",
    )(view, view, view)
    return o, lse


def _dil_combine_kernel(o0, o1, o2, l0, l1, l2, out_ref, o_scr, l_scr, *, tm):
    for h in range(DIL_HEADS):
        for gi, (o_ref, l_ref) in ((1, (o1, l1)), (2, (o2, l2))):
            dil = DIL_DILATIONS[gi]
            for r in range(dil):
                rows = pl.ds(r, tm // dil, stride=dil)
                o_scr[gi - 1, rows, :] = o_ref[h, :, r * LANES:(r + 1) * LANES].astype(F32)
                l_scr[gi - 1, rows, :] = l_ref[h, :, r * LANES:(r + 1) * LANES]
        a0, a1, a2 = l0[h], l_scr[0], l_scr[1]
        mx = jnp.maximum(jnp.maximum(a0, a1), a2)
        e0, e1, e2 = jnp.exp(a0 - mx), jnp.exp(a1 - mx), jnp.exp(a2 - mx)
        num = e0 * o0[h].astype(F32) + e1 * o_scr[0] + e2 * o_scr[1]
        out_ref[:, h * LANES:(h + 1) * LANES] = (num / (e0 + e1 + e2)).astype(BF16)


def _dil_combine(outs, lses):
    m = outs[0].shape[1]
    tm = min(TM_COMBINE, m)
    specs = [pl.BlockSpec((DIL_HEADS, tm // dil, dil * LANES), lambda i: (0, i, 0)) for dil in DIL_DILATIONS]
    return pl.pallas_call(
        functools.partial(_dil_combine_kernel, tm=tm),
        grid=(m // tm,),
        in_specs=specs * 2,
        out_specs=pl.BlockSpec((tm, BRANCH_W), lambda i: (i, 0)),
        out_shape=jax.ShapeDtypeStruct((m, BRANCH_W), BF16),
        scratch_shapes=[pltpu.VMEM((2, tm, LANES), F32), pltpu.VMEM((2, tm, LANES), F32)],
        compiler_params=_params(("parallel",)),
        name="mixer_c_combine",
    )(*outs, *lses)


def _swa_kernel(sink_ref, q_ref, k_ref, v_ref, o_ref, *, t, tq, kw):
    qi = pl.program_id(1)
    start, dist, valid = _band_window(qi, tq, kw, SWA_RADIUS, t)
    k = k_ref[0, pl.ds(start, kw), :]
    v = v_ref[0, pl.ds(start, kw), :]
    lo = _half_masks(tq)
    for g in range(SWA_GROUP):
        q = q_ref[g]
        outs = []
        for hk in range(SWA_KVH):
            head = hk * SWA_GROUP + g
            sink = sink_ref[head]
            s = _dot_nt(_keep_half(q, lo, hk), k) - SLOPES_SWA[head] * dist
            s = jnp.where(valid, s, NEG_INF)
            m = jnp.maximum(jnp.max(s, axis=-1, keepdims=True), sink)
            p = jnp.exp(s - m)
            l = jnp.sum(p, axis=-1, keepdims=True) + jnp.exp(sink - m)
            outs.append(_dot(p.astype(BF16), v) / l)
        o_ref[:, g * LANES:(g + 1) * LANES] = jnp.where(lo, outs[0], outs[1]).astype(BF16)


def _swa(qkv, sink, n_seq, t, row_off):
    tq = TQ_SWA
    kw = min(tq + 2 * SWA_RADIUS, t)
    nq = t // tq
    blk0 = row_off // tq
    seq0 = row_off // t
    return pl.pallas_call(
        functools.partial(_swa_kernel, t=t, tq=tq, kw=kw),
        grid=(n_seq, nq),
        in_specs=[
            pl.BlockSpec(memory_space=pltpu.SMEM),
            pl.BlockSpec((4, tq, LANES), lambda s, i: (BLK_SW_Q // 4, blk0 + s * nq + i, 0)),
            pl.BlockSpec((1, t, LANES), lambda s, i: (BLK_SW_K, seq0 + s, 0)),
            pl.BlockSpec((1, t, LANES), lambda s, i: (BLK_SW_V, seq0 + s, 0)),
        ],
        out_specs=pl.BlockSpec((tq, BRANCH_W), lambda s, i: (s * nq + i, 0)),
        out_shape=jax.ShapeDtypeStruct((n_seq * t, BRANCH_W), BF16),
        compiler_params=_params(("parallel", "arbitrary")),
        name="mixer_d",
    )(sink, qkv, qkv, qkv)


def _merge_kernel(h_ref, oa_ref, ob_ref, oc_ref, od_ref, wg0_ref, wg1_ref, wg2_ref, wg3_ref,
                  bg_ref, wb_ref, wo_ref, xres_ref, o_ref, m_scr, *, nc1, tc):
    j = pl.program_id(1)

    @pl.when(j < nc1)
    def _():
        h = h_ref[...]
        acc = None
        branches = ((oa_ref, wg0_ref), (ob_ref, wg1_ref), (oc_ref, wg2_ref), (od_ref, wg3_ref))
        for n, (b_ref, wg_ref) in enumerate(branches):
            gate = jax.nn.sigmoid(_dot(h, wg_ref[...]) + bg_ref[n:n + 1, :])
            term = gate * _dot(b_ref[...], wb_ref[n])
            acc = term if acc is None else acc + term
        m_scr[j] = acc.astype(BF16)

    @pl.when(j >= nc1)
    def _():
        acc = _dot(m_scr[0], wo_ref[0:tc, :])
        for c in range(1, nc1):
            acc = acc + _dot(m_scr[c], wo_ref[c * tc:(c + 1) * tc, :])
        o_ref[...] = xres_ref[...] + acc


def _merge(x, h, branches, w_gate, b_gate, w_branch, w_out):
    m = x.shape[0]
    tm, tc = TM_MERGE, TC_MERGE
    nc1 = D // tc
    nc2 = D // TN_OUT
    first = lambda j: jnp.minimum(j, nc1 - 1)
    col = lambda i, j: (i, jnp.maximum(j - nc1, 0))
    row_tile = lambda i, j: (i, 0)
    gate_specs = [pl.BlockSpec((D, tc), functools.partial(lambda i, j, n: (0, n * nc1 + first(j)), n=n))
                  for n in range(4)]
    return pl.pallas_call(
        functools.partial(_merge_kernel, nc1=nc1, tc=tc),
        grid=(m // tm, nc1 + nc2),
        in_specs=[pl.BlockSpec((tm, D), row_tile)]
        + [pl.BlockSpec((tm, BRANCH_W), row_tile)] * 4
        + gate_specs
        + [
            pl.BlockSpec((4, tc), lambda i, j: (0, first(j))),
            pl.BlockSpec((4, BRANCH_W, tc), lambda i, j: (0, 0, first(j))),
            pl.BlockSpec((D, TN_OUT), lambda i, j: (0, jnp.maximum(j - nc1, 0))),
            pl.BlockSpec((tm, TN_OUT), col),
        ],
        out_specs=pl.BlockSpec((tm, TN_OUT), col),
        out_shape=jax.ShapeDtypeStruct((m, D), F32),
        scratch_shapes=[pltpu.VMEM((nc1, tm, tc), BF16)],
        compiler_params=_params(("parallel", "arbitrary")),
        name="merge",
    )(h, *branches, w_gate, w_gate, w_gate, w_gate, b_gate, w_branch, w_out, x)


def _final_norm_kernel(x_ref, g_ref, o_ref):
    o_ref[...] = _rms(x_ref[...], g_ref[...], RMS_EPS)


def _final_norm(x, g):
    m = x.shape[0]
    tm = 512
    return pl.pallas_call(
        _final_norm_kernel,
        grid=(m // tm,),
        in_specs=[pl.BlockSpec((tm, D), lambda i: (i, 0)), pl.BlockSpec((1, D), lambda i: (0, 0))],
        out_specs=pl.BlockSpec((tm, D), lambda i: (i, 0)),
        out_shape=jax.ShapeDtypeStruct((m, D), F32),
        compiler_params=_params(("parallel",)),
        name="final_norm",
    )(x, g)


def _prep_ffn(w_in, w_out):
    pad = FFN_FP - D_FF
    gate = jnp.pad(w_in[:, :D_FF].astype(BF16), ((0, 0), (0, pad)))
    up = jnp.pad(w_in[:, D_FF:].astype(BF16), ((0, 0), (0, pad)))
    return gate, up, jnp.pad(w_out.astype(BF16), ((0, pad), (0, 0)))


def _swa_pair_heads(w, axis):
    shape = w.shape
    split = shape[:axis] + (SWA_KVH, SWA_GROUP, SWA_DH) + shape[axis + 1:]
    return jnp.swapaxes(w.reshape(split), axis, axis + 1).reshape(shape)


def _proj_columns(w_in):
    scale = np.ones((IN_WIDTH,), np.float32)
    scale[COL_NA:COL_NA + NA_HEADS * NA_DH] = NA_DH ** -0.5
    scale[COL_DF:COL_DF + DIFF_HEADS * 2 * DIFF_DK] = DIFF_DK ** -0.5 * LOG2E
    scale[COL_SW:COL_SW + SWA_QH * SWA_DH] = SWA_DH ** -0.5
    w = w_in * scale
    gw = DIL_HEADS * DIL_DH

    def dil_group(gi):
        return [w[:, COL_DL + part * DIL_W + gi * gw:COL_DL + part * DIL_W + (gi + 1) * gw] for part in range(3)]

    sw_q_end = COL_SW + SWA_QH * SWA_DH
    cols = [w[:, :COL_DL]] + dil_group(0) + [_swa_pair_heads(w[:, COL_SW:sw_q_end], 1), w[:, sw_q_end:]] \
        + dil_group(1) + dil_group(2)
    return jnp.concatenate(cols, axis=1)


def _prep_w_in(w_in):
    return _proj_columns(w_in).astype(BF16)


def _encoder_layer(x, l, seqs, p):
    x = _ffn(x, p["norm_ffn1"][l][None], *_prep_ffn(p["w_ffn1_in"][l], p["w_ffn1_out"][l]))
    h, qkv, qkv_d4, qkv_d16 = _proj(x, p["norm_mix"][l][None], _prep_w_in(p["w_in"][l]))
    dil_views = (qkv, qkv_d4, qkv_d16)

    lam_init = 0.8 - 0.6 * math.exp(-0.3 * l)
    lv = p["diff_lambda"][l].astype(F32)
    lam = jnp.exp(jnp.sum(lv[0] * lv[1])) - jnp.exp(jnp.sum(lv[2] * lv[3])) + lam_init
    diff_lam = lam.reshape(1)
    na_bias = _na_bias_table(p["na_rpb"][l])
    subln = p["diff_subln"][l].astype(F32)[None]
    sink = p["swa_sink"][l].astype(F32)

    o_a, o_b, o_d = [], [], []
    dil_o = [[] for _ in range(DIL_GROUPS)]
    dil_lse = [[] for _ in range(DIL_GROUPS)]
    for n_seq, t, row_off in seqs:
        o_a.append(_na(qkv, na_bias, n_seq, t, row_off))
        o_b.append(_diff(qkv, diff_lam, subln, n_seq, t, row_off, lam_init))
        for gi in range(DIL_GROUPS):
            o, lse = _dil_group(dil_views[gi], gi, n_seq, t, row_off)
            dil_o[gi].append(o)
            dil_lse[gi].append(lse)
        o_d.append(_swa(qkv, sink, n_seq, t, row_off))
    o_c = _dil_combine([jnp.concatenate(o, axis=1) for o in dil_o],
                       [jnp.concatenate(s, axis=1) for s in dil_lse])
    branches = [jnp.concatenate(o_a, axis=0), jnp.concatenate(o_b, axis=0), o_c, jnp.concatenate(o_d, axis=0)]

    w_branch = p["w_branch"][l]
    w_branch = jnp.concatenate([w_branch[:3], _swa_pair_heads(w_branch[3:], 1)], axis=0).astype(BF16)
    x = _merge(x, h, branches, p["w_gate"][l].astype(BF16), p["b_gate"][l].astype(F32).reshape(4, D),
               w_branch, p["w_out"][l].astype(BF16))
    return _ffn(x, p["norm_ffn2"][l][None], *_prep_ffn(p["w_ffn2_in"][l], p["w_ffn2_out"][l]))


def kernel(x_prompt, x_sample, norm_ffn1, w_ffn1_in, w_ffn1_out, norm_mix, w_in, na_rpb, diff_lambda, diff_subln,
           swa_sink, w_branch, w_gate, b_gate, w_out, norm_ffn2, w_ffn2_in, w_ffn2_out, norm_final):
    p = dict(norm_ffn1=norm_ffn1, w_ffn1_in=w_ffn1_in, w_ffn1_out=w_ffn1_out, norm_mix=norm_mix, w_in=w_in,
             na_rpb=na_rpb, diff_lambda=diff_lambda, diff_subln=diff_subln, swa_sink=swa_sink,
             w_branch=w_branch, w_gate=w_gate, b_gate=b_gate, w_out=w_out, norm_ffn2=norm_ffn2,
             w_ffn2_in=w_ffn2_in, w_ffn2_out=w_ffn2_out)
    bp, tp, _ = x_prompt.shape
    bs, ts, _ = x_sample.shape
    mp = bp * tp
    x = jnp.concatenate([x_prompt.reshape(mp, D), x_sample.reshape(bs * ts, D)], axis=0).astype(F32)
    seqs = ((bp, tp, 0), (bs, ts, mp))
    for l in range(DEPTH):
        x = _encoder_layer(x, l, seqs, p)
    y = _final_norm(x, norm_final.astype(F32)[None])
    return y[:mp].reshape(bp, tp, D), y[mp:].reshape(bs, ts, D)
```

```python
import functools
import math

import jax
import jax.numpy as jnp
import numpy as np
from jax import lax
from jax.experimental import pallas as pl
from jax.experimental.pallas import tpu as pltpu

F32 = jnp.float32
BF16 = jnp.bfloat16

D = 2048
DEPTH = 2
GRID_W = 64
NA_HEADS, NA_DH, NA_ROWS, NA_COLS = 8, 64, 8, 16
DIFF_HEADS, DIFF_DK, DIFF_DV = 4, 64, 128
DIL_WINDOWS, DIL_DILATIONS = (128, 512, 2048), (1, 4, 16)
DIL_GROUPS, DIL_HEADS, DIL_DH = 3, 4, 128
SWA_QH, SWA_KVH, SWA_DH, SWA_RADIUS = 8, 2, 64, 128
SWA_GROUP = SWA_QH // SWA_KVH
BRANCH_W = 512
D_FF = ((8 * D // 3 + 127) // 128) * 128
N_ALIBI = SWA_QH + DIL_GROUPS * DIL_HEADS + DIFF_HEADS
RMS_EPS = 1e-6
NEG_INF = -1e30
LOG2E = math.log2(math.e)

LANES = 128
IN_WIDTH = 3 * NA_HEADS * NA_DH + 2 * DIFF_HEADS * 2 * DIFF_DK + DIFF_HEADS * DIFF_DV \
    + 3 * DIL_GROUPS * DIL_HEADS * DIL_DH + SWA_QH * SWA_DH + 2 * SWA_KVH * SWA_DH
N_BLK = IN_WIDTH // LANES
BLK_NA_Q, BLK_NA_K, BLK_NA_V = 0, 4, 8
BLK_DF_Q, BLK_DF_K, BLK_DF_V = 12, 16, 20
BLK_DL_Q, BLK_DL_K, BLK_DL_V = 24, 28, 32
BLK_SW_Q, BLK_SW_K, BLK_SW_V = 36, 40, 41
N_BLK_MAIN = 42
N_BLK_DIL = 3 * DIL_HEADS
COL_NA, COL_DF, COL_DL, COL_SW = 0, 1536, 3072, 7680
DIL_W = DIL_GROUPS * DIL_HEADS * DIL_DH

FFN_TF = 512
FFN_FP = -(-D_FF // FFN_TF) * FFN_TF
FFN_NC1 = FFN_FP // FFN_TF
TN_OUT = 256
TM_FFN = 1024
TM_PROJ = 1024
NB_PROJ = 6
TN_PROJ = NB_PROJ * LANES
NJ_MAIN = N_BLK_MAIN // NB_PROJ
NJ_DIL = N_BLK_DIL // NB_PROJ
TM_MERGE = 1024
TC_MERGE = 256
TN_MERGE = 512
TM_COMBINE = 1024
RQ_NA = 4
NA_UROWS = 12
TQ_DIFF = 256
TQ_SWA = 256
TQ_DIL = 128
UNITS_DIL = 4
VMEM_LIMIT = 60 * 1024 * 1024

_ALIBI = [2.0 ** (-8.0 * (i + 1) / N_ALIBI) for i in range(N_ALIBI)]
SLOPES_SWA = _ALIBI[:SWA_QH]
SLOPES_DIL = _ALIBI[SWA_QH:SWA_QH + DIL_GROUPS * DIL_HEADS]
SLOPES_DIFF = _ALIBI[SWA_QH + DIL_GROUPS * DIL_HEADS:]


def _params(sem):
    return pltpu.CompilerParams(dimension_semantics=sem, vmem_limit_bytes=VMEM_LIMIT)


def _rms(x, g, eps):
    ms = jnp.mean(x * x, axis=-1, keepdims=True)
    return x * lax.rsqrt(ms + eps) * g


def _dot(a, b):
    return jnp.dot(a, b, preferred_element_type=F32)


def _dot_nt(a, b):
    return lax.dot_general(a, b, (((1,), (1,)), ((), ())), preferred_element_type=F32)


def _half_masks(rows):
    lane = lax.broadcasted_iota(jnp.int32, (rows, LANES), 1)
    return lane < (LANES // 2)


def _keep_half(x, lo, half):
    keep = lo if half == 0 else jnp.logical_not(lo)
    return jnp.where(keep, x.astype(F32), 0.0).astype(BF16)


def _ffn_kernel(x_ref, g_ref, wg_ref, wu_ref, w2_ref, xres_ref, o_ref, h_scr, act_scr):
    j = pl.program_id(1)

    @pl.when(j == 0)
    def _():
        h_scr[...] = _rms(x_ref[...], g_ref[...], RMS_EPS).astype(BF16)

    @pl.when(j < FFN_NC1)
    def _():
        h = h_scr[...]
        gate = _dot(h, wg_ref[...])
        up = _dot(h, wu_ref[...])
        act_scr[j] = (gate * jax.nn.sigmoid(gate) * up).astype(BF16)

    @pl.when(j >= FFN_NC1)
    def _():
        acc = _dot(act_scr[0], w2_ref[0:FFN_TF, :])
        for c in range(1, FFN_NC1):
            acc = acc + _dot(act_scr[c], w2_ref[c * FFN_TF:(c + 1) * FFN_TF, :])
        o_ref[...] = xres_ref[...] + 0.5 * acc


def _ffn(x, g, w1, w2):
    m = x.shape[0]
    tm = TM_FFN
    nc2 = D // TN_OUT
    col = lambda i, j: (i, jnp.maximum(j - FFN_NC1, 0))
    return pl.pallas_call(
        _ffn_kernel,
        grid=(m // tm, FFN_NC1 + nc2),
        in_specs=[
            pl.BlockSpec((tm, D), lambda i, j: (i, 0)),
            pl.BlockSpec((1, D), lambda i, j: (0, 0)),
            pl.BlockSpec((None, D, FFN_TF), lambda i, j: (0, 0, jnp.minimum(j, FFN_NC1 - 1))),
            pl.BlockSpec((None, D, FFN_TF), lambda i, j: (1, 0, jnp.minimum(j, FFN_NC1 - 1))),
            pl.BlockSpec((FFN_FP, TN_OUT), lambda i, j: (0, jnp.maximum(j - FFN_NC1, 0))),
            pl.BlockSpec((tm, TN_OUT), col),
        ],
        out_specs=pl.BlockSpec((tm, TN_OUT), col),
        out_shape=jax.ShapeDtypeStruct((m, D), F32),
        scratch_shapes=[pltpu.VMEM((tm, D), BF16), pltpu.VMEM((FFN_NC1, tm, FFN_TF), BF16)],
        compiler_params=_params(("parallel", "arbitrary")),
        name="ffn",
    )(x, g, w1, w1, w2, x)


def _proj_kernel(x_ref, g_ref, w_ref, h_ref, main_ref, d4_ref, d16_ref, res_scr, *, tm):
    j = pl.program_id(1)

    @pl.when(j == 0)
    def _():
        h_ref[...] = _rms(x_ref[...], g_ref[...], RMS_EPS).astype(BF16)

    res = _dot(h_ref[...], w_ref[...])

    @pl.when(j < NJ_MAIN)
    def _():
        for k in range(NB_PROJ):
            main_ref[k] = res[:, k * LANES:(k + 1) * LANES].astype(BF16)

    @pl.when(j >= NJ_MAIN)
    def _():
        for k in range(NB_PROJ):
            res_scr[k] = res[:, k * LANES:(k + 1) * LANES]

    for out_ref, dil, first in ((d4_ref, DIL_DILATIONS[1], NJ_MAIN), (d16_ref, DIL_DILATIONS[2], NJ_MAIN + NJ_DIL)):
        @pl.when((j >= first) & (j < first + NJ_DIL))
        def _(out_ref=out_ref, dil=dil):
            for k in range(NB_PROJ):
                for r in range(dil):
                    rows = res_scr[k, pl.ds(r, tm // dil, stride=dil), :]
                    out_ref[k, :, r * LANES:(r + 1) * LANES] = rows.astype(BF16)


def _proj(x, g, w):
    m = x.shape[0]
    tm = TM_PROJ
    d4, d16 = DIL_DILATIONS[1], DIL_DILATIONS[2]
    return pl.pallas_call(
        functools.partial(_proj_kernel, tm=tm),
        grid=(m // tm, NJ_MAIN + 2 * NJ_DIL),
        in_specs=[
            pl.BlockSpec((tm, D), lambda i, j: (i, 0)),
            pl.BlockSpec((1, D), lambda i, j: (0, 0)),
            pl.BlockSpec((D, TN_PROJ), lambda i, j: (0, j)),
        ],
        out_specs=[
            pl.BlockSpec((tm, D), lambda i, j: (i, 0)),
            pl.BlockSpec((NB_PROJ, tm, LANES), lambda i, j: (jnp.minimum(j, NJ_MAIN - 1), i, 0)),
            pl.BlockSpec((NB_PROJ, tm // d4, d4 * LANES),
                         lambda i, j: (jnp.clip(j - NJ_MAIN, 0, NJ_DIL - 1), i, 0)),
            pl.BlockSpec((NB_PROJ, tm // d16, d16 * LANES),
                         lambda i, j: (jnp.clip(j - NJ_MAIN - NJ_DIL, 0, NJ_DIL - 1), i, 0)),
        ],
        out_shape=[jax.ShapeDtypeStruct((m, D), BF16),
                   jax.ShapeDtypeStruct((N_BLK_MAIN, m, LANES), BF16),
                   jax.ShapeDtypeStruct((N_BLK_DIL, m // d4, d4 * LANES), BF16),
                   jax.ShapeDtypeStruct((N_BLK_DIL, m // d16, d16 * LANES), BF16)],
        scratch_shapes=[pltpu.VMEM((NB_PROJ, tm, LANES), F32)],
        compiler_params=_params(("parallel", "arbitrary")),
        name="proj",
    )(x, g, w)


def _na_kernel(q_ref, k_ref, v_ref, b_ref, o_ref, *, rows):
    nq = RQ_NA * GRID_W
    nk = NA_UROWS * GRID_W
    row0 = jnp.clip(RQ_NA * pl.program_id(1) - NA_ROWS // 2, 0, rows - NA_UROWS)
    start = pl.multiple_of(row0 * GRID_W, GRID_W)
    lo = _half_masks(nq)
    for hp in range(NA_HEADS // 2):
        q = q_ref[hp]
        k = k_ref[hp, pl.ds(start, nk), :]
        v = v_ref[hp, pl.ds(start, nk), :]
        lhs = jnp.concatenate([_keep_half(q, lo, 0), _keep_half(q, lo, 1)], axis=0)
        s = _dot_nt(lhs, k) + b_ref[0, hp]
        m = jnp.max(s, axis=-1, keepdims=True)
        p = jnp.exp(s - m)
        l = jnp.sum(p, axis=-1, keepdims=True)
        o = _dot(p.astype(BF16), v) / l
        o_ref[:, hp * LANES:(hp + 1) * LANES] = jnp.where(lo, o[:nq], o[nq:]).astype(BF16)


def _na(qkv, bias, n_seq, t, row_off):
    rows = t // GRID_W
    nb = rows // RQ_NA
    nq = RQ_NA * GRID_W
    blk0 = row_off // nq
    seq0 = row_off // t
    group_type = lambda g: jnp.where(g == 0, 0, jnp.where(g == nb - 1, 2, 1))
    return pl.pallas_call(
        functools.partial(_na_kernel, rows=rows),
        grid=(n_seq, nb),
        in_specs=[
            pl.BlockSpec((4, nq, LANES), lambda s, g: (BLK_NA_Q // 4, blk0 + s * nb + g, 0)),
            pl.BlockSpec((4, t, LANES), lambda s, g: (BLK_NA_K // 4, seq0 + s, 0)),
            pl.BlockSpec((4, t, LANES), lambda s, g: (BLK_NA_V // 4, seq0 + s, 0)),
            pl.BlockSpec((1, NA_HEADS // 2, 2 * nq, NA_UROWS * GRID_W), lambda s, g: (group_type(g), 0, 0, 0)),
        ],
        out_specs=pl.BlockSpec((nq, BRANCH_W), lambda s, g: (s * nb + g, 0)),
        out_shape=jax.ShapeDtypeStruct((n_seq * t, BRANCH_W), BF16),
        compiler_params=_params(("parallel", "arbitrary")),
        name="mixer_a",
    )(qkv, qkv, qkv, bias)


def _na_bias_table(rpb):
    c = np.arange(GRID_W)[:, None]
    kc = np.arange(GRID_W)[None, :]
    cstart = np.clip(c - NA_COLS // 2, 0, GRID_W - NA_COLS)
    ok = (kc >= cstart) & (kc < cstart + NA_COLS)
    pad = GRID_W - NA_COLS
    padded = jnp.pad(rpb.astype(F32), ((0, 0), (0, 0), (pad, pad)))
    e = jnp.stack([padded[..., GRID_W - 1 - q:2 * GRID_W - 1 - q] for q in range(GRID_W)], axis=2)
    e = jnp.where(ok[None, None], e, NEG_INF)
    neg = jnp.full((NA_HEADS, GRID_W, GRID_W), NEG_INF, F32)
    half = NA_ROWS // 2
    group_types = (
        [(-i, -i) for i in range(RQ_NA)],
        [(-half - i, -half) for i in range(RQ_NA)],
        [(-(NA_UROWS - RQ_NA) - i, -(NA_ROWS - RQ_NA) - i) for i in range(RQ_NA)],
    )
    tables = []
    for rel in group_types:
        q_rows = []
        for u0, w0 in rel:
            blocks = [e[:, u0 + j + NA_ROWS - 1] if w0 <= u0 + j < w0 + NA_ROWS else neg for j in range(NA_UROWS)]
            q_rows.append(jnp.concatenate(blocks, axis=2))
        tables.append(jnp.concatenate(q_rows, axis=1))
    return jnp.stack(tables).reshape(3, NA_HEADS // 2, 2 * RQ_NA * GRID_W, NA_UROWS * GRID_W)


def _diff_kernel(lam_ref, q_ref, k_ref, v_ref, tab_ref, g_ref, o_ref, *, t, tq, out_scale):
    qi = pl.program_id(2)
    lam = lam_ref[0]
    k = k_ref[0]
    off = pl.multiple_of((t // tq - 1 - qi) * tq, tq)
    q = q_ref[0]
    bias = tab_ref[0, :, pl.ds(off, t)]
    lo = _half_masks(tq)

    def softmax_map(half):
        s = _dot_nt(_keep_half(q, lo, half), k) - bias
        m = jnp.max(s, axis=-1, keepdims=True)
        p = jnp.exp2(s - m)
        return p, jnp.sum(p, axis=-1, keepdims=True)

    p1, l1 = softmax_map(0)
    p2, l2 = softmax_map(1)
    a = p1 * (1.0 / l1) - p2 * (lam / l2)
    o = _dot(a.astype(BF16), v_ref[0])
    o_ref[...] = (_rms(o, g_ref[...], 1e-5) * out_scale).astype(BF16)


def _diff_bias_table(t, tq):
    r = lax.broadcasted_iota(jnp.int32, (tq, 2 * t - tq), 0)
    x = lax.broadcasted_iota(jnp.int32, (tq, 2 * t - tq), 1)
    dist = jnp.abs(r - x + (t - tq)).astype(F32)
    return jnp.asarray([s * LOG2E for s in SLOPES_DIFF], F32)[:, None, None] * dist[None]


def _diff(qkv, lam, subln_g, n_seq, t, row_off, lam_init):
    tq = TQ_DIFF
    nq = t // tq
    blk0 = row_off // tq
    seq0 = row_off // t
    return pl.pallas_call(
        functools.partial(_diff_kernel, t=t, tq=tq, out_scale=1.0 - lam_init),
        grid=(DIFF_HEADS, n_seq, nq),
        in_specs=[
            pl.BlockSpec(memory_space=pltpu.SMEM),
            pl.BlockSpec((1, tq, LANES), lambda h, s, i: (BLK_DF_Q + h, blk0 + s * nq + i, 0)),
            pl.BlockSpec((1, t, LANES), lambda h, s, i: (BLK_DF_K + h, seq0 + s, 0)),
            pl.BlockSpec((1, t, LANES), lambda h, s, i: (BLK_DF_V + h, seq0 + s, 0)),
            pl.BlockSpec((1, tq, 2 * t - tq), lambda h, s, i: (h, 0, 0)),
            pl.BlockSpec((1, DIFF_DV), lambda h, s, i: (0, 0)),
        ],
        out_specs=pl.BlockSpec((tq, LANES), lambda h, s, i: (s * nq + i, h)),
        out_shape=jax.ShapeDtypeStruct((n_seq * t, BRANCH_W), BF16),
        compiler_params=_params(("parallel", "arbitrary", "arbitrary")),
        name="mixer_b",
    )(lam, qkv, qkv, qkv, _diff_bias_table(t, tq), subln_g)


def _band_window(qi, tq, kw, radius, length):
    start = jnp.clip(qi * tq - radius, 0, length - kw)
    rel = (lax.broadcasted_iota(jnp.int32, (tq, kw), 0) + (qi * tq - start)) \
        - lax.broadcasted_iota(jnp.int32, (tq, kw), 1)
    dist = jnp.abs(rel)
    return pl.multiple_of(start, 64), dist.astype(F32), dist <= radius


def _dil_kernel(q_ref, k_ref, v_ref, o_ref, lse_ref, *, n_sub, tq, kw, radius, slopes, scale, nqb, nrb):
    for b in range(nqb):
        start, dist, valid = _band_window(pl.program_id(2) * nqb + b, tq, kw, radius, n_sub)
        rows = slice(b * tq, (b + 1) * tq)
        for rr in range(nrb):
            lanes = slice(rr * LANES, (rr + 1) * LANES)
            for h in range(DIL_HEADS):
                k = k_ref[h, pl.ds(start, kw), lanes]
                v = v_ref[h, pl.ds(start, kw), lanes]
                s = _dot_nt(q_ref[h, rows, lanes], k) * scale - slopes[h] * dist
                s = jnp.where(valid, s, NEG_INF)
                m = jnp.max(s, axis=-1, keepdims=True)
                p = jnp.exp(s - m)
                l = jnp.sum(p, axis=-1, keepdims=True)
                o_ref[h, rows, lanes] = (_dot(p.astype(BF16), v) / l).astype(BF16)
                lse_ref[h, rows, lanes] = jnp.broadcast_to(m + jnp.log(l), (tq, LANES))


def _dil_group(view, gi, n_seq, t, row_off):
    dil = DIL_DILATIONS[gi]
    radius = DIL_WINDOWS[gi] // (2 * dil)
    n_sub = t // dil
    tq = min(TQ_DIL, n_sub)
    kw = min(tq + 2 * radius, n_sub)
    nqb = min(n_sub // tq, UNITS_DIL)
    nrb = min(dil, UNITS_DIL // nqb)
    nq = n_sub // (tq * nqb)
    blk0 = row_off // dil // (tq * nqb)
    seq0 = row_off // t
    qb, kb, vb = (BLK_DL_Q // 4, BLK_DL_K // 4, BLK_DL_V // 4) if gi == 0 else (0, 1, 2)
    slopes = tuple(SLOPES_DIL[gi * DIL_HEADS + h] * dil for h in range(DIL_HEADS))
    rows_out = n_seq * n_sub
    q_idx = lambda s, r, i: (qb, blk0 + s * nq + i, r)
    o_idx = lambda s, r, i: (0, s * nq + i, r)
    o, lse = pl.pallas_call(
        functools.partial(_dil_kernel, n_sub=n_sub, tq=tq, kw=kw, radius=radius, slopes=slopes,
                          scale=DIL_DH ** -0.5, nqb=nqb, nrb=nrb),
        grid=(n_seq, dil // nrb, nq),
        in_specs=[
            pl.BlockSpec((4, nqb * tq, nrb * LANES), q_idx),
            pl.BlockSpec((4, n_sub, nrb * LANES), lambda s, r, i: (kb, seq0 + s, r)),
            pl.BlockSpec((4, n_sub, nrb * LANES), lambda s, r, i: (vb, seq0 + s, r)),
        ],
        out_specs=[pl.BlockSpec((4, nqb * tq, nrb * LANES), o_idx),
                   pl.BlockSpec((4, nqb * tq, nrb * LANES), o_idx)],
        out_shape=[jax.ShapeDtypeStruct((DIL_HEADS, rows_out, dil * LANES), BF16),
                   jax.ShapeDtypeStruct((DIL_HEADS, rows_out, dil * LANES), F32)],
        compiler_params=_params(("parallel", "arbitrary", "arbitrary")),
        name=f"mixer_c{gi}",
    )(view, view, view)
    return o, lse


def _dil_combine_kernel(o0, o1, o2, l0, l1, l2, out_ref, o_scr, l_scr, *, tm):
    for h in range(DIL_HEADS):
        for gi, (o_ref, l_ref) in ((1, (o1, l1)), (2, (o2, l2))):
            dil = DIL_DILATIONS[gi]
            for r in range(dil):
                rows = pl.ds(r, tm // dil, stride=dil)
                o_scr[gi - 1, rows, :] = o_ref[h, :, r * LANES:(r + 1) * LANES].astype(F32)
                l_scr[gi - 1, rows, :] = l_ref[h, :, r * LANES:(r + 1) * LANES]
        a0, a1, a2 = l0[h], l_scr[0], l_scr[1]
        mx = jnp.maximum(jnp.maximum(a0, a1), a2)
        e0, e1, e2 = jnp.exp(a0 - mx), jnp.exp(a1 - mx), jnp.exp(a2 - mx)
        num = e0 * o0[h].astype(F32) + e1 * o_scr[0] + e2 * o_scr[1]
        out_ref[:, h * LANES:(h + 1) * LANES] = (num / (e0 + e1 + e2)).astype(BF16)


def _dil_combine(outs, lses):
    m = outs[0].shape[1]
    tm = min(TM_COMBINE, m)
    specs = [pl.BlockSpec((DIL_HEADS, tm // dil, dil * LANES), lambda i: (0, i, 0)) for dil in DIL_DILATIONS]
    return pl.pallas_call(
        functools.partial(_dil_combine_kernel, tm=tm),
        grid=(m // tm,),
        in_specs=specs * 2,
        out_specs=pl.BlockSpec((tm, BRANCH_W), lambda i: (i, 0)),
        out_shape=jax.ShapeDtypeStruct((m, BRANCH_W), BF16),
        scratch_shapes=[pltpu.VMEM((2, tm, LANES), F32), pltpu.VMEM((2, tm, LANES), F32)],
        compiler_params=_params(("parallel",)),
        name="mixer_c_combine",
    )(*outs, *lses)


def _swa_kernel(sink_ref, q_ref, k_ref, v_ref, o_ref, *, t, tq, kw):
    qi = pl.program_id(1)
    start, dist, valid = _band_window(qi, tq, kw, SWA_RADIUS, t)
    k = k_ref[0, pl.ds(start, kw), :]
    v = v_ref[0, pl.ds(start, kw), :]
    lo = _half_masks(tq)
    for g in range(SWA_GROUP):
        q = q_ref[g]
        outs = []
        for hk in range(SWA_KVH):
            head = hk * SWA_GROUP + g
            sink = sink_ref[head]
            s = _dot_nt(_keep_half(q, lo, hk), k) - SLOPES_SWA[head] * dist
            s = jnp.where(valid, s, NEG_INF)
            m = jnp.maximum(jnp.max(s, axis=-1, keepdims=True), sink)
            p = jnp.exp(s - m)
            l = jnp.sum(p, axis=-1, keepdims=True) + jnp.exp(sink - m)
            outs.append(_dot(p.astype(BF16), v) / l)
        o_ref[:, g * LANES:(g + 1) * LANES] = jnp.where(lo, outs[0], outs[1]).astype(BF16)


def _swa(qkv, sink, n_seq, t, row_off):
    tq = TQ_SWA
    kw = min(tq + 2 * SWA_RADIUS, t)
    nq = t // tq
    blk0 = row_off // tq
    seq0 = row_off // t
    return pl.pallas_call(
        functools.partial(_swa_kernel, t=t, tq=tq, kw=kw),
        grid=(n_seq, nq),
        in_specs=[
            pl.BlockSpec(memory_space=pltpu.SMEM),
            pl.BlockSpec((4, tq, LANES), lambda s, i: (BLK_SW_Q // 4, blk0 + s * nq + i, 0)),
            pl.BlockSpec((1, t, LANES), lambda s, i: (BLK_SW_K, seq0 + s, 0)),
            pl.BlockSpec((1, t, LANES), lambda s, i: (BLK_SW_V, seq0 + s, 0)),
        ],
        out_specs=pl.BlockSpec((tq, BRANCH_W), lambda s, i: (s * nq + i, 0)),
        out_shape=jax.ShapeDtypeStruct((n_seq * t, BRANCH_W), BF16),
        compiler_params=_params(("parallel", "arbitrary")),
        name="mixer_d",
    )(sink, qkv, qkv, qkv)


def _merge_kernel(h_ref, oa_ref, ob_ref, oc_ref, od_ref, wg0_ref, wg1_ref, wg2_ref, wg3_ref,
                  bg_ref, wb_ref, wo_ref, xres_ref, o_ref, m_scr, *, nc1, tc):
    j = pl.program_id(1)

    @pl.when(j < nc1)
    def _():
        h = h_ref[...]
        acc = None
        branches = ((oa_ref, wg0_ref), (ob_ref, wg1_ref), (oc_ref, wg2_ref), (od_ref, wg3_ref))
        for n, (b_ref, wg_ref) in enumerate(branches):
            gate = jax.nn.sigmoid(_dot(h, wg_ref[...]) + bg_ref[n:n + 1, :])
            term = gate * _dot(b_ref[...], wb_ref[n])
            acc = term if acc is None else acc + term
        m_scr[j] = acc.astype(BF16)

    @pl.when(j >= nc1)
    def _():
        acc = _dot(m_scr[0], wo_ref[0:tc, :])
        for c in range(1, nc1):
            acc = acc + _dot(m_scr[c], wo_ref[c * tc:(c + 1) * tc, :])
        o_ref[...] = xres_ref[...] + acc


def _merge(x, h, branches, w_gate, b_gate, w_branch, w_out):
    m = x.shape[0]
    tm, tc, tn = TM_MERGE, TC_MERGE, TN_MERGE
    nc1 = D // tc
    nc2 = D // tn
    first = lambda j: jnp.minimum(j, nc1 - 1)
    col = lambda i, j: (i, jnp.maximum(j - nc1, 0))
    row_tile = lambda i, j: (i, 0)
    gate_specs = [pl.BlockSpec((D, tc), functools.partial(lambda i, j, n: (0, n * nc1 + first(j)), n=n))
                  for n in range(4)]
    return pl.pallas_call(
        functools.partial(_merge_kernel, nc1=nc1, tc=tc),
        grid=(m // tm, nc1 + nc2),
        in_specs=[pl.BlockSpec((tm, D), row_tile)]
        + [pl.BlockSpec((tm, BRANCH_W), row_tile)] * 4
        + gate_specs
        + [
            pl.BlockSpec((4, tc), lambda i, j: (0, first(j))),
            pl.BlockSpec((4, BRANCH_W, tc), lambda i, j: (0, 0, first(j))),
            pl.BlockSpec((D, tn), lambda i, j: (0, jnp.maximum(j - nc1, 0))),
            pl.BlockSpec((tm, tn), col),
        ],
        out_specs=pl.BlockSpec((tm, tn), col),
        out_shape=jax.ShapeDtypeStruct((m, D), F32),
        scratch_shapes=[pltpu.VMEM((nc1, tm, tc), BF16)],
        compiler_params=_params(("parallel", "arbitrary")),
        name="merge",
    )(h, *branches, w_gate, w_gate, w_gate, w_gate, b_gate, w_branch, w_out, x)


def _final_norm_kernel(x_ref, g_ref, first_ref, second_ref, *, n_first):
    y = _rms(x_ref[...], g_ref[...], RMS_EPS)

    @pl.when(pl.program_id(0) < n_first)
    def _():
        first_ref[...] = y

    @pl.when(pl.program_id(0) >= n_first)
    def _():
        second_ref[...] = y


def _final_norm(x, g, m_first):
    m = x.shape[0]
    tm = 512
    n_first = m_first // tm
    return pl.pallas_call(
        functools.partial(_final_norm_kernel, n_first=n_first),
        grid=(m // tm,),
        in_specs=[pl.BlockSpec((tm, D), lambda i: (i, 0)), pl.BlockSpec((1, D), lambda i: (0, 0))],
        out_specs=[pl.BlockSpec((tm, D), lambda i: (jnp.minimum(i, n_first - 1), 0)),
                   pl.BlockSpec((tm, D), lambda i: (jnp.maximum(i - n_first, 0), 0))],
        out_shape=[jax.ShapeDtypeStruct((m_first, D), F32), jax.ShapeDtypeStruct((m - m_first, D), F32)],
        compiler_params=_params(("arbitrary",)),
        name="final_norm",
    )(x, g)


def _cast_ffn_in_kernel(w_ref, o_ref):
    o_ref[0, :, :D_FF] = w_ref[...].astype(BF16)
    o_ref[0, :, D_FF:] = jnp.zeros((o_ref.shape[1], FFN_FP - D_FF), BF16)


def _cast_ffn_in(w_in):
    tr = 256
    return pl.pallas_call(
        _cast_ffn_in_kernel,
        grid=(2, D // tr),
        in_specs=[pl.BlockSpec((tr, D_FF), lambda part, i: (i, part))],
        out_specs=pl.BlockSpec((1, tr, FFN_FP), lambda part, i: (part, i, 0)),
        out_shape=jax.ShapeDtypeStruct((2, D, FFN_FP), BF16),
        compiler_params=_params(("parallel", "parallel")),
        name="cast_ffn_in",
    )(w_in)


def _cast_ffn_out_kernel(w_ref, o_ref, *, tr):
    row = pl.program_id(0) * tr + lax.broadcasted_iota(jnp.int32, (tr, D), 0)
    o_ref[...] = jnp.where(row < D_FF, w_ref[...], 0.0).astype(BF16)


def _cast_ffn_out(w_out):
    tr = 512
    return pl.pallas_call(
        functools.partial(_cast_ffn_out_kernel, tr=tr),
        grid=(FFN_FP // tr,),
        in_specs=[pl.BlockSpec((tr, D), lambda i: (i, 0))],
        out_specs=pl.BlockSpec((tr, D), lambda i: (i, 0)),
        out_shape=jax.ShapeDtypeStruct((FFN_FP, D), BF16),
        compiler_params=_params(("parallel",)),
        name="cast_ffn_out",
    )(w_out)


def _prep_ffn(w_in, w_out):
    return _cast_ffn_in(w_in), _cast_ffn_out(w_out)


def _swa_pair_heads(w, axis):
    shape = w.shape
    split = shape[:axis] + (SWA_KVH, SWA_GROUP, SWA_DH) + shape[axis + 1:]
    return jnp.swapaxes(w.reshape(split), axis, axis + 1).reshape(shape)


def _proj_columns(w_in):
    scale = np.ones((IN_WIDTH,), np.float32)
    scale[COL_NA:COL_NA + NA_HEADS * NA_DH] = NA_DH ** -0.5
    scale[COL_DF:COL_DF + DIFF_HEADS * 2 * DIFF_DK] = DIFF_DK ** -0.5 * LOG2E
    scale[COL_SW:COL_SW + SWA_QH * SWA_DH] = SWA_DH ** -0.5
    w = w_in * scale
    gw = DIL_HEADS * DIL_DH

    def dil_group(gi):
        return [w[:, COL_DL + part * DIL_W + gi * gw:COL_DL + part * DIL_W + (gi + 1) * gw] for part in range(3)]

    sw_q_end = COL_SW + SWA_QH * SWA_DH
    cols = [w[:, :COL_DL]] + dil_group(0) + [_swa_pair_heads(w[:, COL_SW:sw_q_end], 1), w[:, sw_q_end:]] \
        + dil_group(1) + dil_group(2)
    return jnp.concatenate(cols, axis=1)


def _prep_w_in(w_in):
    return _proj_columns(w_in).astype(BF16)


def _encoder_layer(x, l, seqs, p):
    x = _ffn(x, p["norm_ffn1"][l][None], *_prep_ffn(p["w_ffn1_in"][l], p["w_ffn1_out"][l]))
    h, qkv, qkv_d4, qkv_d16 = _proj(x, p["norm_mix"][l][None], _prep_w_in(p["w_in"][l]))
    dil_views = (qkv, qkv_d4, qkv_d16)

    lam_init = 0.8 - 0.6 * math.exp(-0.3 * l)
    lv = p["diff_lambda"][l].astype(F32)
    lam = jnp.exp(jnp.sum(lv[0] * lv[1])) - jnp.exp(jnp.sum(lv[2] * lv[3])) + lam_init
    diff_lam = lam.reshape(1)
    na_bias = _na_bias_table(p["na_rpb"][l])
    subln = p["diff_subln"][l].astype(F32)[None]
    sink = p["swa_sink"][l].astype(F32)

    o_a, o_b, o_d = [], [], []
    dil_o = [[] for _ in range(DIL_GROUPS)]
    dil_lse = [[] for _ in range(DIL_GROUPS)]
    for n_seq, t, row_off in seqs:
        o_a.append(_na(qkv, na_bias, n_seq, t, row_off))
        o_b.append(_diff(qkv, diff_lam, subln, n_seq, t, row_off, lam_init))
        for gi in range(DIL_GROUPS):
            o, lse = _dil_group(dil_views[gi], gi, n_seq, t, row_off)
            dil_o[gi].append(o)
            dil_lse[gi].append(lse)
        o_d.append(_swa(qkv, sink, n_seq, t, row_off))
    o_c = _dil_combine([jnp.concatenate(o, axis=1) for o in dil_o],
                       [jnp.concatenate(s, axis=1) for s in dil_lse])
    branches = [jnp.concatenate(o_a, axis=0), jnp.concatenate(o_b, axis=0), o_c, jnp.concatenate(o_d, axis=0)]

    w_branch = p["w_branch"][l]
    w_branch = jnp.concatenate([w_branch[:3], _swa_pair_heads(w_branch[3:], 1)], axis=0).astype(BF16)
    x = _merge(x, h, branches, p["w_gate"][l].astype(BF16), p["b_gate"][l].astype(F32).reshape(4, D),
               w_branch, p["w_out"][l].astype(BF16))
    return _ffn(x, p["norm_ffn2"][l][None], *_prep_ffn(p["w_ffn2_in"][l], p["w_ffn2_out"][l]))


def kernel(x_prompt, x_sample, norm_ffn1, w_ffn1_in, w_ffn1_out, norm_mix, w_in, na_rpb, diff_lambda, diff_subln,
           swa_sink, w_branch, w_gate, b_gate, w_out, norm_ffn2, w_ffn2_in, w_ffn2_out, norm_final):
    p = dict(norm_ffn1=norm_ffn1, w_ffn1_in=w_ffn1_in, w_ffn1_out=w_ffn1_out, norm_mix=norm_mix, w_in=w_in,
             na_rpb=na_rpb, diff_lambda=diff_lambda, diff_subln=diff_subln, swa_sink=swa_sink,
             w_branch=w_branch, w_gate=w_gate, b_gate=b_gate, w_out=w_out, norm_ffn2=norm_ffn2,
             w_ffn2_in=w_ffn2_in, w_ffn2_out=w_ffn2_out)
    bp, tp, _ = x_prompt.shape
    bs, ts, _ = x_sample.shape
    mp = bp * tp
    x = jnp.concatenate([x_prompt.reshape(mp, D), x_sample.reshape(bs * ts, D)], axis=0).astype(F32)
    seqs = ((bp, tp, 0), (bs, ts, mp))
    for l in range(DEPTH):
        x = _encoder_layer(x, l, seqs, p)
    y_prompt, y_sample = _final_norm(x, norm_final.astype(F32)[None], mp)
    return y_prompt.reshape(bp, tp, D), y_sample.reshape(bs, ts, D)
```

```python
import functools
import math

import jax
import jax.numpy as jnp
import numpy as np
from jax import lax
from jax.experimental import pallas as pl
from jax.experimental.pallas import tpu as pltpu

F32 = jnp.float32
BF16 = jnp.bfloat16

D = 2048
DEPTH = 2
GRID_W = 64
NA_HEADS, NA_DH, NA_ROWS, NA_COLS = 8, 64, 8, 16
DIFF_HEADS, DIFF_DK, DIFF_DV = 4, 64, 128
DIL_WINDOWS, DIL_DILATIONS = (128, 512, 2048), (1, 4, 16)
DIL_GROUPS, DIL_HEADS, DIL_DH = 3, 4, 128
SWA_QH, SWA_KVH, SWA_DH, SWA_RADIUS = 8, 2, 64, 128
SWA_GROUP = SWA_QH // SWA_KVH
BRANCH_W = 512
D_FF = ((8 * D // 3 + 127) // 128) * 128
N_ALIBI = SWA_QH + DIL_GROUPS * DIL_HEADS + DIFF_HEADS
RMS_EPS = 1e-6
NEG_INF = -1e30
LOG2E = math.log2(math.e)

LANES = 128
IN_WIDTH = 3 * NA_HEADS * NA_DH + 2 * DIFF_HEADS * 2 * DIFF_DK + DIFF_HEADS * DIFF_DV \
    + 3 * DIL_GROUPS * DIL_HEADS * DIL_DH + SWA_QH * SWA_DH + 2 * SWA_KVH * SWA_DH
N_BLK = IN_WIDTH // LANES
BLK_NA_Q, BLK_NA_K, BLK_NA_V = 0, 4, 8
BLK_DF_Q, BLK_DF_K, BLK_DF_V = 12, 16, 20
BLK_DL_Q, BLK_DL_K, BLK_DL_V = 24, 28, 32
BLK_SW_Q, BLK_SW_K, BLK_SW_V = 36, 40, 41
N_BLK_MAIN = 42
N_BLK_DIL = 3 * DIL_HEADS
COL_NA, COL_DF, COL_DL, COL_SW = 0, 1536, 3072, 7680
DIL_W = DIL_GROUPS * DIL_HEADS * DIL_DH

FFN_TF = 512
FFN_FP = -(-D_FF // FFN_TF) * FFN_TF
FFN_NC1 = FFN_FP // FFN_TF
TN_OUT = 256
TM_FFN = 1024
TM_PROJ = 1024
NB_PROJ = 6
TN_PROJ = NB_PROJ * LANES
NJ_MAIN = N_BLK_MAIN // NB_PROJ
NJ_DIL = N_BLK_DIL // NB_PROJ
TM_MERGE = 1024
TC_MERGE = 256
TN_MERGE = 512
TM_COMBINE = 1024
RQ_NA = 4
NA_UROWS = 12
TQ_DIFF = 256
TQ_SWA = 256
TQ_DIL = 128
UNITS_DIL = 4
VMEM_LIMIT = 60 * 1024 * 1024

_ALIBI = [2.0 ** (-8.0 * (i + 1) / N_ALIBI) for i in range(N_ALIBI)]
SLOPES_SWA = _ALIBI[:SWA_QH]
SLOPES_DIL = _ALIBI[SWA_QH:SWA_QH + DIL_GROUPS * DIL_HEADS]
SLOPES_DIFF = _ALIBI[SWA_QH + DIL_GROUPS * DIL_HEADS:]


def _params(sem):
    return pltpu.CompilerParams(dimension_semantics=sem, vmem_limit_bytes=VMEM_LIMIT)


def _call_into(body, prev, n_in, **kw):
    if prev is None:
        return pl.pallas_call(body, **kw)
    prev = tuple(prev)

    def aliased(*refs):
        body(*refs[:n_in], *refs[n_in + len(prev):])

    kw["in_specs"] = list(kw["in_specs"]) + [pl.BlockSpec(memory_space=pl.ANY)] * len(prev)
    call = pl.pallas_call(aliased, input_output_aliases={n_in + i: i for i in range(len(prev))}, **kw)
    return lambda *args: call(*args, *prev)


def _rms(x, g, eps):
    ms = jnp.mean(x * x, axis=-1, keepdims=True)
    return x * lax.rsqrt(ms + eps) * g


def _dot(a, b):
    return jnp.dot(a, b, preferred_element_type=F32)


def _dot_nt(a, b):
    return lax.dot_general(a, b, (((1,), (1,)), ((), ())), preferred_element_type=F32)


def _half_masks(rows):
    lane = lax.broadcasted_iota(jnp.int32, (rows, LANES), 1)
    return lane < (LANES // 2)


def _keep_half(x, lo, half):
    keep = lo if half == 0 else jnp.logical_not(lo)
    return jnp.where(keep, x.astype(F32), 0.0).astype(BF16)


def _ffn_kernel(x_ref, g_ref, wg_ref, wu_ref, w2_ref, xres_ref, o_ref, h_scr, act_scr):
    j = pl.program_id(1)

    @pl.when(j == 0)
    def _():
        h_scr[...] = _rms(x_ref[...], g_ref[...], RMS_EPS).astype(BF16)

    @pl.when(j < FFN_NC1)
    def _():
        h = h_scr[...]
        gate = _dot(h, wg_ref[...])
        up = _dot(h, wu_ref[...])
        act_scr[j] = (gate * jax.nn.sigmoid(gate) * up).astype(BF16)

    @pl.when(j >= FFN_NC1)
    def _():
        acc = _dot(act_scr[0], w2_ref[0:FFN_TF, :])
        for c in range(1, FFN_NC1):
            acc = acc + _dot(act_scr[c], w2_ref[c * FFN_TF:(c + 1) * FFN_TF, :])
        o_ref[...] = xres_ref[...] + 0.5 * acc


def _ffn(x, g, w1, w2, m_total=None, row_off=0, prev=None):
    m = x.shape[0]
    m_total = m if m_total is None else m_total
    tm = TM_FFN
    nc2 = D // TN_OUT
    tile0 = row_off // tm
    col = lambda i, j: (i, jnp.maximum(j - FFN_NC1, 0))
    return _call_into(
        _ffn_kernel, prev, 6,
        grid=(m // tm, FFN_NC1 + nc2),
        in_specs=[
            pl.BlockSpec((tm, D), lambda i, j: (i, 0)),
            pl.BlockSpec((1, D), lambda i, j: (0, 0)),
            pl.BlockSpec((None, D, FFN_TF), lambda i, j: (0, 0, jnp.minimum(j, FFN_NC1 - 1))),
            pl.BlockSpec((None, D, FFN_TF), lambda i, j: (1, 0, jnp.minimum(j, FFN_NC1 - 1))),
            pl.BlockSpec((FFN_FP, TN_OUT), lambda i, j: (0, jnp.maximum(j - FFN_NC1, 0))),
            pl.BlockSpec((tm, TN_OUT), col),
        ],
        out_specs=pl.BlockSpec((tm, TN_OUT), lambda i, j: (tile0 + i, jnp.maximum(j - FFN_NC1, 0))),
        out_shape=jax.ShapeDtypeStruct((m_total, D), F32),
        scratch_shapes=[pltpu.VMEM((tm, D), BF16), pltpu.VMEM((FFN_NC1, tm, FFN_TF), BF16)],
        compiler_params=_params(("parallel", "arbitrary")),
        name="ffn",
    )(x, g, w1, w1, w2, x)


def _proj_kernel(x_ref, g_ref, w_ref, h_ref, main_ref, d4_ref, d16_ref, res_scr, *, tm):
    j = pl.program_id(1)

    @pl.when(j == 0)
    def _():
        h_ref[...] = _rms(x_ref[...], g_ref[...], RMS_EPS).astype(BF16)

    res = _dot(h_ref[...], w_ref[...])

    @pl.when(j < NJ_MAIN)
    def _():
        for k in range(NB_PROJ):
            main_ref[k] = res[:, k * LANES:(k + 1) * LANES].astype(BF16)

    @pl.when(j >= NJ_MAIN)
    def _():
        for k in range(NB_PROJ):
            res_scr[k] = res[:, k * LANES:(k + 1) * LANES]

    for out_ref, dil, first in ((d4_ref, DIL_DILATIONS[1], NJ_MAIN), (d16_ref, DIL_DILATIONS[2], NJ_MAIN + NJ_DIL)):
        @pl.when((j >= first) & (j < first + NJ_DIL))
        def _(out_ref=out_ref, dil=dil):
            for k in range(NB_PROJ):
                for r in range(dil):
                    rows = res_scr[k, pl.ds(r, tm // dil, stride=dil), :]
                    out_ref[k, :, r * LANES:(r + 1) * LANES] = rows.astype(BF16)


def _proj(x, g, w):
    m = x.shape[0]
    tm = TM_PROJ
    d4, d16 = DIL_DILATIONS[1], DIL_DILATIONS[2]
    return pl.pallas_call(
        functools.partial(_proj_kernel, tm=tm),
        grid=(m // tm, NJ_MAIN + 2 * NJ_DIL),
        in_specs=[
            pl.BlockSpec((tm, D), lambda i, j: (i, 0)),
            pl.BlockSpec((1, D), lambda i, j: (0, 0)),
            pl.BlockSpec((D, TN_PROJ), lambda i, j: (0, j)),
        ],
        out_specs=[
            pl.BlockSpec((tm, D), lambda i, j: (i, 0)),
            pl.BlockSpec((NB_PROJ, tm, LANES), lambda i, j: (jnp.minimum(j, NJ_MAIN - 1), i, 0)),
            pl.BlockSpec((NB_PROJ, tm // d4, d4 * LANES),
                         lambda i, j: (jnp.clip(j - NJ_MAIN, 0, NJ_DIL - 1), i, 0)),
            pl.BlockSpec((NB_PROJ, tm // d16, d16 * LANES),
                         lambda i, j: (jnp.clip(j - NJ_MAIN - NJ_DIL, 0, NJ_DIL - 1), i, 0)),
        ],
        out_shape=[jax.ShapeDtypeStruct((m, D), BF16),
                   jax.ShapeDtypeStruct((N_BLK_MAIN, m, LANES), BF16),
                   jax.ShapeDtypeStruct((N_BLK_DIL, m // d4, d4 * LANES), BF16),
                   jax.ShapeDtypeStruct((N_BLK_DIL, m // d16, d16 * LANES), BF16)],
        scratch_shapes=[pltpu.VMEM((NB_PROJ, tm, LANES), F32)],
        compiler_params=_params(("parallel", "arbitrary")),
        name="proj",
    )(x, g, w)


def _na_kernel(q_ref, k_ref, v_ref, b_ref, o_ref, *, rows):
    nq = RQ_NA * GRID_W
    nk = NA_UROWS * GRID_W
    row0 = jnp.clip(RQ_NA * pl.program_id(1) - NA_ROWS // 2, 0, rows - NA_UROWS)
    start = pl.multiple_of(row0 * GRID_W, GRID_W)
    lo = _half_masks(nq)
    for hp in range(NA_HEADS // 2):
        q = q_ref[hp]
        k = k_ref[hp, pl.ds(start, nk), :]
        v = v_ref[hp, pl.ds(start, nk), :]
        lhs = jnp.concatenate([_keep_half(q, lo, 0), _keep_half(q, lo, 1)], axis=0)
        s = _dot_nt(lhs, k) + b_ref[0, hp]
        m = jnp.max(s, axis=-1, keepdims=True)
        p = jnp.exp(s - m)
        l = jnp.sum(p, axis=-1, keepdims=True)
        o = _dot(p.astype(BF16), v) / l
        o_ref[:, hp * LANES:(hp + 1) * LANES] = jnp.where(lo, o[:nq], o[nq:]).astype(BF16)


def _na(qkv, bias, n_seq, t, row_off, prev):
    rows = t // GRID_W
    nb = rows // RQ_NA
    nq = RQ_NA * GRID_W
    blk0 = row_off // nq
    seq0 = row_off // t
    group_type = lambda g: jnp.where(g == 0, 0, jnp.where(g == nb - 1, 2, 1))
    return _call_into(
        functools.partial(_na_kernel, rows=rows), prev, 4,
        grid=(n_seq, nb),
        in_specs=[
            pl.BlockSpec((4, nq, LANES), lambda s, g: (BLK_NA_Q // 4, blk0 + s * nb + g, 0)),
            pl.BlockSpec((4, t, LANES), lambda s, g: (BLK_NA_K // 4, seq0 + s, 0)),
            pl.BlockSpec((4, t, LANES), lambda s, g: (BLK_NA_V // 4, seq0 + s, 0)),
            pl.BlockSpec((1, NA_HEADS // 2, 2 * nq, NA_UROWS * GRID_W), lambda s, g: (group_type(g), 0, 0, 0)),
        ],
        out_specs=pl.BlockSpec((nq, BRANCH_W), lambda s, g: (blk0 + s * nb + g, 0)),
        out_shape=jax.ShapeDtypeStruct((qkv.shape[1], BRANCH_W), BF16),
        compiler_params=_params(("parallel", "arbitrary")),
        name="mixer_a",
    )(qkv, qkv, qkv, bias)


def _na_bias_table(rpb):
    c = np.arange(GRID_W)[:, None]
    kc = np.arange(GRID_W)[None, :]
    cstart = np.clip(c - NA_COLS // 2, 0, GRID_W - NA_COLS)
    ok = (kc >= cstart) & (kc < cstart + NA_COLS)
    pad = GRID_W - NA_COLS
    padded = jnp.pad(rpb.astype(F32), ((0, 0), (0, 0), (pad, pad)))
    e = jnp.stack([padded[..., GRID_W - 1 - q:2 * GRID_W - 1 - q] for q in range(GRID_W)], axis=2)
    e = jnp.where(ok[None, None], e, NEG_INF)
    neg = jnp.full((NA_HEADS, GRID_W, GRID_W), NEG_INF, F32)
    half = NA_ROWS // 2
    group_types = (
        [(-i, -i) for i in range(RQ_NA)],
        [(-half - i, -half) for i in range(RQ_NA)],
        [(-(NA_UROWS - RQ_NA) - i, -(NA_ROWS - RQ_NA) - i) for i in range(RQ_NA)],
    )
    tables = []
    for rel in group_types:
        q_rows = []
        for u0, w0 in rel:
            blocks = [e[:, u0 + j + NA_ROWS - 1] if w0 <= u0 + j < w0 + NA_ROWS else neg for j in range(NA_UROWS)]
            q_rows.append(jnp.concatenate(blocks, axis=2))
        tables.append(jnp.concatenate(q_rows, axis=1))
    return jnp.stack(tables).reshape(3, NA_HEADS // 2, 2 * RQ_NA * GRID_W, NA_UROWS * GRID_W)


def _diff_kernel(lam_ref, q_ref, k_ref, v_ref, tab_ref, g_ref, o_ref, *, t, tq, out_scale):
    qi = pl.program_id(2)
    lam = lam_ref[0]
    k = k_ref[0]
    off = pl.multiple_of((t // tq - 1 - qi) * tq, tq)
    q = q_ref[0]
    bias = tab_ref[0, :, pl.ds(off, t)]
    lo = _half_masks(tq)

    def softmax_map(half):
        s = _dot_nt(_keep_half(q, lo, half), k) - bias
        m = jnp.max(s, axis=-1, keepdims=True)
        p = jnp.exp2(s - m)
        return p, jnp.sum(p, axis=-1, keepdims=True)

    p1, l1 = softmax_map(0)
    p2, l2 = softmax_map(1)
    a = p1 * (1.0 / l1) - p2 * (lam / l2)
    o = _dot(a.astype(BF16), v_ref[0])
    o_ref[...] = (_rms(o, g_ref[...], 1e-5) * out_scale).astype(BF16)


def _diff_bias_table(t, tq):
    r = lax.broadcasted_iota(jnp.int32, (tq, 2 * t - tq), 0)
    x = lax.broadcasted_iota(jnp.int32, (tq, 2 * t - tq), 1)
    dist = jnp.abs(r - x + (t - tq)).astype(F32)
    return jnp.asarray([s * LOG2E for s in SLOPES_DIFF], F32)[:, None, None] * dist[None]


def _diff(qkv, lam, subln_g, n_seq, t, row_off, lam_init, prev):
    tq = TQ_DIFF
    nq = t // tq
    blk0 = row_off // tq
    seq0 = row_off // t
    return _call_into(
        functools.partial(_diff_kernel, t=t, tq=tq, out_scale=1.0 - lam_init), prev, 6,
        grid=(DIFF_HEADS, n_seq, nq),
        in_specs=[
            pl.BlockSpec(memory_space=pltpu.SMEM),
            pl.BlockSpec((1, tq, LANES), lambda h, s, i: (BLK_DF_Q + h, blk0 + s * nq + i, 0)),
            pl.BlockSpec((1, t, LANES), lambda h, s, i: (BLK_DF_K + h, seq0 + s, 0)),
            pl.BlockSpec((1, t, LANES), lambda h, s, i: (BLK_DF_V + h, seq0 + s, 0)),
            pl.BlockSpec((1, tq, 2 * t - tq), lambda h, s, i: (h, 0, 0)),
            pl.BlockSpec((1, DIFF_DV), lambda h, s, i: (0, 0)),
        ],
        out_specs=pl.BlockSpec((tq, LANES), lambda h, s, i: (blk0 + s * nq + i, h)),
        out_shape=jax.ShapeDtypeStruct((qkv.shape[1], BRANCH_W), BF16),
        compiler_params=_params(("parallel", "arbitrary", "arbitrary")),
        name="mixer_b",
    )(lam, qkv, qkv, qkv, _diff_bias_table(t, tq), subln_g)


def _band_window(qi, tq, kw, radius, length):
    start = jnp.clip(qi * tq - radius, 0, length - kw)
    rel = (lax.broadcasted_iota(jnp.int32, (tq, kw), 0) + (qi * tq - start)) \
        - lax.broadcasted_iota(jnp.int32, (tq, kw), 1)
    dist = jnp.abs(rel)
    return pl.multiple_of(start, 64), dist.astype(F32), dist <= radius


def _dil_kernel(q_ref, k_ref, v_ref, o_ref, lse_ref, *, n_sub, tq, kw, radius, slopes, scale, nqb, nrb):
    for b in range(nqb):
        start, dist, valid = _band_window(pl.program_id(2) * nqb + b, tq, kw, radius, n_sub)
        rows = slice(b * tq, (b + 1) * tq)
        for rr in range(nrb):
            lanes = slice(rr * LANES, (rr + 1) * LANES)
            for h in range(DIL_HEADS):
                k = k_ref[h, pl.ds(start, kw), lanes]
                v = v_ref[h, pl.ds(start, kw), lanes]
                s = _dot_nt(q_ref[h, rows, lanes], k) * scale - slopes[h] * dist
                s = jnp.where(valid, s, NEG_INF)
                m = jnp.max(s, axis=-1, keepdims=True)
                p = jnp.exp(s - m)
                l = jnp.sum(p, axis=-1, keepdims=True)
                o_ref[h, rows, lanes] = (_dot(p.astype(BF16), v) / l).astype(BF16)
                lse_ref[h, rows, lanes] = jnp.broadcast_to(m + jnp.log(l), (tq, LANES))


def _dil_group(view, gi, n_seq, t, row_off, prev):
    dil = DIL_DILATIONS[gi]
    radius = DIL_WINDOWS[gi] // (2 * dil)
    n_sub = t // dil
    tq = min(TQ_DIL, n_sub)
    kw = min(tq + 2 * radius, n_sub)
    nqb = min(n_sub // tq, UNITS_DIL)
    nrb = min(dil, UNITS_DIL // nqb)
    nq = n_sub // (tq * nqb)
    blk0 = row_off // dil // (tq * nqb)
    seq0 = row_off // t
    qb, kb, vb = (BLK_DL_Q // 4, BLK_DL_K // 4, BLK_DL_V // 4) if gi == 0 else (0, 1, 2)
    slopes = tuple(SLOPES_DIL[gi * DIL_HEADS + h] * dil for h in range(DIL_HEADS))
    rows_out = view.shape[1]
    q_idx = lambda s, r, i: (qb, blk0 + s * nq + i, r)
    o_idx = lambda s, r, i: (0, blk0 + s * nq + i, r)
    o, lse = _call_into(
        functools.partial(_dil_kernel, n_sub=n_sub, tq=tq, kw=kw, radius=radius, slopes=slopes,
                          scale=DIL_DH ** -0.5, nqb=nqb, nrb=nrb), prev, 3,
        grid=(n_seq, dil // nrb, nq),
        in_specs=[
            pl.BlockSpec((4, nqb * tq, nrb * LANES), q_idx),
            pl.BlockSpec((4, n_sub, nrb * LANES), lambda s, r, i: (kb, seq0 + s, r)),
            pl.BlockSpec((4, n_sub, nrb * LANES), lambda s, r, i: (vb, seq0 + s, r)),
        ],
        out_specs=[pl.BlockSpec((4, nqb * tq, nrb * LANES), o_idx),
                   pl.BlockSpec((4, nqb * tq, nrb * LANES), o_idx)],
        out_shape=[jax.ShapeDtypeStruct((DIL_HEADS, rows_out, dil * LANES), BF16),
                   jax.ShapeDtypeStruct((DIL_HEADS, rows_out, dil * LANES), F32)],
        compiler_params=_params(("parallel", "arbitrary", "arbitrary")),
        name=f"mixer_c{gi}",
    )(view, view, view)
    return o, lse


def _dil_combine_kernel(o0, o1, o2, l0, l1, l2, out_ref, o_scr, l_scr, *, tm):
    for h in range(DIL_HEADS):
        for gi, (o_ref, l_ref) in ((1, (o1, l1)), (2, (o2, l2))):
            dil = DIL_DILATIONS[gi]
            for r in range(dil):
                rows = pl.ds(r, tm // dil, stride=dil)
                o_scr[gi - 1, rows, :] = o_ref[h, :, r * LANES:(r + 1) * LANES].astype(F32)
                l_scr[gi - 1, rows, :] = l_ref[h, :, r * LANES:(r + 1) * LANES]
        a0, a1, a2 = l0[h], l_scr[0], l_scr[1]
        mx = jnp.maximum(jnp.maximum(a0, a1), a2)
        e0, e1, e2 = jnp.exp(a0 - mx), jnp.exp(a1 - mx), jnp.exp(a2 - mx)
        num = e0 * o0[h].astype(F32) + e1 * o_scr[0] + e2 * o_scr[1]
        out_ref[:, h * LANES:(h + 1) * LANES] = (num / (e0 + e1 + e2)).astype(BF16)


def _dil_combine(outs, lses):
    m = outs[0].shape[1]
    tm = min(TM_COMBINE, m)
    specs = [pl.BlockSpec((DIL_HEADS, tm // dil, dil * LANES), lambda i: (0, i, 0)) for dil in DIL_DILATIONS]
    return pl.pallas_call(
        functools.partial(_dil_combine_kernel, tm=tm),
        grid=(m // tm,),
        in_specs=specs * 2,
        out_specs=pl.BlockSpec((tm, BRANCH_W), lambda i: (i, 0)),
        out_shape=jax.ShapeDtypeStruct((m, BRANCH_W), BF16),
        scratch_shapes=[pltpu.VMEM((2, tm, LANES), F32), pltpu.VMEM((2, tm, LANES), F32)],
        compiler_params=_params(("parallel",)),
        name="mixer_c_combine",
    )(*outs, *lses)


def _swa_kernel(sink_ref, q_ref, k_ref, v_ref, o_ref, *, t, tq, kw):
    qi = pl.program_id(1)
    start, dist, valid = _band_window(qi, tq, kw, SWA_RADIUS, t)
    k = k_ref[0, pl.ds(start, kw), :]
    v = v_ref[0, pl.ds(start, kw), :]
    lo = _half_masks(tq)
    for g in range(SWA_GROUP):
        q = q_ref[g]
        outs = []
        for hk in range(SWA_KVH):
            head = hk * SWA_GROUP + g
            sink = sink_ref[head]
            s = _dot_nt(_keep_half(q, lo, hk), k) - SLOPES_SWA[head] * dist
            s = jnp.where(valid, s, NEG_INF)
            m = jnp.maximum(jnp.max(s, axis=-1, keepdims=True), sink)
            p = jnp.exp(s - m)
            l = jnp.sum(p, axis=-1, keepdims=True) + jnp.exp(sink - m)
            outs.append(_dot(p.astype(BF16), v) / l)
        o_ref[:, g * LANES:(g + 1) * LANES] = jnp.where(lo, outs[0], outs[1]).astype(BF16)


def _swa(qkv, sink, n_seq, t, row_off, prev):
    tq = TQ_SWA
    kw = min(tq + 2 * SWA_RADIUS, t)
    nq = t // tq
    blk0 = row_off // tq
    seq0 = row_off // t
    return _call_into(
        functools.partial(_swa_kernel, t=t, tq=tq, kw=kw), prev, 4,
        grid=(n_seq, nq),
        in_specs=[
            pl.BlockSpec(memory_space=pltpu.SMEM),
            pl.BlockSpec((4, tq, LANES), lambda s, i: (BLK_SW_Q // 4, blk0 + s * nq + i, 0)),
            pl.BlockSpec((1, t, LANES), lambda s, i: (BLK_SW_K, seq0 + s, 0)),
            pl.BlockSpec((1, t, LANES), lambda s, i: (BLK_SW_V, seq0 + s, 0)),
        ],
        out_specs=pl.BlockSpec((tq, BRANCH_W), lambda s, i: (blk0 + s * nq + i, 0)),
        out_shape=jax.ShapeDtypeStruct((qkv.shape[1], BRANCH_W), BF16),
        compiler_params=_params(("parallel", "arbitrary")),
        name="mixer_d",
    )(sink, qkv, qkv, qkv)


def _merge_kernel(h_ref, oa_ref, ob_ref, oc_ref, od_ref, wg0_ref, wg1_ref, wg2_ref, wg3_ref,
                  bg_ref, wb_ref, wo_ref, xres_ref, o_ref, m_scr, *, nc1, tc):
    j = pl.program_id(1)

    @pl.when(j < nc1)
    def _():
        h = h_ref[...]
        acc = None
        branches = ((oa_ref, wg0_ref), (ob_ref, wg1_ref), (oc_ref, wg2_ref), (od_ref, wg3_ref))
        for n, (b_ref, wg_ref) in enumerate(branches):
            gate = jax.nn.sigmoid(_dot(h, wg_ref[...]) + bg_ref[n:n + 1, :])
            term = gate * _dot(b_ref[...], wb_ref[n])
            acc = term if acc is None else acc + term
        m_scr[j] = acc.astype(BF16)

    @pl.when(j >= nc1)
    def _():
        acc = _dot(m_scr[0], wo_ref[0:tc, :])
        for c in range(1, nc1):
            acc = acc + _dot(m_scr[c], wo_ref[c * tc:(c + 1) * tc, :])
        o_ref[...] = xres_ref[...] + acc


def _merge(x, h, branches, w_gate, b_gate, w_branch, w_out):
    m = x.shape[0]
    tm, tc, tn = TM_MERGE, TC_MERGE, TN_MERGE
    nc1 = D // tc
    nc2 = D // tn
    first = lambda j: jnp.minimum(j, nc1 - 1)
    col = lambda i, j: (i, jnp.maximum(j - nc1, 0))
    row_tile = lambda i, j: (i, 0)
    gate_specs = [pl.BlockSpec((D, tc), functools.partial(lambda i, j, n: (0, n * nc1 + first(j)), n=n))
                  for n in range(4)]
    return pl.pallas_call(
        functools.partial(_merge_kernel, nc1=nc1, tc=tc),
        grid=(m // tm, nc1 + nc2),
        in_specs=[pl.BlockSpec((tm, D), row_tile)]
        + [pl.BlockSpec((tm, BRANCH_W), row_tile)] * 4
        + gate_specs
        + [
            pl.BlockSpec((4, tc), lambda i, j: (0, first(j))),
            pl.BlockSpec((4, BRANCH_W, tc), lambda i, j: (0, 0, first(j))),
            pl.BlockSpec((D, tn), lambda i, j: (0, jnp.maximum(j - nc1, 0))),
            pl.BlockSpec((tm, tn), col),
        ],
        out_specs=pl.BlockSpec((tm, tn), col),
        out_shape=jax.ShapeDtypeStruct((m, D), F32),
        scratch_shapes=[pltpu.VMEM((nc1, tm, tc), BF16)],
        compiler_params=_params(("parallel", "arbitrary")),
        name="merge",
    )(h, *branches, w_gate, w_gate, w_gate, w_gate, b_gate, w_branch, w_out, x)


def _final_norm_kernel(x_ref, g_ref, first_ref, second_ref, *, n_first):
    y = _rms(x_ref[...], g_ref[...], RMS_EPS)

    @pl.when(pl.program_id(0) < n_first)
    def _():
        first_ref[...] = y

    @pl.when(pl.program_id(0) >= n_first)
    def _():
        second_ref[...] = y


def _final_norm(x, g, m_first):
    m = x.shape[0]
    tm = 512
    n_first = m_first // tm
    return pl.pallas_call(
        functools.partial(_final_norm_kernel, n_first=n_first),
        grid=(m // tm,),
        in_specs=[pl.BlockSpec((tm, D), lambda i: (i, 0)), pl.BlockSpec((1, D), lambda i: (0, 0))],
        out_specs=[pl.BlockSpec((tm, D), lambda i: (jnp.minimum(i, n_first - 1), 0)),
                   pl.BlockSpec((tm, D), lambda i: (jnp.maximum(i - n_first, 0), 0))],
        out_shape=[jax.ShapeDtypeStruct((m_first, D), F32), jax.ShapeDtypeStruct((m - m_first, D), F32)],
        compiler_params=_params(("arbitrary",)),
        name="final_norm",
    )(x, g)


def _cast_ffn_in_kernel(w_ref, o_ref):
    o_ref[0, :, :D_FF] = w_ref[...].astype(BF16)
    o_ref[0, :, D_FF:] = jnp.zeros((o_ref.shape[1], FFN_FP - D_FF), BF16)


def _cast_ffn_in(w_in, l):
    tr = 256
    return pl.pallas_call(
        _cast_ffn_in_kernel,
        grid=(2, D // tr),
        in_specs=[pl.BlockSpec((None, tr, D_FF), lambda part, i: (l, i, part))],
        out_specs=pl.BlockSpec((1, tr, FFN_FP), lambda part, i: (part, i, 0)),
        out_shape=jax.ShapeDtypeStruct((2, D, FFN_FP), BF16),
        compiler_params=_params(("parallel", "parallel")),
        name="cast_ffn_in",
    )(w_in)


def _cast_ffn_out_kernel(w_ref, o_ref, *, tr):
    row = pl.program_id(0) * tr + lax.broadcasted_iota(jnp.int32, (tr, D), 0)
    o_ref[...] = jnp.where(row < D_FF, w_ref[...], 0.0).astype(BF16)


def _cast_ffn_out(w_out, l):
    tr = 512
    return pl.pallas_call(
        functools.partial(_cast_ffn_out_kernel, tr=tr),
        grid=(FFN_FP // tr,),
        in_specs=[pl.BlockSpec((None, tr, D), lambda i: (l, i, 0))],
        out_specs=pl.BlockSpec((tr, D), lambda i: (i, 0)),
        out_shape=jax.ShapeDtypeStruct((FFN_FP, D), BF16),
        compiler_params=_params(("parallel",)),
        name="cast_ffn_out",
    )(w_out)


def _prep_ffn(w_in, w_out, l):
    return _cast_ffn_in(w_in, l), _cast_ffn_out(w_out, l)


def _swa_pair_heads(w, axis):
    shape = w.shape
    split = shape[:axis] + (SWA_KVH, SWA_GROUP, SWA_DH) + shape[axis + 1:]
    return jnp.swapaxes(w.reshape(split), axis, axis + 1).reshape(shape)


def _proj_columns(w_in):
    scale = np.ones((IN_WIDTH,), np.float32)
    scale[COL_NA:COL_NA + NA_HEADS * NA_DH] = NA_DH ** -0.5
    scale[COL_DF:COL_DF + DIFF_HEADS * 2 * DIFF_DK] = DIFF_DK ** -0.5 * LOG2E
    scale[COL_SW:COL_SW + SWA_QH * SWA_DH] = SWA_DH ** -0.5
    w = w_in * scale
    gw = DIL_HEADS * DIL_DH

    def dil_group(gi):
        return [w[:, COL_DL + part * DIL_W + gi * gw:COL_DL + part * DIL_W + (gi + 1) * gw] for part in range(3)]

    sw_q_end = COL_SW + SWA_QH * SWA_DH
    cols = [w[:, :COL_DL]] + dil_group(0) + [_swa_pair_heads(w[:, COL_SW:sw_q_end], 1), w[:, sw_q_end:]] \
        + dil_group(1) + dil_group(2)
    return jnp.concatenate(cols, axis=1)


def _prep_w_in(w_in):
    return _proj_columns(w_in).astype(BF16)


def _encoder_layer(x, l, seqs, p):
    ffn1 = (p["norm_ffn1"][l][None], *_prep_ffn(p["w_ffn1_in"], p["w_ffn1_out"], l))
    if isinstance(x, tuple):
        m_total = sum(xb.shape[0] for xb in x)
        out, row_off = None, 0
        for xb in x:
            out = _ffn(xb, *ffn1, m_total=m_total, row_off=row_off, prev=None if out is None else (out,))
            row_off += xb.shape[0]
        x = out
    else:
        x = _ffn(x, *ffn1)
    h, qkv, qkv_d4, qkv_d16 = _proj(x, p["norm_mix"][l][None], _prep_w_in(p["w_in"][l]))
    dil_views = (qkv, qkv_d4, qkv_d16)

    lam_init = 0.8 - 0.6 * math.exp(-0.3 * l)
    lv = p["diff_lambda"][l].astype(F32)
    lam = jnp.exp(jnp.sum(lv[0] * lv[1])) - jnp.exp(jnp.sum(lv[2] * lv[3])) + lam_init
    diff_lam = lam.reshape(1)
    na_bias = _na_bias_table(p["na_rpb"][l])
    subln = p["diff_subln"][l].astype(F32)[None]
    sink = p["swa_sink"][l].astype(F32)

    o_a = o_b = o_d = None
    dil = [None] * DIL_GROUPS
    for n_seq, t, row_off in seqs:
        o_a = _na(qkv, na_bias, n_seq, t, row_off, None if o_a is None else (o_a,))
        o_b = _diff(qkv, diff_lam, subln, n_seq, t, row_off, lam_init, None if o_b is None else (o_b,))
        for gi in range(DIL_GROUPS):
            dil[gi] = _dil_group(dil_views[gi], gi, n_seq, t, row_off, dil[gi])
        o_d = _swa(qkv, sink, n_seq, t, row_off, None if o_d is None else (o_d,))
    o_c = _dil_combine([o for o, _ in dil], [lse for _, lse in dil])
    branches = [o_a, o_b, o_c, o_d]

    w_branch = p["w_branch"][l]
    w_branch = jnp.concatenate([w_branch[:3], _swa_pair_heads(w_branch[3:], 1)], axis=0).astype(BF16)
    x = _merge(x, h, branches, p["w_gate"][l].astype(BF16), p["b_gate"][l].astype(F32).reshape(4, D),
               w_branch, p["w_out"][l].astype(BF16))
    return _ffn(x, p["norm_ffn2"][l][None], *_prep_ffn(p["w_ffn2_in"], p["w_ffn2_out"], l))


def kernel(x_prompt, x_sample, norm_ffn1, w_ffn1_in, w_ffn1_out, norm_mix, w_in, na_rpb, diff_lambda, diff_subln,
           swa_sink, w_branch, w_gate, b_gate, w_out, norm_ffn2, w_ffn2_in, w_ffn2_out, norm_final):
    p = dict(norm_ffn1=norm_ffn1, w_ffn1_in=w_ffn1_in, w_ffn1_out=w_ffn1_out, norm_mix=norm_mix, w_in=w_in,
             na_rpb=na_rpb, diff_lambda=diff_lambda, diff_subln=diff_subln, swa_sink=swa_sink,
             w_branch=w_branch, w_gate=w_gate, b_gate=b_gate, w_out=w_out, norm_ffn2=norm_ffn2,
             w_ffn2_in=w_ffn2_in, w_ffn2_out=w_ffn2_out)
    bp, tp, _ = x_prompt.shape
    bs, ts, _ = x_sample.shape
    mp = bp * tp
    x = (x_prompt.reshape(mp, D).astype(F32), x_sample.reshape(bs * ts, D).astype(F32))
    seqs = ((bp, tp, 0), (bs, ts, mp))
    for l in range(DEPTH):
        x = _encoder_layer(x, l, seqs, p)
    y_prompt, y_sample = _final_norm(x, norm_final.astype(F32)[None], mp)
    return y_prompt.reshape(bp, tp, D), y_sample.reshape(bs, ts, D)
```

```python
import functools
import math

import jax
import jax.numpy as jnp
import numpy as np
from jax import lax
from jax.experimental import pallas as pl
from jax.experimental.pallas import tpu as pltpu

F32 = jnp.float32
BF16 = jnp.bfloat16

D = 2048
DEPTH = 2
GRID_W = 64
NA_HEADS, NA_DH, NA_ROWS, NA_COLS = 8, 64, 8, 16
DIFF_HEADS, DIFF_DK, DIFF_DV = 4, 64, 128
DIL_WINDOWS, DIL_DILATIONS = (128, 512, 2048), (1, 4, 16)
DIL_GROUPS, DIL_HEADS, DIL_DH = 3, 4, 128
SWA_QH, SWA_KVH, SWA_DH, SWA_RADIUS = 8, 2, 64, 128
SWA_GROUP = SWA_QH // SWA_KVH
BRANCH_W = 512
D_FF = ((8 * D // 3 + 127) // 128) * 128
N_ALIBI = SWA_QH + DIL_GROUPS * DIL_HEADS + DIFF_HEADS
RMS_EPS = 1e-6
NEG_INF = -1e30
LOG2E = math.log2(math.e)

LANES = 128
IN_WIDTH = 3 * NA_HEADS * NA_DH + 2 * DIFF_HEADS * 2 * DIFF_DK + DIFF_HEADS * DIFF_DV \
    + 3 * DIL_GROUPS * DIL_HEADS * DIL_DH + SWA_QH * SWA_DH + 2 * SWA_KVH * SWA_DH
N_BLK = IN_WIDTH // LANES
BLK_NA_Q, BLK_NA_K, BLK_NA_V = 0, 4, 8
BLK_DF_Q, BLK_DF_K, BLK_DF_V = 12, 16, 20
BLK_DL_Q, BLK_DL_K, BLK_DL_V = 24, 28, 32
BLK_SW_Q, BLK_SW_K, BLK_SW_V = 36, 40, 41
N_BLK_MAIN = 42
N_BLK_DIL = 3 * DIL_HEADS
COL_NA, COL_DF, COL_DL, COL_SW = 0, 1536, 3072, 7680
DIL_W = DIL_GROUPS * DIL_HEADS * DIL_DH

FFN_TF = 512
FFN_FP = -(-D_FF // FFN_TF) * FFN_TF
FFN_NC1 = FFN_FP // FFN_TF
TN_OUT = 256
TM_FFN = 1024
TM_PROJ = 1024
NB_PROJ = 6
TN_PROJ = NB_PROJ * LANES
NJ_MAIN = N_BLK_MAIN // NB_PROJ
NJ_DIL = N_BLK_DIL // NB_PROJ
TM_MERGE = 1024
TC_MERGE = 256
TN_MERGE = 512
TM_COMBINE = 1024
RQ_NA = 4
NA_UROWS = 12
TQ_DIFF = 256
TQ_SWA = 256
TQ_DIL = 128
UNITS_DIL = 4
VMEM_LIMIT = 60 * 1024 * 1024

_ALIBI = [2.0 ** (-8.0 * (i + 1) / N_ALIBI) for i in range(N_ALIBI)]
SLOPES_SWA = _ALIBI[:SWA_QH]
SLOPES_DIL = _ALIBI[SWA_QH:SWA_QH + DIL_GROUPS * DIL_HEADS]
SLOPES_DIFF = _ALIBI[SWA_QH + DIL_GROUPS * DIL_HEADS:]


def _params(sem):
    return pltpu.CompilerParams(dimension_semantics=sem, vmem_limit_bytes=VMEM_LIMIT)


def _call_into(body, prev, n_in, **kw):
    if prev is None:
        return pl.pallas_call(body, **kw)
    prev = tuple(prev)

    def aliased(*refs):
        body(*refs[:n_in], *refs[n_in + len(prev):])

    kw["in_specs"] = list(kw["in_specs"]) + [pl.BlockSpec(memory_space=pl.ANY)] * len(prev)
    call = pl.pallas_call(aliased, input_output_aliases={n_in + i: i for i in range(len(prev))}, **kw)
    return lambda *args: call(*args, *prev)


def _rms(x, g, eps):
    ms = jnp.mean(x * x, axis=-1, keepdims=True)
    return x * lax.rsqrt(ms + eps) * g


def _dot(a, b):
    return jnp.dot(a, b, preferred_element_type=F32)


def _dot_nt(a, b):
    return lax.dot_general(a, b, (((1,), (1,)), ((), ())), preferred_element_type=F32)


def _half_masks(rows):
    lane = lax.broadcasted_iota(jnp.int32, (rows, LANES), 1)
    return lane < (LANES // 2)


def _keep_half(x, lo, half):
    keep = lo if half == 0 else jnp.logical_not(lo)
    return jnp.where(keep, x.astype(F32), 0.0).astype(BF16)


def _ffn_kernel(x_ref, g_ref, wg_ref, wu_ref, w2_ref, xres_ref, o_ref, h_scr, act_scr):
    j = pl.program_id(1)

    @pl.when(j == 0)
    def _():
        h_scr[...] = _rms(x_ref[...], g_ref[...], RMS_EPS).astype(BF16)

    @pl.when(j < FFN_NC1)
    def _():
        h = h_scr[...]
        gate = _dot(h, wg_ref[...])
        up = _dot(h, wu_ref[...])
        act_scr[j] = (gate * jax.nn.sigmoid(gate) * up).astype(BF16)

    @pl.when(j >= FFN_NC1)
    def _():
        acc = _dot(act_scr[0], w2_ref[0:FFN_TF, :])
        for c in range(1, FFN_NC1):
            acc = acc + _dot(act_scr[c], w2_ref[c * FFN_TF:(c + 1) * FFN_TF, :])
        o_ref[...] = xres_ref[...] + 0.5 * acc


def _ffn(x, g, w1, w2, m_total=None, row_off=0, prev=None):
    m = x.shape[0]
    m_total = m if m_total is None else m_total
    tm = TM_FFN
    nc2 = D // TN_OUT
    tile0 = row_off // tm
    col = lambda i, j: (i, jnp.maximum(j - FFN_NC1, 0))
    return _call_into(
        _ffn_kernel, prev, 6,
        grid=(m // tm, FFN_NC1 + nc2),
        in_specs=[
            pl.BlockSpec((tm, D), lambda i, j: (i, 0)),
            pl.BlockSpec((1, D), lambda i, j: (0, 0)),
            pl.BlockSpec((None, D, FFN_TF), lambda i, j: (0, 0, jnp.minimum(j, FFN_NC1 - 1))),
            pl.BlockSpec((None, D, FFN_TF), lambda i, j: (1, 0, jnp.minimum(j, FFN_NC1 - 1))),
            pl.BlockSpec((FFN_FP, TN_OUT), lambda i, j: (0, jnp.maximum(j - FFN_NC1, 0))),
            pl.BlockSpec((tm, TN_OUT), col),
        ],
        out_specs=pl.BlockSpec((tm, TN_OUT), lambda i, j: (tile0 + i, jnp.maximum(j - FFN_NC1, 0))),
        out_shape=jax.ShapeDtypeStruct((m_total, D), F32),
        scratch_shapes=[pltpu.VMEM((tm, D), BF16), pltpu.VMEM((FFN_NC1, tm, FFN_TF), BF16)],
        compiler_params=_params(("parallel", "arbitrary")),
        name="ffn",
    )(x, g, w1, w1, w2, x)


def _proj_kernel(x_ref, g_ref, w_ref, h_ref, main_ref, d4_ref, d16_ref, res_scr, *, tm):
    j = pl.program_id(1)

    @pl.when(j == 0)
    def _():
        h_ref[...] = _rms(x_ref[...], g_ref[...], RMS_EPS).astype(BF16)

    res = _dot(h_ref[...], w_ref[...])

    @pl.when(j < NJ_MAIN)
    def _():
        for k in range(NB_PROJ):
            main_ref[k] = res[:, k * LANES:(k + 1) * LANES].astype(BF16)

    @pl.when(j >= NJ_MAIN)
    def _():
        for k in range(NB_PROJ):
            res_scr[k] = res[:, k * LANES:(k + 1) * LANES]

    for out_ref, dil, first in ((d4_ref, DIL_DILATIONS[1], NJ_MAIN), (d16_ref, DIL_DILATIONS[2], NJ_MAIN + NJ_DIL)):
        @pl.when((j >= first) & (j < first + NJ_DIL))
        def _(out_ref=out_ref, dil=dil):
            for k in range(NB_PROJ):
                for r in range(dil):
                    rows = res_scr[k, pl.ds(r, tm // dil, stride=dil), :]
                    out_ref[k, :, r * LANES:(r + 1) * LANES] = rows.astype(BF16)


def _proj(x, g, w):
    m = x.shape[0]
    tm = TM_PROJ
    d4, d16 = DIL_DILATIONS[1], DIL_DILATIONS[2]
    return pl.pallas_call(
        functools.partial(_proj_kernel, tm=tm),
        grid=(m // tm, NJ_MAIN + 2 * NJ_DIL),
        in_specs=[
            pl.BlockSpec((tm, D), lambda i, j: (i, 0)),
            pl.BlockSpec((1, D), lambda i, j: (0, 0)),
            pl.BlockSpec((D, TN_PROJ), lambda i, j: (0, j)),
        ],
        out_specs=[
            pl.BlockSpec((tm, D), lambda i, j: (i, 0)),
            pl.BlockSpec((NB_PROJ, tm, LANES), lambda i, j: (jnp.minimum(j, NJ_MAIN - 1), i, 0)),
            pl.BlockSpec((NB_PROJ, tm // d4, d4 * LANES),
                         lambda i, j: (jnp.clip(j - NJ_MAIN, 0, NJ_DIL - 1), i, 0)),
            pl.BlockSpec((NB_PROJ, tm // d16, d16 * LANES),
                         lambda i, j: (jnp.clip(j - NJ_MAIN - NJ_DIL, 0, NJ_DIL - 1), i, 0)),
        ],
        out_shape=[jax.ShapeDtypeStruct((m, D), BF16),
                   jax.ShapeDtypeStruct((N_BLK_MAIN, m, LANES), BF16),
                   jax.ShapeDtypeStruct((N_BLK_DIL, m // d4, d4 * LANES), BF16),
                   jax.ShapeDtypeStruct((N_BLK_DIL, m // d16, d16 * LANES), BF16)],
        scratch_shapes=[pltpu.VMEM((NB_PROJ, tm, LANES), F32)],
        compiler_params=_params(("parallel", "arbitrary")),
        name="proj",
    )(x, g, w)


def _na_kernel(q_ref, k_ref, v_ref, b_ref, o_ref, *, rows):
    nq = RQ_NA * GRID_W
    nk = NA_UROWS * GRID_W
    row0 = jnp.clip(RQ_NA * pl.program_id(1) - NA_ROWS // 2, 0, rows - NA_UROWS)
    start = pl.multiple_of(row0 * GRID_W, GRID_W)
    lo = _half_masks(nq)
    for hp in range(NA_HEADS // 2):
        q = q_ref[hp]
        k = k_ref[hp, pl.ds(start, nk), :]
        v = v_ref[hp, pl.ds(start, nk), :]
        lhs = jnp.concatenate([_keep_half(q, lo, 0), _keep_half(q, lo, 1)], axis=0)
        s = _dot_nt(lhs, k) + b_ref[0, hp]
        m = jnp.max(s, axis=-1, keepdims=True)
        p = jnp.exp(s - m)
        l = jnp.sum(p, axis=-1, keepdims=True)
        o = _dot(p.astype(BF16), v) / l
        o_ref[:, hp * LANES:(hp + 1) * LANES] = jnp.where(lo, o[:nq], o[nq:]).astype(BF16)


def _na(qkv, bias, n_seq, t, row_off, prev):
    rows = t // GRID_W
    nb = rows // RQ_NA
    nq = RQ_NA * GRID_W
    blk0 = row_off // nq
    seq0 = row_off // t
    group_type = lambda g: jnp.where(g == 0, 0, jnp.where(g == nb - 1, 2, 1))
    return _call_into(
        functools.partial(_na_kernel, rows=rows), prev, 4,
        grid=(n_seq, nb),
        in_specs=[
            pl.BlockSpec((4, nq, LANES), lambda s, g: (BLK_NA_Q // 4, blk0 + s * nb + g, 0)),
            pl.BlockSpec((4, t, LANES), lambda s, g: (BLK_NA_K // 4, seq0 + s, 0)),
            pl.BlockSpec((4, t, LANES), lambda s, g: (BLK_NA_V // 4, seq0 + s, 0)),
            pl.BlockSpec((1, NA_HEADS // 2, 2 * nq, NA_UROWS * GRID_W), lambda s, g: (group_type(g), 0, 0, 0)),
        ],
        out_specs=pl.BlockSpec((nq, BRANCH_W), lambda s, g: (blk0 + s * nb + g, 0)),
        out_shape=jax.ShapeDtypeStruct((qkv.shape[1], BRANCH_W), BF16),
        compiler_params=_params(("parallel", "arbitrary")),
        name="mixer_a",
    )(qkv, qkv, qkv, bias)


def _na_bias_table(rpb):
    c = np.arange(GRID_W)[:, None]
    kc = np.arange(GRID_W)[None, :]
    cstart = np.clip(c - NA_COLS // 2, 0, GRID_W - NA_COLS)
    ok = (kc >= cstart) & (kc < cstart + NA_COLS)
    pad = GRID_W - NA_COLS
    padded = jnp.pad(rpb.astype(F32), ((0, 0), (0, 0), (pad, pad)))
    e = jnp.stack([padded[..., GRID_W - 1 - q:2 * GRID_W - 1 - q] for q in range(GRID_W)], axis=2)
    e = jnp.where(ok[None, None], e, NEG_INF)
    neg = jnp.full((NA_HEADS, GRID_W, GRID_W), NEG_INF, F32)
    half = NA_ROWS // 2
    group_types = (
        [(-i, -i) for i in range(RQ_NA)],
        [(-half - i, -half) for i in range(RQ_NA)],
        [(-(NA_UROWS - RQ_NA) - i, -(NA_ROWS - RQ_NA) - i) for i in range(RQ_NA)],
    )
    tables = []
    for rel in group_types:
        q_rows = []
        for u0, w0 in rel:
            blocks = [e[:, u0 + j + NA_ROWS - 1] if w0 <= u0 + j < w0 + NA_ROWS else neg for j in range(NA_UROWS)]
            q_rows.append(jnp.concatenate(blocks, axis=2))
        tables.append(jnp.concatenate(q_rows, axis=1))
    return jnp.stack(tables).reshape(3, NA_HEADS // 2, 2 * RQ_NA * GRID_W, NA_UROWS * GRID_W)


def _diff_kernel(lam_ref, q_ref, k_ref, v_ref, tab_ref, g_ref, o_ref, vext_scr, *, t, tq, out_scale):
    qi = pl.program_id(2)

    @pl.when(qi == 0)
    def _():
        vext_scr[:, :LANES] = v_ref[0]
        vext_scr[:, LANES:] = jnp.ones((t, LANES), BF16)

    lam = lam_ref[0]
    off = pl.multiple_of((t // tq - 1 - qi) * tq, tq)
    bias = tab_ref[0, :, pl.ds(off, t)]
    q = q_ref[0]
    lo = _half_masks(tq)
    k = k_ref[0]
    v_ext = vext_scr[...]

    scores = [_dot_nt(_keep_half(q, lo, half), k) - bias for half in range(2)]
    probs = [jnp.exp2((s - jnp.max(s, axis=-1, keepdims=True)).astype(BF16)) for s in scores]
    outs = [_dot(p, v_ext) for p in probs]
    o1, o2 = (ol[:, :LANES] / ol[:, LANES:] for ol in outs)
    o = o1 - lam * o2
    o_ref[...] = (_rms(o, g_ref[...], 1e-5) * out_scale).astype(BF16)


def _diff_bias_table(t, tq):
    r = lax.broadcasted_iota(jnp.int32, (tq, 2 * t - tq), 0)
    x = lax.broadcasted_iota(jnp.int32, (tq, 2 * t - tq), 1)
    dist = jnp.abs(r - x + (t - tq)).astype(F32)
    return jnp.asarray([s * LOG2E for s in SLOPES_DIFF], F32)[:, None, None] * dist[None]


def _diff(qkv, lam, subln_g, n_seq, t, row_off, lam_init, prev):
    tq = TQ_DIFF
    nq = t // tq
    blk0 = row_off // tq
    seq0 = row_off // t
    return _call_into(
        functools.partial(_diff_kernel, t=t, tq=tq, out_scale=1.0 - lam_init), prev, 6,
        grid=(DIFF_HEADS, n_seq, nq),
        in_specs=[
            pl.BlockSpec(memory_space=pltpu.SMEM),
            pl.BlockSpec((1, tq, LANES), lambda h, s, i: (BLK_DF_Q + h, blk0 + s * nq + i, 0)),
            pl.BlockSpec((1, t, LANES), lambda h, s, i: (BLK_DF_K + h, seq0 + s, 0)),
            pl.BlockSpec((1, t, LANES), lambda h, s, i: (BLK_DF_V + h, seq0 + s, 0)),
            pl.BlockSpec((1, tq, 2 * t - tq), lambda h, s, i: (h, 0, 0)),
            pl.BlockSpec((1, DIFF_DV), lambda h, s, i: (0, 0)),
        ],
        out_specs=pl.BlockSpec((tq, LANES), lambda h, s, i: (blk0 + s * nq + i, h)),
        out_shape=jax.ShapeDtypeStruct((qkv.shape[1], BRANCH_W), BF16),
        scratch_shapes=[pltpu.VMEM((t, 2 * LANES), BF16)],
        compiler_params=_params(("parallel", "arbitrary", "arbitrary")),
        name="mixer_b",
    )(lam, qkv, qkv, qkv, _diff_bias_table(t, tq), subln_g)


def _band_window(qi, tq, kw, radius, length):
    start = jnp.clip(qi * tq - radius, 0, length - kw)
    rel = (lax.broadcasted_iota(jnp.int32, (tq, kw), 0) + (qi * tq - start)) \
        - lax.broadcasted_iota(jnp.int32, (tq, kw), 1)
    dist = jnp.abs(rel)
    return pl.multiple_of(start, 64), dist.astype(F32), dist <= radius


def _dil_kernel(q_ref, k_ref, v_ref, o_ref, lse_ref, *, n_sub, tq, kw, radius, slopes, scale, nqb, nrb):
    for b in range(nqb):
        start, dist, valid = _band_window(pl.program_id(2) * nqb + b, tq, kw, radius, n_sub)
        rows = slice(b * tq, (b + 1) * tq)
        for rr in range(nrb):
            lanes = slice(rr * LANES, (rr + 1) * LANES)
            for h in range(DIL_HEADS):
                k = k_ref[h, pl.ds(start, kw), lanes]
                v = v_ref[h, pl.ds(start, kw), lanes]
                s = _dot_nt(q_ref[h, rows, lanes], k) * scale - slopes[h] * dist
                s = jnp.where(valid, s, NEG_INF)
                m = jnp.max(s, axis=-1, keepdims=True)
                p = jnp.exp(s - m)
                l = jnp.sum(p, axis=-1, keepdims=True)
                o_ref[h, rows, lanes] = (_dot(p.astype(BF16), v) / l).astype(BF16)
                lse_ref[h, rows, lanes] = jnp.broadcast_to(m + jnp.log(l), (tq, LANES))


def _dil_group(view, gi, n_seq, t, row_off, prev):
    dil = DIL_DILATIONS[gi]
    radius = DIL_WINDOWS[gi] // (2 * dil)
    n_sub = t // dil
    tq = min(TQ_DIL, n_sub)
    kw = min(tq + 2 * radius, n_sub)
    nqb = min(n_sub // tq, UNITS_DIL)
    nrb = min(dil, UNITS_DIL // nqb)
    nq = n_sub // (tq * nqb)
    blk0 = row_off // dil // (tq * nqb)
    seq0 = row_off // t
    qb, kb, vb = (BLK_DL_Q // 4, BLK_DL_K // 4, BLK_DL_V // 4) if gi == 0 else (0, 1, 2)
    slopes = tuple(SLOPES_DIL[gi * DIL_HEADS + h] * dil for h in range(DIL_HEADS))
    rows_out = view.shape[1]
    q_idx = lambda s, r, i: (qb, blk0 + s * nq + i, r)
    o_idx = lambda s, r, i: (0, blk0 + s * nq + i, r)
    o, lse = _call_into(
        functools.partial(_dil_kernel, n_sub=n_sub, tq=tq, kw=kw, radius=radius, slopes=slopes,
                          scale=DIL_DH ** -0.5, nqb=nqb, nrb=nrb), prev, 3,
        grid=(n_seq, dil // nrb, nq),
        in_specs=[
            pl.BlockSpec((4, nqb * tq, nrb * LANES), q_idx),
            pl.BlockSpec((4, n_sub, nrb * LANES), lambda s, r, i: (kb, seq0 + s, r)),
            pl.BlockSpec((4, n_sub, nrb * LANES), lambda s, r, i: (vb, seq0 + s, r)),
        ],
        out_specs=[pl.BlockSpec((4, nqb * tq, nrb * LANES), o_idx),
                   pl.BlockSpec((4, nqb * tq, nrb * LANES), o_idx)],
        out_shape=[jax.ShapeDtypeStruct((DIL_HEADS, rows_out, dil * LANES), BF16),
                   jax.ShapeDtypeStruct((DIL_HEADS, rows_out, dil * LANES), F32)],
        compiler_params=_params(("parallel", "arbitrary", "arbitrary")),
        name=f"mixer_c{gi}",
    )(view, view, view)
    return o, lse


def _dil_combine_kernel(o0, o1, o2, l0, l1, l2, out_ref, o_scr, l_scr, *, tm):
    for h in range(DIL_HEADS):
        for gi, (o_ref, l_ref) in ((1, (o1, l1)), (2, (o2, l2))):
            dil = DIL_DILATIONS[gi]
            for r in range(dil):
                rows = pl.ds(r, tm // dil, stride=dil)
                o_scr[gi - 1, rows, :] = o_ref[h, :, r * LANES:(r + 1) * LANES].astype(F32)
                l_scr[gi - 1, rows, :] = l_ref[h, :, r * LANES:(r + 1) * LANES]
        a0, a1, a2 = l0[h], l_scr[0], l_scr[1]
        mx = jnp.maximum(jnp.maximum(a0, a1), a2)
        e0, e1, e2 = jnp.exp(a0 - mx), jnp.exp(a1 - mx), jnp.exp(a2 - mx)
        num = e0 * o0[h].astype(F32) + e1 * o_scr[0] + e2 * o_scr[1]
        out_ref[:, h * LANES:(h + 1) * LANES] = (num / (e0 + e1 + e2)).astype(BF16)


def _dil_combine(outs, lses):
    m = outs[0].shape[1]
    tm = min(TM_COMBINE, m)
    specs = [pl.BlockSpec((DIL_HEADS, tm // dil, dil * LANES), lambda i: (0, i, 0)) for dil in DIL_DILATIONS]
    return pl.pallas_call(
        functools.partial(_dil_combine_kernel, tm=tm),
        grid=(m // tm,),
        in_specs=specs * 2,
        out_specs=pl.BlockSpec((tm, BRANCH_W), lambda i: (i, 0)),
        out_shape=jax.ShapeDtypeStruct((m, BRANCH_W), BF16),
        scratch_shapes=[pltpu.VMEM((2, tm, LANES), F32), pltpu.VMEM((2, tm, LANES), F32)],
        compiler_params=_params(("parallel",)),
        name="mixer_c_combine",
    )(*outs, *lses)


def _swa_kernel(sink_ref, q_ref, k_ref, v_ref, o_ref, *, t, tq, kw):
    qi = pl.program_id(1)
    start, dist, valid = _band_window(qi, tq, kw, SWA_RADIUS, t)
    k = k_ref[0, pl.ds(start, kw), :]
    v = v_ref[0, pl.ds(start, kw), :]
    lo = _half_masks(tq)
    for g in range(SWA_GROUP):
        q = q_ref[g]
        outs = []
        for hk in range(SWA_KVH):
            head = hk * SWA_GROUP + g
            sink = sink_ref[head]
            s = _dot_nt(_keep_half(q, lo, hk), k) - SLOPES_SWA[head] * dist
            s = jnp.where(valid, s, NEG_INF)
            m = jnp.maximum(jnp.max(s, axis=-1, keepdims=True), sink)
            p = jnp.exp(s - m)
            l = jnp.sum(p, axis=-1, keepdims=True) + jnp.exp(sink - m)
            outs.append(_dot(p.astype(BF16), v) / l)
        o_ref[:, g * LANES:(g + 1) * LANES] = jnp.where(lo, outs[0], outs[1]).astype(BF16)


def _swa(qkv, sink, n_seq, t, row_off, prev):
    tq = TQ_SWA
    kw = min(tq + 2 * SWA_RADIUS, t)
    nq = t // tq
    blk0 = row_off // tq
    seq0 = row_off // t
    return _call_into(
        functools.partial(_swa_kernel, t=t, tq=tq, kw=kw), prev, 4,
        grid=(n_seq, nq),
        in_specs=[
            pl.BlockSpec(memory_space=pltpu.SMEM),
            pl.BlockSpec((4, tq, LANES), lambda s, i: (BLK_SW_Q // 4, blk0 + s * nq + i, 0)),
            pl.BlockSpec((1, t, LANES), lambda s, i: (BLK_SW_K, seq0 + s, 0)),
            pl.BlockSpec((1, t, LANES), lambda s, i: (BLK_SW_V, seq0 + s, 0)),
        ],
        out_specs=pl.BlockSpec((tq, BRANCH_W), lambda s, i: (blk0 + s * nq + i, 0)),
        out_shape=jax.ShapeDtypeStruct((qkv.shape[1], BRANCH_W), BF16),
        compiler_params=_params(("parallel", "arbitrary")),
        name="mixer_d",
    )(sink, qkv, qkv, qkv)


def _merge_kernel(h_ref, oa_ref, ob_ref, oc_ref, od_ref, wg0_ref, wg1_ref, wg2_ref, wg3_ref,
                  bg_ref, wb_ref, wo_ref, xres_ref, o_ref, m_scr, *, nc1, tc):
    j = pl.program_id(1)

    @pl.when(j < nc1)
    def _():
        h = h_ref[...]
        acc = None
        branches = ((oa_ref, wg0_ref), (ob_ref, wg1_ref), (oc_ref, wg2_ref), (od_ref, wg3_ref))
        for n, (b_ref, wg_ref) in enumerate(branches):
            gate = jax.nn.sigmoid(_dot(h, wg_ref[...]) + bg_ref[n:n + 1, :])
            term = gate * _dot(b_ref[...], wb_ref[n])
            acc = term if acc is None else acc + term
        m_scr[j] = acc.astype(BF16)

    @pl.when(j >= nc1)
    def _():
        acc = _dot(m_scr[0], wo_ref[0:tc, :])
        for c in range(1, nc1):
            acc = acc + _dot(m_scr[c], wo_ref[c * tc:(c + 1) * tc, :])
        o_ref[...] = xres_ref[...] + acc


def _merge(x, h, branches, w_gate, b_gate, w_branch, w_out):
    m = x.shape[0]
    tm, tc, tn = TM_MERGE, TC_MERGE, TN_MERGE
    nc1 = D // tc
    nc2 = D // tn
    first = lambda j: jnp.minimum(j, nc1 - 1)
    col = lambda i, j: (i, jnp.maximum(j - nc1, 0))
    row_tile = lambda i, j: (i, 0)
    gate_specs = [pl.BlockSpec((D, tc), functools.partial(lambda i, j, n: (0, n * nc1 + first(j)), n=n))
                  for n in range(4)]
    return pl.pallas_call(
        functools.partial(_merge_kernel, nc1=nc1, tc=tc),
        grid=(m // tm, nc1 + nc2),
        in_specs=[pl.BlockSpec((tm, D), row_tile)]
        + [pl.BlockSpec((tm, BRANCH_W), row_tile)] * 4
        + gate_specs
        + [
            pl.BlockSpec((4, tc), lambda i, j: (0, first(j))),
            pl.BlockSpec((4, BRANCH_W, tc), lambda i, j: (0, 0, first(j))),
            pl.BlockSpec((D, tn), lambda i, j: (0, jnp.maximum(j - nc1, 0))),
            pl.BlockSpec((tm, tn), col),
        ],
        out_specs=pl.BlockSpec((tm, tn), col),
        out_shape=jax.ShapeDtypeStruct((m, D), F32),
        scratch_shapes=[pltpu.VMEM((nc1, tm, tc), BF16)],
        compiler_params=_params(("parallel", "arbitrary")),
        name="merge",
    )(h, *branches, w_gate, w_gate, w_gate, w_gate, b_gate, w_branch, w_out, x)


def _final_norm_kernel(x_ref, g_ref, first_ref, second_ref, *, n_first):
    y = _rms(x_ref[...], g_ref[...], RMS_EPS)

    @pl.when(pl.program_id(0) < n_first)
    def _():
        first_ref[...] = y

    @pl.when(pl.program_id(0) >= n_first)
    def _():
        second_ref[...] = y


def _final_norm(x, g, m_first):
    m = x.shape[0]
    tm = 512
    n_first = m_first // tm
    return pl.pallas_call(
        functools.partial(_final_norm_kernel, n_first=n_first),
        grid=(m // tm,),
        in_specs=[pl.BlockSpec((tm, D), lambda i: (i, 0)), pl.BlockSpec((1, D), lambda i: (0, 0))],
        out_specs=[pl.BlockSpec((tm, D), lambda i: (jnp.minimum(i, n_first - 1), 0)),
                   pl.BlockSpec((tm, D), lambda i: (jnp.maximum(i - n_first, 0), 0))],
        out_shape=[jax.ShapeDtypeStruct((m_first, D), F32), jax.ShapeDtypeStruct((m - m_first, D), F32)],
        compiler_params=_params(("arbitrary",)),
        name="final_norm",
    )(x, g)


def _cast_ffn_in_kernel(w_ref, o_ref):
    o_ref[0, :, :D_FF] = w_ref[...].astype(BF16)
    o_ref[0, :, D_FF:] = jnp.zeros((o_ref.shape[1], FFN_FP - D_FF), BF16)


def _cast_ffn_in(w_in, l):
    tr = 256
    return pl.pallas_call(
        _cast_ffn_in_kernel,
        grid=(2, D // tr),
        in_specs=[pl.BlockSpec((None, tr, D_FF), lambda part, i: (l, i, part))],
        out_specs=pl.BlockSpec((1, tr, FFN_FP), lambda part, i: (part, i, 0)),
        out_shape=jax.ShapeDtypeStruct((2, D, FFN_FP), BF16),
        compiler_params=_params(("parallel", "parallel")),
        name="cast_ffn_in",
    )(w_in)


def _cast_ffn_out_kernel(w_ref, o_ref, *, tr):
    row = pl.program_id(0) * tr + lax.broadcasted_iota(jnp.int32, (tr, D), 0)
    o_ref[...] = jnp.where(row < D_FF, w_ref[...], 0.0).astype(BF16)


def _cast_ffn_out(w_out, l):
    tr = 512
    return pl.pallas_call(
        functools.partial(_cast_ffn_out_kernel, tr=tr),
        grid=(FFN_FP // tr,),
        in_specs=[pl.BlockSpec((None, tr, D), lambda i: (l, i, 0))],
        out_specs=pl.BlockSpec((tr, D), lambda i: (i, 0)),
        out_shape=jax.ShapeDtypeStruct((FFN_FP, D), BF16),
        compiler_params=_params(("parallel",)),
        name="cast_ffn_out",
    )(w_out)


def _prep_ffn(w_in, w_out, l):
    return _cast_ffn_in(w_in, l), _cast_ffn_out(w_out, l)


def _swa_pair_heads(w, axis):
    shape = w.shape
    split = shape[:axis] + (SWA_KVH, SWA_GROUP, SWA_DH) + shape[axis + 1:]
    return jnp.swapaxes(w.reshape(split), axis, axis + 1).reshape(shape)


def _proj_columns(w_in):
    scale = np.ones((IN_WIDTH,), np.float32)
    scale[COL_NA:COL_NA + NA_HEADS * NA_DH] = NA_DH ** -0.5
    scale[COL_DF:COL_DF + DIFF_HEADS * 2 * DIFF_DK] = DIFF_DK ** -0.5 * LOG2E
    scale[COL_SW:COL_SW + SWA_QH * SWA_DH] = SWA_DH ** -0.5
    w = w_in * scale
    gw = DIL_HEADS * DIL_DH

    def dil_group(gi):
        return [w[:, COL_DL + part * DIL_W + gi * gw:COL_DL + part * DIL_W + (gi + 1) * gw] for part in range(3)]

    sw_q_end = COL_SW + SWA_QH * SWA_DH
    cols = [w[:, :COL_DL]] + dil_group(0) + [_swa_pair_heads(w[:, COL_SW:sw_q_end], 1), w[:, sw_q_end:]] \
        + dil_group(1) + dil_group(2)
    return jnp.concatenate(cols, axis=1)


def _prep_w_in(w_in):
    return _proj_columns(w_in).astype(BF16)


def _encoder_layer(x, l, seqs, p):
    ffn1 = (p["norm_ffn1"][l][None], *_prep_ffn(p["w_ffn1_in"], p["w_ffn1_out"], l))
    if isinstance(x, tuple):
        m_total = sum(xb.shape[0] for xb in x)
        out, row_off = None, 0
        for xb in x:
            out = _ffn(xb, *ffn1, m_total=m_total, row_off=row_off, prev=None if out is None else (out,))
            row_off += xb.shape[0]
        x = out
    else:
        x = _ffn(x, *ffn1)
    h, qkv, qkv_d4, qkv_d16 = _proj(x, p["norm_mix"][l][None], _prep_w_in(p["w_in"][l]))
    dil_views = (qkv, qkv_d4, qkv_d16)

    lam_init = 0.8 - 0.6 * math.exp(-0.3 * l)
    lv = p["diff_lambda"][l].astype(F32)
    lam = jnp.exp(jnp.sum(lv[0] * lv[1])) - jnp.exp(jnp.sum(lv[2] * lv[3])) + lam_init
    diff_lam = lam.reshape(1)
    na_bias = _na_bias_table(p["na_rpb"][l])
    subln = p["diff_subln"][l].astype(F32)[None]
    sink = p["swa_sink"][l].astype(F32)

    o_a = o_b = o_d = None
    dil = [None] * DIL_GROUPS
    for n_seq, t, row_off in seqs:
        o_a = _na(qkv, na_bias, n_seq, t, row_off, None if o_a is None else (o_a,))
        o_b = _diff(qkv, diff_lam, subln, n_seq, t, row_off, lam_init, None if o_b is None else (o_b,))
        for gi in range(DIL_GROUPS):
            dil[gi] = _dil_group(dil_views[gi], gi, n_seq, t, row_off, dil[gi])
        o_d = _swa(qkv, sink, n_seq, t, row_off, None if o_d is None else (o_d,))
    o_c = _dil_combine([o for o, _ in dil], [lse for _, lse in dil])
    branches = [o_a, o_b, o_c, o_d]

    w_branch = p["w_branch"][l]
    w_branch = jnp.concatenate([w_branch[:3], _swa_pair_heads(w_branch[3:], 1)], axis=0).astype(BF16)
    x = _merge(x, h, branches, p["w_gate"][l].astype(BF16), p["b_gate"][l].astype(F32).reshape(4, D),
               w_branch, p["w_out"][l].astype(BF16))
    return _ffn(x, p["norm_ffn2"][l][None], *_prep_ffn(p["w_ffn2_in"], p["w_ffn2_out"], l))


def kernel(x_prompt, x_sample, norm_ffn1, w_ffn1_in, w_ffn1_out, norm_mix, w_in, na_rpb, diff_lambda, diff_subln,
           swa_sink, w_branch, w_gate, b_gate, w_out, norm_ffn2, w_ffn2_in, w_ffn2_out, norm_final):
    p = dict(norm_ffn1=norm_ffn1, w_ffn1_in=w_ffn1_in, w_ffn1_out=w_ffn1_out, norm_mix=norm_mix, w_in=w_in,
             na_rpb=na_rpb, diff_lambda=diff_lambda, diff_subln=diff_subln, swa_sink=swa_sink,
             w_branch=w_branch, w_gate=w_gate, b_gate=b_gate, w_out=w_out, norm_ffn2=norm_ffn2,
             w_ffn2_in=w_ffn2_in, w_ffn2_out=w_ffn2_out)
    bp, tp, _ = x_prompt.shape
    bs, ts, _ = x_sample.shape
    mp = bp * tp
    x = (x_prompt.reshape(mp, D).astype(F32), x_sample.reshape(bs * ts, D).astype(F32))
    seqs = ((bp, tp, 0), (bs, ts, mp))
    for l in range(DEPTH):
        x = _encoder_layer(x, l, seqs, p)
    y_prompt, y_sample = _final_norm(x, norm_final.astype(F32)[None], mp)
    return y_prompt.reshape(bp, tp, D), y_sample.reshape(bs, ts, D)
```

```python
import functools
import math

import jax
import jax.numpy as jnp
import numpy as np
from jax import lax
from jax.experimental import pallas as pl
from jax.experimental.pallas import tpu as pltpu

F32 = jnp.float32
BF16 = jnp.bfloat16

D = 2048
DEPTH = 2
GRID_W = 64
NA_HEADS, NA_DH, NA_ROWS, NA_COLS = 8, 64, 8, 16
DIFF_HEADS, DIFF_DK, DIFF_DV = 4, 64, 128
DIL_WINDOWS, DIL_DILATIONS = (128, 512, 2048), (1, 4, 16)
DIL_GROUPS, DIL_HEADS, DIL_DH = 3, 4, 128
SWA_QH, SWA_KVH, SWA_DH, SWA_RADIUS = 8, 2, 64, 128
SWA_GROUP = SWA_QH // SWA_KVH
BRANCH_W = 512
D_FF = ((8 * D // 3 + 127) // 128) * 128
N_ALIBI = SWA_QH + DIL_GROUPS * DIL_HEADS + DIFF_HEADS
RMS_EPS = 1e-6
NEG_INF = -1e30
LOG2E = math.log2(math.e)
MASKED_DIST = 1e32

LANES = 128
IN_WIDTH = 3 * NA_HEADS * NA_DH + 2 * DIFF_HEADS * 2 * DIFF_DK + DIFF_HEADS * DIFF_DV \
    + 3 * DIL_GROUPS * DIL_HEADS * DIL_DH + SWA_QH * SWA_DH + 2 * SWA_KVH * SWA_DH
N_BLK = IN_WIDTH // LANES
BLK_NA_Q, BLK_NA_K, BLK_NA_V = 0, 4, 8
BLK_DF_Q, BLK_DF_K, BLK_DF_V = 12, 16, 20
BLK_DL_Q, BLK_DL_K, BLK_DL_V = 24, 28, 32
BLK_SW_Q, BLK_SW_K, BLK_SW_V = 36, 40, 41
N_BLK_MAIN = 42
N_BLK_DIL = 3 * DIL_HEADS
COL_NA, COL_DF, COL_DL, COL_SW = 0, 1536, 3072, 7680
DIL_W = DIL_GROUPS * DIL_HEADS * DIL_DH

FFN_TF = 512
FFN_FP = -(-D_FF // FFN_TF) * FFN_TF
FFN_NC1 = FFN_FP // FFN_TF
TN_OUT = 256
TM_FFN = 1024
TM_PROJ = 1024
NB_PROJ = 6
TN_PROJ = NB_PROJ * LANES
NJ_MAIN = N_BLK_MAIN // NB_PROJ
NJ_DIL = N_BLK_DIL // NB_PROJ
TM_MERGE = 1024
TC_MERGE = 256
TN_MERGE = 512
TM_COMBINE = 1024
RQ_NA = 4
NA_UROWS = 12
TQ_DIFF = 256
TQ_SWA = 256
TQ_DIL = 128
UNITS_DIL = 4
VMEM_LIMIT = 60 * 1024 * 1024

_ALIBI = [2.0 ** (-8.0 * (i + 1) / N_ALIBI) for i in range(N_ALIBI)]
SLOPES_SWA = _ALIBI[:SWA_QH]
SLOPES_DIL = _ALIBI[SWA_QH:SWA_QH + DIL_GROUPS * DIL_HEADS]
SLOPES_DIFF = _ALIBI[SWA_QH + DIL_GROUPS * DIL_HEADS:]


def _params(sem):
    return pltpu.CompilerParams(dimension_semantics=sem, vmem_limit_bytes=VMEM_LIMIT)


def _call_into(body, prev, n_in, **kw):
    if prev is None:
        return pl.pallas_call(body, **kw)
    prev = tuple(prev)

    def aliased(*refs):
        body(*refs[:n_in], *refs[n_in + len(prev):])

    kw["in_specs"] = list(kw["in_specs"]) + [pl.BlockSpec(memory_space=pl.ANY)] * len(prev)
    call = pl.pallas_call(aliased, input_output_aliases={n_in + i: i for i in range(len(prev))}, **kw)
    return lambda *args: call(*args, *prev)


def _rms(x, g, eps):
    ms = jnp.mean(x * x, axis=-1, keepdims=True)
    return x * lax.rsqrt(ms + eps) * g


def _dot(a, b):
    return jnp.dot(a, b, preferred_element_type=F32)


def _dot_nt(a, b):
    return lax.dot_general(a, b, (((1,), (1,)), ((), ())), preferred_element_type=F32)


def _half_masks(rows):
    lane = lax.broadcasted_iota(jnp.int32, (rows, LANES), 1)
    return lane < (LANES // 2)


def _keep_half(x, lo, half):
    keep = lo if half == 0 else jnp.logical_not(lo)
    return jnp.where(keep, x.astype(F32), 0.0).astype(BF16)


def _ffn_kernel(x_ref, g_ref, wg_ref, wu_ref, w2_ref, xres_ref, o_ref, h_scr, act_scr):
    j = pl.program_id(1)

    @pl.when(j == 0)
    def _():
        h_scr[...] = _rms(x_ref[...], g_ref[...], RMS_EPS).astype(BF16)

    @pl.when(j < FFN_NC1)
    def _():
        h = h_scr[...]
        gate = _dot(h, wg_ref[...])
        up = _dot(h, wu_ref[...])
        act_scr[j] = (gate * jax.nn.sigmoid(gate) * up).astype(BF16)

    @pl.when(j >= FFN_NC1)
    def _():
        acc = _dot(act_scr[0], w2_ref[0:FFN_TF, :])
        for c in range(1, FFN_NC1):
            acc = acc + _dot(act_scr[c], w2_ref[c * FFN_TF:(c + 1) * FFN_TF, :])
        o_ref[...] = xres_ref[...] + 0.5 * acc


def _ffn(x, g, w1, w2, m_total=None, row_off=0, prev=None):
    m = x.shape[0]
    m_total = m if m_total is None else m_total
    tm = TM_FFN
    nc2 = D // TN_OUT
    tile0 = row_off // tm
    col = lambda i, j: (i, jnp.maximum(j - FFN_NC1, 0))
    return _call_into(
        _ffn_kernel, prev, 6,
        grid=(m // tm, FFN_NC1 + nc2),
        in_specs=[
            pl.BlockSpec((tm, D), lambda i, j: (i, 0)),
            pl.BlockSpec((1, D), lambda i, j: (0, 0)),
            pl.BlockSpec((None, D, FFN_TF), lambda i, j: (0, 0, jnp.minimum(j, FFN_NC1 - 1))),
            pl.BlockSpec((None, D, FFN_TF), lambda i, j: (1, 0, jnp.minimum(j, FFN_NC1 - 1))),
            pl.BlockSpec((FFN_FP, TN_OUT), lambda i, j: (0, jnp.maximum(j - FFN_NC1, 0))),
            pl.BlockSpec((tm, TN_OUT), col),
        ],
        out_specs=pl.BlockSpec((tm, TN_OUT), lambda i, j: (tile0 + i, jnp.maximum(j - FFN_NC1, 0))),
        out_shape=jax.ShapeDtypeStruct((m_total, D), F32),
        scratch_shapes=[pltpu.VMEM((tm, D), BF16), pltpu.VMEM((FFN_NC1, tm, FFN_TF), BF16)],
        compiler_params=_params(("parallel", "arbitrary")),
        name="ffn",
    )(x, g, w1, w1, w2, x)


def _proj_kernel(x_ref, g_ref, w_ref, h_ref, main_ref, d4_ref, d16_ref, res_scr, *, tm):
    j = pl.program_id(1)

    @pl.when(j == 0)
    def _():
        h_ref[...] = _rms(x_ref[...], g_ref[...], RMS_EPS).astype(BF16)

    res = _dot(h_ref[...], w_ref[...])

    @pl.when(j < NJ_MAIN)
    def _():
        for k in range(NB_PROJ):
            main_ref[k] = res[:, k * LANES:(k + 1) * LANES].astype(BF16)

    @pl.when(j >= NJ_MAIN)
    def _():
        for k in range(NB_PROJ):
            res_scr[k] = res[:, k * LANES:(k + 1) * LANES]

    for out_ref, dil, first in ((d4_ref, DIL_DILATIONS[1], NJ_MAIN), (d16_ref, DIL_DILATIONS[2], NJ_MAIN + NJ_DIL)):
        @pl.when((j >= first) & (j < first + NJ_DIL))
        def _(out_ref=out_ref, dil=dil):
            for k in range(NB_PROJ):
                for r in range(dil):
                    rows = res_scr[k, pl.ds(r, tm // dil, stride=dil), :]
                    out_ref[k, :, r * LANES:(r + 1) * LANES] = rows.astype(BF16)


def _proj(x, g, w):
    m = x.shape[0]
    tm = TM_PROJ
    d4, d16 = DIL_DILATIONS[1], DIL_DILATIONS[2]
    return pl.pallas_call(
        functools.partial(_proj_kernel, tm=tm),
        grid=(m // tm, NJ_MAIN + 2 * NJ_DIL),
        in_specs=[
            pl.BlockSpec((tm, D), lambda i, j: (i, 0)),
            pl.BlockSpec((1, D), lambda i, j: (0, 0)),
            pl.BlockSpec((D, TN_PROJ), lambda i, j: (0, j)),
        ],
        out_specs=[
            pl.BlockSpec((tm, D), lambda i, j: (i, 0)),
            pl.BlockSpec((NB_PROJ, tm, LANES), lambda i, j: (jnp.minimum(j, NJ_MAIN - 1), i, 0)),
            pl.BlockSpec((NB_PROJ, tm // d4, d4 * LANES),
                         lambda i, j: (jnp.clip(j - NJ_MAIN, 0, NJ_DIL - 1), i, 0)),
            pl.BlockSpec((NB_PROJ, tm // d16, d16 * LANES),
                         lambda i, j: (jnp.clip(j - NJ_MAIN - NJ_DIL, 0, NJ_DIL - 1), i, 0)),
        ],
        out_shape=[jax.ShapeDtypeStruct((m, D), BF16),
                   jax.ShapeDtypeStruct((N_BLK_MAIN, m, LANES), BF16),
                   jax.ShapeDtypeStruct((N_BLK_DIL, m // d4, d4 * LANES), BF16),
                   jax.ShapeDtypeStruct((N_BLK_DIL, m // d16, d16 * LANES), BF16)],
        scratch_shapes=[pltpu.VMEM((NB_PROJ, tm, LANES), F32)],
        compiler_params=_params(("parallel", "arbitrary")),
        name="proj",
    )(x, g, w)


def _na_kernel(q_ref, k_ref, v_ref, b_ref, o_ref, *, rows):
    nq = RQ_NA * GRID_W
    nk = NA_UROWS * GRID_W
    row0 = jnp.clip(RQ_NA * pl.program_id(1) - NA_ROWS // 2, 0, rows - NA_UROWS)
    start = pl.multiple_of(row0 * GRID_W, GRID_W)
    lo = _half_masks(nq)
    for hp in range(NA_HEADS // 2):
        q = q_ref[hp]
        k = k_ref[hp, pl.ds(start, nk), :]
        v_ext = _with_ones(v_ref[hp, pl.ds(start, nk), :])
        lhs = jnp.concatenate([_keep_half(q, lo, 0), _keep_half(q, lo, 1)], axis=0)
        o, l, _ = _softmax_pv(_dot_nt(lhs, k) + b_ref[0, hp], v_ext)
        o = o / l
        o_ref[:, hp * LANES:(hp + 1) * LANES] = jnp.where(lo, o[:nq], o[nq:]).astype(BF16)


def _na(qkv, bias, n_seq, t, row_off, prev):
    rows = t // GRID_W
    nb = rows // RQ_NA
    nq = RQ_NA * GRID_W
    blk0 = row_off // nq
    seq0 = row_off // t
    group_type = lambda g: jnp.where(g == 0, 0, jnp.where(g == nb - 1, 2, 1))
    return _call_into(
        functools.partial(_na_kernel, rows=rows), prev, 4,
        grid=(n_seq, nb),
        in_specs=[
            pl.BlockSpec((4, nq, LANES), lambda s, g: (BLK_NA_Q // 4, blk0 + s * nb + g, 0)),
            pl.BlockSpec((4, t, LANES), lambda s, g: (BLK_NA_K // 4, seq0 + s, 0)),
            pl.BlockSpec((4, t, LANES), lambda s, g: (BLK_NA_V // 4, seq0 + s, 0)),
            pl.BlockSpec((1, NA_HEADS // 2, 2 * nq, NA_UROWS * GRID_W), lambda s, g: (group_type(g), 0, 0, 0)),
        ],
        out_specs=pl.BlockSpec((nq, BRANCH_W), lambda s, g: (blk0 + s * nb + g, 0)),
        out_shape=jax.ShapeDtypeStruct((qkv.shape[1], BRANCH_W), BF16),
        compiler_params=_params(("parallel", "arbitrary")),
        name="mixer_a",
    )(qkv, qkv, qkv, bias)


def _na_bias_table(rpb):
    c = np.arange(GRID_W)[:, None]
    kc = np.arange(GRID_W)[None, :]
    cstart = np.clip(c - NA_COLS // 2, 0, GRID_W - NA_COLS)
    ok = (kc >= cstart) & (kc < cstart + NA_COLS)
    pad = GRID_W - NA_COLS
    padded = jnp.pad(rpb.astype(F32) * LOG2E, ((0, 0), (0, 0), (pad, pad)))
    e = jnp.stack([padded[..., GRID_W - 1 - q:2 * GRID_W - 1 - q] for q in range(GRID_W)], axis=2)
    e = jnp.where(ok[None, None], e, NEG_INF)
    neg = jnp.full((NA_HEADS, GRID_W, GRID_W), NEG_INF, F32)
    half = NA_ROWS // 2
    group_types = (
        [(-i, -i) for i in range(RQ_NA)],
        [(-half - i, -half) for i in range(RQ_NA)],
        [(-(NA_UROWS - RQ_NA) - i, -(NA_ROWS - RQ_NA) - i) for i in range(RQ_NA)],
    )
    tables = []
    for rel in group_types:
        q_rows = []
        for u0, w0 in rel:
            blocks = [e[:, u0 + j + NA_ROWS - 1] if w0 <= u0 + j < w0 + NA_ROWS else neg for j in range(NA_UROWS)]
            q_rows.append(jnp.concatenate(blocks, axis=2))
        tables.append(jnp.concatenate(q_rows, axis=1))
    return jnp.stack(tables).reshape(3, NA_HEADS // 2, 2 * RQ_NA * GRID_W, NA_UROWS * GRID_W)


def _diff_kernel(lam_ref, q_ref, k_ref, v_ref, tab_ref, g_ref, o_ref, vext_scr, *, t, tq, out_scale):
    qi = pl.program_id(2)

    @pl.when(qi == 0)
    def _():
        vext_scr[:, :LANES] = v_ref[0]
        vext_scr[:, LANES:] = jnp.ones((t, LANES), BF16)

    lam = lam_ref[0]
    off = pl.multiple_of((t // tq - 1 - qi) * tq, tq)
    bias = tab_ref[0, :, pl.ds(off, t)]
    q = q_ref[0]
    lo = _half_masks(tq)
    k = k_ref[0]
    v_ext = vext_scr[...]

    scores = [_dot_nt(_keep_half(q, lo, half), k) - bias for half in range(2)]
    probs = [jnp.exp2((s - jnp.max(s, axis=-1, keepdims=True)).astype(BF16)) for s in scores]
    outs = [_dot(p, v_ext) for p in probs]
    o1, o2 = (ol[:, :LANES] / ol[:, LANES:] for ol in outs)
    o = o1 - lam * o2
    o_ref[...] = (_rms(o, g_ref[...], 1e-5) * out_scale).astype(BF16)


def _diff_bias_table(t, tq):
    r = lax.broadcasted_iota(jnp.int32, (tq, 2 * t - tq), 0)
    x = lax.broadcasted_iota(jnp.int32, (tq, 2 * t - tq), 1)
    dist = jnp.abs(r - x + (t - tq)).astype(F32)
    return jnp.asarray([s * LOG2E for s in SLOPES_DIFF], F32)[:, None, None] * dist[None]


def _diff(qkv, lam, subln_g, n_seq, t, row_off, lam_init, prev):
    tq = TQ_DIFF
    nq = t // tq
    blk0 = row_off // tq
    seq0 = row_off // t
    return _call_into(
        functools.partial(_diff_kernel, t=t, tq=tq, out_scale=1.0 - lam_init), prev, 6,
        grid=(DIFF_HEADS, n_seq, nq),
        in_specs=[
            pl.BlockSpec(memory_space=pltpu.SMEM),
            pl.BlockSpec((1, tq, LANES), lambda h, s, i: (BLK_DF_Q + h, blk0 + s * nq + i, 0)),
            pl.BlockSpec((1, t, LANES), lambda h, s, i: (BLK_DF_K + h, seq0 + s, 0)),
            pl.BlockSpec((1, t, LANES), lambda h, s, i: (BLK_DF_V + h, seq0 + s, 0)),
            pl.BlockSpec((1, tq, 2 * t - tq), lambda h, s, i: (h, 0, 0)),
            pl.BlockSpec((1, DIFF_DV), lambda h, s, i: (0, 0)),
        ],
        out_specs=pl.BlockSpec((tq, LANES), lambda h, s, i: (blk0 + s * nq + i, h)),
        out_shape=jax.ShapeDtypeStruct((qkv.shape[1], BRANCH_W), BF16),
        scratch_shapes=[pltpu.VMEM((t, 2 * LANES), BF16)],
        compiler_params=_params(("parallel", "arbitrary", "arbitrary")),
        name="mixer_b",
    )(lam, qkv, qkv, qkv, _diff_bias_table(t, tq), subln_g)


def _band_window(qi, tq, kw, radius, length):
    start = jnp.clip(qi * tq - radius, 0, length - kw)
    rel = (lax.broadcasted_iota(jnp.int32, (tq, kw), 0) + (qi * tq - start)) \
        - lax.broadcasted_iota(jnp.int32, (tq, kw), 1)
    dist = jnp.abs(rel)
    return pl.multiple_of(start, 64), jnp.where(dist <= radius, dist.astype(F32), MASKED_DIST)


def _with_ones(v):
    return jnp.concatenate([v, jnp.ones(v.shape, v.dtype)], axis=1)


def _softmax_pv(s, v_ext, floor=None):
    m = jnp.max(s, axis=-1, keepdims=True)
    if floor is not None:
        m = jnp.maximum(m, floor)
    ol = _dot(jnp.exp2((s - m).astype(BF16)), v_ext)
    return ol[:, :LANES], ol[:, LANES:], m


def _dil_kernel(q_ref, k_ref, v_ref, o_ref, lse_ref, *, n_sub, tq, kw, radius, slopes, scale, nqb, nrb):
    for b in range(nqb):
        start, dist = _band_window(pl.program_id(2) * nqb + b, tq, kw, radius, n_sub)
        rows = slice(b * tq, (b + 1) * tq)
        for rr in range(nrb):
            lanes = slice(rr * LANES, (rr + 1) * LANES)
            for h in range(DIL_HEADS):
                k = k_ref[h, pl.ds(start, kw), lanes]
                v_ext = _with_ones(v_ref[h, pl.ds(start, kw), lanes])
                s = _dot_nt(q_ref[h, rows, lanes], k) * scale - slopes[h] * dist
                o, l, m = _softmax_pv(s, v_ext)
                o_ref[h, rows, lanes] = (o / l).astype(BF16)
                lse_ref[h, rows, lanes] = (m + jnp.log2(l)) * (1.0 / LOG2E)


def _dil_group(view, gi, n_seq, t, row_off, prev):
    dil = DIL_DILATIONS[gi]
    radius = DIL_WINDOWS[gi] // (2 * dil)
    n_sub = t // dil
    tq = min(TQ_DIL, n_sub)
    kw = min(tq + 2 * radius, n_sub)
    nqb = min(n_sub // tq, UNITS_DIL)
    nrb = min(dil, UNITS_DIL // nqb)
    nq = n_sub // (tq * nqb)
    blk0 = row_off // dil // (tq * nqb)
    seq0 = row_off // t
    qb, kb, vb = (BLK_DL_Q // 4, BLK_DL_K // 4, BLK_DL_V // 4) if gi == 0 else (0, 1, 2)
    slopes = tuple(SLOPES_DIL[gi * DIL_HEADS + h] * dil * LOG2E for h in range(DIL_HEADS))
    rows_out = view.shape[1]
    q_idx = lambda s, r, i: (qb, blk0 + s * nq + i, r)
    o_idx = lambda s, r, i: (0, blk0 + s * nq + i, r)
    o, lse = _call_into(
        functools.partial(_dil_kernel, n_sub=n_sub, tq=tq, kw=kw, radius=radius, slopes=slopes,
                          scale=DIL_DH ** -0.5 * LOG2E, nqb=nqb, nrb=nrb), prev, 3,
        grid=(n_seq, dil // nrb, nq),
        in_specs=[
            pl.BlockSpec((4, nqb * tq, nrb * LANES), q_idx),
            pl.BlockSpec((4, n_sub, nrb * LANES), lambda s, r, i: (kb, seq0 + s, r)),
            pl.BlockSpec((4, n_sub, nrb * LANES), lambda s, r, i: (vb, seq0 + s, r)),
        ],
        out_specs=[pl.BlockSpec((4, nqb * tq, nrb * LANES), o_idx),
                   pl.BlockSpec((4, nqb * tq, nrb * LANES), o_idx)],
        out_shape=[jax.ShapeDtypeStruct((DIL_HEADS, rows_out, dil * LANES), BF16),
                   jax.ShapeDtypeStruct((DIL_HEADS, rows_out, dil * LANES), F32)],
        compiler_params=_params(("parallel", "arbitrary", "arbitrary")),
        name=f"mixer_c{gi}",
    )(view, view, view)
    return o, lse


def _dil_combine_kernel(o0, o1, o2, l0, l1, l2, out_ref, o_scr, l_scr, *, tm):
    for h in range(DIL_HEADS):
        for gi, (o_ref, l_ref) in ((1, (o1, l1)), (2, (o2, l2))):
            dil = DIL_DILATIONS[gi]
            for r in range(dil):
                rows = pl.ds(r, tm // dil, stride=dil)
                o_scr[gi - 1, rows, :] = o_ref[h, :, r * LANES:(r + 1) * LANES].astype(F32)
                l_scr[gi - 1, rows, :] = l_ref[h, :, r * LANES:(r + 1) * LANES]
        a0, a1, a2 = l0[h], l_scr[0], l_scr[1]
        mx = jnp.maximum(jnp.maximum(a0, a1), a2)
        e0, e1, e2 = jnp.exp(a0 - mx), jnp.exp(a1 - mx), jnp.exp(a2 - mx)
        num = e0 * o0[h].astype(F32) + e1 * o_scr[0] + e2 * o_scr[1]
        out_ref[:, h * LANES:(h + 1) * LANES] = (num / (e0 + e1 + e2)).astype(BF16)


def _dil_combine(outs, lses):
    m = outs[0].shape[1]
    tm = min(TM_COMBINE, m)
    specs = [pl.BlockSpec((DIL_HEADS, tm // dil, dil * LANES), lambda i: (0, i, 0)) for dil in DIL_DILATIONS]
    return pl.pallas_call(
        functools.partial(_dil_combine_kernel, tm=tm),
        grid=(m // tm,),
        in_specs=specs * 2,
        out_specs=pl.BlockSpec((tm, BRANCH_W), lambda i: (i, 0)),
        out_shape=jax.ShapeDtypeStruct((m, BRANCH_W), BF16),
        scratch_shapes=[pltpu.VMEM((2, tm, LANES), F32), pltpu.VMEM((2, tm, LANES), F32)],
        compiler_params=_params(("parallel",)),
        name="mixer_c_combine",
    )(*outs, *lses)


def _swa_kernel(sink_ref, q_ref, k_ref, v_ref, o_ref, *, t, tq, kw):
    qi = pl.program_id(1)
    start, dist = _band_window(qi, tq, kw, SWA_RADIUS, t)
    k = k_ref[0, pl.ds(start, kw), :]
    v_ext = _with_ones(v_ref[0, pl.ds(start, kw), :])
    lo = _half_masks(tq)
    for g in range(SWA_GROUP):
        q = q_ref[g]
        outs = []
        for hk in range(SWA_KVH):
            head = hk * SWA_GROUP + g
            sink = sink_ref[head] * LOG2E
            s = _dot_nt(_keep_half(q, lo, hk), k) - (SLOPES_SWA[head] * LOG2E) * dist
            o, l, m = _softmax_pv(s, v_ext, floor=sink)
            outs.append(o / (l + jnp.exp2(sink - m)))
        o_ref[:, g * LANES:(g + 1) * LANES] = jnp.where(lo, outs[0], outs[1]).astype(BF16)


def _swa(qkv, sink, n_seq, t, row_off, prev):
    tq = TQ_SWA
    kw = min(tq + 2 * SWA_RADIUS, t)
    nq = t // tq
    blk0 = row_off // tq
    seq0 = row_off // t
    return _call_into(
        functools.partial(_swa_kernel, t=t, tq=tq, kw=kw), prev, 4,
        grid=(n_seq, nq),
        in_specs=[
            pl.BlockSpec(memory_space=pltpu.SMEM),
            pl.BlockSpec((4, tq, LANES), lambda s, i: (BLK_SW_Q // 4, blk0 + s * nq + i, 0)),
            pl.BlockSpec((1, t, LANES), lambda s, i: (BLK_SW_K, seq0 + s, 0)),
            pl.BlockSpec((1, t, LANES), lambda s, i: (BLK_SW_V, seq0 + s, 0)),
        ],
        out_specs=pl.BlockSpec((tq, BRANCH_W), lambda s, i: (blk0 + s * nq + i, 0)),
        out_shape=jax.ShapeDtypeStruct((qkv.shape[1], BRANCH_W), BF16),
        compiler_params=_params(("parallel", "arbitrary")),
        name="mixer_d",
    )(sink, qkv, qkv, qkv)


def _merge_kernel(h_ref, oa_ref, ob_ref, oc_ref, od_ref, wg0_ref, wg1_ref, wg2_ref, wg3_ref,
                  bg_ref, wb_ref, wo_ref, xres_ref, o_ref, m_scr, *, nc1, tc):
    j = pl.program_id(1)

    @pl.when(j < nc1)
    def _():
        h = h_ref[...]
        acc = None
        branches = ((oa_ref, wg0_ref), (ob_ref, wg1_ref), (oc_ref, wg2_ref), (od_ref, wg3_ref))
        for n, (b_ref, wg_ref) in enumerate(branches):
            gate = jax.nn.sigmoid(_dot(h, wg_ref[...]) + bg_ref[n:n + 1, :])
            term = gate * _dot(b_ref[...], wb_ref[n])
            acc = term if acc is None else acc + term
        m_scr[j] = acc.astype(BF16)

    @pl.when(j >= nc1)
    def _():
        acc = _dot(m_scr[0], wo_ref[0:tc, :])
        for c in range(1, nc1):
            acc = acc + _dot(m_scr[c], wo_ref[c * tc:(c + 1) * tc, :])
        o_ref[...] = xres_ref[...] + acc


def _merge(x, h, branches, w_gate, b_gate, w_branch, w_out):
    m = x.shape[0]
    tm, tc, tn = TM_MERGE, TC_MERGE, TN_MERGE
    nc1 = D // tc
    nc2 = D // tn
    first = lambda j: jnp.minimum(j, nc1 - 1)
    col = lambda i, j: (i, jnp.maximum(j - nc1, 0))
    row_tile = lambda i, j: (i, 0)
    gate_specs = [pl.BlockSpec((D, tc), functools.partial(lambda i, j, n: (0, n * nc1 + first(j)), n=n))
                  for n in range(4)]
    return pl.pallas_call(
        functools.partial(_merge_kernel, nc1=nc1, tc=tc),
        grid=(m // tm, nc1 + nc2),
        in_specs=[pl.BlockSpec((tm, D), row_tile)]
        + [pl.BlockSpec((tm, BRANCH_W), row_tile)] * 4
        + gate_specs
        + [
            pl.BlockSpec((4, tc), lambda i, j: (0, first(j))),
            pl.BlockSpec((4, BRANCH_W, tc), lambda i, j: (0, 0, first(j))),
            pl.BlockSpec((D, tn), lambda i, j: (0, jnp.maximum(j - nc1, 0))),
            pl.BlockSpec((tm, tn), col),
        ],
        out_specs=pl.BlockSpec((tm, tn), col),
        out_shape=jax.ShapeDtypeStruct((m, D), F32),
        scratch_shapes=[pltpu.VMEM((nc1, tm, tc), BF16)],
        compiler_params=_params(("parallel", "arbitrary")),
        name="merge",
    )(h, *branches, w_gate, w_gate, w_gate, w_gate, b_gate, w_branch, w_out, x)


def _final_norm_kernel(x_ref, g_ref, first_ref, second_ref, *, n_first):
    y = _rms(x_ref[...], g_ref[...], RMS_EPS)

    @pl.when(pl.program_id(0) < n_first)
    def _():
        first_ref[...] = y

    @pl.when(pl.program_id(0) >= n_first)
    def _():
        second_ref[...] = y


def _final_norm(x, g, m_first):
    m = x.shape[0]
    tm = 512
    n_first = m_first // tm
    return pl.pallas_call(
        functools.partial(_final_norm_kernel, n_first=n_first),
        grid=(m // tm,),
        in_specs=[pl.BlockSpec((tm, D), lambda i: (i, 0)), pl.BlockSpec((1, D), lambda i: (0, 0))],
        out_specs=[pl.BlockSpec((tm, D), lambda i: (jnp.minimum(i, n_first - 1), 0)),
                   pl.BlockSpec((tm, D), lambda i: (jnp.maximum(i - n_first, 0), 0))],
        out_shape=[jax.ShapeDtypeStruct((m_first, D), F32), jax.ShapeDtypeStruct((m - m_first, D), F32)],
        compiler_params=_params(("arbitrary",)),
        name="final_norm",
    )(x, g)


def _cast_ffn_in_kernel(w_ref, o_ref):
    o_ref[0, :, :D_FF] = w_ref[...].astype(BF16)
    o_ref[0, :, D_FF:] = jnp.zeros((o_ref.shape[1], FFN_FP - D_FF), BF16)


def _cast_ffn_in(w_in, l):
    tr = 256
    return pl.pallas_call(
        _cast_ffn_in_kernel,
        grid=(2, D // tr),
        in_specs=[pl.BlockSpec((None, tr, D_FF), lambda part, i: (l, i, part))],
        out_specs=pl.BlockSpec((1, tr, FFN_FP), lambda part, i: (part, i, 0)),
        out_shape=jax.ShapeDtypeStruct((2, D, FFN_FP), BF16),
        compiler_params=_params(("parallel", "parallel")),
        name="cast_ffn_in",
    )(w_in)


def _cast_ffn_out_kernel(w_ref, o_ref, *, tr):
    row = pl.program_id(0) * tr + lax.broadcasted_iota(jnp.int32, (tr, D), 0)
    o_ref[...] = jnp.where(row < D_FF, w_ref[...], 0.0).astype(BF16)


def _cast_ffn_out(w_out, l):
    tr = 512
    return pl.pallas_call(
        functools.partial(_cast_ffn_out_kernel, tr=tr),
        grid=(FFN_FP // tr,),
        in_specs=[pl.BlockSpec((None, tr, D), lambda i: (l, i, 0))],
        out_specs=pl.BlockSpec((tr, D), lambda i: (i, 0)),
        out_shape=jax.ShapeDtypeStruct((FFN_FP, D), BF16),
        compiler_params=_params(("parallel",)),
        name="cast_ffn_out",
    )(w_out)


def _prep_ffn(w_in, w_out, l):
    return _cast_ffn_in(w_in, l), _cast_ffn_out(w_out, l)


def _swa_pair_heads(w, axis):
    shape = w.shape
    split = shape[:axis] + (SWA_KVH, SWA_GROUP, SWA_DH) + shape[axis + 1:]
    return jnp.swapaxes(w.reshape(split), axis, axis + 1).reshape(shape)


def _proj_columns(w_in):
    scale = np.ones((IN_WIDTH,), np.float32)
    scale[COL_NA:COL_NA + NA_HEADS * NA_DH] = NA_DH ** -0.5 * LOG2E
    scale[COL_DF:COL_DF + DIFF_HEADS * 2 * DIFF_DK] = DIFF_DK ** -0.5 * LOG2E
    scale[COL_SW:COL_SW + SWA_QH * SWA_DH] = SWA_DH ** -0.5 * LOG2E
    w = w_in * scale
    gw = DIL_HEADS * DIL_DH

    def dil_group(gi):
        return [w[:, COL_DL + part * DIL_W + gi * gw:COL_DL + part * DIL_W + (gi + 1) * gw] for part in range(3)]

    sw_q_end = COL_SW + SWA_QH * SWA_DH
    cols = [w[:, :COL_DL]] + dil_group(0) + [_swa_pair_heads(w[:, COL_SW:sw_q_end], 1), w[:, sw_q_end:]] \
        + dil_group(1) + dil_group(2)
    return jnp.concatenate(cols, axis=1)


def _prep_w_in(w_in):
    return _proj_columns(w_in).astype(BF16)


def _encoder_layer(x, l, seqs, p):
    ffn1 = (p["norm_ffn1"][l][None], *_prep_ffn(p["w_ffn1_in"], p["w_ffn1_out"], l))
    if isinstance(x, tuple):
        m_total = sum(xb.shape[0] for xb in x)
        out, row_off = None, 0
        for xb in x:
            out = _ffn(xb, *ffn1, m_total=m_total, row_off=row_off, prev=None if out is None else (out,))
            row_off += xb.shape[0]
        x = out
    else:
        x = _ffn(x, *ffn1)
    h, qkv, qkv_d4, qkv_d16 = _proj(x, p["norm_mix"][l][None], _prep_w_in(p["w_in"][l]))
    dil_views = (qkv, qkv_d4, qkv_d16)

    lam_init = 0.8 - 0.6 * math.exp(-0.3 * l)
    lv = p["diff_lambda"][l].astype(F32)
    lam = jnp.exp(jnp.sum(lv[0] * lv[1])) - jnp.exp(jnp.sum(lv[2] * lv[3])) + lam_init
    diff_lam = lam.reshape(1)
    na_bias = _na_bias_table(p["na_rpb"][l])
    subln = p["diff_subln"][l].astype(F32)[None]
    sink = p["swa_sink"][l].astype(F32)

    o_a = o_b = o_d = None
    dil = [None] * DIL_GROUPS
    for n_seq, t, row_off in seqs:
        o_a = _na(qkv, na_bias, n_seq, t, row_off, None if o_a is None else (o_a,))
        o_b = _diff(qkv, diff_lam, subln, n_seq, t, row_off, lam_init, None if o_b is None else (o_b,))
        for gi in range(DIL_GROUPS):
            dil[gi] = _dil_group(dil_views[gi], gi, n_seq, t, row_off, dil[gi])
        o_d = _swa(qkv, sink, n_seq, t, row_off, None if o_d is None else (o_d,))
    o_c = _dil_combine([o for o, _ in dil], [lse for _, lse in dil])
    branches = [o_a, o_b, o_c, o_d]

    w_branch = p["w_branch"][l]
    w_branch = jnp.concatenate([w_branch[:3], _swa_pair_heads(w_branch[3:], 1)], axis=0).astype(BF16)
    x = _merge(x, h, branches, p["w_gate"][l].astype(BF16), p["b_gate"][l].astype(F32).reshape(4, D),
               w_branch, p["w_out"][l].astype(BF16))
    return _ffn(x, p["norm_ffn2"][l][None], *_prep_ffn(p["w_ffn2_in"], p["w_ffn2_out"], l))


def kernel(x_prompt, x_sample, norm_ffn1, w_ffn1_in, w_ffn1_out, norm_mix, w_in, na_rpb, diff_lambda, diff_subln,
           swa_sink, w_branch, w_gate, b_gate, w_out, norm_ffn2, w_ffn2_in, w_ffn2_out, norm_final):
    p = dict(norm_ffn1=norm_ffn1, w_ffn1_in=w_ffn1_in, w_ffn1_out=w_ffn1_out, norm_mix=norm_mix, w_in=w_in,
             na_rpb=na_rpb, diff_lambda=diff_lambda, diff_subln=diff_subln, swa_sink=swa_sink,
             w_branch=w_branch, w_gate=w_gate, b_gate=b_gate, w_out=w_out, norm_ffn2=norm_ffn2,
             w_ffn2_in=w_ffn2_in, w_ffn2_out=w_ffn2_out)
    bp, tp, _ = x_prompt.shape
    bs, ts, _ = x_sample.shape
    mp = bp * tp
    x = (x_prompt.reshape(mp, D).astype(F32), x_sample.reshape(bs * ts, D).astype(F32))
    seqs = ((bp, tp, 0), (bs, ts, mp))
    for l in range(DEPTH):
        x = _encoder_layer(x, l, seqs, p)
    y_prompt, y_sample = _final_norm(x, norm_final.astype(F32)[None], mp)
    return y_prompt.reshape(bp, tp, D), y_sample.reshape(bs, ts, D)
```

```python
import functools
import math

import jax
import jax.numpy as jnp
import numpy as np
from jax import lax
from jax.experimental import pallas as pl
from jax.experimental.pallas import tpu as pltpu

F32 = jnp.float32
BF16 = jnp.bfloat16

D = 2048
DEPTH = 2
GRID_W = 64
NA_HEADS, NA_DH, NA_ROWS, NA_COLS = 8, 64, 8, 16
DIFF_HEADS, DIFF_DK, DIFF_DV = 4, 64, 128
DIL_WINDOWS, DIL_DILATIONS = (128, 512, 2048), (1, 4, 16)
DIL_GROUPS, DIL_HEADS, DIL_DH = 3, 4, 128
SWA_QH, SWA_KVH, SWA_DH, SWA_RADIUS = 8, 2, 64, 128
SWA_GROUP = SWA_QH // SWA_KVH
BRANCH_W = 512
D_FF = ((8 * D // 3 + 127) // 128) * 128
N_ALIBI = SWA_QH + DIL_GROUPS * DIL_HEADS + DIFF_HEADS
RMS_EPS = 1e-6
NEG_INF = -1e30
LOG2E = math.log2(math.e)
MASKED_DIST = 1e32

LANES = 128
IN_WIDTH = 3 * NA_HEADS * NA_DH + 2 * DIFF_HEADS * 2 * DIFF_DK + DIFF_HEADS * DIFF_DV \
    + 3 * DIL_GROUPS * DIL_HEADS * DIL_DH + SWA_QH * SWA_DH + 2 * SWA_KVH * SWA_DH
N_BLK = IN_WIDTH // LANES
BLK_NA_Q, BLK_NA_K, BLK_NA_V = 0, 4, 8
BLK_DF_Q, BLK_DF_K, BLK_DF_V = 12, 16, 20
BLK_DL_Q, BLK_DL_K, BLK_DL_V = 24, 28, 32
BLK_SW_Q, BLK_SW_K, BLK_SW_V = 36, 40, 41
N_BLK_MAIN = 42
N_BLK_DIL = 3 * DIL_HEADS
COL_NA, COL_DF, COL_DL, COL_SW = 0, 1536, 3072, 7680
DIL_W = DIL_GROUPS * DIL_HEADS * DIL_DH

FFN_TF = 512
FFN_FP = -(-D_FF // FFN_TF) * FFN_TF
FFN_NC1 = FFN_FP // FFN_TF
FFN_TN = 512
TM_FFN = 1024
TM_PROJ = 1024
NB_PROJ = 6
TN_PROJ = NB_PROJ * LANES
NJ_MAIN = N_BLK_MAIN // NB_PROJ
NJ_DIL = N_BLK_DIL // NB_PROJ
TM_MERGE = 1024
TC_MERGE = 256
TN_MERGE = 512
TM_COMBINE = 1024
RQ_NA = 4
NA_UROWS = 12
TQ_DIFF = 256
TQ_SWA = 256
TQ_DIL = 128
UNITS_DIL = 4
VMEM_LIMIT = 60 * 1024 * 1024

_ALIBI = [2.0 ** (-8.0 * (i + 1) / N_ALIBI) for i in range(N_ALIBI)]
SLOPES_SWA = _ALIBI[:SWA_QH]
SLOPES_DIL = _ALIBI[SWA_QH:SWA_QH + DIL_GROUPS * DIL_HEADS]
SLOPES_DIFF = _ALIBI[SWA_QH + DIL_GROUPS * DIL_HEADS:]


def _params(sem):
    return pltpu.CompilerParams(dimension_semantics=sem, vmem_limit_bytes=VMEM_LIMIT)


def _call_into(body, prev, n_in, **kw):
    if prev is None:
        return pl.pallas_call(body, **kw)
    prev = tuple(prev)

    def aliased(*refs):
        body(*refs[:n_in], *refs[n_in + len(prev):])

    kw["in_specs"] = list(kw["in_specs"]) + [pl.BlockSpec(memory_space=pl.ANY)] * len(prev)
    call = pl.pallas_call(aliased, input_output_aliases={n_in + i: i for i in range(len(prev))}, **kw)
    return lambda *args: call(*args, *prev)


def _rms(x, g, eps):
    ms = jnp.mean(x * x, axis=-1, keepdims=True)
    return x * lax.rsqrt(ms + eps) * g


def _dot(a, b):
    return jnp.dot(a, b, preferred_element_type=F32)


def _dot_nt(a, b):
    return lax.dot_general(a, b, (((1,), (1,)), ((), ())), preferred_element_type=F32)


def _half_masks(rows):
    lane = lax.broadcasted_iota(jnp.int32, (rows, LANES), 1)
    return lane < (LANES // 2)


def _keep_half(x, lo, half):
    keep = lo if half == 0 else jnp.logical_not(lo)
    return jnp.where(keep, x.astype(F32), 0.0).astype(BF16)


def _ffn_kernel(*refs, final_norm):
    x_ref, g_ref, wg_ref, wu_ref, w2_ref = refs[:5]
    o_ref, h_scr = refs[-2:]
    j = pl.program_id(1)

    @pl.when(j == 0)
    def _():
        x = x_ref[...]
        h_scr[...] = _rms(x, g_ref[...], RMS_EPS).astype(BF16)
        o_ref[...] = x

    h = h_scr[...]
    gate = _dot(h, wg_ref[...])
    up = _dot(h, wu_ref[...])
    act = (gate * jax.nn.sigmoid(gate) * (0.5 * up)).astype(BF16)
    for n in range(D // FFN_TN):
        cols = slice(n * FFN_TN, (n + 1) * FFN_TN)
        o_ref[:, cols] += _dot(act, w2_ref[:, cols])

    if final_norm:
        @pl.when(j == FFN_NC1 - 1)
        def _():
            o_ref[...] = _rms(o_ref[...], refs[5][...], RMS_EPS)


def _ffn(x, g, w1, w2, rows=None, m_out=None, out_off=0, prev=None, final_gain=None):
    start, m = (0, x.shape[0]) if rows is None else rows
    m_out = m if m_out is None else m_out
    tm = TM_FFN
    in0, out0 = start // tm, out_off // tm
    inputs = [x, g, w1, w1, w2] + ([] if final_gain is None else [final_gain])
    in_specs = [
        pl.BlockSpec((tm, D), lambda i, j: (in0 + i, 0)),
        pl.BlockSpec((1, D), lambda i, j: (0, 0)),
        pl.BlockSpec((None, D, FFN_TF), lambda i, j: (0, 0, j)),
        pl.BlockSpec((None, D, FFN_TF), lambda i, j: (1, 0, j)),
        pl.BlockSpec((FFN_TF, D), lambda i, j: (j, 0)),
    ] + ([] if final_gain is None else [pl.BlockSpec((1, D), lambda i, j: (0, 0))])
    return _call_into(
        functools.partial(_ffn_kernel, final_norm=final_gain is not None), prev, len(inputs),
        grid=(m // tm, FFN_NC1),
        in_specs=in_specs,
        out_specs=pl.BlockSpec((tm, D), lambda i, j: (out0 + i, 0)),
        out_shape=jax.ShapeDtypeStruct((m_out, D), F32),
        scratch_shapes=[pltpu.VMEM((tm, D), BF16)],
        compiler_params=_params(("parallel", "arbitrary")),
        name="ffn",
    )(*inputs)


def _proj_kernel(x_ref, g_ref, w_ref, h_ref, main_ref, d4_ref, d16_ref, res_scr, *, tm):
    j = pl.program_id(1)

    @pl.when(j == 0)
    def _():
        h_ref[...] = _rms(x_ref[...], g_ref[...], RMS_EPS).astype(BF16)

    res = _dot(h_ref[...], w_ref[...])

    @pl.when(j < NJ_MAIN)
    def _():
        for k in range(NB_PROJ):
            main_ref[k] = res[:, k * LANES:(k + 1) * LANES].astype(BF16)

    @pl.when(j >= NJ_MAIN)
    def _():
        for k in range(NB_PROJ):
            res_scr[k] = res[:, k * LANES:(k + 1) * LANES]

    for out_ref, dil, first in ((d4_ref, DIL_DILATIONS[1], NJ_MAIN), (d16_ref, DIL_DILATIONS[2], NJ_MAIN + NJ_DIL)):
        @pl.when((j >= first) & (j < first + NJ_DIL))
        def _(out_ref=out_ref, dil=dil):
            for k in range(NB_PROJ):
                for r in range(dil):
                    rows = res_scr[k, pl.ds(r, tm // dil, stride=dil), :]
                    out_ref[k, :, r * LANES:(r + 1) * LANES] = rows.astype(BF16)


def _proj(x, g, w):
    m = x.shape[0]
    tm = TM_PROJ
    d4, d16 = DIL_DILATIONS[1], DIL_DILATIONS[2]
    return pl.pallas_call(
        functools.partial(_proj_kernel, tm=tm),
        grid=(m // tm, NJ_MAIN + 2 * NJ_DIL),
        in_specs=[
            pl.BlockSpec((tm, D), lambda i, j: (i, 0)),
            pl.BlockSpec((1, D), lambda i, j: (0, 0)),
            pl.BlockSpec((D, TN_PROJ), lambda i, j: (0, j)),
        ],
        out_specs=[
            pl.BlockSpec((tm, D), lambda i, j: (i, 0)),
            pl.BlockSpec((NB_PROJ, tm, LANES), lambda i, j: (jnp.minimum(j, NJ_MAIN - 1), i, 0)),
            pl.BlockSpec((NB_PROJ, tm // d4, d4 * LANES),
                         lambda i, j: (jnp.clip(j - NJ_MAIN, 0, NJ_DIL - 1), i, 0)),
            pl.BlockSpec((NB_PROJ, tm // d16, d16 * LANES),
                         lambda i, j: (jnp.clip(j - NJ_MAIN - NJ_DIL, 0, NJ_DIL - 1), i, 0)),
        ],
        out_shape=[jax.ShapeDtypeStruct((m, D), BF16),
                   jax.ShapeDtypeStruct((N_BLK_MAIN, m, LANES), BF16),
                   jax.ShapeDtypeStruct((N_BLK_DIL, m // d4, d4 * LANES), BF16),
                   jax.ShapeDtypeStruct((N_BLK_DIL, m // d16, d16 * LANES), BF16)],
        scratch_shapes=[pltpu.VMEM((NB_PROJ, tm, LANES), F32)],
        compiler_params=_params(("parallel", "arbitrary")),
        name="proj",
    )(x, g, w)


def _na_kernel(q_ref, k_ref, v_ref, b_ref, o_ref, *, rows):
    nq = RQ_NA * GRID_W
    nk = NA_UROWS * GRID_W
    row0 = jnp.clip(RQ_NA * pl.program_id(1) - NA_ROWS // 2, 0, rows - NA_UROWS)
    start = pl.multiple_of(row0 * GRID_W, GRID_W)
    lo = _half_masks(nq)
    for hp in range(NA_HEADS // 2):
        q = q_ref[hp]
        k = k_ref[hp, pl.ds(start, nk), :]
        v_ext = _with_ones(v_ref[hp, pl.ds(start, nk), :])
        lhs = jnp.concatenate([_keep_half(q, lo, 0), _keep_half(q, lo, 1)], axis=0)
        o, l, _ = _softmax_pv(_dot_nt(lhs, k) + b_ref[0, hp], v_ext)
        o = o / l
        o_ref[:, hp * LANES:(hp + 1) * LANES] = jnp.where(lo, o[:nq], o[nq:]).astype(BF16)


def _na(qkv, bias, n_seq, t, row_off, prev):
    rows = t // GRID_W
    nb = rows // RQ_NA
    nq = RQ_NA * GRID_W
    blk0 = row_off // nq
    seq0 = row_off // t
    group_type = lambda g: jnp.where(g == 0, 0, jnp.where(g == nb - 1, 2, 1))
    return _call_into(
        functools.partial(_na_kernel, rows=rows), prev, 4,
        grid=(n_seq, nb),
        in_specs=[
            pl.BlockSpec((4, nq, LANES), lambda s, g: (BLK_NA_Q // 4, blk0 + s * nb + g, 0)),
            pl.BlockSpec((4, t, LANES), lambda s, g: (BLK_NA_K // 4, seq0 + s, 0)),
            pl.BlockSpec((4, t, LANES), lambda s, g: (BLK_NA_V // 4, seq0 + s, 0)),
            pl.BlockSpec((1, NA_HEADS // 2, 2 * nq, NA_UROWS * GRID_W), lambda s, g: (group_type(g), 0, 0, 0)),
        ],
        out_specs=pl.BlockSpec((nq, BRANCH_W), lambda s, g: (blk0 + s * nb + g, 0)),
        out_shape=jax.ShapeDtypeStruct((qkv.shape[1], BRANCH_W), BF16),
        compiler_params=_params(("parallel", "arbitrary")),
        name="mixer_a",
    )(qkv, qkv, qkv, bias)


def _na_bias_table(rpb):
    c = np.arange(GRID_W)[:, None]
    kc = np.arange(GRID_W)[None, :]
    cstart = np.clip(c - NA_COLS // 2, 0, GRID_W - NA_COLS)
    ok = (kc >= cstart) & (kc < cstart + NA_COLS)
    pad = GRID_W - NA_COLS
    padded = jnp.pad(rpb.astype(F32) * LOG2E, ((0, 0), (0, 0), (pad, pad)))
    e = jnp.stack([padded[..., GRID_W - 1 - q:2 * GRID_W - 1 - q] for q in range(GRID_W)], axis=2)
    e = jnp.where(ok[None, None], e, NEG_INF)
    neg = jnp.full((NA_HEADS, GRID_W, GRID_W), NEG_INF, F32)
    half = NA_ROWS // 2
    group_types = (
        [(-i, -i) for i in range(RQ_NA)],
        [(-half - i, -half) for i in range(RQ_NA)],
        [(-(NA_UROWS - RQ_NA) - i, -(NA_ROWS - RQ_NA) - i) for i in range(RQ_NA)],
    )
    tables = []
    for rel in group_types:
        q_rows = []
        for u0, w0 in rel:
            blocks = [e[:, u0 + j + NA_ROWS - 1] if w0 <= u0 + j < w0 + NA_ROWS else neg for j in range(NA_UROWS)]
            q_rows.append(jnp.concatenate(blocks, axis=2))
        tables.append(jnp.concatenate(q_rows, axis=1))
    return jnp.stack(tables).reshape(3, NA_HEADS // 2, 2 * RQ_NA * GRID_W, NA_UROWS * GRID_W)


def _diff_kernel(lam_ref, q_ref, k_ref, v_ref, tab_ref, g_ref, o_ref, vext_scr, *, t, tq, out_scale):
    qi = pl.program_id(2)

    @pl.when(qi == 0)
    def _():
        vext_scr[:, :LANES] = v_ref[0]
        vext_scr[:, LANES:] = jnp.ones((t, LANES), BF16)

    lam = lam_ref[0]
    off = pl.multiple_of((t // tq - 1 - qi) * tq, tq)
    bias = tab_ref[0, :, pl.ds(off, t)]
    q = q_ref[0]
    lo = _half_masks(tq)
    k = k_ref[0]
    v_ext = vext_scr[...]

    scores = [_dot_nt(_keep_half(q, lo, half), k) - bias for half in range(2)]
    probs = [jnp.exp2((s - jnp.max(s, axis=-1, keepdims=True)).astype(BF16)) for s in scores]
    outs = [_dot(p, v_ext) for p in probs]
    o1, o2 = (ol[:, :LANES] / ol[:, LANES:] for ol in outs)
    o = o1 - lam * o2
    o_ref[...] = (_rms(o, g_ref[...], 1e-5) * out_scale).astype(BF16)


def _diff_bias_table(t, tq):
    r = lax.broadcasted_iota(jnp.int32, (tq, 2 * t - tq), 0)
    x = lax.broadcasted_iota(jnp.int32, (tq, 2 * t - tq), 1)
    dist = jnp.abs(r - x + (t - tq)).astype(F32)
    return jnp.asarray([s * LOG2E for s in SLOPES_DIFF], F32)[:, None, None] * dist[None]


def _diff(qkv, lam, subln_g, n_seq, t, row_off, lam_init, prev):
    tq = TQ_DIFF
    nq = t // tq
    blk0 = row_off // tq
    seq0 = row_off // t
    return _call_into(
        functools.partial(_diff_kernel, t=t, tq=tq, out_scale=1.0 - lam_init), prev, 6,
        grid=(DIFF_HEADS, n_seq, nq),
        in_specs=[
            pl.BlockSpec(memory_space=pltpu.SMEM),
            pl.BlockSpec((1, tq, LANES), lambda h, s, i: (BLK_DF_Q + h, blk0 + s * nq + i, 0)),
            pl.BlockSpec((1, t, LANES), lambda h, s, i: (BLK_DF_K + h, seq0 + s, 0)),
            pl.BlockSpec((1, t, LANES), lambda h, s, i: (BLK_DF_V + h, seq0 + s, 0)),
            pl.BlockSpec((1, tq, 2 * t - tq), lambda h, s, i: (h, 0, 0)),
            pl.BlockSpec((1, DIFF_DV), lambda h, s, i: (0, 0)),
        ],
        out_specs=pl.BlockSpec((tq, LANES), lambda h, s, i: (blk0 + s * nq + i, h)),
        out_shape=jax.ShapeDtypeStruct((qkv.shape[1], BRANCH_W), BF16),
        scratch_shapes=[pltpu.VMEM((t, 2 * LANES), BF16)],
        compiler_params=_params(("parallel", "arbitrary", "arbitrary")),
        name="mixer_b",
    )(lam, qkv, qkv, qkv, _diff_bias_table(t, tq), subln_g)


def _band_window(qi, tq, kw, radius, length):
    start = jnp.clip(qi * tq - radius, 0, length - kw)
    rel = (lax.broadcasted_iota(jnp.int32, (tq, kw), 0) + (qi * tq - start)) \
        - lax.broadcasted_iota(jnp.int32, (tq, kw), 1)
    dist = jnp.abs(rel)
    return pl.multiple_of(start, 64), jnp.where(dist <= radius, dist.astype(F32), MASKED_DIST)


def _with_ones(v):
    return jnp.concatenate([v, jnp.ones(v.shape, v.dtype)], axis=1)


def _softmax_pv(s, v_ext, floor=None):
    m = jnp.max(s, axis=-1, keepdims=True)
    if floor is not None:
        m = jnp.maximum(m, floor)
    ol = _dot(jnp.exp2((s - m).astype(BF16)), v_ext)
    return ol[:, :LANES], ol[:, LANES:], m


def _dil_kernel(q_ref, k_ref, v_ref, o_ref, lse_ref, *, n_sub, tq, kw, radius, slopes, scale, nqb, nrb):
    for b in range(nqb):
        start, dist = _band_window(pl.program_id(2) * nqb + b, tq, kw, radius, n_sub)
        rows = slice(b * tq, (b + 1) * tq)
        for rr in range(nrb):
            lanes = slice(rr * LANES, (rr + 1) * LANES)
            for h in range(DIL_HEADS):
                k = k_ref[h, pl.ds(start, kw), lanes]
                v_ext = _with_ones(v_ref[h, pl.ds(start, kw), lanes])
                s = _dot_nt(q_ref[h, rows, lanes], k) * scale - slopes[h] * dist
                o, l, m = _softmax_pv(s, v_ext)
                o_ref[h, rows, lanes] = (o / l).astype(BF16)
                lse_ref[h, rows, lanes] = (m + jnp.log2(l)) * (1.0 / LOG2E)


def _dil_group(view, gi, n_seq, t, row_off, prev):
    dil = DIL_DILATIONS[gi]
    radius = DIL_WINDOWS[gi] // (2 * dil)
    n_sub = t // dil
    tq = min(TQ_DIL, n_sub)
    kw = min(tq + 2 * radius, n_sub)
    nqb = min(n_sub // tq, UNITS_DIL)
    nrb = min(dil, UNITS_DIL // nqb)
    nq = n_sub // (tq * nqb)
    blk0 = row_off // dil // (tq * nqb)
    seq0 = row_off // t
    qb, kb, vb = (BLK_DL_Q // 4, BLK_DL_K // 4, BLK_DL_V // 4) if gi == 0 else (0, 1, 2)
    slopes = tuple(SLOPES_DIL[gi * DIL_HEADS + h] * dil * LOG2E for h in range(DIL_HEADS))
    rows_out = view.shape[1]
    q_idx = lambda s, r, i: (qb, blk0 + s * nq + i, r)
    o_idx = lambda s, r, i: (0, blk0 + s * nq + i, r)
    o, lse = _call_into(
        functools.partial(_dil_kernel, n_sub=n_sub, tq=tq, kw=kw, radius=radius, slopes=slopes,
                          scale=DIL_DH ** -0.5 * LOG2E, nqb=nqb, nrb=nrb), prev, 3,
        grid=(n_seq, dil // nrb, nq),
        in_specs=[
            pl.BlockSpec((4, nqb * tq, nrb * LANES), q_idx),
            pl.BlockSpec((4, n_sub, nrb * LANES), lambda s, r, i: (kb, seq0 + s, r)),
            pl.BlockSpec((4, n_sub, nrb * LANES), lambda s, r, i: (vb, seq0 + s, r)),
        ],
        out_specs=[pl.BlockSpec((4, nqb * tq, nrb * LANES), o_idx),
                   pl.BlockSpec((4, nqb * tq, nrb * LANES), o_idx)],
        out_shape=[jax.ShapeDtypeStruct((DIL_HEADS, rows_out, dil * LANES), BF16),
                   jax.ShapeDtypeStruct((DIL_HEADS, rows_out, dil * LANES), F32)],
        compiler_params=_params(("parallel", "arbitrary", "arbitrary")),
        name=f"mixer_c{gi}",
    )(view, view, view)
    return o, lse


def _dil_combine_kernel(o0, o1, o2, l0, l1, l2, out_ref, o_scr, l_scr, *, tm):
    for h in range(DIL_HEADS):
        for gi, (o_ref, l_ref) in ((1, (o1, l1)), (2, (o2, l2))):
            dil = DIL_DILATIONS[gi]
            for r in range(dil):
                rows = pl.ds(r, tm // dil, stride=dil)
                o_scr[gi - 1, rows, :] = o_ref[h, :, r * LANES:(r + 1) * LANES].astype(F32)
                l_scr[gi - 1, rows, :] = l_ref[h, :, r * LANES:(r + 1) * LANES]
        a0, a1, a2 = l0[h], l_scr[0], l_scr[1]
        mx = jnp.maximum(jnp.maximum(a0, a1), a2)
        e0, e1, e2 = jnp.exp(a0 - mx), jnp.exp(a1 - mx), jnp.exp(a2 - mx)
        num = e0 * o0[h].astype(F32) + e1 * o_scr[0] + e2 * o_scr[1]
        out_ref[:, h * LANES:(h + 1) * LANES] = (num / (e0 + e1 + e2)).astype(BF16)


def _dil_combine(outs, lses):
    m = outs[0].shape[1]
    tm = min(TM_COMBINE, m)
    specs = [pl.BlockSpec((DIL_HEADS, tm // dil, dil * LANES), lambda i: (0, i, 0)) for dil in DIL_DILATIONS]
    return pl.pallas_call(
        functools.partial(_dil_combine_kernel, tm=tm),
        grid=(m // tm,),
        in_specs=specs * 2,
        out_specs=pl.BlockSpec((tm, BRANCH_W), lambda i: (i, 0)),
        out_shape=jax.ShapeDtypeStruct((m, BRANCH_W), BF16),
        scratch_shapes=[pltpu.VMEM((2, tm, LANES), F32), pltpu.VMEM((2, tm, LANES), F32)],
        compiler_params=_params(("parallel",)),
        name="mixer_c_combine",
    )(*outs, *lses)


def _swa_kernel(sink_ref, q_ref, k_ref, v_ref, o_ref, *, t, tq, kw):
    qi = pl.program_id(1)
    start, dist = _band_window(qi, tq, kw, SWA_RADIUS, t)
    k = k_ref[0, pl.ds(start, kw), :]
    v_ext = _with_ones(v_ref[0, pl.ds(start, kw), :])
    lo = _half_masks(tq)
    for g in range(SWA_GROUP):
        q = q_ref[g]
        outs = []
        for hk in range(SWA_KVH):
            head = hk * SWA_GROUP + g
            sink = sink_ref[head] * LOG2E
            s = _dot_nt(_keep_half(q, lo, hk), k) - (SLOPES_SWA[head] * LOG2E) * dist
            o, l, m = _softmax_pv(s, v_ext, floor=sink)
            outs.append(o / (l + jnp.exp2(sink - m)))
        o_ref[:, g * LANES:(g + 1) * LANES] = jnp.where(lo, outs[0], outs[1]).astype(BF16)


def _swa(qkv, sink, n_seq, t, row_off, prev):
    tq = TQ_SWA
    kw = min(tq + 2 * SWA_RADIUS, t)
    nq = t // tq
    blk0 = row_off // tq
    seq0 = row_off // t
    return _call_into(
        functools.partial(_swa_kernel, t=t, tq=tq, kw=kw), prev, 4,
        grid=(n_seq, nq),
        in_specs=[
            pl.BlockSpec(memory_space=pltpu.SMEM),
            pl.BlockSpec((4, tq, LANES), lambda s, i: (BLK_SW_Q // 4, blk0 + s * nq + i, 0)),
            pl.BlockSpec((1, t, LANES), lambda s, i: (BLK_SW_K, seq0 + s, 0)),
            pl.BlockSpec((1, t, LANES), lambda s, i: (BLK_SW_V, seq0 + s, 0)),
        ],
        out_specs=pl.BlockSpec((tq, BRANCH_W), lambda s, i: (blk0 + s * nq + i, 0)),
        out_shape=jax.ShapeDtypeStruct((qkv.shape[1], BRANCH_W), BF16),
        compiler_params=_params(("parallel", "arbitrary")),
        name="mixer_d",
    )(sink, qkv, qkv, qkv)


def _merge_kernel(h_ref, oa_ref, ob_ref, oc_ref, od_ref, wg0_ref, wg1_ref, wg2_ref, wg3_ref,
                  bg_ref, wb_ref, wo_ref, xres_ref, o_ref, m_scr, *, nc1, tc):
    j = pl.program_id(1)

    @pl.when(j < nc1)
    def _():
        h = h_ref[...]
        acc = None
        branches = ((oa_ref, wg0_ref), (ob_ref, wg1_ref), (oc_ref, wg2_ref), (od_ref, wg3_ref))
        for n, (b_ref, wg_ref) in enumerate(branches):
            gate = jax.nn.sigmoid(_dot(h, wg_ref[...]) + bg_ref[n:n + 1, :])
            term = gate * _dot(b_ref[...], wb_ref[n])
            acc = term if acc is None else acc + term
        m_scr[j] = acc.astype(BF16)

    @pl.when(j >= nc1)
    def _():
        acc = _dot(m_scr[0], wo_ref[0:tc, :])
        for c in range(1, nc1):
            acc = acc + _dot(m_scr[c], wo_ref[c * tc:(c + 1) * tc, :])
        o_ref[...] = xres_ref[...] + acc


def _merge(x, h, branches, w_gate, b_gate, w_branch, w_out):
    m = x.shape[0]
    tm, tc, tn = TM_MERGE, TC_MERGE, TN_MERGE
    nc1 = D // tc
    nc2 = D // tn
    first = lambda j: jnp.minimum(j, nc1 - 1)
    col = lambda i, j: (i, jnp.maximum(j - nc1, 0))
    row_tile = lambda i, j: (i, 0)
    gate_specs = [pl.BlockSpec((D, tc), functools.partial(lambda i, j, n: (0, n * nc1 + first(j)), n=n))
                  for n in range(4)]
    return pl.pallas_call(
        functools.partial(_merge_kernel, nc1=nc1, tc=tc),
        grid=(m // tm, nc1 + nc2),
        in_specs=[pl.BlockSpec((tm, D), row_tile)]
        + [pl.BlockSpec((tm, BRANCH_W), row_tile)] * 4
        + gate_specs
        + [
            pl.BlockSpec((4, tc), lambda i, j: (0, first(j))),
            pl.BlockSpec((4, BRANCH_W, tc), lambda i, j: (0, 0, first(j))),
            pl.BlockSpec((D, tn), lambda i, j: (0, jnp.maximum(j - nc1, 0))),
            pl.BlockSpec((tm, tn), col),
        ],
        out_specs=pl.BlockSpec((tm, tn), col),
        out_shape=jax.ShapeDtypeStruct((m, D), F32),
        scratch_shapes=[pltpu.VMEM((nc1, tm, tc), BF16)],
        compiler_params=_params(("parallel", "arbitrary")),
        name="merge",
    )(h, *branches, w_gate, w_gate, w_gate, w_gate, b_gate, w_branch, w_out, x)


def _cast_ffn_in_kernel(w_ref, o_ref):
    o_ref[0, :, :D_FF] = w_ref[...].astype(BF16)
    o_ref[0, :, D_FF:] = jnp.zeros((o_ref.shape[1], FFN_FP - D_FF), BF16)


def _cast_ffn_in(w_in, l):
    tr = 256
    return pl.pallas_call(
        _cast_ffn_in_kernel,
        grid=(2, D // tr),
        in_specs=[pl.BlockSpec((None, tr, D_FF), lambda part, i: (l, i, part))],
        out_specs=pl.BlockSpec((1, tr, FFN_FP), lambda part, i: (part, i, 0)),
        out_shape=jax.ShapeDtypeStruct((2, D, FFN_FP), BF16),
        compiler_params=_params(("parallel", "parallel")),
        name="cast_ffn_in",
    )(w_in)


def _cast_ffn_out_kernel(w_ref, o_ref, *, tr):
    row = pl.program_id(0) * tr + lax.broadcasted_iota(jnp.int32, (tr, D), 0)
    o_ref[...] = jnp.where(row < D_FF, w_ref[...], 0.0).astype(BF16)


def _cast_ffn_out(w_out, l):
    tr = 512
    return pl.pallas_call(
        functools.partial(_cast_ffn_out_kernel, tr=tr),
        grid=(FFN_FP // tr,),
        in_specs=[pl.BlockSpec((None, tr, D), lambda i: (l, i, 0))],
        out_specs=pl.BlockSpec((tr, D), lambda i: (i, 0)),
        out_shape=jax.ShapeDtypeStruct((FFN_FP, D), BF16),
        compiler_params=_params(("parallel",)),
        name="cast_ffn_out",
    )(w_out)


def _prep_ffn(w_in, w_out, l):
    return _cast_ffn_in(w_in, l), _cast_ffn_out(w_out, l)


def _swa_pair_heads(w, axis):
    shape = w.shape
    split = shape[:axis] + (SWA_KVH, SWA_GROUP, SWA_DH) + shape[axis + 1:]
    return jnp.swapaxes(w.reshape(split), axis, axis + 1).reshape(shape)


def _proj_columns(w_in):
    scale = np.ones((IN_WIDTH,), np.float32)
    scale[COL_NA:COL_NA + NA_HEADS * NA_DH] = NA_DH ** -0.5 * LOG2E
    scale[COL_DF:COL_DF + DIFF_HEADS * 2 * DIFF_DK] = DIFF_DK ** -0.5 * LOG2E
    scale[COL_SW:COL_SW + SWA_QH * SWA_DH] = SWA_DH ** -0.5 * LOG2E
    w = w_in * scale
    gw = DIL_HEADS * DIL_DH

    def dil_group(gi):
        return [w[:, COL_DL + part * DIL_W + gi * gw:COL_DL + part * DIL_W + (gi + 1) * gw] for part in range(3)]

    sw_q_end = COL_SW + SWA_QH * SWA_DH
    cols = [w[:, :COL_DL]] + dil_group(0) + [_swa_pair_heads(w[:, COL_SW:sw_q_end], 1), w[:, sw_q_end:]] \
        + dil_group(1) + dil_group(2)
    return jnp.concatenate(cols, axis=1)


def _prep_w_in(w_in):
    return _proj_columns(w_in).astype(BF16)


def _encoder_layer(x, l, seqs, p, final_gain=None):
    ffn1 = (p["norm_ffn1"][l][None], *_prep_ffn(p["w_ffn1_in"], p["w_ffn1_out"], l))
    if isinstance(x, tuple):
        m_total = sum(xb.shape[0] for xb in x)
        out, row_off = None, 0
        for xb in x:
            out = _ffn(xb, *ffn1, m_out=m_total, out_off=row_off, prev=None if out is None else (out,))
            row_off += xb.shape[0]
        x = out
    else:
        x = _ffn(x, *ffn1)
    h, qkv, qkv_d4, qkv_d16 = _proj(x, p["norm_mix"][l][None], _prep_w_in(p["w_in"][l]))
    dil_views = (qkv, qkv_d4, qkv_d16)

    lam_init = 0.8 - 0.6 * math.exp(-0.3 * l)
    lv = p["diff_lambda"][l].astype(F32)
    lam = jnp.exp(jnp.sum(lv[0] * lv[1])) - jnp.exp(jnp.sum(lv[2] * lv[3])) + lam_init
    diff_lam = lam.reshape(1)
    na_bias = _na_bias_table(p["na_rpb"][l])
    subln = p["diff_subln"][l].astype(F32)[None]
    sink = p["swa_sink"][l].astype(F32)

    o_a = o_b = o_d = None
    dil = [None] * DIL_GROUPS
    for n_seq, t, row_off in seqs:
        o_a = _na(qkv, na_bias, n_seq, t, row_off, None if o_a is None else (o_a,))
        o_b = _diff(qkv, diff_lam, subln, n_seq, t, row_off, lam_init, None if o_b is None else (o_b,))
        for gi in range(DIL_GROUPS):
            dil[gi] = _dil_group(dil_views[gi], gi, n_seq, t, row_off, dil[gi])
        o_d = _swa(qkv, sink, n_seq, t, row_off, None if o_d is None else (o_d,))
    o_c = _dil_combine([o for o, _ in dil], [lse for _, lse in dil])
    branches = [o_a, o_b, o_c, o_d]

    w_branch = p["w_branch"][l]
    w_branch = jnp.concatenate([w_branch[:3], _swa_pair_heads(w_branch[3:], 1)], axis=0).astype(BF16)
    x = _merge(x, h, branches, p["w_gate"][l].astype(BF16), p["b_gate"][l].astype(F32).reshape(4, D),
               w_branch, p["w_out"][l].astype(BF16))
    ffn2 = (p["norm_ffn2"][l][None], *_prep_ffn(p["w_ffn2_in"], p["w_ffn2_out"], l))
    if final_gain is None:
        return _ffn(x, *ffn2)
    return tuple(_ffn(x, *ffn2, rows=(row_off, n_seq * t), final_gain=final_gain) for n_seq, t, row_off in seqs)


def kernel(x_prompt, x_sample, norm_ffn1, w_ffn1_in, w_ffn1_out, norm_mix, w_in, na_rpb, diff_lambda, diff_subln,
           swa_sink, w_branch, w_gate, b_gate, w_out, norm_ffn2, w_ffn2_in, w_ffn2_out, norm_final):
    p = dict(norm_ffn1=norm_ffn1, w_ffn1_in=w_ffn1_in, w_ffn1_out=w_ffn1_out, norm_mix=norm_mix, w_in=w_in,
             na_rpb=na_rpb, diff_lambda=diff_lambda, diff_subln=diff_subln, swa_sink=swa_sink,
             w_branch=w_branch, w_gate=w_gate, b_gate=b_gate, w_out=w_out, norm_ffn2=norm_ffn2,
             w_ffn2_in=w_ffn2_in, w_ffn2_out=w_ffn2_out)
    bp, tp, _ = x_prompt.shape
    bs, ts, _ = x_sample.shape
    mp = bp * tp
    x = (x_prompt.reshape(mp, D).astype(F32), x_sample.reshape(bs * ts, D).astype(F32))
    seqs = ((bp, tp, 0), (bs, ts, mp))
    for l in range(DEPTH):
        x = _encoder_layer(x, l, seqs, p, final_gain=norm_final.astype(F32)[None] if l == DEPTH - 1 else None)
    y_prompt, y_sample = x
    return y_prompt.reshape(bp, tp, D), y_sample.reshape(bs, ts, D)
```

```python
import functools
import math

import jax
import jax.numpy as jnp
import numpy as np
from jax import lax
from jax.experimental import pallas as pl
from jax.experimental.pallas import tpu as pltpu

F32 = jnp.float32
BF16 = jnp.bfloat16

D = 2048
DEPTH = 2
GRID_W = 64
NA_HEADS, NA_DH, NA_ROWS, NA_COLS = 8, 64, 8, 16
DIFF_HEADS, DIFF_DK, DIFF_DV = 4, 64, 128
DIL_WINDOWS, DIL_DILATIONS = (128, 512, 2048), (1, 4, 16)
DIL_GROUPS, DIL_HEADS, DIL_DH = 3, 4, 128
SWA_QH, SWA_KVH, SWA_DH, SWA_RADIUS = 8, 2, 64, 128
SWA_GROUP = SWA_QH // SWA_KVH
BRANCH_W = 512
D_FF = ((8 * D // 3 + 127) // 128) * 128
N_ALIBI = SWA_QH + DIL_GROUPS * DIL_HEADS + DIFF_HEADS
RMS_EPS = 1e-6
NEG_INF = -1e30
LOG2E = math.log2(math.e)
MASKED_DIST = 1e32

LANES = 128
IN_WIDTH = 3 * NA_HEADS * NA_DH + 2 * DIFF_HEADS * 2 * DIFF_DK + DIFF_HEADS * DIFF_DV \
    + 3 * DIL_GROUPS * DIL_HEADS * DIL_DH + SWA_QH * SWA_DH + 2 * SWA_KVH * SWA_DH
N_BLK = IN_WIDTH // LANES
BLK_NA_Q, BLK_NA_K, BLK_NA_V = 0, 4, 8
BLK_DF_Q, BLK_DF_K, BLK_DF_V = 12, 16, 20
BLK_DL_Q, BLK_DL_K, BLK_DL_V = 24, 28, 32
BLK_SW_Q, BLK_SW_K, BLK_SW_V = 36, 40, 41
N_BLK_MAIN = 42
N_BLK_DIL = 3 * DIL_HEADS
COL_NA, COL_DF, COL_DL, COL_SW = 0, 1536, 3072, 7680
DIL_W = DIL_GROUPS * DIL_HEADS * DIL_DH

FFN_TF = 512
FFN_FP = -(-D_FF // FFN_TF) * FFN_TF
FFN_NC1 = FFN_FP // FFN_TF
FFN_TN = 512
TM_FFN = 1024
TM_PROJ = 1024
NB_PROJ = 6
TN_PROJ = NB_PROJ * LANES
NJ_MAIN = N_BLK_MAIN // NB_PROJ
NJ_DIL = N_BLK_DIL // NB_PROJ
TM_MERGE = 1024
TC_MERGE = 256
TM_COMBINE = 1024
RQ_NA = 4
NA_UROWS = 12
TQ_DIFF = 256
TQ_SWA = 256
TQ_DIL = 128
UNITS_DIL = 4
VMEM_LIMIT = 60 * 1024 * 1024

_ALIBI = [2.0 ** (-8.0 * (i + 1) / N_ALIBI) for i in range(N_ALIBI)]
SLOPES_SWA = _ALIBI[:SWA_QH]
SLOPES_DIL = _ALIBI[SWA_QH:SWA_QH + DIL_GROUPS * DIL_HEADS]
SLOPES_DIFF = _ALIBI[SWA_QH + DIL_GROUPS * DIL_HEADS:]


def _params(sem):
    return pltpu.CompilerParams(dimension_semantics=sem, vmem_limit_bytes=VMEM_LIMIT)


def _call_into(body, prev, n_in, **kw):
    if prev is None:
        return pl.pallas_call(body, **kw)
    prev = tuple(prev)

    def aliased(*refs):
        body(*refs[:n_in], *refs[n_in + len(prev):])

    kw["in_specs"] = list(kw["in_specs"]) + [pl.BlockSpec(memory_space=pl.ANY)] * len(prev)
    call = pl.pallas_call(aliased, input_output_aliases={n_in + i: i for i in range(len(prev))}, **kw)
    return lambda *args: call(*args, *prev)


def _rms(x, g, eps):
    ms = jnp.mean(x * x, axis=-1, keepdims=True)
    return x * lax.rsqrt(ms + eps) * g


def _dot(a, b):
    return jnp.dot(a, b, preferred_element_type=F32)


def _dot_nt(a, b):
    return lax.dot_general(a, b, (((1,), (1,)), ((), ())), preferred_element_type=F32)


def _half_masks(rows):
    lane = lax.broadcasted_iota(jnp.int32, (rows, LANES), 1)
    return lane < (LANES // 2)


def _keep_half(x, lo, half):
    keep = lo if half == 0 else jnp.logical_not(lo)
    return jnp.where(keep, x.astype(F32), 0.0).astype(BF16)


def _ffn_kernel(*refs, final_norm):
    x_ref, g_ref, wg_ref, wu_ref, w2_ref = refs[:5]
    o_ref, h_scr = refs[-2:]
    j = pl.program_id(1)

    @pl.when(j == 0)
    def _():
        x = x_ref[...]
        h_scr[...] = _rms(x, g_ref[...], RMS_EPS).astype(BF16)
        o_ref[...] = x

    h = h_scr[...]
    gate = _dot(h, wg_ref[...])
    up = _dot(h, wu_ref[...])
    act = (gate * jax.nn.sigmoid(gate) * (0.5 * up)).astype(BF16)
    for n in range(D // FFN_TN):
        cols = slice(n * FFN_TN, (n + 1) * FFN_TN)
        o_ref[:, cols] += _dot(act, w2_ref[:, cols])

    if final_norm:
        @pl.when(j == FFN_NC1 - 1)
        def _():
            o_ref[...] = _rms(o_ref[...], refs[5][...], RMS_EPS)


def _ffn(x, g, w1, w2, rows=None, m_out=None, out_off=0, prev=None, final_gain=None):
    start, m = (0, x.shape[0]) if rows is None else rows
    m_out = m if m_out is None else m_out
    tm = TM_FFN
    in0, out0 = start // tm, out_off // tm
    inputs = [x, g, w1, w1, w2] + ([] if final_gain is None else [final_gain])
    in_specs = [
        pl.BlockSpec((tm, D), lambda i, j: (in0 + i, 0)),
        pl.BlockSpec((1, D), lambda i, j: (0, 0)),
        pl.BlockSpec((None, D, FFN_TF), lambda i, j: (0, 0, j)),
        pl.BlockSpec((None, D, FFN_TF), lambda i, j: (1, 0, j)),
        pl.BlockSpec((FFN_TF, D), lambda i, j: (j, 0)),
    ] + ([] if final_gain is None else [pl.BlockSpec((1, D), lambda i, j: (0, 0))])
    return _call_into(
        functools.partial(_ffn_kernel, final_norm=final_gain is not None), prev, len(inputs),
        grid=(m // tm, FFN_NC1),
        in_specs=in_specs,
        out_specs=pl.BlockSpec((tm, D), lambda i, j: (out0 + i, 0)),
        out_shape=jax.ShapeDtypeStruct((m_out, D), F32),
        scratch_shapes=[pltpu.VMEM((tm, D), BF16)],
        compiler_params=_params(("parallel", "arbitrary")),
        name="ffn",
    )(*inputs)


def _proj_kernel(x_ref, g_ref, w_ref, h_ref, main_ref, d4_ref, d16_ref, res_scr, *, tm):
    j = pl.program_id(1)

    @pl.when(j == 0)
    def _():
        h_ref[...] = _rms(x_ref[...], g_ref[...], RMS_EPS).astype(BF16)

    res = _dot(h_ref[...], w_ref[...])

    @pl.when(j < NJ_MAIN)
    def _():
        for k in range(NB_PROJ):
            main_ref[k] = res[:, k * LANES:(k + 1) * LANES].astype(BF16)

    @pl.when(j >= NJ_MAIN)
    def _():
        for k in range(NB_PROJ):
            res_scr[k] = res[:, k * LANES:(k + 1) * LANES]

    for out_ref, dil, first in ((d4_ref, DIL_DILATIONS[1], NJ_MAIN), (d16_ref, DIL_DILATIONS[2], NJ_MAIN + NJ_DIL)):
        @pl.when((j >= first) & (j < first + NJ_DIL))
        def _(out_ref=out_ref, dil=dil):
            for k in range(NB_PROJ):
                for r in range(dil):
                    rows = res_scr[k, pl.ds(r, tm // dil, stride=dil), :]
                    out_ref[k, :, r * LANES:(r + 1) * LANES] = rows.astype(BF16)


def _proj(x, g, w):
    m = x.shape[0]
    tm = TM_PROJ
    d4, d16 = DIL_DILATIONS[1], DIL_DILATIONS[2]
    return pl.pallas_call(
        functools.partial(_proj_kernel, tm=tm),
        grid=(m // tm, NJ_MAIN + 2 * NJ_DIL),
        in_specs=[
            pl.BlockSpec((tm, D), lambda i, j: (i, 0)),
            pl.BlockSpec((1, D), lambda i, j: (0, 0)),
            pl.BlockSpec((D, TN_PROJ), lambda i, j: (0, j)),
        ],
        out_specs=[
            pl.BlockSpec((tm, D), lambda i, j: (i, 0)),
            pl.BlockSpec((NB_PROJ, tm, LANES), lambda i, j: (jnp.minimum(j, NJ_MAIN - 1), i, 0)),
            pl.BlockSpec((NB_PROJ, tm // d4, d4 * LANES),
                         lambda i, j: (jnp.clip(j - NJ_MAIN, 0, NJ_DIL - 1), i, 0)),
            pl.BlockSpec((NB_PROJ, tm // d16, d16 * LANES),
                         lambda i, j: (jnp.clip(j - NJ_MAIN - NJ_DIL, 0, NJ_DIL - 1), i, 0)),
        ],
        out_shape=[jax.ShapeDtypeStruct((m, D), BF16),
                   jax.ShapeDtypeStruct((N_BLK_MAIN, m, LANES), BF16),
                   jax.ShapeDtypeStruct((N_BLK_DIL, m // d4, d4 * LANES), BF16),
                   jax.ShapeDtypeStruct((N_BLK_DIL, m // d16, d16 * LANES), BF16)],
        scratch_shapes=[pltpu.VMEM((NB_PROJ, tm, LANES), F32)],
        compiler_params=_params(("parallel", "arbitrary")),
        name="proj",
    )(x, g, w)


def _na_kernel(q_ref, k_ref, v_ref, b_ref, o_ref, *, rows):
    nq = RQ_NA * GRID_W
    nk = NA_UROWS * GRID_W
    row0 = jnp.clip(RQ_NA * pl.program_id(1) - NA_ROWS // 2, 0, rows - NA_UROWS)
    start = pl.multiple_of(row0 * GRID_W, GRID_W)
    lo = _half_masks(nq)
    for hp in range(NA_HEADS // 2):
        q = q_ref[hp]
        k = k_ref[hp, pl.ds(start, nk), :]
        v_ext = _with_ones(v_ref[hp, pl.ds(start, nk), :])
        lhs = jnp.concatenate([_keep_half(q, lo, 0), _keep_half(q, lo, 1)], axis=0)
        o, l, _ = _softmax_pv(_dot_nt(lhs, k) + b_ref[0, hp], v_ext)
        o = o / l
        o_ref[:, hp * LANES:(hp + 1) * LANES] = jnp.where(lo, o[:nq], o[nq:]).astype(BF16)


def _na(qkv, bias, n_seq, t, row_off, prev):
    rows = t // GRID_W
    nb = rows // RQ_NA
    nq = RQ_NA * GRID_W
    blk0 = row_off // nq
    seq0 = row_off // t
    group_type = lambda g: jnp.where(g == 0, 0, jnp.where(g == nb - 1, 2, 1))
    return _call_into(
        functools.partial(_na_kernel, rows=rows), prev, 4,
        grid=(n_seq, nb),
        in_specs=[
            pl.BlockSpec((4, nq, LANES), lambda s, g: (BLK_NA_Q // 4, blk0 + s * nb + g, 0)),
            pl.BlockSpec((4, t, LANES), lambda s, g: (BLK_NA_K // 4, seq0 + s, 0)),
            pl.BlockSpec((4, t, LANES), lambda s, g: (BLK_NA_V // 4, seq0 + s, 0)),
            pl.BlockSpec((1, NA_HEADS // 2, 2 * nq, NA_UROWS * GRID_W), lambda s, g: (group_type(g), 0, 0, 0)),
        ],
        out_specs=pl.BlockSpec((nq, BRANCH_W), lambda s, g: (blk0 + s * nb + g, 0)),
        out_shape=jax.ShapeDtypeStruct((qkv.shape[1], BRANCH_W), BF16),
        compiler_params=_params(("parallel", "arbitrary")),
        name="mixer_a",
    )(qkv, qkv, qkv, bias)


def _na_bias_table(rpb):
    c = np.arange(GRID_W)[:, None]
    kc = np.arange(GRID_W)[None, :]
    cstart = np.clip(c - NA_COLS // 2, 0, GRID_W - NA_COLS)
    ok = (kc >= cstart) & (kc < cstart + NA_COLS)
    pad = GRID_W - NA_COLS
    padded = jnp.pad(rpb.astype(F32) * LOG2E, ((0, 0), (0, 0), (pad, pad)))
    e = jnp.stack([padded[..., GRID_W - 1 - q:2 * GRID_W - 1 - q] for q in range(GRID_W)], axis=2)
    e = jnp.where(ok[None, None], e, NEG_INF)
    neg = jnp.full((NA_HEADS, GRID_W, GRID_W), NEG_INF, F32)
    half = NA_ROWS // 2
    group_types = (
        [(-i, -i) for i in range(RQ_NA)],
        [(-half - i, -half) for i in range(RQ_NA)],
        [(-(NA_UROWS - RQ_NA) - i, -(NA_ROWS - RQ_NA) - i) for i in range(RQ_NA)],
    )
    tables = []
    for rel in group_types:
        q_rows = []
        for u0, w0 in rel:
            blocks = [e[:, u0 + j + NA_ROWS - 1] if w0 <= u0 + j < w0 + NA_ROWS else neg for j in range(NA_UROWS)]
            q_rows.append(jnp.concatenate(blocks, axis=2))
        tables.append(jnp.concatenate(q_rows, axis=1))
    return jnp.stack(tables).reshape(3, NA_HEADS // 2, 2 * RQ_NA * GRID_W, NA_UROWS * GRID_W)


def _diff_kernel(lam_ref, q_ref, k_ref, v_ref, tab_ref, g_ref, o_ref, vext_scr, *, t, tq, out_scale):
    qi = pl.program_id(2)

    @pl.when(qi == 0)
    def _():
        vext_scr[:, :LANES] = v_ref[0]
        vext_scr[:, LANES:] = jnp.ones((t, LANES), BF16)

    lam = lam_ref[0]
    off = pl.multiple_of((t // tq - 1 - qi) * tq, tq)
    bias = tab_ref[0, :, pl.ds(off, t)]
    q = q_ref[0]
    lo = _half_masks(tq)
    k = k_ref[0]
    v_ext = vext_scr[...]

    scores = [_dot_nt(_keep_half(q, lo, half), k) - bias for half in range(2)]
    probs = [jnp.exp2((s - jnp.max(s, axis=-1, keepdims=True)).astype(BF16)) for s in scores]
    outs = [_dot(p, v_ext) for p in probs]
    o1, o2 = (ol[:, :LANES] / ol[:, LANES:] for ol in outs)
    o = o1 - lam * o2
    o_ref[...] = (_rms(o, g_ref[...], 1e-5) * out_scale).astype(BF16)


def _diff_bias_table(t, tq):
    r = lax.broadcasted_iota(jnp.int32, (tq, 2 * t - tq), 0)
    x = lax.broadcasted_iota(jnp.int32, (tq, 2 * t - tq), 1)
    dist = jnp.abs(r - x + (t - tq)).astype(F32)
    return jnp.asarray([s * LOG2E for s in SLOPES_DIFF], F32)[:, None, None] * dist[None]


def _diff(qkv, lam, subln_g, n_seq, t, row_off, lam_init, prev):
    tq = TQ_DIFF
    nq = t // tq
    blk0 = row_off // tq
    seq0 = row_off // t
    return _call_into(
        functools.partial(_diff_kernel, t=t, tq=tq, out_scale=1.0 - lam_init), prev, 6,
        grid=(DIFF_HEADS, n_seq, nq),
        in_specs=[
            pl.BlockSpec(memory_space=pltpu.SMEM),
            pl.BlockSpec((1, tq, LANES), lambda h, s, i: (BLK_DF_Q + h, blk0 + s * nq + i, 0)),
            pl.BlockSpec((1, t, LANES), lambda h, s, i: (BLK_DF_K + h, seq0 + s, 0)),
            pl.BlockSpec((1, t, LANES), lambda h, s, i: (BLK_DF_V + h, seq0 + s, 0)),
            pl.BlockSpec((1, tq, 2 * t - tq), lambda h, s, i: (h, 0, 0)),
            pl.BlockSpec((1, DIFF_DV), lambda h, s, i: (0, 0)),
        ],
        out_specs=pl.BlockSpec((tq, LANES), lambda h, s, i: (blk0 + s * nq + i, h)),
        out_shape=jax.ShapeDtypeStruct((qkv.shape[1], BRANCH_W), BF16),
        scratch_shapes=[pltpu.VMEM((t, 2 * LANES), BF16)],
        compiler_params=_params(("parallel", "arbitrary", "arbitrary")),
        name="mixer_b",
    )(lam, qkv, qkv, qkv, _diff_bias_table(t, tq), subln_g)


def _band_window(qi, tq, kw, radius, length):
    start = jnp.clip(qi * tq - radius, 0, length - kw)
    rel = (lax.broadcasted_iota(jnp.int32, (tq, kw), 0) + (qi * tq - start)) \
        - lax.broadcasted_iota(jnp.int32, (tq, kw), 1)
    dist = jnp.abs(rel)
    return pl.multiple_of(start, 64), jnp.where(dist <= radius, dist.astype(F32), MASKED_DIST)


def _with_ones(v):
    return jnp.concatenate([v, jnp.ones(v.shape, v.dtype)], axis=1)


def _softmax_pv(s, v_ext, floor=None):
    m = jnp.max(s, axis=-1, keepdims=True)
    if floor is not None:
        m = jnp.maximum(m, floor)
    ol = _dot(jnp.exp2((s - m).astype(BF16)), v_ext)
    return ol[:, :LANES], ol[:, LANES:], m


def _dil_kernel(q_ref, k_ref, v_ref, o_ref, lse_ref, *, n_sub, tq, kw, radius, slopes, scale, nqb, nrb):
    for b in range(nqb):
        start, dist = _band_window(pl.program_id(2) * nqb + b, tq, kw, radius, n_sub)
        rows = slice(b * tq, (b + 1) * tq)
        for rr in range(nrb):
            lanes = slice(rr * LANES, (rr + 1) * LANES)
            for h in range(DIL_HEADS):
                k = k_ref[h, pl.ds(start, kw), lanes]
                v_ext = _with_ones(v_ref[h, pl.ds(start, kw), lanes])
                s = _dot_nt(q_ref[h, rows, lanes], k) * scale - slopes[h] * dist
                o, l, m = _softmax_pv(s, v_ext)
                o_ref[h, rows, lanes] = (o / l).astype(BF16)
                lse_ref[h, rows, lanes] = (m + jnp.log2(l)) * (1.0 / LOG2E)


def _dil_group(view, gi, n_seq, t, row_off, prev):
    dil = DIL_DILATIONS[gi]
    radius = DIL_WINDOWS[gi] // (2 * dil)
    n_sub = t // dil
    tq = min(TQ_DIL, n_sub)
    kw = min(tq + 2 * radius, n_sub)
    nqb = min(n_sub // tq, UNITS_DIL)
    nrb = min(dil, UNITS_DIL // nqb)
    nq = n_sub // (tq * nqb)
    blk0 = row_off // dil // (tq * nqb)
    seq0 = row_off // t
    qb, kb, vb = (BLK_DL_Q // 4, BLK_DL_K // 4, BLK_DL_V // 4) if gi == 0 else (0, 1, 2)
    slopes = tuple(SLOPES_DIL[gi * DIL_HEADS + h] * dil * LOG2E for h in range(DIL_HEADS))
    rows_out = view.shape[1]
    q_idx = lambda s, r, i: (qb, blk0 + s * nq + i, r)
    o_idx = lambda s, r, i: (0, blk0 + s * nq + i, r)
    o, lse = _call_into(
        functools.partial(_dil_kernel, n_sub=n_sub, tq=tq, kw=kw, radius=radius, slopes=slopes,
                          scale=DIL_DH ** -0.5 * LOG2E, nqb=nqb, nrb=nrb), prev, 3,
        grid=(n_seq, dil // nrb, nq),
        in_specs=[
            pl.BlockSpec((4, nqb * tq, nrb * LANES), q_idx),
            pl.BlockSpec((4, n_sub, nrb * LANES), lambda s, r, i: (kb, seq0 + s, r)),
            pl.BlockSpec((4, n_sub, nrb * LANES), lambda s, r, i: (vb, seq0 + s, r)),
        ],
        out_specs=[pl.BlockSpec((4, nqb * tq, nrb * LANES), o_idx),
                   pl.BlockSpec((4, nqb * tq, nrb * LANES), o_idx)],
        out_shape=[jax.ShapeDtypeStruct((DIL_HEADS, rows_out, dil * LANES), BF16),
                   jax.ShapeDtypeStruct((DIL_HEADS, rows_out, dil * LANES), F32)],
        compiler_params=_params(("parallel", "arbitrary", "arbitrary")),
        name=f"mixer_c{gi}",
    )(view, view, view)
    return o, lse


def _dil_combine_kernel(o0, o1, o2, l0, l1, l2, out_ref, o_scr, l_scr, *, tm):
    for h in range(DIL_HEADS):
        for gi, (o_ref, l_ref) in ((1, (o1, l1)), (2, (o2, l2))):
            dil = DIL_DILATIONS[gi]
            for r in range(dil):
                rows = pl.ds(r, tm // dil, stride=dil)
                o_scr[gi - 1, rows, :] = o_ref[h, :, r * LANES:(r + 1) * LANES].astype(F32)
                l_scr[gi - 1, rows, :] = l_ref[h, :, r * LANES:(r + 1) * LANES]
        a0, a1, a2 = l0[h], l_scr[0], l_scr[1]
        mx = jnp.maximum(jnp.maximum(a0, a1), a2)
        e0, e1, e2 = jnp.exp(a0 - mx), jnp.exp(a1 - mx), jnp.exp(a2 - mx)
        num = e0 * o0[h].astype(F32) + e1 * o_scr[0] + e2 * o_scr[1]
        out_ref[:, h * LANES:(h + 1) * LANES] = (num / (e0 + e1 + e2)).astype(BF16)


def _dil_combine(outs, lses):
    m = outs[0].shape[1]
    tm = min(TM_COMBINE, m)
    specs = [pl.BlockSpec((DIL_HEADS, tm // dil, dil * LANES), lambda i: (0, i, 0)) for dil in DIL_DILATIONS]
    return pl.pallas_call(
        functools.partial(_dil_combine_kernel, tm=tm),
        grid=(m // tm,),
        in_specs=specs * 2,
        out_specs=pl.BlockSpec((tm, BRANCH_W), lambda i: (i, 0)),
        out_shape=jax.ShapeDtypeStruct((m, BRANCH_W), BF16),
        scratch_shapes=[pltpu.VMEM((2, tm, LANES), F32), pltpu.VMEM((2, tm, LANES), F32)],
        compiler_params=_params(("parallel",)),
        name="mixer_c_combine",
    )(*outs, *lses)


def _swa_kernel(sink_ref, q_ref, k_ref, v_ref, o_ref, *, t, tq, kw):
    qi = pl.program_id(1)
    start, dist = _band_window(qi, tq, kw, SWA_RADIUS, t)
    k = k_ref[0, pl.ds(start, kw), :]
    v_ext = _with_ones(v_ref[0, pl.ds(start, kw), :])
    lo = _half_masks(tq)
    for g in range(SWA_GROUP):
        q = q_ref[g]
        outs = []
        for hk in range(SWA_KVH):
            head = hk * SWA_GROUP + g
            sink = sink_ref[head] * LOG2E
            s = _dot_nt(_keep_half(q, lo, hk), k) - (SLOPES_SWA[head] * LOG2E) * dist
            o, l, m = _softmax_pv(s, v_ext, floor=sink)
            outs.append(o / (l + jnp.exp2(sink - m)))
        o_ref[:, g * LANES:(g + 1) * LANES] = jnp.where(lo, outs[0], outs[1]).astype(BF16)


def _swa(qkv, sink, n_seq, t, row_off, prev):
    tq = TQ_SWA
    kw = min(tq + 2 * SWA_RADIUS, t)
    nq = t // tq
    blk0 = row_off // tq
    seq0 = row_off // t
    return _call_into(
        functools.partial(_swa_kernel, t=t, tq=tq, kw=kw), prev, 4,
        grid=(n_seq, nq),
        in_specs=[
            pl.BlockSpec(memory_space=pltpu.SMEM),
            pl.BlockSpec((4, tq, LANES), lambda s, i: (BLK_SW_Q // 4, blk0 + s * nq + i, 0)),
            pl.BlockSpec((1, t, LANES), lambda s, i: (BLK_SW_K, seq0 + s, 0)),
            pl.BlockSpec((1, t, LANES), lambda s, i: (BLK_SW_V, seq0 + s, 0)),
        ],
        out_specs=pl.BlockSpec((tq, BRANCH_W), lambda s, i: (blk0 + s * nq + i, 0)),
        out_shape=jax.ShapeDtypeStruct((qkv.shape[1], BRANCH_W), BF16),
        compiler_params=_params(("parallel", "arbitrary")),
        name="mixer_d",
    )(sink, qkv, qkv, qkv)


def _merge_kernel(h_ref, oa_ref, ob_ref, oc_ref, od_ref, wg_ref, bg_ref, wb_ref, wo_ref, xres_ref, o_ref, *, nc, tc):
    j = pl.program_id(1)
    h = h_ref[...]
    acc = None
    for n, b_ref in enumerate((oa_ref, ob_ref, oc_ref, od_ref)):
        gate = jax.nn.sigmoid(_dot(h, wg_ref[n]) + bg_ref[n:n + 1, :])
        term = gate * _dot(b_ref[...], wb_ref[n])
        acc = term if acc is None else acc + term
    merged = acc.astype(BF16)

    def project(first):
        for n in range(D // FFN_TN):
            cols = slice(n * FFN_TN, (n + 1) * FFN_TN)
            part = _dot(merged, wo_ref[:, cols])
            o_ref[:, cols] = part if first else o_ref[:, cols] + part

    pl.when(j == 0)(functools.partial(project, True))
    pl.when(j > 0)(functools.partial(project, False))

    for c in range(nc):
        @pl.when(j == c)
        def _(c=c):
            o_ref[:, c * tc:(c + 1) * tc] += xres_ref[...]


def _merge(x, h, branches, w_gate, b_gate, w_branch, w_out):
    m = x.shape[0]
    tm, tc = TM_MERGE, TC_MERGE
    nc = D // tc
    row_tile = lambda i, j: (i, 0)
    return pl.pallas_call(
        functools.partial(_merge_kernel, nc=nc, tc=tc),
        grid=(m // tm, nc),
        in_specs=[pl.BlockSpec((tm, D), row_tile)]
        + [pl.BlockSpec((tm, BRANCH_W), row_tile)] * 4
        + [
            pl.BlockSpec((None, 4, D, tc), lambda i, j: (j, 0, 0, 0)),
            pl.BlockSpec((4, tc), lambda i, j: (0, j)),
            pl.BlockSpec((4, BRANCH_W, tc), lambda i, j: (0, 0, j)),
            pl.BlockSpec((tc, D), lambda i, j: (j, 0)),
            pl.BlockSpec((tm, tc), lambda i, j: (i, j)),
        ],
        out_specs=pl.BlockSpec((tm, D), row_tile),
        out_shape=jax.ShapeDtypeStruct((m, D), F32),
        compiler_params=_params(("parallel", "arbitrary")),
        name="merge",
    )(h, *branches, w_gate, b_gate, w_branch, w_out, x)


def _cast_ffn_in_kernel(w_ref, o_ref):
    o_ref[0, :, :D_FF] = w_ref[...].astype(BF16)
    o_ref[0, :, D_FF:] = jnp.zeros((o_ref.shape[1], FFN_FP - D_FF), BF16)


def _cast_ffn_in(w_in, l):
    tr = 256
    return pl.pallas_call(
        _cast_ffn_in_kernel,
        grid=(2, D // tr),
        in_specs=[pl.BlockSpec((None, tr, D_FF), lambda part, i: (l, i, part))],
        out_specs=pl.BlockSpec((1, tr, FFN_FP), lambda part, i: (part, i, 0)),
        out_shape=jax.ShapeDtypeStruct((2, D, FFN_FP), BF16),
        compiler_params=_params(("parallel", "parallel")),
        name="cast_ffn_in",
    )(w_in)


def _cast_ffn_out_kernel(w_ref, o_ref, *, tr):
    row = pl.program_id(0) * tr + lax.broadcasted_iota(jnp.int32, (tr, D), 0)
    o_ref[...] = jnp.where(row < D_FF, w_ref[...], 0.0).astype(BF16)


def _cast_ffn_out(w_out, l):
    tr = 512
    return pl.pallas_call(
        functools.partial(_cast_ffn_out_kernel, tr=tr),
        grid=(FFN_FP // tr,),
        in_specs=[pl.BlockSpec((None, tr, D), lambda i: (l, i, 0))],
        out_specs=pl.BlockSpec((tr, D), lambda i: (i, 0)),
        out_shape=jax.ShapeDtypeStruct((FFN_FP, D), BF16),
        compiler_params=_params(("parallel",)),
        name="cast_ffn_out",
    )(w_out)


def _cast_kernel(w_ref, o_ref):
    o_ref[...] = w_ref[...].astype(BF16)


def _cast_gate(w_gate, l):
    tc = TC_MERGE
    nc = D // tc
    return pl.pallas_call(
        _cast_kernel,
        grid=(nc, 4),
        in_specs=[pl.BlockSpec((None, D, tc), lambda c, n: (l, 0, n * nc + c))],
        out_specs=pl.BlockSpec((None, None, D, tc), lambda c, n: (c, n, 0, 0)),
        out_shape=jax.ShapeDtypeStruct((nc, 4, D, tc), BF16),
        compiler_params=_params(("parallel", "parallel")),
        name="cast_gate",
    )(w_gate)


def _prep_ffn(w_in, w_out, l):
    return _cast_ffn_in(w_in, l), _cast_ffn_out(w_out, l)


def _swa_pair_heads(w, axis):
    shape = w.shape
    split = shape[:axis] + (SWA_KVH, SWA_GROUP, SWA_DH) + shape[axis + 1:]
    return jnp.swapaxes(w.reshape(split), axis, axis + 1).reshape(shape)


def _proj_columns(w_in):
    scale = np.ones((IN_WIDTH,), np.float32)
    scale[COL_NA:COL_NA + NA_HEADS * NA_DH] = NA_DH ** -0.5 * LOG2E
    scale[COL_DF:COL_DF + DIFF_HEADS * 2 * DIFF_DK] = DIFF_DK ** -0.5 * LOG2E
    scale[COL_SW:COL_SW + SWA_QH * SWA_DH] = SWA_DH ** -0.5 * LOG2E
    w = w_in * scale
    gw = DIL_HEADS * DIL_DH

    def dil_group(gi):
        return [w[:, COL_DL + part * DIL_W + gi * gw:COL_DL + part * DIL_W + (gi + 1) * gw] for part in range(3)]

    sw_q_end = COL_SW + SWA_QH * SWA_DH
    cols = [w[:, :COL_DL]] + dil_group(0) + [_swa_pair_heads(w[:, COL_SW:sw_q_end], 1), w[:, sw_q_end:]] \
        + dil_group(1) + dil_group(2)
    return jnp.concatenate(cols, axis=1)


def _prep_w_in(w_in):
    return _proj_columns(w_in).astype(BF16)


def _encoder_layer(x, l, seqs, p, final_gain=None):
    ffn1 = (p["norm_ffn1"][l][None], *_prep_ffn(p["w_ffn1_in"], p["w_ffn1_out"], l))
    if isinstance(x, tuple):
        m_total = sum(xb.shape[0] for xb in x)
        out, row_off = None, 0
        for xb in x:
            out = _ffn(xb, *ffn1, m_out=m_total, out_off=row_off, prev=None if out is None else (out,))
            row_off += xb.shape[0]
        x = out
    else:
        x = _ffn(x, *ffn1)
    h, qkv, qkv_d4, qkv_d16 = _proj(x, p["norm_mix"][l][None], _prep_w_in(p["w_in"][l]))
    dil_views = (qkv, qkv_d4, qkv_d16)

    lam_init = 0.8 - 0.6 * math.exp(-0.3 * l)
    lv = p["diff_lambda"][l].astype(F32)
    lam = jnp.exp(jnp.sum(lv[0] * lv[1])) - jnp.exp(jnp.sum(lv[2] * lv[3])) + lam_init
    diff_lam = lam.reshape(1)
    na_bias = _na_bias_table(p["na_rpb"][l])
    subln = p["diff_subln"][l].astype(F32)[None]
    sink = p["swa_sink"][l].astype(F32)

    o_a = o_b = o_d = None
    dil = [None] * DIL_GROUPS
    for n_seq, t, row_off in seqs:
        o_a = _na(qkv, na_bias, n_seq, t, row_off, None if o_a is None else (o_a,))
        o_b = _diff(qkv, diff_lam, subln, n_seq, t, row_off, lam_init, None if o_b is None else (o_b,))
        for gi in range(DIL_GROUPS):
            dil[gi] = _dil_group(dil_views[gi], gi, n_seq, t, row_off, dil[gi])
        o_d = _swa(qkv, sink, n_seq, t, row_off, None if o_d is None else (o_d,))
    o_c = _dil_combine([o for o, _ in dil], [lse for _, lse in dil])
    branches = [o_a, o_b, o_c, o_d]

    w_branch = p["w_branch"][l]
    w_branch = jnp.concatenate([w_branch[:3], _swa_pair_heads(w_branch[3:], 1)], axis=0).astype(BF16)
    x = _merge(x, h, branches, _cast_gate(p["w_gate"], l), p["b_gate"][l].astype(F32).reshape(4, D),
               w_branch, p["w_out"][l].astype(BF16))
    ffn2 = (p["norm_ffn2"][l][None], *_prep_ffn(p["w_ffn2_in"], p["w_ffn2_out"], l))
    if final_gain is None:
        return _ffn(x, *ffn2)
    return tuple(_ffn(x, *ffn2, rows=(row_off, n_seq * t), final_gain=final_gain) for n_seq, t, row_off in seqs)


def kernel(x_prompt, x_sample, norm_ffn1, w_ffn1_in, w_ffn1_out, norm_mix, w_in, na_rpb, diff_lambda, diff_subln,
           swa_sink, w_branch, w_gate, b_gate, w_out, norm_ffn2, w_ffn2_in, w_ffn2_out, norm_final):
    p = dict(norm_ffn1=norm_ffn1, w_ffn1_in=w_ffn1_in, w_ffn1_out=w_ffn1_out, norm_mix=norm_mix, w_in=w_in,
             na_rpb=na_rpb, diff_lambda=diff_lambda, diff_subln=diff_subln, swa_sink=swa_sink,
             w_branch=w_branch, w_gate=w_gate, b_gate=b_gate, w_out=w_out, norm_ffn2=norm_ffn2,
             w_ffn2_in=w_ffn2_in, w_ffn2_out=w_ffn2_out)
    bp, tp, _ = x_prompt.shape
    bs, ts, _ = x_sample.shape
    mp = bp * tp
    x = (x_prompt.reshape(mp, D).astype(F32), x_sample.reshape(bs * ts, D).astype(F32))
    seqs = ((bp, tp, 0), (bs, ts, mp))
    for l in range(DEPTH):
        x = _encoder_layer(x, l, seqs, p, final_gain=norm_final.astype(F32)[None] if l == DEPTH - 1 else None)
    y_prompt, y_sample = x
    return y_prompt.reshape(bp, tp, D), y_sample.reshape(bs, ts, D)
```

```python
import functools
import math

import jax
import jax.numpy as jnp
import numpy as np
from jax import lax
from jax.experimental import pallas as pl
from jax.experimental.pallas import tpu as pltpu

F32 = jnp.float32
BF16 = jnp.bfloat16

D = 2048
DEPTH = 2
GRID_W = 64
NA_HEADS, NA_DH, NA_ROWS, NA_COLS = 8, 64, 8, 16
DIFF_HEADS, DIFF_DK, DIFF_DV = 4, 64, 128
DIL_WINDOWS, DIL_DILATIONS = (128, 512, 2048), (1, 4, 16)
DIL_GROUPS, DIL_HEADS, DIL_DH = 3, 4, 128
SWA_QH, SWA_KVH, SWA_DH, SWA_RADIUS = 8, 2, 64, 128
SWA_GROUP = SWA_QH // SWA_KVH
BRANCH_W = 512
D_FF = ((8 * D // 3 + 127) // 128) * 128
N_ALIBI = SWA_QH + DIL_GROUPS * DIL_HEADS + DIFF_HEADS
RMS_EPS = 1e-6
NEG_INF = -1e30
LOG2E = math.log2(math.e)
MASKED_DIST = 1e32

LANES = 128
IN_WIDTH = 3 * NA_HEADS * NA_DH + 2 * DIFF_HEADS * 2 * DIFF_DK + DIFF_HEADS * DIFF_DV \
    + 3 * DIL_GROUPS * DIL_HEADS * DIL_DH + SWA_QH * SWA_DH + 2 * SWA_KVH * SWA_DH
N_BLK = IN_WIDTH // LANES
BLK_NA_Q, BLK_NA_K, BLK_NA_V = 0, 4, 8
BLK_DF_Q, BLK_DF_K, BLK_DF_V = 12, 16, 20
BLK_DL_Q, BLK_DL_K, BLK_DL_V = 24, 28, 32
BLK_SW_Q, BLK_SW_K, BLK_SW_V = 36, 40, 41
N_BLK_MAIN = 42
N_BLK_DIL = 3 * DIL_HEADS
COL_NA, COL_DF, COL_DL, COL_SW = 0, 1536, 3072, 7680
DIL_W = DIL_GROUPS * DIL_HEADS * DIL_DH

FFN_TF = 512
FFN_FP = -(-D_FF // FFN_TF) * FFN_TF
FFN_NC1 = FFN_FP // FFN_TF
FFN_TN = 512
TM_FFN = 1024
TM_PROJ = 1024
NB_PROJ = 6
TN_PROJ = NB_PROJ * LANES
NJ_MAIN = N_BLK_MAIN // NB_PROJ
NJ_DIL = N_BLK_DIL // NB_PROJ
TM_MERGE = 1024
TC_MERGE = 256
TM_COMBINE = 1024
RQ_NA = 4
NA_UROWS = 12
TQ_DIFF = 256
BLOCKS_DIFF = 4
TQ_SWA = 256
TQ_DIL = 128
UNITS_DIL = 8
VMEM_LIMIT = 60 * 1024 * 1024

_ALIBI = [2.0 ** (-8.0 * (i + 1) / N_ALIBI) for i in range(N_ALIBI)]
SLOPES_SWA = _ALIBI[:SWA_QH]
SLOPES_DIL = _ALIBI[SWA_QH:SWA_QH + DIL_GROUPS * DIL_HEADS]
SLOPES_DIFF = _ALIBI[SWA_QH + DIL_GROUPS * DIL_HEADS:]


def _params(sem):
    return pltpu.CompilerParams(dimension_semantics=sem, vmem_limit_bytes=VMEM_LIMIT)


def _call_into(body, prev, n_in, **kw):
    if prev is None:
        return pl.pallas_call(body, **kw)
    prev = tuple(prev)

    def aliased(*refs):
        body(*refs[:n_in], *refs[n_in + len(prev):])

    kw["in_specs"] = list(kw["in_specs"]) + [pl.BlockSpec(memory_space=pl.ANY)] * len(prev)
    call = pl.pallas_call(aliased, input_output_aliases={n_in + i: i for i in range(len(prev))}, **kw)
    return lambda *args: call(*args, *prev)


def _rms(x, g, eps):
    ms = jnp.mean(x * x, axis=-1, keepdims=True)
    return x * lax.rsqrt(ms + eps) * g


def _dot(a, b):
    return jnp.dot(a, b, preferred_element_type=F32)


def _dot_nt(a, b):
    return lax.dot_general(a, b, (((1,), (1,)), ((), ())), preferred_element_type=F32)


def _half_masks(rows):
    lane = lax.broadcasted_iota(jnp.int32, (rows, LANES), 1)
    return lane < (LANES // 2)


def _keep_half(x, lo, half):
    keep = lo if half == 0 else jnp.logical_not(lo)
    return jnp.where(keep, x.astype(F32), 0.0).astype(BF16)


def _ffn_kernel(*refs, final_norm):
    x_ref, g_ref, wg_ref, wu_ref, w2_ref = refs[:5]
    o_ref, h_scr = refs[-2:]
    j = pl.program_id(1)

    @pl.when(j == 0)
    def _():
        x = x_ref[...]
        h_scr[...] = _rms(x, g_ref[...], RMS_EPS).astype(BF16)
        o_ref[...] = x

    h = h_scr[...]
    gate = _dot(h, wg_ref[...])
    up = _dot(h, wu_ref[...])
    act = (gate * jax.nn.sigmoid(gate) * (0.5 * up)).astype(BF16)
    for n in range(D // FFN_TN):
        cols = slice(n * FFN_TN, (n + 1) * FFN_TN)
        o_ref[:, cols] += _dot(act, w2_ref[:, cols])

    if final_norm:
        @pl.when(j == FFN_NC1 - 1)
        def _():
            o_ref[...] = _rms(o_ref[...], refs[5][...], RMS_EPS)


def _ffn(x, g, w1, w2, rows=None, m_out=None, out_off=0, prev=None, final_gain=None):
    start, m = (0, x.shape[0]) if rows is None else rows
    m_out = m if m_out is None else m_out
    tm = TM_FFN
    in0, out0 = start // tm, out_off // tm
    inputs = [x, g, w1, w1, w2] + ([] if final_gain is None else [final_gain])
    in_specs = [
        pl.BlockSpec((tm, D), lambda i, j: (in0 + i, 0)),
        pl.BlockSpec((1, D), lambda i, j: (0, 0)),
        pl.BlockSpec((None, D, FFN_TF), lambda i, j: (0, 0, j)),
        pl.BlockSpec((None, D, FFN_TF), lambda i, j: (1, 0, j)),
        pl.BlockSpec((FFN_TF, D), lambda i, j: (j, 0)),
    ] + ([] if final_gain is None else [pl.BlockSpec((1, D), lambda i, j: (0, 0))])
    return _call_into(
        functools.partial(_ffn_kernel, final_norm=final_gain is not None), prev, len(inputs),
        grid=(m // tm, FFN_NC1),
        in_specs=in_specs,
        out_specs=pl.BlockSpec((tm, D), lambda i, j: (out0 + i, 0)),
        out_shape=jax.ShapeDtypeStruct((m_out, D), F32),
        scratch_shapes=[pltpu.VMEM((tm, D), BF16)],
        compiler_params=_params(("parallel", "arbitrary")),
        name="ffn",
    )(*inputs)


def _proj_kernel(x_ref, g_ref, w_ref, h_ref, main_ref, d4_ref, d16_ref, res_scr, *, tm):
    j = pl.program_id(1)

    @pl.when(j == 0)
    def _():
        h_ref[...] = _rms(x_ref[...], g_ref[...], RMS_EPS).astype(BF16)

    res = _dot(h_ref[...], w_ref[...])

    @pl.when(j < NJ_MAIN)
    def _():
        for k in range(NB_PROJ):
            main_ref[k] = res[:, k * LANES:(k + 1) * LANES].astype(BF16)

    @pl.when(j >= NJ_MAIN)
    def _():
        for k in range(NB_PROJ):
            res_scr[k] = res[:, k * LANES:(k + 1) * LANES]

    for out_ref, dil, first in ((d4_ref, DIL_DILATIONS[1], NJ_MAIN), (d16_ref, DIL_DILATIONS[2], NJ_MAIN + NJ_DIL)):
        @pl.when((j >= first) & (j < first + NJ_DIL))
        def _(out_ref=out_ref, dil=dil):
            for k in range(NB_PROJ):
                for r in range(dil):
                    rows = res_scr[k, pl.ds(r, tm // dil, stride=dil), :]
                    out_ref[k, :, r * LANES:(r + 1) * LANES] = rows.astype(BF16)


def _proj(x, g, w):
    m = x.shape[0]
    tm = TM_PROJ
    d4, d16 = DIL_DILATIONS[1], DIL_DILATIONS[2]
    return pl.pallas_call(
        functools.partial(_proj_kernel, tm=tm),
        grid=(m // tm, NJ_MAIN + 2 * NJ_DIL),
        in_specs=[
            pl.BlockSpec((tm, D), lambda i, j: (i, 0)),
            pl.BlockSpec((1, D), lambda i, j: (0, 0)),
            pl.BlockSpec((D, TN_PROJ), lambda i, j: (0, j)),
        ],
        out_specs=[
            pl.BlockSpec((tm, D), lambda i, j: (i, 0)),
            pl.BlockSpec((NB_PROJ, tm, LANES), lambda i, j: (jnp.minimum(j, NJ_MAIN - 1), i, 0)),
            pl.BlockSpec((NB_PROJ, tm // d4, d4 * LANES),
                         lambda i, j: (jnp.clip(j - NJ_MAIN, 0, NJ_DIL - 1), i, 0)),
            pl.BlockSpec((NB_PROJ, tm // d16, d16 * LANES),
                         lambda i, j: (jnp.clip(j - NJ_MAIN - NJ_DIL, 0, NJ_DIL - 1), i, 0)),
        ],
        out_shape=[jax.ShapeDtypeStruct((m, D), BF16),
                   jax.ShapeDtypeStruct((N_BLK_MAIN, m, LANES), BF16),
                   jax.ShapeDtypeStruct((N_BLK_DIL, m // d4, d4 * LANES), BF16),
                   jax.ShapeDtypeStruct((N_BLK_DIL, m // d16, d16 * LANES), BF16)],
        scratch_shapes=[pltpu.VMEM((NB_PROJ, tm, LANES), F32)],
        compiler_params=_params(("parallel", "arbitrary")),
        name="proj",
    )(x, g, w)


def _na_kernel(q_ref, k_ref, v_ref, b_ref, o_ref, *, rows):
    nq = RQ_NA * GRID_W
    nk = NA_UROWS * GRID_W
    row0 = jnp.clip(RQ_NA * pl.program_id(1) - NA_ROWS // 2, 0, rows - NA_UROWS)
    start = pl.multiple_of(row0 * GRID_W, GRID_W)
    lo = _half_masks(nq)
    for hp in range(NA_HEADS // 2):
        q = q_ref[hp]
        k = k_ref[hp, pl.ds(start, nk), :]
        v_ext = _with_ones(v_ref[hp, pl.ds(start, nk), :])
        lhs = jnp.concatenate([_keep_half(q, lo, 0), _keep_half(q, lo, 1)], axis=0)
        o, l, _ = _softmax_pv(_dot_nt(lhs, k) + b_ref[0, hp], v_ext)
        o = o / l
        o_ref[:, hp * LANES:(hp + 1) * LANES] = jnp.where(lo, o[:nq], o[nq:]).astype(BF16)


def _na(qkv, bias, n_seq, t, row_off, prev):
    rows = t // GRID_W
    nb = rows // RQ_NA
    nq = RQ_NA * GRID_W
    blk0 = row_off // nq
    seq0 = row_off // t
    group_type = lambda g: jnp.where(g == 0, 0, jnp.where(g == nb - 1, 2, 1))
    return _call_into(
        functools.partial(_na_kernel, rows=rows), prev, 4,
        grid=(n_seq, nb),
        in_specs=[
            pl.BlockSpec((4, nq, LANES), lambda s, g: (BLK_NA_Q // 4, blk0 + s * nb + g, 0)),
            pl.BlockSpec((4, t, LANES), lambda s, g: (BLK_NA_K // 4, seq0 + s, 0)),
            pl.BlockSpec((4, t, LANES), lambda s, g: (BLK_NA_V // 4, seq0 + s, 0)),
            pl.BlockSpec((1, NA_HEADS // 2, 2 * nq, NA_UROWS * GRID_W), lambda s, g: (group_type(g), 0, 0, 0)),
        ],
        out_specs=pl.BlockSpec((nq, BRANCH_W), lambda s, g: (blk0 + s * nb + g, 0)),
        out_shape=jax.ShapeDtypeStruct((qkv.shape[1], BRANCH_W), BF16),
        compiler_params=_params(("parallel", "arbitrary")),
        name="mixer_a",
    )(qkv, qkv, qkv, bias)


def _na_bias_table(rpb):
    c = np.arange(GRID_W)[:, None]
    kc = np.arange(GRID_W)[None, :]
    cstart = np.clip(c - NA_COLS // 2, 0, GRID_W - NA_COLS)
    ok = (kc >= cstart) & (kc < cstart + NA_COLS)
    pad = GRID_W - NA_COLS
    padded = jnp.pad(rpb.astype(F32) * LOG2E, ((0, 0), (0, 0), (pad, pad)))
    e = jnp.stack([padded[..., GRID_W - 1 - q:2 * GRID_W - 1 - q] for q in range(GRID_W)], axis=2)
    e = jnp.where(ok[None, None], e, NEG_INF)
    neg = jnp.full((NA_HEADS, GRID_W, GRID_W), NEG_INF, F32)
    half = NA_ROWS // 2
    group_types = (
        [(-i, -i) for i in range(RQ_NA)],
        [(-half - i, -half) for i in range(RQ_NA)],
        [(-(NA_UROWS - RQ_NA) - i, -(NA_ROWS - RQ_NA) - i) for i in range(RQ_NA)],
    )
    tables = []
    for rel in group_types:
        q_rows = []
        for u0, w0 in rel:
            blocks = [e[:, u0 + j + NA_ROWS - 1] if w0 <= u0 + j < w0 + NA_ROWS else neg for j in range(NA_UROWS)]
            q_rows.append(jnp.concatenate(blocks, axis=2))
        tables.append(jnp.concatenate(q_rows, axis=1))
    return jnp.stack(tables).reshape(3, NA_HEADS // 2, 2 * RQ_NA * GRID_W, NA_UROWS * GRID_W)


def _diff_kernel(lam_ref, q_ref, k_ref, v_ref, tab_ref, g_ref, o_ref, vext_scr, *, t, tq, out_scale):
    qi = pl.program_id(2)

    @pl.when(qi == 0)
    def _():
        vext_scr[:, :LANES] = v_ref[0]
        vext_scr[:, LANES:] = jnp.ones((t, LANES), BF16)

    lam = lam_ref[0]
    lo = _half_masks(tq)
    k = k_ref[0]
    v_ext = vext_scr[...]

    def scores(b):
        off = pl.multiple_of((t // tq - 1 - (qi * BLOCKS_DIFF + b)) * tq, tq)
        bias = tab_ref[0, :, pl.ds(off, t)]
        q = q_ref[0, b * tq:(b + 1) * tq, :]
        return [_dot_nt(_keep_half(q, lo, half), k) - bias for half in range(2)]

    def probs(ss):
        return [jnp.exp2((s - jnp.max(s, axis=-1, keepdims=True)).astype(BF16)) for s in ss]

    def finish(b, ps):
        outs = [_dot(p, v_ext) for p in ps]
        o1, o2 = (ol[:, :LANES] / ol[:, LANES:] for ol in outs)
        o_ref[b * tq:(b + 1) * tq, :] = (_rms(o1 - lam * o2, g_ref[...], 1e-5) * out_scale).astype(BF16)

    ss = scores(0)
    for b in range(BLOCKS_DIFF):
        ss_next = scores(b + 1) if b + 1 < BLOCKS_DIFF else None
        finish(b, probs(ss))
        ss = ss_next


def _diff_bias_table(t, tq):
    r = lax.broadcasted_iota(jnp.int32, (tq, 2 * t - tq), 0)
    x = lax.broadcasted_iota(jnp.int32, (tq, 2 * t - tq), 1)
    dist = jnp.abs(r - x + (t - tq)).astype(F32)
    return jnp.asarray([s * LOG2E for s in SLOPES_DIFF], F32)[:, None, None] * dist[None]


def _diff(qkv, lam, subln_g, n_seq, t, row_off, lam_init, prev):
    tq = TQ_DIFF
    rows = BLOCKS_DIFF * tq
    nq = t // rows
    blk0 = row_off // rows
    seq0 = row_off // t
    return _call_into(
        functools.partial(_diff_kernel, t=t, tq=tq, out_scale=1.0 - lam_init), prev, 6,
        grid=(DIFF_HEADS, n_seq, nq),
        in_specs=[
            pl.BlockSpec(memory_space=pltpu.SMEM),
            pl.BlockSpec((1, rows, LANES), lambda h, s, i: (BLK_DF_Q + h, blk0 + s * nq + i, 0)),
            pl.BlockSpec((1, t, LANES), lambda h, s, i: (BLK_DF_K + h, seq0 + s, 0)),
            pl.BlockSpec((1, t, LANES), lambda h, s, i: (BLK_DF_V + h, seq0 + s, 0)),
            pl.BlockSpec((1, tq, 2 * t - tq), lambda h, s, i: (h, 0, 0)),
            pl.BlockSpec((1, DIFF_DV), lambda h, s, i: (0, 0)),
        ],
        out_specs=pl.BlockSpec((rows, LANES), lambda h, s, i: (blk0 + s * nq + i, h)),
        out_shape=jax.ShapeDtypeStruct((qkv.shape[1], BRANCH_W), BF16),
        scratch_shapes=[pltpu.VMEM((t, 2 * LANES), BF16)],
        compiler_params=_params(("parallel", "arbitrary", "arbitrary")),
        name="mixer_b",
    )(lam, qkv, qkv, qkv, _diff_bias_table(t, tq), subln_g)


def _band_window(qi, tq, kw, radius, length):
    start = 0 if kw == length else pl.multiple_of(jnp.clip(qi * tq - radius, 0, length - kw), 64)
    rel = (lax.broadcasted_iota(jnp.int32, (tq, kw), 0) + (qi * tq - start)) \
        - lax.broadcasted_iota(jnp.int32, (tq, kw), 1)
    dist = jnp.abs(rel)
    return start, jnp.where(dist <= radius, dist.astype(F32), MASKED_DIST)


def _with_ones(v):
    return jnp.concatenate([v, jnp.ones(v.shape, v.dtype)], axis=1)


def _softmax_pv(s, v_ext, floor=None):
    m = jnp.max(s, axis=-1, keepdims=True)
    if floor is not None:
        m = jnp.maximum(m, floor)
    ol = _dot(jnp.exp2((s - m).astype(BF16)), v_ext)
    return ol[:, :LANES], ol[:, LANES:], m


def _dil_kernel(q_ref, k_ref, v_ref, o_ref, lse_ref, *, n_sub, tq, kw, radius, slopes, scale, nqb, nrb):
    for b in range(nqb):
        start, dist = _band_window(pl.program_id(2) * nqb + b, tq, kw, radius, n_sub)
        rows = slice(b * tq, (b + 1) * tq)
        for rr in range(nrb):
            lanes = slice(rr * LANES, (rr + 1) * LANES)
            for h in range(DIL_HEADS):
                k = k_ref[h, pl.ds(start, kw), lanes]
                v_ext = _with_ones(v_ref[h, pl.ds(start, kw), lanes])
                s = _dot_nt(q_ref[h, rows, lanes], k) * scale - slopes[h] * dist
                o, l, m = _softmax_pv(s, v_ext)
                o_ref[h, rows, lanes] = (o / l).astype(BF16)
                lse_ref[h, rows, lanes] = (m + jnp.log2(l)) * (1.0 / LOG2E)


def _dil_group(view, gi, n_seq, t, row_off, prev):
    dil = DIL_DILATIONS[gi]
    radius = DIL_WINDOWS[gi] // (2 * dil)
    n_sub = t // dil
    tq = min(TQ_DIL, n_sub)
    kw = min(tq + 2 * radius, n_sub)
    nqb = min(n_sub // tq, UNITS_DIL)
    nrb = min(dil, UNITS_DIL // nqb)
    nq = n_sub // (tq * nqb)
    blk0 = row_off // dil // (tq * nqb)
    seq0 = row_off // t
    qb, kb, vb = (BLK_DL_Q // 4, BLK_DL_K // 4, BLK_DL_V // 4) if gi == 0 else (0, 1, 2)
    slopes = tuple(SLOPES_DIL[gi * DIL_HEADS + h] * dil * LOG2E for h in range(DIL_HEADS))
    rows_out = view.shape[1]
    q_idx = lambda s, r, i: (qb, blk0 + s * nq + i, r)
    o_idx = lambda s, r, i: (0, blk0 + s * nq + i, r)
    o, lse = _call_into(
        functools.partial(_dil_kernel, n_sub=n_sub, tq=tq, kw=kw, radius=radius, slopes=slopes,
                          scale=DIL_DH ** -0.5 * LOG2E, nqb=nqb, nrb=nrb), prev, 3,
        grid=(n_seq, dil // nrb, nq),
        in_specs=[
            pl.BlockSpec((4, nqb * tq, nrb * LANES), q_idx),
            pl.BlockSpec((4, n_sub, nrb * LANES), lambda s, r, i: (kb, seq0 + s, r)),
            pl.BlockSpec((4, n_sub, nrb * LANES), lambda s, r, i: (vb, seq0 + s, r)),
        ],
        out_specs=[pl.BlockSpec((4, nqb * tq, nrb * LANES), o_idx),
                   pl.BlockSpec((4, nqb * tq, nrb * LANES), o_idx)],
        out_shape=[jax.ShapeDtypeStruct((DIL_HEADS, rows_out, dil * LANES), BF16),
                   jax.ShapeDtypeStruct((DIL_HEADS, rows_out, dil * LANES), F32)],
        compiler_params=_params(("parallel", "arbitrary", "arbitrary")),
        name=f"mixer_c{gi}",
    )(view, view, view)
    return o, lse


def _dil_combine_kernel(o0, o1, o2, l0, l1, l2, out_ref, o_scr, l_scr, *, tm):
    for h in range(DIL_HEADS):
        for gi, (o_ref, l_ref) in ((1, (o1, l1)), (2, (o2, l2))):
            dil = DIL_DILATIONS[gi]
            for r in range(dil):
                rows = pl.ds(r, tm // dil, stride=dil)
                o_scr[gi - 1, rows, :] = o_ref[h, :, r * LANES:(r + 1) * LANES].astype(F32)
                l_scr[gi - 1, rows, :] = l_ref[h, :, r * LANES:(r + 1) * LANES]
        a0, a1, a2 = l0[h], l_scr[0], l_scr[1]
        mx = jnp.maximum(jnp.maximum(a0, a1), a2)
        e0, e1, e2 = jnp.exp(a0 - mx), jnp.exp(a1 - mx), jnp.exp(a2 - mx)
        num = e0 * o0[h].astype(F32) + e1 * o_scr[0] + e2 * o_scr[1]
        out_ref[:, h * LANES:(h + 1) * LANES] = (num / (e0 + e1 + e2)).astype(BF16)


def _dil_combine(outs, lses):
    m = outs[0].shape[1]
    tm = min(TM_COMBINE, m)
    specs = [pl.BlockSpec((DIL_HEADS, tm // dil, dil * LANES), lambda i: (0, i, 0)) for dil in DIL_DILATIONS]
    return pl.pallas_call(
        functools.partial(_dil_combine_kernel, tm=tm),
        grid=(m // tm,),
        in_specs=specs * 2,
        out_specs=pl.BlockSpec((tm, BRANCH_W), lambda i: (i, 0)),
        out_shape=jax.ShapeDtypeStruct((m, BRANCH_W), BF16),
        scratch_shapes=[pltpu.VMEM((2, tm, LANES), F32), pltpu.VMEM((2, tm, LANES), F32)],
        compiler_params=_params(("parallel",)),
        name="mixer_c_combine",
    )(*outs, *lses)


def _swa_kernel(sink_ref, q_ref, k_ref, v_ref, o_ref, *, t, tq, kw):
    qi = pl.program_id(1)
    start, dist = _band_window(qi, tq, kw, SWA_RADIUS, t)
    k = k_ref[0, pl.ds(start, kw), :]
    v_ext = _with_ones(v_ref[0, pl.ds(start, kw), :])
    lo = _half_masks(tq)
    for g in range(SWA_GROUP):
        q = q_ref[g]
        outs = []
        for hk in range(SWA_KVH):
            head = hk * SWA_GROUP + g
            sink = sink_ref[head] * LOG2E
            s = _dot_nt(_keep_half(q, lo, hk), k) - (SLOPES_SWA[head] * LOG2E) * dist
            o, l, m = _softmax_pv(s, v_ext, floor=sink)
            outs.append(o / (l + jnp.exp2(sink - m)))
        o_ref[:, g * LANES:(g + 1) * LANES] = jnp.where(lo, outs[0], outs[1]).astype(BF16)


def _swa(qkv, sink, n_seq, t, row_off, prev):
    tq = TQ_SWA
    kw = min(tq + 2 * SWA_RADIUS, t)
    nq = t // tq
    blk0 = row_off // tq
    seq0 = row_off // t
    return _call_into(
        functools.partial(_swa_kernel, t=t, tq=tq, kw=kw), prev, 4,
        grid=(n_seq, nq),
        in_specs=[
            pl.BlockSpec(memory_space=pltpu.SMEM),
            pl.BlockSpec((4, tq, LANES), lambda s, i: (BLK_SW_Q // 4, blk0 + s * nq + i, 0)),
            pl.BlockSpec((1, t, LANES), lambda s, i: (BLK_SW_K, seq0 + s, 0)),
            pl.BlockSpec((1, t, LANES), lambda s, i: (BLK_SW_V, seq0 + s, 0)),
        ],
        out_specs=pl.BlockSpec((tq, BRANCH_W), lambda s, i: (blk0 + s * nq + i, 0)),
        out_shape=jax.ShapeDtypeStruct((qkv.shape[1], BRANCH_W), BF16),
        compiler_params=_params(("parallel", "arbitrary")),
        name="mixer_d",
    )(sink, qkv, qkv, qkv)


def _merge_kernel(h_ref, oa_ref, ob_ref, oc_ref, od_ref, wg_ref, bg_ref, wb_ref, wo_ref, xres_ref, o_ref, *, nc, tc):
    j = pl.program_id(1)
    h = h_ref[...]
    acc = None
    for n, b_ref in enumerate((oa_ref, ob_ref, oc_ref, od_ref)):
        gate = jax.nn.sigmoid(_dot(h, wg_ref[n]) + bg_ref[n:n + 1, :])
        term = gate * _dot(b_ref[...], wb_ref[n])
        acc = term if acc is None else acc + term
    merged = acc.astype(BF16)

    def project(first):
        for n in range(D // FFN_TN):
            cols = slice(n * FFN_TN, (n + 1) * FFN_TN)
            part = _dot(merged, wo_ref[:, cols])
            o_ref[:, cols] = part if first else o_ref[:, cols] + part

    pl.when(j == 0)(functools.partial(project, True))
    pl.when(j > 0)(functools.partial(project, False))

    for c in range(nc):
        @pl.when(j == c)
        def _(c=c):
            o_ref[:, c * tc:(c + 1) * tc] += xres_ref[...]


def _merge(x, h, branches, w_gate, b_gate, w_branch, w_out):
    m = x.shape[0]
    tm, tc = TM_MERGE, TC_MERGE
    nc = D // tc
    row_tile = lambda i, j: (i, 0)
    return pl.pallas_call(
        functools.partial(_merge_kernel, nc=nc, tc=tc),
        grid=(m // tm, nc),
        in_specs=[pl.BlockSpec((tm, D), row_tile)]
        + [pl.BlockSpec((tm, BRANCH_W), row_tile)] * 4
        + [
            pl.BlockSpec((None, 4, D, tc), lambda i, j: (j, 0, 0, 0)),
            pl.BlockSpec((4, tc), lambda i, j: (0, j)),
            pl.BlockSpec((4, BRANCH_W, tc), lambda i, j: (0, 0, j)),
            pl.BlockSpec((tc, D), lambda i, j: (j, 0)),
            pl.BlockSpec((tm, tc), lambda i, j: (i, j)),
        ],
        out_specs=pl.BlockSpec((tm, D), row_tile),
        out_shape=jax.ShapeDtypeStruct((m, D), F32),
        compiler_params=_params(("parallel", "arbitrary")),
        name="merge",
    )(h, *branches, w_gate, b_gate, w_branch, w_out, x)


def _cast_ffn_in_kernel(w_ref, o_ref):
    o_ref[0, :, :D_FF] = w_ref[...].astype(BF16)
    o_ref[0, :, D_FF:] = jnp.zeros((o_ref.shape[1], FFN_FP - D_FF), BF16)


def _cast_ffn_in(w_in, l):
    tr = 256
    return pl.pallas_call(
        _cast_ffn_in_kernel,
        grid=(2, D // tr),
        in_specs=[pl.BlockSpec((None, tr, D_FF), lambda part, i: (l, i, part))],
        out_specs=pl.BlockSpec((1, tr, FFN_FP), lambda part, i: (part, i, 0)),
        out_shape=jax.ShapeDtypeStruct((2, D, FFN_FP), BF16),
        compiler_params=_params(("parallel", "parallel")),
        name="cast_ffn_in",
    )(w_in)


def _cast_ffn_out_kernel(w_ref, o_ref, *, tr):
    row = pl.program_id(0) * tr + lax.broadcasted_iota(jnp.int32, (tr, D), 0)
    o_ref[...] = jnp.where(row < D_FF, w_ref[...], 0.0).astype(BF16)


def _cast_ffn_out(w_out, l):
    tr = 512
    return pl.pallas_call(
        functools.partial(_cast_ffn_out_kernel, tr=tr),
        grid=(FFN_FP // tr,),
        in_specs=[pl.BlockSpec((None, tr, D), lambda i: (l, i, 0))],
        out_specs=pl.BlockSpec((tr, D), lambda i: (i, 0)),
        out_shape=jax.ShapeDtypeStruct((FFN_FP, D), BF16),
        compiler_params=_params(("parallel",)),
        name="cast_ffn_out",
    )(w_out)


def _cast_kernel(w_ref, o_ref):
    o_ref[...] = w_ref[...].astype(BF16)


def _cast_gate(w_gate, l):
    tc = TC_MERGE
    nc = D // tc
    return pl.pallas_call(
        _cast_kernel,
        grid=(nc, 4),
        in_specs=[pl.BlockSpec((None, D, tc), lambda c, n: (l, 0, n * nc + c))],
        out_specs=pl.BlockSpec((None, None, D, tc), lambda c, n: (c, n, 0, 0)),
        out_shape=jax.ShapeDtypeStruct((nc, 4, D, tc), BF16),
        compiler_params=_params(("parallel", "parallel")),
        name="cast_gate",
    )(w_gate)


def _prep_ffn(w_in, w_out, l):
    return _cast_ffn_in(w_in, l), _cast_ffn_out(w_out, l)


def _swa_pair_heads(w, axis):
    shape = w.shape
    split = shape[:axis] + (SWA_KVH, SWA_GROUP, SWA_DH) + shape[axis + 1:]
    return jnp.swapaxes(w.reshape(split), axis, axis + 1).reshape(shape)


def _proj_columns(w_in):
    scale = np.ones((IN_WIDTH,), np.float32)
    scale[COL_NA:COL_NA + NA_HEADS * NA_DH] = NA_DH ** -0.5 * LOG2E
    scale[COL_DF:COL_DF + DIFF_HEADS * 2 * DIFF_DK] = DIFF_DK ** -0.5 * LOG2E
    scale[COL_SW:COL_SW + SWA_QH * SWA_DH] = SWA_DH ** -0.5 * LOG2E
    w = w_in * scale
    gw = DIL_HEADS * DIL_DH

    def dil_group(gi):
        return [w[:, COL_DL + part * DIL_W + gi * gw:COL_DL + part * DIL_W + (gi + 1) * gw] for part in range(3)]

    sw_q_end = COL_SW + SWA_QH * SWA_DH
    cols = [w[:, :COL_DL]] + dil_group(0) + [_swa_pair_heads(w[:, COL_SW:sw_q_end], 1), w[:, sw_q_end:]] \
        + dil_group(1) + dil_group(2)
    return jnp.concatenate(cols, axis=1)


def _prep_w_in(w_in):
    return _proj_columns(w_in).astype(BF16)


def _encoder_layer(x, l, seqs, p, final_gain=None):
    ffn1 = (p["norm_ffn1"][l][None], *_prep_ffn(p["w_ffn1_in"], p["w_ffn1_out"], l))
    if isinstance(x, tuple):
        m_total = sum(xb.shape[0] for xb in x)
        out, row_off = None, 0
        for xb in x:
            out = _ffn(xb, *ffn1, m_out=m_total, out_off=row_off, prev=None if out is None else (out,))
            row_off += xb.shape[0]
        x = out
    else:
        x = _ffn(x, *ffn1)
    h, qkv, qkv_d4, qkv_d16 = _proj(x, p["norm_mix"][l][None], _prep_w_in(p["w_in"][l]))
    dil_views = (qkv, qkv_d4, qkv_d16)

    lam_init = 0.8 - 0.6 * math.exp(-0.3 * l)
    lv = p["diff_lambda"][l].astype(F32)
    lam = jnp.exp(jnp.sum(lv[0] * lv[1])) - jnp.exp(jnp.sum(lv[2] * lv[3])) + lam_init
    diff_lam = lam.reshape(1)
    na_bias = _na_bias_table(p["na_rpb"][l])
    subln = p["diff_subln"][l].astype(F32)[None]
    sink = p["swa_sink"][l].astype(F32)

    o_a = o_b = o_d = None
    dil = [None] * DIL_GROUPS
    for n_seq, t, row_off in seqs:
        o_a = _na(qkv, na_bias, n_seq, t, row_off, None if o_a is None else (o_a,))
        o_b = _diff(qkv, diff_lam, subln, n_seq, t, row_off, lam_init, None if o_b is None else (o_b,))
        for gi in range(DIL_GROUPS):
            dil[gi] = _dil_group(dil_views[gi], gi, n_seq, t, row_off, dil[gi])
        o_d = _swa(qkv, sink, n_seq, t, row_off, None if o_d is None else (o_d,))
    o_c = _dil_combine([o for o, _ in dil], [lse for _, lse in dil])
    branches = [o_a, o_b, o_c, o_d]

    w_branch = p["w_branch"][l]
    w_branch = jnp.concatenate([w_branch[:3], _swa_pair_heads(w_branch[3:], 1)], axis=0).astype(BF16)
    x = _merge(x, h, branches, _cast_gate(p["w_gate"], l), p["b_gate"][l].astype(F32).reshape(4, D),
               w_branch, p["w_out"][l].astype(BF16))
    ffn2 = (p["norm_ffn2"][l][None], *_prep_ffn(p["w_ffn2_in"], p["w_ffn2_out"], l))
    if final_gain is None:
        return _ffn(x, *ffn2)
    return tuple(_ffn(x, *ffn2, rows=(row_off, n_seq * t), final_gain=final_gain) for n_seq, t, row_off in seqs)


def kernel(x_prompt, x_sample, norm_ffn1, w_ffn1_in, w_ffn1_out, norm_mix, w_in, na_rpb, diff_lambda, diff_subln,
           swa_sink, w_branch, w_gate, b_gate, w_out, norm_ffn2, w_ffn2_in, w_ffn2_out, norm_final):
    p = dict(norm_ffn1=norm_ffn1, w_ffn1_in=w_ffn1_in, w_ffn1_out=w_ffn1_out, norm_mix=norm_mix, w_in=w_in,
             na_rpb=na_rpb, diff_lambda=diff_lambda, diff_subln=diff_subln, swa_sink=swa_sink,
             w_branch=w_branch, w_gate=w_gate, b_gate=b_gate, w_out=w_out, norm_ffn2=norm_ffn2,
             w_ffn2_in=w_ffn2_in, w_ffn2_out=w_ffn2_out)
    bp, tp, _ = x_prompt.shape
    bs, ts, _ = x_sample.shape
    mp = bp * tp
    x = (x_prompt.reshape(mp, D).astype(F32), x_sample.reshape(bs * ts, D).astype(F32))
    seqs = ((bp, tp, 0), (bs, ts, mp))
    for l in range(DEPTH):
        x = _encoder_layer(x, l, seqs, p, final_gain=norm_final.astype(F32)[None] if l == DEPTH - 1 else None)
    y_prompt, y_sample = x
    return y_prompt.reshape(bp, tp, D), y_sample.reshape(bs, ts, D)
```

```python
import functools
import math

import jax
import jax.numpy as jnp
import numpy as np
from jax import lax
from jax.experimental import pallas as pl
from jax.experimental.pallas import tpu as pltpu

F32 = jnp.float32
BF16 = jnp.bfloat16

D = 2048
DEPTH = 2
GRID_W = 64
NA_HEADS, NA_DH, NA_ROWS, NA_COLS = 8, 64, 8, 16
DIFF_HEADS, DIFF_DK, DIFF_DV = 4, 64, 128
DIL_WINDOWS, DIL_DILATIONS = (128, 512, 2048), (1, 4, 16)
DIL_GROUPS, DIL_HEADS, DIL_DH = 3, 4, 128
SWA_QH, SWA_KVH, SWA_DH, SWA_RADIUS = 8, 2, 64, 128
SWA_GROUP = SWA_QH // SWA_KVH
BRANCH_W = 512
D_FF = ((8 * D // 3 + 127) // 128) * 128
N_ALIBI = SWA_QH + DIL_GROUPS * DIL_HEADS + DIFF_HEADS
RMS_EPS = 1e-6
NEG_INF = -1e30
LOG2E = math.log2(math.e)
MASKED_DIST = 1e32

LANES = 128
IN_WIDTH = 3 * NA_HEADS * NA_DH + 2 * DIFF_HEADS * 2 * DIFF_DK + DIFF_HEADS * DIFF_DV \
    + 3 * DIL_GROUPS * DIL_HEADS * DIL_DH + SWA_QH * SWA_DH + 2 * SWA_KVH * SWA_DH
N_BLK = IN_WIDTH // LANES
BLK_NA_Q, BLK_NA_K, BLK_NA_V = 0, 4, 8
BLK_DF_Q, BLK_DF_K, BLK_DF_V = 12, 16, 20
BLK_DL_Q, BLK_DL_K, BLK_DL_V = 24, 28, 32
BLK_SW_Q, BLK_SW_K, BLK_SW_V = 36, 40, 41
N_BLK_MAIN = 42
N_BLK_DIL = 3 * DIL_HEADS
COL_NA, COL_DF, COL_DL, COL_SW = 0, 1536, 3072, 7680
DIL_W = DIL_GROUPS * DIL_HEADS * DIL_DH

FFN_TF = 512
FFN_FP = -(-D_FF // FFN_TF) * FFN_TF
FFN_NC1 = FFN_FP // FFN_TF
FFN_TN = 512
TM_FFN = 1024
TM_PROJ = 1024
NB_PROJ = 6
TN_PROJ = NB_PROJ * LANES
NJ_MAIN = N_BLK_MAIN // NB_PROJ
NJ_DIL = N_BLK_DIL // NB_PROJ
TM_MERGE = 1024
TC_MERGE = 256
TM_COMBINE = 1024
RQ_NA = 4
NA_UROWS = 12
TQ_DIFF = 256
BLOCKS_DIFF = 4
TQ_SWA = 256
TQ_DIL = 128
UNITS_DIL = 8
VMEM_LIMIT = 60 * 1024 * 1024

_ALIBI = [2.0 ** (-8.0 * (i + 1) / N_ALIBI) for i in range(N_ALIBI)]
SLOPES_SWA = _ALIBI[:SWA_QH]
SLOPES_DIL = _ALIBI[SWA_QH:SWA_QH + DIL_GROUPS * DIL_HEADS]
SLOPES_DIFF = _ALIBI[SWA_QH + DIL_GROUPS * DIL_HEADS:]


def _params(sem):
    return pltpu.CompilerParams(dimension_semantics=sem, vmem_limit_bytes=VMEM_LIMIT)


def _call_into(body, prev, n_in, **kw):
    if prev is None:
        return pl.pallas_call(body, **kw)
    prev = tuple(prev)

    def aliased(*refs):
        body(*refs[:n_in], *refs[n_in + len(prev):])

    kw["in_specs"] = list(kw["in_specs"]) + [pl.BlockSpec(memory_space=pl.ANY)] * len(prev)
    call = pl.pallas_call(aliased, input_output_aliases={n_in + i: i for i in range(len(prev))}, **kw)
    return lambda *args: call(*args, *prev)


def _rms(x, g, eps):
    ms = jnp.mean(x * x, axis=-1, keepdims=True)
    return x * lax.rsqrt(ms + eps) * g


def _dot(a, b):
    return jnp.dot(a, b, preferred_element_type=F32)


def _dot_nt(a, b):
    return lax.dot_general(a, b, (((1,), (1,)), ((), ())), preferred_element_type=F32)


def _half_masks(rows):
    lane = lax.broadcasted_iota(jnp.int32, (rows, LANES), 1)
    return lane < (LANES // 2)


def _keep_half(x, lo, half):
    keep = lo if half == 0 else jnp.logical_not(lo)
    return jnp.where(keep, x.astype(F32), 0.0).astype(BF16)


def _ffn_kernel(*refs, final_norm):
    x_ref, g_ref, w1_ref, w2_ref = refs[:4]
    o_ref, h_scr = refs[-2:]
    j = pl.program_id(1)

    @pl.when(j == 0)
    def _():
        x = x_ref[...]
        h_scr[...] = _rms(x, g_ref[...], RMS_EPS).astype(BF16)
        o_ref[...] = x

    h = h_scr[...]
    gate = _dot(h, w1_ref[0])
    up = _dot(h, w1_ref[1])
    act = (gate * jax.nn.sigmoid(gate) * (0.5 * up)).astype(BF16)
    for n in range(D // FFN_TN):
        cols = slice(n * FFN_TN, (n + 1) * FFN_TN)
        o_ref[:, cols] += _dot(act, w2_ref[:, cols])

    if final_norm:
        @pl.when(j == FFN_NC1 - 1)
        def _():
            o_ref[...] = _rms(o_ref[...], refs[4][...], RMS_EPS)


def _ffn(x, g, w1, w2, rows=None, m_out=None, out_off=0, prev=None, final_gain=None):
    start, m = (0, x.shape[0]) if rows is None else rows
    m_out = m if m_out is None else m_out
    tm = TM_FFN
    in0, out0 = start // tm, out_off // tm
    inputs = [x, g, w1, w2] + ([] if final_gain is None else [final_gain])
    in_specs = [
        pl.BlockSpec((tm, D), lambda i, j: (in0 + i, 0)),
        pl.BlockSpec((1, D), lambda i, j: (0, 0)),
        pl.BlockSpec((None, 2, D, FFN_TF), lambda i, j: (j, 0, 0, 0)),
        pl.BlockSpec((FFN_TF, D), lambda i, j: (j, 0)),
    ] + ([] if final_gain is None else [pl.BlockSpec((1, D), lambda i, j: (0, 0))])
    return _call_into(
        functools.partial(_ffn_kernel, final_norm=final_gain is not None), prev, len(inputs),
        grid=(m // tm, FFN_NC1),
        in_specs=in_specs,
        out_specs=pl.BlockSpec((tm, D), lambda i, j: (out0 + i, 0)),
        out_shape=jax.ShapeDtypeStruct((m_out, D), F32),
        scratch_shapes=[pltpu.VMEM((tm, D), BF16)],
        compiler_params=_params(("parallel", "arbitrary")),
        name="ffn",
    )(*inputs)


def _proj_kernel(x_ref, g_ref, w_ref, h_ref, main_ref, d4_ref, d16_ref, res_scr, *, tm):
    j = pl.program_id(1)

    @pl.when(j == 0)
    def _():
        h_ref[...] = _rms(x_ref[...], g_ref[...], RMS_EPS).astype(BF16)

    res = _dot(h_ref[...], w_ref[...])

    @pl.when(j < NJ_MAIN)
    def _():
        for k in range(NB_PROJ):
            main_ref[k] = res[:, k * LANES:(k + 1) * LANES].astype(BF16)

    @pl.when(j >= NJ_MAIN)
    def _():
        for k in range(NB_PROJ):
            res_scr[k] = res[:, k * LANES:(k + 1) * LANES]

    for out_ref, dil, first in ((d4_ref, DIL_DILATIONS[1], NJ_MAIN), (d16_ref, DIL_DILATIONS[2], NJ_MAIN + NJ_DIL)):
        @pl.when((j >= first) & (j < first + NJ_DIL))
        def _(out_ref=out_ref, dil=dil):
            for k in range(NB_PROJ):
                for r in range(dil):
                    rows = res_scr[k, pl.ds(r, tm // dil, stride=dil), :]
                    out_ref[k, :, r * LANES:(r + 1) * LANES] = rows.astype(BF16)


def _proj(x, g, w):
    m = x.shape[0]
    tm = TM_PROJ
    d4, d16 = DIL_DILATIONS[1], DIL_DILATIONS[2]
    return pl.pallas_call(
        functools.partial(_proj_kernel, tm=tm),
        grid=(m // tm, NJ_MAIN + 2 * NJ_DIL),
        in_specs=[
            pl.BlockSpec((tm, D), lambda i, j: (i, 0)),
            pl.BlockSpec((1, D), lambda i, j: (0, 0)),
            pl.BlockSpec((D, TN_PROJ), lambda i, j: (0, j)),
        ],
        out_specs=[
            pl.BlockSpec((tm, D), lambda i, j: (i, 0)),
            pl.BlockSpec((NB_PROJ, tm, LANES), lambda i, j: (jnp.minimum(j, NJ_MAIN - 1), i, 0)),
            pl.BlockSpec((NB_PROJ, tm // d4, d4 * LANES),
                         lambda i, j: (jnp.clip(j - NJ_MAIN, 0, NJ_DIL - 1), i, 0)),
            pl.BlockSpec((NB_PROJ, tm // d16, d16 * LANES),
                         lambda i, j: (jnp.clip(j - NJ_MAIN - NJ_DIL, 0, NJ_DIL - 1), i, 0)),
        ],
        out_shape=[jax.ShapeDtypeStruct((m, D), BF16),
                   jax.ShapeDtypeStruct((N_BLK_MAIN, m, LANES), BF16),
                   jax.ShapeDtypeStruct((N_BLK_DIL, m // d4, d4 * LANES), BF16),
                   jax.ShapeDtypeStruct((N_BLK_DIL, m // d16, d16 * LANES), BF16)],
        scratch_shapes=[pltpu.VMEM((NB_PROJ, tm, LANES), F32)],
        compiler_params=_params(("parallel", "arbitrary")),
        name="proj",
    )(x, g, w)


def _na_kernel(q_ref, k_ref, v_ref, b_ref, o_ref, *, rows):
    nq = RQ_NA * GRID_W
    nk = NA_UROWS * GRID_W
    row0 = jnp.clip(RQ_NA * pl.program_id(1) - NA_ROWS // 2, 0, rows - NA_UROWS)
    start = pl.multiple_of(row0 * GRID_W, GRID_W)
    lo = _half_masks(nq)
    for hp in range(NA_HEADS // 2):
        q = q_ref[hp]
        k = k_ref[hp, pl.ds(start, nk), :]
        v_ext = _with_ones(v_ref[hp, pl.ds(start, nk), :])
        lhs = jnp.concatenate([_keep_half(q, lo, 0), _keep_half(q, lo, 1)], axis=0)
        o, l, _ = _softmax_pv(_dot_nt(lhs, k) + b_ref[0, hp], v_ext)
        o = o / l
        o_ref[:, hp * LANES:(hp + 1) * LANES] = jnp.where(lo, o[:nq], o[nq:]).astype(BF16)


def _na(qkv, bias, n_seq, t, row_off, prev):
    rows = t // GRID_W
    nb = rows // RQ_NA
    nq = RQ_NA * GRID_W
    blk0 = row_off // nq
    seq0 = row_off // t
    group_type = lambda g: jnp.where(g == 0, 0, jnp.where(g == nb - 1, 2, 1))
    return _call_into(
        functools.partial(_na_kernel, rows=rows), prev, 4,
        grid=(n_seq, nb),
        in_specs=[
            pl.BlockSpec((4, nq, LANES), lambda s, g: (BLK_NA_Q // 4, blk0 + s * nb + g, 0)),
            pl.BlockSpec((4, t, LANES), lambda s, g: (BLK_NA_K // 4, seq0 + s, 0)),
            pl.BlockSpec((4, t, LANES), lambda s, g: (BLK_NA_V // 4, seq0 + s, 0)),
            pl.BlockSpec((1, NA_HEADS // 2, 2 * nq, NA_UROWS * GRID_W), lambda s, g: (group_type(g), 0, 0, 0)),
        ],
        out_specs=pl.BlockSpec((nq, BRANCH_W), lambda s, g: (blk0 + s * nb + g, 0)),
        out_shape=jax.ShapeDtypeStruct((qkv.shape[1], BRANCH_W), BF16),
        compiler_params=_params(("parallel", "arbitrary")),
        name="mixer_a",
    )(qkv, qkv, qkv, bias)


def _na_bias_table(rpb):
    c = np.arange(GRID_W)[:, None]
    kc = np.arange(GRID_W)[None, :]
    cstart = np.clip(c - NA_COLS // 2, 0, GRID_W - NA_COLS)
    ok = (kc >= cstart) & (kc < cstart + NA_COLS)
    pad = GRID_W - NA_COLS
    padded = jnp.pad(rpb.astype(F32) * LOG2E, ((0, 0), (0, 0), (pad, pad)))
    e = jnp.stack([padded[..., GRID_W - 1 - q:2 * GRID_W - 1 - q] for q in range(GRID_W)], axis=2)
    e = jnp.where(ok[None, None], e, NEG_INF)
    neg = jnp.full((NA_HEADS, GRID_W, GRID_W), NEG_INF, F32)
    half = NA_ROWS // 2
    group_types = (
        [(-i, -i) for i in range(RQ_NA)],
        [(-half - i, -half) for i in range(RQ_NA)],
        [(-(NA_UROWS - RQ_NA) - i, -(NA_ROWS - RQ_NA) - i) for i in range(RQ_NA)],
    )
    tables = []
    for rel in group_types:
        q_rows = []
        for u0, w0 in rel:
            blocks = [e[:, u0 + j + NA_ROWS - 1] if w0 <= u0 + j < w0 + NA_ROWS else neg for j in range(NA_UROWS)]
            q_rows.append(jnp.concatenate(blocks, axis=2))
        tables.append(jnp.concatenate(q_rows, axis=1))
    return jnp.stack(tables).reshape(3, NA_HEADS // 2, 2 * RQ_NA * GRID_W, NA_UROWS * GRID_W)


def _diff_kernel(lam_ref, q_ref, k_ref, v_ref, tab_ref, g_ref, o_ref, vext_scr, *, t, tq, out_scale):
    qi = pl.program_id(2)

    @pl.when(qi == 0)
    def _():
        vext_scr[:, :LANES] = v_ref[0]
        vext_scr[:, LANES:] = jnp.ones((t, LANES), BF16)

    lam = lam_ref[0]
    lo = _half_masks(tq)
    k = k_ref[0]
    v_ext = vext_scr[...]

    def scores(b):
        off = pl.multiple_of((t // tq - 1 - (qi * BLOCKS_DIFF + b)) * tq, tq)
        bias = tab_ref[0, :, pl.ds(off, t)]
        q = q_ref[0, b * tq:(b + 1) * tq, :]
        return [_dot_nt(_keep_half(q, lo, half), k) - bias for half in range(2)]

    def probs(ss):
        return [jnp.exp2((s - jnp.max(s, axis=-1, keepdims=True)).astype(BF16)) for s in ss]

    def finish(b, ps):
        outs = [_dot(p, v_ext) for p in ps]
        o1, o2 = (ol[:, :LANES] / ol[:, LANES:] for ol in outs)
        o_ref[b * tq:(b + 1) * tq, :] = (_rms(o1 - lam * o2, g_ref[...], 1e-5) * out_scale).astype(BF16)

    ss = scores(0)
    for b in range(BLOCKS_DIFF):
        ss_next = scores(b + 1) if b + 1 < BLOCKS_DIFF else None
        finish(b, probs(ss))
        ss = ss_next


def _diff_bias_table(t, tq):
    r = lax.broadcasted_iota(jnp.int32, (tq, 2 * t - tq), 0)
    x = lax.broadcasted_iota(jnp.int32, (tq, 2 * t - tq), 1)
    dist = jnp.abs(r - x + (t - tq)).astype(F32)
    return jnp.asarray([s * LOG2E for s in SLOPES_DIFF], F32)[:, None, None] * dist[None]


def _diff(qkv, lam, subln_g, n_seq, t, row_off, lam_init, prev):
    tq = TQ_DIFF
    rows = BLOCKS_DIFF * tq
    nq = t // rows
    blk0 = row_off // rows
    seq0 = row_off // t
    return _call_into(
        functools.partial(_diff_kernel, t=t, tq=tq, out_scale=1.0 - lam_init), prev, 6,
        grid=(DIFF_HEADS, n_seq, nq),
        in_specs=[
            pl.BlockSpec(memory_space=pltpu.SMEM),
            pl.BlockSpec((1, rows, LANES), lambda h, s, i: (BLK_DF_Q + h, blk0 + s * nq + i, 0)),
            pl.BlockSpec((1, t, LANES), lambda h, s, i: (BLK_DF_K + h, seq0 + s, 0)),
            pl.BlockSpec((1, t, LANES), lambda h, s, i: (BLK_DF_V + h, seq0 + s, 0)),
            pl.BlockSpec((1, tq, 2 * t - tq), lambda h, s, i: (h, 0, 0)),
            pl.BlockSpec((1, DIFF_DV), lambda h, s, i: (0, 0)),
        ],
        out_specs=pl.BlockSpec((rows, LANES), lambda h, s, i: (blk0 + s * nq + i, h)),
        out_shape=jax.ShapeDtypeStruct((qkv.shape[1], BRANCH_W), BF16),
        scratch_shapes=[pltpu.VMEM((t, 2 * LANES), BF16)],
        compiler_params=_params(("parallel", "arbitrary", "arbitrary")),
        name="mixer_b",
    )(lam, qkv, qkv, qkv, _diff_bias_table(t, tq), subln_g)


def _band_window(qi, tq, kw, radius, length, segment=None):
    start = 0 if kw == length else pl.multiple_of(jnp.clip(qi * tq - radius, 0, length - kw), 64)
    col = lax.broadcasted_iota(jnp.int32, (tq, kw), 1)
    dist = jnp.abs(lax.broadcasted_iota(jnp.int32, (tq, kw), 0) + (qi * tq - start) - col)
    masked = jnp.where(dist <= radius, dist.astype(F32), MASKED_DIST)
    if segment is not None:
        masked = jnp.where((col + start) // segment == qi, masked, MASKED_DIST)
    return start, masked


def _with_ones(v):
    return jnp.concatenate([v, jnp.ones(v.shape, v.dtype)], axis=1)


def _softmax_pv(s, v_ext, floor=None):
    m = jnp.max(s, axis=-1, keepdims=True)
    if floor is not None:
        m = jnp.maximum(m, floor)
    ol = _dot(jnp.exp2((s - m).astype(BF16)), v_ext)
    return ol[:, :LANES], ol[:, LANES:], m


def _dil_kernel(q_ref, k_ref, v_ref, o_ref, lse_ref, *, n_sub, tq, kw, radius, slopes, scale, nqb, nrb, segment):
    for b in range(nqb):
        start, dist = _band_window(pl.program_id(2) * nqb + b, tq, kw, radius, n_sub, segment)
        rows = slice(b * tq, (b + 1) * tq)
        for rr in range(nrb):
            lanes = slice(rr * LANES, (rr + 1) * LANES)
            for h in range(DIL_HEADS):
                k = k_ref[h, pl.ds(start, kw), lanes]
                v_ext = _with_ones(v_ref[h, pl.ds(start, kw), lanes])
                s = _dot_nt(q_ref[h, rows, lanes], k) * scale - slopes[h] * dist
                o, l, m = _softmax_pv(s, v_ext)
                o_ref[h, rows, lanes] = (o / l).astype(BF16)
                lse_ref[h, rows, lanes] = (m + jnp.log2(l)) * (1.0 / LOG2E)


def _dil_group(view, gi, n_seq, t, row_off, prev):
    dil = DIL_DILATIONS[gi]
    radius = DIL_WINDOWS[gi] // (2 * dil)
    n_sub = t // dil
    segment = None
    if n_sub == TQ_DIL and n_seq % 2 == 0:
        segment, n_sub, n_seq, t = n_sub, 2 * n_sub, n_seq // 2, 2 * t
    tq = min(TQ_DIL, n_sub)
    kw = min(tq + 2 * radius, n_sub)
    nqb = min(n_sub // tq, UNITS_DIL)
    nrb = min(dil, UNITS_DIL // nqb)
    nq = n_sub // (tq * nqb)
    blk0 = row_off // dil // (tq * nqb)
    seq0 = row_off // t
    qb, kb, vb = (BLK_DL_Q // 4, BLK_DL_K // 4, BLK_DL_V // 4) if gi == 0 else (0, 1, 2)
    slopes = tuple(SLOPES_DIL[gi * DIL_HEADS + h] * dil * LOG2E for h in range(DIL_HEADS))
    rows_out = view.shape[1]
    q_idx = lambda s, r, i: (qb, blk0 + s * nq + i, r)
    o_idx = lambda s, r, i: (0, blk0 + s * nq + i, r)
    o, lse = _call_into(
        functools.partial(_dil_kernel, n_sub=n_sub, tq=tq, kw=kw, radius=radius, slopes=slopes,
                          scale=DIL_DH ** -0.5 * LOG2E, nqb=nqb, nrb=nrb, segment=segment), prev, 3,
        grid=(n_seq, dil // nrb, nq),
        in_specs=[
            pl.BlockSpec((4, nqb * tq, nrb * LANES), q_idx),
            pl.BlockSpec((4, n_sub, nrb * LANES), lambda s, r, i: (kb, seq0 + s, r)),
            pl.BlockSpec((4, n_sub, nrb * LANES), lambda s, r, i: (vb, seq0 + s, r)),
        ],
        out_specs=[pl.BlockSpec((4, nqb * tq, nrb * LANES), o_idx),
                   pl.BlockSpec((4, nqb * tq, nrb * LANES), o_idx)],
        out_shape=[jax.ShapeDtypeStruct((DIL_HEADS, rows_out, dil * LANES), BF16),
                   jax.ShapeDtypeStruct((DIL_HEADS, rows_out, dil * LANES), F32)],
        compiler_params=_params(("parallel", "arbitrary", "arbitrary")),
        name=f"mixer_c{gi}",
    )(view, view, view)
    return o, lse


def _dil_combine_kernel(o0, o1, o2, l0, l1, l2, out_ref, o_scr, l_scr, *, tm):
    for h in range(DIL_HEADS):
        for gi, (o_ref, l_ref) in ((1, (o1, l1)), (2, (o2, l2))):
            dil = DIL_DILATIONS[gi]
            for r in range(dil):
                rows = pl.ds(r, tm // dil, stride=dil)
                o_scr[gi - 1, rows, :] = o_ref[h, :, r * LANES:(r + 1) * LANES].astype(F32)
                l_scr[gi - 1, rows, :] = l_ref[h, :, r * LANES:(r + 1) * LANES]
        a0, a1, a2 = l0[h], l_scr[0], l_scr[1]
        mx = jnp.maximum(jnp.maximum(a0, a1), a2)
        e0, e1, e2 = jnp.exp(a0 - mx), jnp.exp(a1 - mx), jnp.exp(a2 - mx)
        num = e0 * o0[h].astype(F32) + e1 * o_scr[0] + e2 * o_scr[1]
        out_ref[:, h * LANES:(h + 1) * LANES] = (num / (e0 + e1 + e2)).astype(BF16)


def _dil_combine(outs, lses):
    m = outs[0].shape[1]
    tm = min(TM_COMBINE, m)
    specs = [pl.BlockSpec((DIL_HEADS, tm // dil, dil * LANES), lambda i: (0, i, 0)) for dil in DIL_DILATIONS]
    return pl.pallas_call(
        functools.partial(_dil_combine_kernel, tm=tm),
        grid=(m // tm,),
        in_specs=specs * 2,
        out_specs=pl.BlockSpec((tm, BRANCH_W), lambda i: (i, 0)),
        out_shape=jax.ShapeDtypeStruct((m, BRANCH_W), BF16),
        scratch_shapes=[pltpu.VMEM((2, tm, LANES), F32), pltpu.VMEM((2, tm, LANES), F32)],
        compiler_params=_params(("parallel",)),
        name="mixer_c_combine",
    )(*outs, *lses)


def _swa_kernel(sink_ref, q_ref, k_ref, v_ref, o_ref, *, t, tq, kw):
    qi = pl.program_id(1)
    start, dist = _band_window(qi, tq, kw, SWA_RADIUS, t)
    k = k_ref[0, pl.ds(start, kw), :]
    v_ext = _with_ones(v_ref[0, pl.ds(start, kw), :])
    lo = _half_masks(tq)
    for g in range(SWA_GROUP):
        q = q_ref[g]
        outs = []
        for hk in range(SWA_KVH):
            head = hk * SWA_GROUP + g
            sink = sink_ref[head] * LOG2E
            s = _dot_nt(_keep_half(q, lo, hk), k) - (SLOPES_SWA[head] * LOG2E) * dist
            o, l, m = _softmax_pv(s, v_ext, floor=sink)
            outs.append(o / (l + jnp.exp2(sink - m)))
        o_ref[:, g * LANES:(g + 1) * LANES] = jnp.where(lo, outs[0], outs[1]).astype(BF16)


def _swa(qkv, sink, n_seq, t, row_off, prev):
    tq = TQ_SWA
    kw = min(tq + 2 * SWA_RADIUS, t)
    nq = t // tq
    blk0 = row_off // tq
    seq0 = row_off // t
    return _call_into(
        functools.partial(_swa_kernel, t=t, tq=tq, kw=kw), prev, 4,
        grid=(n_seq, nq),
        in_specs=[
            pl.BlockSpec(memory_space=pltpu.SMEM),
            pl.BlockSpec((4, tq, LANES), lambda s, i: (BLK_SW_Q // 4, blk0 + s * nq + i, 0)),
            pl.BlockSpec((1, t, LANES), lambda s, i: (BLK_SW_K, seq0 + s, 0)),
            pl.BlockSpec((1, t, LANES), lambda s, i: (BLK_SW_V, seq0 + s, 0)),
        ],
        out_specs=pl.BlockSpec((tq, BRANCH_W), lambda s, i: (blk0 + s * nq + i, 0)),
        out_shape=jax.ShapeDtypeStruct((qkv.shape[1], BRANCH_W), BF16),
        compiler_params=_params(("parallel", "arbitrary")),
        name="mixer_d",
    )(sink, qkv, qkv, qkv)


def _merge_kernel(h_ref, oa_ref, ob_ref, oc_ref, od_ref, wg_ref, bg_ref, wb_ref, wo_ref, xres_ref, o_ref, *, nc, tc):
    j = pl.program_id(1)
    h = h_ref[...]
    acc = None
    for n, b_ref in enumerate((oa_ref, ob_ref, oc_ref, od_ref)):
        gate = jax.nn.sigmoid(_dot(h, wg_ref[n]) + bg_ref[n:n + 1, :])
        term = gate * _dot(b_ref[...], wb_ref[n])
        acc = term if acc is None else acc + term
    merged = acc.astype(BF16)

    def project(first):
        for n in range(D // FFN_TN):
            cols = slice(n * FFN_TN, (n + 1) * FFN_TN)
            part = _dot(merged, wo_ref[:, cols])
            o_ref[:, cols] = part if first else o_ref[:, cols] + part

    pl.when(j == 0)(functools.partial(project, True))
    pl.when(j > 0)(functools.partial(project, False))

    for c in range(nc):
        @pl.when(j == c)
        def _(c=c):
            o_ref[:, c * tc:(c + 1) * tc] += xres_ref[...]


def _merge(x, h, branches, w_gate, b_gate, w_branch, w_out):
    m = x.shape[0]
    tm, tc = TM_MERGE, TC_MERGE
    nc = D // tc
    row_tile = lambda i, j: (i, 0)
    return pl.pallas_call(
        functools.partial(_merge_kernel, nc=nc, tc=tc),
        grid=(m // tm, nc),
        in_specs=[pl.BlockSpec((tm, D), row_tile)]
        + [pl.BlockSpec((tm, BRANCH_W), row_tile)] * 4
        + [
            pl.BlockSpec((None, 4, D, tc), lambda i, j: (j, 0, 0, 0)),
            pl.BlockSpec((4, tc), lambda i, j: (0, j)),
            pl.BlockSpec((4, BRANCH_W, tc), lambda i, j: (0, 0, j)),
            pl.BlockSpec((tc, D), lambda i, j: (j, 0)),
            pl.BlockSpec((tm, tc), lambda i, j: (i, j)),
        ],
        out_specs=pl.BlockSpec((tm, D), row_tile),
        out_shape=jax.ShapeDtypeStruct((m, D), F32),
        compiler_params=_params(("parallel", "arbitrary")),
        name="merge",
    )(h, *branches, w_gate, b_gate, w_branch, w_out, x)


def _cast_ffn_in_kernel(w_ref, o_ref):
    for c in range(FFN_NC1):
        width = min(FFN_TF, D_FF - c * FFN_TF)
        o_ref[c, :, :width] = w_ref[:, c * FFN_TF:c * FFN_TF + width].astype(BF16)
        if width < FFN_TF:
            o_ref[c, :, width:] = jnp.zeros((o_ref.shape[1], FFN_TF - width), BF16)


def _cast_ffn_in(w_in, l):
    tr = 256
    return pl.pallas_call(
        _cast_ffn_in_kernel,
        grid=(2, D // tr),
        in_specs=[pl.BlockSpec((None, tr, D_FF), lambda part, i: (l, i, part))],
        out_specs=pl.BlockSpec((FFN_NC1, None, tr, FFN_TF), lambda part, i: (0, part, i, 0)),
        out_shape=jax.ShapeDtypeStruct((FFN_NC1, 2, D, FFN_TF), BF16),
        compiler_params=_params(("parallel", "parallel")),
        name="cast_ffn_in",
    )(w_in)


def _cast_ffn_out_kernel(w_ref, o_ref, *, tr):
    row = pl.program_id(0) * tr + lax.broadcasted_iota(jnp.int32, (tr, D), 0)
    o_ref[...] = jnp.where(row < D_FF, w_ref[...], 0.0).astype(BF16)


def _cast_ffn_out(w_out, l):
    tr = 512
    return pl.pallas_call(
        functools.partial(_cast_ffn_out_kernel, tr=tr),
        grid=(FFN_FP // tr,),
        in_specs=[pl.BlockSpec((None, tr, D), lambda i: (l, i, 0))],
        out_specs=pl.BlockSpec((tr, D), lambda i: (i, 0)),
        out_shape=jax.ShapeDtypeStruct((FFN_FP, D), BF16),
        compiler_params=_params(("parallel",)),
        name="cast_ffn_out",
    )(w_out)


def _cast_kernel(w_ref, o_ref):
    o_ref[...] = w_ref[...].astype(BF16)


def _cast_gate(w_gate, l):
    tc = TC_MERGE
    nc = D // tc
    return pl.pallas_call(
        _cast_kernel,
        grid=(nc, 4),
        in_specs=[pl.BlockSpec((None, D, tc), lambda c, n: (l, 0, n * nc + c))],
        out_specs=pl.BlockSpec((None, None, D, tc), lambda c, n: (c, n, 0, 0)),
        out_shape=jax.ShapeDtypeStruct((nc, 4, D, tc), BF16),
        compiler_params=_params(("parallel", "parallel")),
        name="cast_gate",
    )(w_gate)


def _prep_ffn(w_in, w_out, l):
    return _cast_ffn_in(w_in, l), _cast_ffn_out(w_out, l)


def _swa_pair_heads(w, axis):
    shape = w.shape
    split = shape[:axis] + (SWA_KVH, SWA_GROUP, SWA_DH) + shape[axis + 1:]
    return jnp.swapaxes(w.reshape(split), axis, axis + 1).reshape(shape)


def _proj_columns(w_in):
    scale = np.ones((IN_WIDTH,), np.float32)
    scale[COL_NA:COL_NA + NA_HEADS * NA_DH] = NA_DH ** -0.5 * LOG2E
    scale[COL_DF:COL_DF + DIFF_HEADS * 2 * DIFF_DK] = DIFF_DK ** -0.5 * LOG2E
    scale[COL_SW:COL_SW + SWA_QH * SWA_DH] = SWA_DH ** -0.5 * LOG2E
    w = w_in * scale
    gw = DIL_HEADS * DIL_DH

    def dil_group(gi):
        return [w[:, COL_DL + part * DIL_W + gi * gw:COL_DL + part * DIL_W + (gi + 1) * gw] for part in range(3)]

    sw_q_end = COL_SW + SWA_QH * SWA_DH
    cols = [w[:, :COL_DL]] + dil_group(0) + [_swa_pair_heads(w[:, COL_SW:sw_q_end], 1), w[:, sw_q_end:]] \
        + dil_group(1) + dil_group(2)
    return jnp.concatenate(cols, axis=1)


def _prep_w_in(w_in):
    return _proj_columns(w_in).astype(BF16)


def _encoder_layer(x, l, seqs, p, final_gain=None):
    ffn1 = (p["norm_ffn1"][l][None], *_prep_ffn(p["w_ffn1_in"], p["w_ffn1_out"], l))
    if isinstance(x, tuple):
        m_total = sum(xb.shape[0] for xb in x)
        out, row_off = None, 0
        for xb in x:
            out = _ffn(xb, *ffn1, m_out=m_total, out_off=row_off, prev=None if out is None else (out,))
            row_off += xb.shape[0]
        x = out
    else:
        x = _ffn(x, *ffn1)
    h, qkv, qkv_d4, qkv_d16 = _proj(x, p["norm_mix"][l][None], _prep_w_in(p["w_in"][l]))
    dil_views = (qkv, qkv_d4, qkv_d16)

    lam_init = 0.8 - 0.6 * math.exp(-0.3 * l)
    lv = p["diff_lambda"][l].astype(F32)
    lam = jnp.exp(jnp.sum(lv[0] * lv[1])) - jnp.exp(jnp.sum(lv[2] * lv[3])) + lam_init
    diff_lam = lam.reshape(1)
    na_bias = _na_bias_table(p["na_rpb"][l])
    subln = p["diff_subln"][l].astype(F32)[None]
    sink = p["swa_sink"][l].astype(F32)

    o_a = o_b = o_d = None
    dil = [None] * DIL_GROUPS
    for n_seq, t, row_off in seqs:
        o_a = _na(qkv, na_bias, n_seq, t, row_off, None if o_a is None else (o_a,))
        o_b = _diff(qkv, diff_lam, subln, n_seq, t, row_off, lam_init, None if o_b is None else (o_b,))
        for gi in range(DIL_GROUPS):
            dil[gi] = _dil_group(dil_views[gi], gi, n_seq, t, row_off, dil[gi])
        o_d = _swa(qkv, sink, n_seq, t, row_off, None if o_d is None else (o_d,))
    o_c = _dil_combine([o for o, _ in dil], [lse for _, lse in dil])
    branches = [o_a, o_b, o_c, o_d]

    w_branch = p["w_branch"][l]
    w_branch = jnp.concatenate([w_branch[:3], _swa_pair_heads(w_branch[3:], 1)], axis=0).astype(BF16)
    x = _merge(x, h, branches, _cast_gate(p["w_gate"], l), p["b_gate"][l].astype(F32).reshape(4, D),
               w_branch, p["w_out"][l].astype(BF16))
    ffn2 = (p["norm_ffn2"][l][None], *_prep_ffn(p["w_ffn2_in"], p["w_ffn2_out"], l))
    if final_gain is None:
        return _ffn(x, *ffn2)
    return tuple(_ffn(x, *ffn2, rows=(row_off, n_seq * t), final_gain=final_gain) for n_seq, t, row_off in seqs)


def kernel(x_prompt, x_sample, norm_ffn1, w_ffn1_in, w_ffn1_out, norm_mix, w_in, na_rpb, diff_lambda, diff_subln,
           swa_sink, w_branch, w_gate, b_gate, w_out, norm_ffn2, w_ffn2_in, w_ffn2_out, norm_final):
    p = dict(norm_ffn1=norm_ffn1, w_ffn1_in=w_ffn1_in, w_ffn1_out=w_ffn1_out, norm_mix=norm_mix, w_in=w_in,
             na_rpb=na_rpb, diff_lambda=diff_lambda, diff_subln=diff_subln, swa_sink=swa_sink,
             w_branch=w_branch, w_gate=w_gate, b_gate=b_gate, w_out=w_out, norm_ffn2=norm_ffn2,
             w_ffn2_in=w_ffn2_in, w_ffn2_out=w_ffn2_out)
    bp, tp, _ = x_prompt.shape
    bs, ts, _ = x_sample.shape
    mp = bp * tp
    x = (x_prompt.reshape(mp, D).astype(F32), x_sample.reshape(bs * ts, D).astype(F32))
    seqs = ((bp, tp, 0), (bs, ts, mp))
    for l in range(DEPTH):
        x = _encoder_layer(x, l, seqs, p, final_gain=norm_final.astype(F32)[None] if l == DEPTH - 1 else None)
    y_prompt, y_sample = x
    return y_prompt.reshape(bp, tp, D), y_sample.reshape(bs, ts, D)
```

```python
import functools
import math

import jax
import jax.numpy as jnp
import numpy as np
from jax import lax
from jax.experimental import pallas as pl
from jax.experimental.pallas import tpu as pltpu

F32 = jnp.float32
BF16 = jnp.bfloat16

D = 2048
DEPTH = 2
GRID_W = 64
NA_HEADS, NA_DH, NA_ROWS, NA_COLS = 8, 64, 8, 16
DIFF_HEADS, DIFF_DK, DIFF_DV = 4, 64, 128
DIL_WINDOWS, DIL_DILATIONS = (128, 512, 2048), (1, 4, 16)
DIL_GROUPS, DIL_HEADS, DIL_DH = 3, 4, 128
SWA_QH, SWA_KVH, SWA_DH, SWA_RADIUS = 8, 2, 64, 128
SWA_GROUP = SWA_QH // SWA_KVH
BRANCH_W = 512
D_FF = ((8 * D // 3 + 127) // 128) * 128
N_ALIBI = SWA_QH + DIL_GROUPS * DIL_HEADS + DIFF_HEADS
RMS_EPS = 1e-6
NEG_INF = -1e30
LOG2E = math.log2(math.e)
MASKED_DIST = 1e32

LANES = 128
IN_WIDTH = 3 * NA_HEADS * NA_DH + 2 * DIFF_HEADS * 2 * DIFF_DK + DIFF_HEADS * DIFF_DV \
    + 3 * DIL_GROUPS * DIL_HEADS * DIL_DH + SWA_QH * SWA_DH + 2 * SWA_KVH * SWA_DH
N_BLK = IN_WIDTH // LANES
BLK_NA_Q, BLK_NA_K, BLK_NA_V = 0, 4, 8
BLK_DF_Q, BLK_DF_K, BLK_DF_V = 12, 16, 20
BLK_DL_Q, BLK_DL_K, BLK_DL_V = 24, 28, 32
BLK_SW_Q, BLK_SW_K, BLK_SW_V = 36, 40, 41
N_BLK_MAIN = 42
N_BLK_DIL = 3 * DIL_HEADS
COL_NA, COL_DF, COL_DL, COL_SW = 0, 1536, 3072, 7680
DIL_W = DIL_GROUPS * DIL_HEADS * DIL_DH

FFN_TF = 512
FFN_FP = -(-D_FF // FFN_TF) * FFN_TF
FFN_NC1 = FFN_FP // FFN_TF
FFN_TN = 512
TM_FFN = 1024
TM_PROJ = 1024
NB_PROJ = 6
TN_PROJ = NB_PROJ * LANES
NJ_MAIN = N_BLK_MAIN // NB_PROJ
NJ_DIL = N_BLK_DIL // NB_PROJ
TM_MERGE = 1024
TC_MERGE = 256
TM_COMBINE = 1024
RQ_NA = 4
NA_UROWS = 12
TQ_DIFF = 256
BLOCKS_DIFF = 4
TQ_SWA = 256
TQ_DIL = 128
UNITS_DIL = 8
VMEM_LIMIT = 60 * 1024 * 1024

_ALIBI = [2.0 ** (-8.0 * (i + 1) / N_ALIBI) for i in range(N_ALIBI)]
SLOPES_SWA = _ALIBI[:SWA_QH]
SLOPES_DIL = _ALIBI[SWA_QH:SWA_QH + DIL_GROUPS * DIL_HEADS]
SLOPES_DIFF = _ALIBI[SWA_QH + DIL_GROUPS * DIL_HEADS:]


def _params(sem):
    return pltpu.CompilerParams(dimension_semantics=sem, vmem_limit_bytes=VMEM_LIMIT)


def _call_into(body, prev, n_in, **kw):
    if prev is None:
        return pl.pallas_call(body, **kw)
    prev = tuple(prev)

    def aliased(*refs):
        body(*refs[:n_in], *refs[n_in + len(prev):])

    kw["in_specs"] = list(kw["in_specs"]) + [pl.BlockSpec(memory_space=pl.ANY)] * len(prev)
    call = pl.pallas_call(aliased, input_output_aliases={n_in + i: i for i in range(len(prev))}, **kw)
    return lambda *args: call(*args, *prev)


def _rms(x, g, eps):
    ms = jnp.mean(x * x, axis=-1, keepdims=True)
    return x * lax.rsqrt(ms + eps) * g


def _dot(a, b):
    return jnp.dot(a, b, preferred_element_type=F32)


def _dot_nt(a, b):
    return lax.dot_general(a, b, (((1,), (1,)), ((), ())), preferred_element_type=F32)


def _half_masks(rows):
    lane = lax.broadcasted_iota(jnp.int32, (rows, LANES), 1)
    return lane < (LANES // 2)


def _keep_half(x, lo, half):
    keep = lo if half == 0 else jnp.logical_not(lo)
    return jnp.where(keep, x.astype(F32), 0.0).astype(BF16)


def _ffn_kernel(*refs, final_norm):
    x_ref, g_ref, w1_ref, w2_ref = refs[:4]
    o_ref, h_scr = refs[-2:]
    j = pl.program_id(1)

    @pl.when(j == 0)
    def _():
        x = x_ref[...]
        h_scr[...] = _rms(x, g_ref[...], RMS_EPS).astype(BF16)
        o_ref[...] = x

    h = h_scr[...]
    gate = _dot(h, w1_ref[0])
    up = _dot(h, w1_ref[1])
    act = (gate * jax.nn.sigmoid(gate) * (0.5 * up)).astype(BF16)
    for n in range(D // FFN_TN):
        cols = slice(n * FFN_TN, (n + 1) * FFN_TN)
        o_ref[:, cols] += _dot(act, w2_ref[:, cols])

    if final_norm:
        @pl.when(j == FFN_NC1 - 1)
        def _():
            o_ref[...] = _rms(o_ref[...], refs[4][...], RMS_EPS)


def _ffn(x, g, w1, w2, rows=None, m_out=None, out_off=0, prev=None, final_gain=None):
    start, m = (0, x.shape[0]) if rows is None else rows
    m_out = m if m_out is None else m_out
    tm = TM_FFN
    in0, out0 = start // tm, out_off // tm
    inputs = [x, g, w1, w2] + ([] if final_gain is None else [final_gain])
    in_specs = [
        pl.BlockSpec((tm, D), lambda i, j: (in0 + i, 0)),
        pl.BlockSpec((1, D), lambda i, j: (0, 0)),
        pl.BlockSpec((None, 2, D, FFN_TF), lambda i, j: (j, 0, 0, 0)),
        pl.BlockSpec((FFN_TF, D), lambda i, j: (j, 0)),
    ] + ([] if final_gain is None else [pl.BlockSpec((1, D), lambda i, j: (0, 0))])
    return _call_into(
        functools.partial(_ffn_kernel, final_norm=final_gain is not None), prev, len(inputs),
        grid=(m // tm, FFN_NC1),
        in_specs=in_specs,
        out_specs=pl.BlockSpec((tm, D), lambda i, j: (out0 + i, 0)),
        out_shape=jax.ShapeDtypeStruct((m_out, D), F32),
        scratch_shapes=[pltpu.VMEM((tm, D), BF16)],
        compiler_params=_params(("parallel", "arbitrary")),
        name="ffn",
    )(*inputs)


def _proj_kernel(x_ref, g_ref, w_ref, h_ref, main_ref, d4_ref, d16_ref, res_scr, *, tm):
    j = pl.program_id(1)

    @pl.when(j == 0)
    def _():
        h_ref[...] = _rms(x_ref[...], g_ref[...], RMS_EPS).astype(BF16)

    res = _dot(h_ref[...], w_ref[...])

    @pl.when(j < NJ_MAIN)
    def _():
        for k in range(NB_PROJ):
            main_ref[k] = res[:, k * LANES:(k + 1) * LANES].astype(BF16)

    @pl.when(j >= NJ_MAIN)
    def _():
        for k in range(NB_PROJ):
            res_scr[k] = res[:, k * LANES:(k + 1) * LANES]

    for out_ref, dil, first in ((d4_ref, DIL_DILATIONS[1], NJ_MAIN), (d16_ref, DIL_DILATIONS[2], NJ_MAIN + NJ_DIL)):
        @pl.when((j >= first) & (j < first + NJ_DIL))
        def _(out_ref=out_ref, dil=dil):
            for k in range(NB_PROJ):
                for r in range(dil):
                    rows = res_scr[k, pl.ds(r, tm // dil, stride=dil), :]
                    out_ref[k, :, r * LANES:(r + 1) * LANES] = rows.astype(BF16)


def _proj(x, g, w):
    m = x.shape[0]
    tm = TM_PROJ
    d4, d16 = DIL_DILATIONS[1], DIL_DILATIONS[2]
    return pl.pallas_call(
        functools.partial(_proj_kernel, tm=tm),
        grid=(m // tm, NJ_MAIN + 2 * NJ_DIL),
        in_specs=[
            pl.BlockSpec((tm, D), lambda i, j: (i, 0)),
            pl.BlockSpec((1, D), lambda i, j: (0, 0)),
            pl.BlockSpec((D, TN_PROJ), lambda i, j: (0, j)),
        ],
        out_specs=[
            pl.BlockSpec((tm, D), lambda i, j: (i, 0)),
            pl.BlockSpec((NB_PROJ, tm, LANES), lambda i, j: (jnp.minimum(j, NJ_MAIN - 1), i, 0)),
            pl.BlockSpec((NB_PROJ, tm // d4, d4 * LANES),
                         lambda i, j: (jnp.clip(j - NJ_MAIN, 0, NJ_DIL - 1), i, 0)),
            pl.BlockSpec((NB_PROJ, tm // d16, d16 * LANES),
                         lambda i, j: (jnp.clip(j - NJ_MAIN - NJ_DIL, 0, NJ_DIL - 1), i, 0)),
        ],
        out_shape=[jax.ShapeDtypeStruct((m, D), BF16),
                   jax.ShapeDtypeStruct((N_BLK_MAIN, m, LANES), BF16),
                   jax.ShapeDtypeStruct((N_BLK_DIL, m // d4, d4 * LANES), BF16),
                   jax.ShapeDtypeStruct((N_BLK_DIL, m // d16, d16 * LANES), BF16)],
        scratch_shapes=[pltpu.VMEM((NB_PROJ, tm, LANES), F32)],
        compiler_params=_params(("parallel", "arbitrary")),
        name="proj",
    )(x, g, w)


def _na_kernel(q_ref, k_ref, v_ref, b_ref, o_ref, *, rows):
    nq = RQ_NA * GRID_W
    nk = NA_UROWS * GRID_W
    row0 = jnp.clip(RQ_NA * pl.program_id(1) - NA_ROWS // 2, 0, rows - NA_UROWS)
    start = pl.multiple_of(row0 * GRID_W, GRID_W)
    lo = _half_masks(nq)
    for hp in range(NA_HEADS // 2):
        q = q_ref[hp]
        k = k_ref[hp, pl.ds(start, nk), :]
        v_ext = _with_ones(v_ref[hp, pl.ds(start, nk), :])
        lhs = jnp.concatenate([_keep_half(q, lo, 0), _keep_half(q, lo, 1)], axis=0)
        o, l, _ = _softmax_pv(_dot_nt(lhs, k) + b_ref[0, hp], v_ext)
        o = o / l
        o_ref[:, hp * LANES:(hp + 1) * LANES] = jnp.where(lo, o[:nq], o[nq:]).astype(BF16)


def _na(qkv, bias, n_seq, t, row_off, prev):
    rows = t // GRID_W
    nb = rows // RQ_NA
    nq = RQ_NA * GRID_W
    blk0 = row_off // nq
    seq0 = row_off // t
    group_type = lambda g: jnp.where(g == 0, 0, jnp.where(g == nb - 1, 2, 1))
    return _call_into(
        functools.partial(_na_kernel, rows=rows), prev, 4,
        grid=(n_seq, nb),
        in_specs=[
            pl.BlockSpec((4, nq, LANES), lambda s, g: (BLK_NA_Q // 4, blk0 + s * nb + g, 0)),
            pl.BlockSpec((4, t, LANES), lambda s, g: (BLK_NA_K // 4, seq0 + s, 0)),
            pl.BlockSpec((4, t, LANES), lambda s, g: (BLK_NA_V // 4, seq0 + s, 0)),
            pl.BlockSpec((1, NA_HEADS // 2, 2 * nq, NA_UROWS * GRID_W), lambda s, g: (group_type(g), 0, 0, 0)),
        ],
        out_specs=pl.BlockSpec((nq, BRANCH_W), lambda s, g: (blk0 + s * nb + g, 0)),
        out_shape=jax.ShapeDtypeStruct((qkv.shape[1], BRANCH_W), BF16),
        compiler_params=_params(("parallel", "arbitrary")),
        name="mixer_a",
    )(qkv, qkv, qkv, bias)


def _na_bias_table(rpb):
    c = np.arange(GRID_W)[:, None]
    kc = np.arange(GRID_W)[None, :]
    cstart = np.clip(c - NA_COLS // 2, 0, GRID_W - NA_COLS)
    ok = (kc >= cstart) & (kc < cstart + NA_COLS)
    n_off = 2 * NA_COLS - 1
    col_pick = (np.arange(n_off)[:, None, None] == (kc - c + NA_COLS - 1)[None]) & ok[None]
    half = NA_ROWS // 2
    group_types = (
        [(-i, -i) for i in range(RQ_NA)],
        [(-half - i, -half) for i in range(RQ_NA)],
        [(-(NA_UROWS - RQ_NA) - i, -(NA_ROWS - RQ_NA) - i) for i in range(RQ_NA)],
    )
    row_pick = np.zeros((3, RQ_NA, NA_UROWS, 2 * NA_ROWS - 1), bool)
    for ty, rel in enumerate(group_types):
        for i, (u0, w0) in enumerate(rel):
            for j in range(NA_UROWS):
                if w0 <= u0 + j < w0 + NA_ROWS:
                    row_pick[ty, i, j, u0 + j + NA_ROWS - 1] = True
    bias = rpb.astype(F32) * LOG2E
    by_col = jnp.einsum('hdx,xck->hdck', bias, col_pick.astype(np.float32), precision=lax.Precision.HIGHEST)
    tab = jnp.einsum('hdck,tijd->thicjk', by_col, row_pick.astype(np.float32), precision=lax.Precision.HIGHEST)
    inside = row_pick.any(axis=-1)[:, None, :, None, :, None] & ok[None, None, None, :, None, :]
    tab = jnp.where(inside, tab, NEG_INF)
    return tab.reshape(3, NA_HEADS // 2, 2 * RQ_NA * GRID_W, NA_UROWS * GRID_W)


def _diff_kernel(lam_ref, q_ref, k_ref, v_ref, tab_ref, g_ref, o_ref, vext_scr, *, t, tq, out_scale):
    qi = pl.program_id(2)

    @pl.when(qi == 0)
    def _():
        vext_scr[:, :LANES] = v_ref[0]
        vext_scr[:, LANES:] = jnp.ones((t, LANES), BF16)

    lam = lam_ref[0]
    lo = _half_masks(tq)
    k = k_ref[0]
    v_ext = vext_scr[...]

    def scores(b):
        off = pl.multiple_of((t // tq - 1 - (qi * BLOCKS_DIFF + b)) * tq, tq)
        bias = tab_ref[0, :, pl.ds(off, t)]
        q = q_ref[0, b * tq:(b + 1) * tq, :]
        return [_dot_nt(_keep_half(q, lo, half), k) - bias for half in range(2)]

    def probs(ss):
        return [jnp.exp2((s - jnp.max(s, axis=-1, keepdims=True)).astype(BF16)) for s in ss]

    def finish(b, ps):
        outs = [_dot(p, v_ext) for p in ps]
        o1, o2 = (ol[:, :LANES] / ol[:, LANES:] for ol in outs)
        o_ref[b * tq:(b + 1) * tq, :] = (_rms(o1 - lam * o2, g_ref[...], 1e-5) * out_scale).astype(BF16)

    ss = scores(0)
    for b in range(BLOCKS_DIFF):
        ss_next = scores(b + 1) if b + 1 < BLOCKS_DIFF else None
        finish(b, probs(ss))
        ss = ss_next


def _diff_bias_table(t, tq):
    r = lax.broadcasted_iota(jnp.int32, (tq, 2 * t - tq), 0)
    x = lax.broadcasted_iota(jnp.int32, (tq, 2 * t - tq), 1)
    dist = jnp.abs(r - x + (t - tq)).astype(F32)
    return jnp.asarray([s * LOG2E for s in SLOPES_DIFF], F32)[:, None, None] * dist[None]


def _diff(qkv, lam, subln_g, n_seq, t, row_off, lam_init, prev):
    tq = TQ_DIFF
    rows = BLOCKS_DIFF * tq
    nq = t // rows
    blk0 = row_off // rows
    seq0 = row_off // t
    return _call_into(
        functools.partial(_diff_kernel, t=t, tq=tq, out_scale=1.0 - lam_init), prev, 6,
        grid=(DIFF_HEADS, n_seq, nq),
        in_specs=[
            pl.BlockSpec(memory_space=pltpu.SMEM),
            pl.BlockSpec((1, rows, LANES), lambda h, s, i: (BLK_DF_Q + h, blk0 + s * nq + i, 0)),
            pl.BlockSpec((1, t, LANES), lambda h, s, i: (BLK_DF_K + h, seq0 + s, 0)),
            pl.BlockSpec((1, t, LANES), lambda h, s, i: (BLK_DF_V + h, seq0 + s, 0)),
            pl.BlockSpec((1, tq, 2 * t - tq), lambda h, s, i: (h, 0, 0)),
            pl.BlockSpec((1, DIFF_DV), lambda h, s, i: (0, 0)),
        ],
        out_specs=pl.BlockSpec((rows, LANES), lambda h, s, i: (blk0 + s * nq + i, h)),
        out_shape=jax.ShapeDtypeStruct((qkv.shape[1], BRANCH_W), BF16),
        scratch_shapes=[pltpu.VMEM((t, 2 * LANES), BF16)],
        compiler_params=_params(("parallel", "arbitrary", "arbitrary")),
        name="mixer_b",
    )(lam, qkv, qkv, qkv, _diff_bias_table(t, tq), subln_g)


def _band_window(qi, tq, kw, radius, length, segment=None):
    start = 0 if kw == length else pl.multiple_of(jnp.clip(qi * tq - radius, 0, length - kw), 64)
    col = lax.broadcasted_iota(jnp.int32, (tq, kw), 1)
    dist = jnp.abs(lax.broadcasted_iota(jnp.int32, (tq, kw), 0) + (qi * tq - start) - col)
    masked = jnp.where(dist <= radius, dist.astype(F32), MASKED_DIST)
    if segment is not None:
        masked = jnp.where((col + start) // segment == qi, masked, MASKED_DIST)
    return start, masked


def _with_ones(v):
    return jnp.concatenate([v, jnp.ones(v.shape, v.dtype)], axis=1)


def _softmax_pv(s, v_ext, floor=None):
    m = jnp.max(s, axis=-1, keepdims=True)
    if floor is not None:
        m = jnp.maximum(m, floor)
    ol = _dot(jnp.exp2((s - m).astype(BF16)), v_ext)
    return ol[:, :LANES], ol[:, LANES:], m


def _dil_kernel(q_ref, k_ref, v_ref, o_ref, lse_ref, *, n_sub, tq, kw, radius, slopes, scale, nqb, nrb, segment):
    for b in range(nqb):
        start, dist = _band_window(pl.program_id(2) * nqb + b, tq, kw, radius, n_sub, segment)
        rows = slice(b * tq, (b + 1) * tq)
        for rr in range(nrb):
            lanes = slice(rr * LANES, (rr + 1) * LANES)
            for h in range(DIL_HEADS):
                k = k_ref[h, pl.ds(start, kw), lanes]
                v_ext = _with_ones(v_ref[h, pl.ds(start, kw), lanes])
                s = _dot_nt(q_ref[h, rows, lanes], k) * scale - slopes[h] * dist
                o, l, m = _softmax_pv(s, v_ext)
                o_ref[h, rows, lanes] = (o / l).astype(BF16)
                lse_ref[h, rows, lanes] = (m + jnp.log2(l)) * (1.0 / LOG2E)


def _dil_group(view, gi, n_seq, t, row_off, prev):
    dil = DIL_DILATIONS[gi]
    radius = DIL_WINDOWS[gi] // (2 * dil)
    n_sub = t // dil
    segment = None
    if n_sub == TQ_DIL and n_seq % 2 == 0:
        segment, n_sub, n_seq, t = n_sub, 2 * n_sub, n_seq // 2, 2 * t
    tq = min(TQ_DIL, n_sub)
    kw = min(tq + 2 * radius, n_sub)
    nqb = min(n_sub // tq, UNITS_DIL)
    nrb = min(dil, UNITS_DIL // nqb)
    nq = n_sub // (tq * nqb)
    blk0 = row_off // dil // (tq * nqb)
    seq0 = row_off // t
    qb, kb, vb = (BLK_DL_Q // 4, BLK_DL_K // 4, BLK_DL_V // 4) if gi == 0 else (0, 1, 2)
    slopes = tuple(SLOPES_DIL[gi * DIL_HEADS + h] * dil * LOG2E for h in range(DIL_HEADS))
    rows_out = view.shape[1]
    q_idx = lambda s, r, i: (qb, blk0 + s * nq + i, r)
    o_idx = lambda s, r, i: (0, blk0 + s * nq + i, r)
    o, lse = _call_into(
        functools.partial(_dil_kernel, n_sub=n_sub, tq=tq, kw=kw, radius=radius, slopes=slopes,
                          scale=DIL_DH ** -0.5 * LOG2E, nqb=nqb, nrb=nrb, segment=segment), prev, 3,
        grid=(n_seq, dil // nrb, nq),
        in_specs=[
            pl.BlockSpec((4, nqb * tq, nrb * LANES), q_idx),
            pl.BlockSpec((4, n_sub, nrb * LANES), lambda s, r, i: (kb, seq0 + s, r)),
            pl.BlockSpec((4, n_sub, nrb * LANES), lambda s, r, i: (vb, seq0 + s, r)),
        ],
        out_specs=[pl.BlockSpec((4, nqb * tq, nrb * LANES), o_idx),
                   pl.BlockSpec((4, nqb * tq, nrb * LANES), o_idx)],
        out_shape=[jax.ShapeDtypeStruct((DIL_HEADS, rows_out, dil * LANES), BF16),
                   jax.ShapeDtypeStruct((DIL_HEADS, rows_out, dil * LANES), F32)],
        compiler_params=_params(("parallel", "arbitrary", "arbitrary")),
        name=f"mixer_c{gi}",
    )(view, view, view)
    return o, lse


def _dil_combine_kernel(o0, o1, o2, l0, l1, l2, out_ref, o_scr, l_scr, *, tm):
    for h in range(DIL_HEADS):
        for gi, (o_ref, l_ref) in ((1, (o1, l1)), (2, (o2, l2))):
            dil = DIL_DILATIONS[gi]
            for r in range(dil):
                rows = pl.ds(r, tm // dil, stride=dil)
                o_scr[gi - 1, rows, :] = o_ref[h, :, r * LANES:(r + 1) * LANES].astype(F32)
                l_scr[gi - 1, rows, :] = l_ref[h, :, r * LANES:(r + 1) * LANES]
        a0, a1, a2 = l0[h], l_scr[0], l_scr[1]
        mx = jnp.maximum(jnp.maximum(a0, a1), a2)
        e0, e1, e2 = jnp.exp(a0 - mx), jnp.exp(a1 - mx), jnp.exp(a2 - mx)
        num = e0 * o0[h].astype(F32) + e1 * o_scr[0] + e2 * o_scr[1]
        out_ref[:, h * LANES:(h + 1) * LANES] = (num / (e0 + e1 + e2)).astype(BF16)


def _dil_combine(outs, lses):
    m = outs[0].shape[1]
    tm = min(TM_COMBINE, m)
    specs = [pl.BlockSpec((DIL_HEADS, tm // dil, dil * LANES), lambda i: (0, i, 0)) for dil in DIL_DILATIONS]
    return pl.pallas_call(
        functools.partial(_dil_combine_kernel, tm=tm),
        grid=(m // tm,),
        in_specs=specs * 2,
        out_specs=pl.BlockSpec((tm, BRANCH_W), lambda i: (i, 0)),
        out_shape=jax.ShapeDtypeStruct((m, BRANCH_W), BF16),
        scratch_shapes=[pltpu.VMEM((2, tm, LANES), F32), pltpu.VMEM((2, tm, LANES), F32)],
        compiler_params=_params(("parallel",)),
        name="mixer_c_combine",
    )(*outs, *lses)


def _swa_kernel(sink_ref, q_ref, k_ref, v_ref, o_ref, *, t, tq, kw):
    qi = pl.program_id(1)
    start, dist = _band_window(qi, tq, kw, SWA_RADIUS, t)
    k = k_ref[0, pl.ds(start, kw), :]
    v_ext = _with_ones(v_ref[0, pl.ds(start, kw), :])
    lo = _half_masks(tq)
    for g in range(SWA_GROUP):
        q = q_ref[g]
        outs = []
        for hk in range(SWA_KVH):
            head = hk * SWA_GROUP + g
            sink = sink_ref[head] * LOG2E
            s = _dot_nt(_keep_half(q, lo, hk), k) - (SLOPES_SWA[head] * LOG2E) * dist
            o, l, m = _softmax_pv(s, v_ext, floor=sink)
            outs.append(o / (l + jnp.exp2(sink - m)))
        o_ref[:, g * LANES:(g + 1) * LANES] = jnp.where(lo, outs[0], outs[1]).astype(BF16)


def _swa(qkv, sink, n_seq, t, row_off, prev):
    tq = TQ_SWA
    kw = min(tq + 2 * SWA_RADIUS, t)
    nq = t // tq
    blk0 = row_off // tq
    seq0 = row_off // t
    return _call_into(
        functools.partial(_swa_kernel, t=t, tq=tq, kw=kw), prev, 4,
        grid=(n_seq, nq),
        in_specs=[
            pl.BlockSpec(memory_space=pltpu.SMEM),
            pl.BlockSpec((4, tq, LANES), lambda s, i: (BLK_SW_Q // 4, blk0 + s * nq + i, 0)),
            pl.BlockSpec((1, t, LANES), lambda s, i: (BLK_SW_K, seq0 + s, 0)),
            pl.BlockSpec((1, t, LANES), lambda s, i: (BLK_SW_V, seq0 + s, 0)),
        ],
        out_specs=pl.BlockSpec((tq, BRANCH_W), lambda s, i: (blk0 + s * nq + i, 0)),
        out_shape=jax.ShapeDtypeStruct((qkv.shape[1], BRANCH_W), BF16),
        compiler_params=_params(("parallel", "arbitrary")),
        name="mixer_d",
    )(sink, qkv, qkv, qkv)


def _merge_kernel(h_ref, oa_ref, ob_ref, oc_ref, od_ref, wg_ref, bg_ref, wb_ref, wo_ref, xres_ref, o_ref, *, nc, tc):
    j = pl.program_id(1)
    h = h_ref[...]
    acc = None
    for n, b_ref in enumerate((oa_ref, ob_ref, oc_ref, od_ref)):
        gate = jax.nn.sigmoid(_dot(h, wg_ref[n]) + bg_ref[n:n + 1, :])
        term = gate * _dot(b_ref[...], wb_ref[n])
        acc = term if acc is None else acc + term
    merged = acc.astype(BF16)

    def project(first):
        for n in range(D // FFN_TN):
            cols = slice(n * FFN_TN, (n + 1) * FFN_TN)
            part = _dot(merged, wo_ref[:, cols])
            o_ref[:, cols] = part if first else o_ref[:, cols] + part

    pl.when(j == 0)(functools.partial(project, True))
    pl.when(j > 0)(functools.partial(project, False))

    for c in range(nc):
        @pl.when(j == c)
        def _(c=c):
            o_ref[:, c * tc:(c + 1) * tc] += xres_ref[...]


def _merge(x, h, branches, w_gate, b_gate, w_branch, w_out):
    m = x.shape[0]
    tm, tc = TM_MERGE, TC_MERGE
    nc = D // tc
    row_tile = lambda i, j: (i, 0)
    return pl.pallas_call(
        functools.partial(_merge_kernel, nc=nc, tc=tc),
        grid=(m // tm, nc),
        in_specs=[pl.BlockSpec((tm, D), row_tile)]
        + [pl.BlockSpec((tm, BRANCH_W), row_tile)] * 4
        + [
            pl.BlockSpec((None, 4, D, tc), lambda i, j: (j, 0, 0, 0)),
            pl.BlockSpec((4, tc), lambda i, j: (0, j)),
            pl.BlockSpec((4, BRANCH_W, tc), lambda i, j: (0, 0, j)),
            pl.BlockSpec((tc, D), lambda i, j: (j, 0)),
            pl.BlockSpec((tm, tc), lambda i, j: (i, j)),
        ],
        out_specs=pl.BlockSpec((tm, D), row_tile),
        out_shape=jax.ShapeDtypeStruct((m, D), F32),
        compiler_params=_params(("parallel", "arbitrary")),
        name="merge",
    )(h, *branches, w_gate, b_gate, w_branch, w_out, x)


def _cast_ffn_in_kernel(w_ref, o_ref):
    for c in range(FFN_NC1):
        width = min(FFN_TF, D_FF - c * FFN_TF)
        o_ref[c, :, :width] = w_ref[:, c * FFN_TF:c * FFN_TF + width].astype(BF16)
        if width < FFN_TF:
            o_ref[c, :, width:] = jnp.zeros((o_ref.shape[1], FFN_TF - width), BF16)


def _cast_ffn_in(w_in, l):
    tr = 256
    return pl.pallas_call(
        _cast_ffn_in_kernel,
        grid=(2, D // tr),
        in_specs=[pl.BlockSpec((None, tr, D_FF), lambda part, i: (l, i, part))],
        out_specs=pl.BlockSpec((FFN_NC1, None, tr, FFN_TF), lambda part, i: (0, part, i, 0)),
        out_shape=jax.ShapeDtypeStruct((FFN_NC1, 2, D, FFN_TF), BF16),
        compiler_params=_params(("parallel", "parallel")),
        name="cast_ffn_in",
    )(w_in)


def _cast_ffn_out_kernel(w_ref, o_ref, *, tr):
    row = pl.program_id(0) * tr + lax.broadcasted_iota(jnp.int32, (tr, D), 0)
    o_ref[...] = jnp.where(row < D_FF, w_ref[...], 0.0).astype(BF16)


def _cast_ffn_out(w_out, l):
    tr = 512
    return pl.pallas_call(
        functools.partial(_cast_ffn_out_kernel, tr=tr),
        grid=(FFN_FP // tr,),
        in_specs=[pl.BlockSpec((None, tr, D), lambda i: (l, i, 0))],
        out_specs=pl.BlockSpec((tr, D), lambda i: (i, 0)),
        out_shape=jax.ShapeDtypeStruct((FFN_FP, D), BF16),
        compiler_params=_params(("parallel",)),
        name="cast_ffn_out",
    )(w_out)


def _cast_kernel(w_ref, o_ref):
    o_ref[...] = w_ref[...].astype(BF16)


def _cast_gate(w_gate, l):
    tc = TC_MERGE
    nc = D // tc
    return pl.pallas_call(
        _cast_kernel,
        grid=(nc, 4),
        in_specs=[pl.BlockSpec((None, D, tc), lambda c, n: (l, 0, n * nc + c))],
        out_specs=pl.BlockSpec((None, None, D, tc), lambda c, n: (c, n, 0, 0)),
        out_shape=jax.ShapeDtypeStruct((nc, 4, D, tc), BF16),
        compiler_params=_params(("parallel", "parallel")),
        name="cast_gate",
    )(w_gate)


def _prep_ffn(w_in, w_out, l):
    return _cast_ffn_in(w_in, l), _cast_ffn_out(w_out, l)


def _swa_pair_heads(w, axis):
    shape = w.shape
    split = shape[:axis] + (SWA_KVH, SWA_GROUP, SWA_DH) + shape[axis + 1:]
    return jnp.swapaxes(w.reshape(split), axis, axis + 1).reshape(shape)


def _proj_columns(w_in):
    scale = np.ones((IN_WIDTH,), np.float32)
    scale[COL_NA:COL_NA + NA_HEADS * NA_DH] = NA_DH ** -0.5 * LOG2E
    scale[COL_DF:COL_DF + DIFF_HEADS * 2 * DIFF_DK] = DIFF_DK ** -0.5 * LOG2E
    scale[COL_SW:COL_SW + SWA_QH * SWA_DH] = SWA_DH ** -0.5 * LOG2E
    w = w_in * scale
    gw = DIL_HEADS * DIL_DH

    def dil_group(gi):
        return [w[:, COL_DL + part * DIL_W + gi * gw:COL_DL + part * DIL_W + (gi + 1) * gw] for part in range(3)]

    sw_q_end = COL_SW + SWA_QH * SWA_DH
    cols = [w[:, :COL_DL]] + dil_group(0) + [_swa_pair_heads(w[:, COL_SW:sw_q_end], 1), w[:, sw_q_end:]] \
        + dil_group(1) + dil_group(2)
    return jnp.concatenate(cols, axis=1)


def _prep_w_in(w_in):
    return _proj_columns(w_in).astype(BF16)


def _encoder_layer(x, l, seqs, p, final_gain=None):
    ffn1 = (p["norm_ffn1"][l][None], *_prep_ffn(p["w_ffn1_in"], p["w_ffn1_out"], l))
    if isinstance(x, tuple):
        m_total = sum(xb.shape[0] for xb in x)
        out, row_off = None, 0
        for xb in x:
            out = _ffn(xb, *ffn1, m_out=m_total, out_off=row_off, prev=None if out is None else (out,))
            row_off += xb.shape[0]
        x = out
    else:
        x = _ffn(x, *ffn1)
    h, qkv, qkv_d4, qkv_d16 = _proj(x, p["norm_mix"][l][None], _prep_w_in(p["w_in"][l]))
    dil_views = (qkv, qkv_d4, qkv_d16)

    lam_init = 0.8 - 0.6 * math.exp(-0.3 * l)
    lv = p["diff_lambda"][l].astype(F32)
    lam = jnp.exp(jnp.sum(lv[0] * lv[1])) - jnp.exp(jnp.sum(lv[2] * lv[3])) + lam_init
    diff_lam = lam.reshape(1)
    na_bias = _na_bias_table(p["na_rpb"][l])
    subln = p["diff_subln"][l].astype(F32)[None]
    sink = p["swa_sink"][l].astype(F32)

    o_a = o_b = o_d = None
    dil = [None] * DIL_GROUPS
    for n_seq, t, row_off in seqs:
        o_a = _na(qkv, na_bias, n_seq, t, row_off, None if o_a is None else (o_a,))
        o_b = _diff(qkv, diff_lam, subln, n_seq, t, row_off, lam_init, None if o_b is None else (o_b,))
        for gi in range(DIL_GROUPS):
            dil[gi] = _dil_group(dil_views[gi], gi, n_seq, t, row_off, dil[gi])
        o_d = _swa(qkv, sink, n_seq, t, row_off, None if o_d is None else (o_d,))
    o_c = _dil_combine([o for o, _ in dil], [lse for _, lse in dil])
    branches = [o_a, o_b, o_c, o_d]

    w_branch = p["w_branch"][l]
    w_branch = jnp.concatenate([w_branch[:3], _swa_pair_heads(w_branch[3:], 1)], axis=0).astype(BF16)
    x = _merge(x, h, branches, _cast_gate(p["w_gate"], l), p["b_gate"][l].astype(F32).reshape(4, D),
               w_branch, p["w_out"][l].astype(BF16))
    ffn2 = (p["norm_ffn2"][l][None], *_prep_ffn(p["w_ffn2_in"], p["w_ffn2_out"], l))
    if final_gain is None:
        return _ffn(x, *ffn2)
    return tuple(_ffn(x, *ffn2, rows=(row_off, n_seq * t), final_gain=final_gain) for n_seq, t, row_off in seqs)


def kernel(x_prompt, x_sample, norm_ffn1, w_ffn1_in, w_ffn1_out, norm_mix, w_in, na_rpb, diff_lambda, diff_subln,
           swa_sink, w_branch, w_gate, b_gate, w_out, norm_ffn2, w_ffn2_in, w_ffn2_out, norm_final):
    p = dict(norm_ffn1=norm_ffn1, w_ffn1_in=w_ffn1_in, w_ffn1_out=w_ffn1_out, norm_mix=norm_mix, w_in=w_in,
             na_rpb=na_rpb, diff_lambda=diff_lambda, diff_subln=diff_subln, swa_sink=swa_sink,
             w_branch=w_branch, w_gate=w_gate, b_gate=b_gate, w_out=w_out, norm_ffn2=norm_ffn2,
             w_ffn2_in=w_ffn2_in, w_ffn2_out=w_ffn2_out)
    bp, tp, _ = x_prompt.shape
    bs, ts, _ = x_sample.shape
    mp = bp * tp
    x = (x_prompt.reshape(mp, D).astype(F32), x_sample.reshape(bs * ts, D).astype(F32))
    seqs = ((bp, tp, 0), (bs, ts, mp))
    for l in range(DEPTH):
        x = _encoder_layer(x, l, seqs, p, final_gain=norm_final.astype(F32)[None] if l == DEPTH - 1 else None)
    y_prompt, y_sample = x
    return y_prompt.reshape(bp, tp, D), y_sample.reshape(bs, ts, D)
```

```python
import functools
import math

import jax
import jax.numpy as jnp
import numpy as np
from jax import lax
from jax.experimental import pallas as pl
from jax.experimental.pallas import tpu as pltpu

F32 = jnp.float32
BF16 = jnp.bfloat16

D = 2048
DEPTH = 2
GRID_W = 64
NA_HEADS, NA_DH, NA_ROWS, NA_COLS = 8, 64, 8, 16
DIFF_HEADS, DIFF_DK, DIFF_DV = 4, 64, 128
DIL_WINDOWS, DIL_DILATIONS = (128, 512, 2048), (1, 4, 16)
DIL_GROUPS, DIL_HEADS, DIL_DH = 3, 4, 128
SWA_QH, SWA_KVH, SWA_DH, SWA_RADIUS = 8, 2, 64, 128
SWA_GROUP = SWA_QH // SWA_KVH
BRANCH_W = 512
D_FF = ((8 * D // 3 + 127) // 128) * 128
N_ALIBI = SWA_QH + DIL_GROUPS * DIL_HEADS + DIFF_HEADS
RMS_EPS = 1e-6
NEG_INF = -1e30
LOG2E = math.log2(math.e)
MASKED_DIST = 1e32

LANES = 128
IN_WIDTH = 3 * NA_HEADS * NA_DH + 2 * DIFF_HEADS * 2 * DIFF_DK + DIFF_HEADS * DIFF_DV \
    + 3 * DIL_GROUPS * DIL_HEADS * DIL_DH + SWA_QH * SWA_DH + 2 * SWA_KVH * SWA_DH
N_BLK = IN_WIDTH // LANES
BLK_NA_Q, BLK_NA_K, BLK_NA_V = 0, 4, 8
BLK_DF_Q, BLK_DF_K, BLK_DF_V = 12, 16, 20
BLK_DL_Q, BLK_DL_K, BLK_DL_V = 24, 28, 32
BLK_SW_Q, BLK_SW_K, BLK_SW_V = 36, 40, 41
N_BLK_MAIN = 42
N_BLK_DIL = 3 * DIL_HEADS
COL_NA, COL_DF, COL_DL, COL_SW = 0, 1536, 3072, 7680
DIL_W = DIL_GROUPS * DIL_HEADS * DIL_DH

FFN_TF = 512
FFN_FP = -(-D_FF // FFN_TF) * FFN_TF
FFN_NC1 = FFN_FP // FFN_TF
FFN_TN = 512
TM_FFN = 1024
TM_PROJ = 1024
NB_PROJ = 6
TN_PROJ = NB_PROJ * LANES
NJ_MAIN = N_BLK_MAIN // NB_PROJ
NJ_DIL = N_BLK_DIL // NB_PROJ
TM_MERGE = 1024
TC_MERGE = 256
TM_COMBINE = 1024
RQ_NA = 4
NA_UROWS = 12
TQ_DIFF = 256
BLOCKS_DIFF = 4
TQ_SWA = 256
TQ_DIL = 128
UNITS_DIL = 8
VMEM_LIMIT = 60 * 1024 * 1024

_ALIBI = [2.0 ** (-8.0 * (i + 1) / N_ALIBI) for i in range(N_ALIBI)]
SLOPES_SWA = _ALIBI[:SWA_QH]
SLOPES_DIL = _ALIBI[SWA_QH:SWA_QH + DIL_GROUPS * DIL_HEADS]
SLOPES_DIFF = _ALIBI[SWA_QH + DIL_GROUPS * DIL_HEADS:]


def _params(sem):
    return pltpu.CompilerParams(dimension_semantics=sem, vmem_limit_bytes=VMEM_LIMIT)


def _call_into(body, prev, n_in, **kw):
    if prev is None:
        return pl.pallas_call(body, **kw)
    prev = tuple(prev)

    def aliased(*refs):
        body(*refs[:n_in], *refs[n_in + len(prev):])

    kw["in_specs"] = list(kw["in_specs"]) + [pl.BlockSpec(memory_space=pl.ANY)] * len(prev)
    call = pl.pallas_call(aliased, input_output_aliases={n_in + i: i for i in range(len(prev))}, **kw)
    return lambda *args: call(*args, *prev)


def _rms(x, g, eps):
    ms = jnp.mean(x * x, axis=-1, keepdims=True)
    return x * lax.rsqrt(ms + eps) * g


def _dot(a, b):
    return jnp.dot(a, b, preferred_element_type=F32)


def _dot_nt(a, b):
    return lax.dot_general(a, b, (((1,), (1,)), ((), ())), preferred_element_type=F32)


def _half_masks(rows):
    lane = lax.broadcasted_iota(jnp.int32, (rows, LANES), 1)
    return lane < (LANES // 2)


def _keep_half(x, lo, half):
    keep = lo if half == 0 else jnp.logical_not(lo)
    return jnp.where(keep, x.astype(F32), 0.0).astype(BF16)


def _ffn_kernel(*refs, final_norm):
    x_ref, g_ref, w1_ref, w2_ref = refs[:4]
    o_ref, h_scr = refs[-2:]
    j = pl.program_id(1)

    @pl.when(j == 0)
    def _():
        x = x_ref[...]
        h_scr[...] = _rms(x, g_ref[...], RMS_EPS).astype(BF16)
        o_ref[...] = x

    h = h_scr[...]
    gate = _dot(h, w1_ref[0])
    up = _dot(h, w1_ref[1])
    act = (gate * jax.nn.sigmoid(gate) * (0.5 * up)).astype(BF16)
    in_range = j * FFN_TF + lax.broadcasted_iota(jnp.int32, (FFN_TF, FFN_TN), 0) < D_FF
    for n in range(D // FFN_TN):
        cols = slice(n * FFN_TN, (n + 1) * FFN_TN)
        o_ref[:, cols] += _dot(act, jnp.where(in_range, w2_ref[:, cols], 0.0).astype(BF16))

    if final_norm:
        @pl.when(j == FFN_NC1 - 1)
        def _():
            o_ref[...] = _rms(o_ref[...], refs[4][...], RMS_EPS)


def _ffn(x, g, w1, w2, layer, rows=None, m_out=None, out_off=0, prev=None, final_gain=None):
    start, m = (0, x.shape[0]) if rows is None else rows
    m_out = m if m_out is None else m_out
    tm = TM_FFN
    in0, out0 = start // tm, out_off // tm
    inputs = [x, g, w1, w2] + ([] if final_gain is None else [final_gain])
    in_specs = [
        pl.BlockSpec((tm, D), lambda i, j: (in0 + i, 0)),
        pl.BlockSpec((1, D), lambda i, j: (0, 0)),
        pl.BlockSpec((None, 2, D, FFN_TF), lambda i, j: (j, 0, 0, 0)),
        pl.BlockSpec((None, FFN_TF, D), lambda i, j: (layer, j, 0)),
    ] + ([] if final_gain is None else [pl.BlockSpec((1, D), lambda i, j: (0, 0))])
    return _call_into(
        functools.partial(_ffn_kernel, final_norm=final_gain is not None), prev, len(inputs),
        grid=(m // tm, FFN_NC1),
        in_specs=in_specs,
        out_specs=pl.BlockSpec((tm, D), lambda i, j: (out0 + i, 0)),
        out_shape=jax.ShapeDtypeStruct((m_out, D), F32),
        scratch_shapes=[pltpu.VMEM((tm, D), BF16)],
        compiler_params=_params(("parallel", "arbitrary")),
        name="ffn",
    )(*inputs)


def _proj_kernel(x_ref, g_ref, w_ref, h_ref, main_ref, d4_ref, d16_ref, res_scr, *, tm):
    j = pl.program_id(1)

    @pl.when(j == 0)
    def _():
        h_ref[...] = _rms(x_ref[...], g_ref[...], RMS_EPS).astype(BF16)

    res = _dot(h_ref[...], w_ref[...])

    @pl.when(j < NJ_MAIN)
    def _():
        for k in range(NB_PROJ):
            main_ref[k] = res[:, k * LANES:(k + 1) * LANES].astype(BF16)

    @pl.when(j >= NJ_MAIN)
    def _():
        for k in range(NB_PROJ):
            res_scr[k] = res[:, k * LANES:(k + 1) * LANES]

    for out_ref, dil, first in ((d4_ref, DIL_DILATIONS[1], NJ_MAIN), (d16_ref, DIL_DILATIONS[2], NJ_MAIN + NJ_DIL)):
        @pl.when((j >= first) & (j < first + NJ_DIL))
        def _(out_ref=out_ref, dil=dil):
            for k in range(NB_PROJ):
                for r in range(dil):
                    rows = res_scr[k, pl.ds(r, tm // dil, stride=dil), :]
                    out_ref[k, :, r * LANES:(r + 1) * LANES] = rows.astype(BF16)


def _proj(x, g, w):
    m = x.shape[0]
    tm = TM_PROJ
    d4, d16 = DIL_DILATIONS[1], DIL_DILATIONS[2]
    return pl.pallas_call(
        functools.partial(_proj_kernel, tm=tm),
        grid=(m // tm, NJ_MAIN + 2 * NJ_DIL),
        in_specs=[
            pl.BlockSpec((tm, D), lambda i, j: (i, 0)),
            pl.BlockSpec((1, D), lambda i, j: (0, 0)),
            pl.BlockSpec((D, TN_PROJ), lambda i, j: (0, j)),
        ],
        out_specs=[
            pl.BlockSpec((tm, D), lambda i, j: (i, 0)),
            pl.BlockSpec((NB_PROJ, tm, LANES), lambda i, j: (jnp.minimum(j, NJ_MAIN - 1), i, 0)),
            pl.BlockSpec((NB_PROJ, tm // d4, d4 * LANES),
                         lambda i, j: (jnp.clip(j - NJ_MAIN, 0, NJ_DIL - 1), i, 0)),
            pl.BlockSpec((NB_PROJ, tm // d16, d16 * LANES),
                         lambda i, j: (jnp.clip(j - NJ_MAIN - NJ_DIL, 0, NJ_DIL - 1), i, 0)),
        ],
        out_shape=[jax.ShapeDtypeStruct((m, D), BF16),
                   jax.ShapeDtypeStruct((N_BLK_MAIN, m, LANES), BF16),
                   jax.ShapeDtypeStruct((N_BLK_DIL, m // d4, d4 * LANES), BF16),
                   jax.ShapeDtypeStruct((N_BLK_DIL, m // d16, d16 * LANES), BF16)],
        scratch_shapes=[pltpu.VMEM((NB_PROJ, tm, LANES), F32)],
        compiler_params=_params(("parallel", "arbitrary")),
        name="proj",
    )(x, g, w)


def _na_kernel(q_ref, k_ref, v_ref, b_ref, o_ref, *, rows):
    nq = RQ_NA * GRID_W
    nk = NA_UROWS * GRID_W
    row0 = jnp.clip(RQ_NA * pl.program_id(1) - NA_ROWS // 2, 0, rows - NA_UROWS)
    start = pl.multiple_of(row0 * GRID_W, GRID_W)
    lo = _half_masks(nq)
    for hp in range(NA_HEADS // 2):
        q = q_ref[hp]
        k = k_ref[hp, pl.ds(start, nk), :]
        v_ext = _with_ones(v_ref[hp, pl.ds(start, nk), :])
        lhs = jnp.concatenate([_keep_half(q, lo, 0), _keep_half(q, lo, 1)], axis=0)
        o, l, _ = _softmax_pv(_dot_nt(lhs, k) + b_ref[0, hp], v_ext)
        o = o / l
        o_ref[:, hp * LANES:(hp + 1) * LANES] = jnp.where(lo, o[:nq], o[nq:]).astype(BF16)


def _na(qkv, bias, n_seq, t, row_off, prev):
    rows = t // GRID_W
    nb = rows // RQ_NA
    nq = RQ_NA * GRID_W
    blk0 = row_off // nq
    seq0 = row_off // t
    group_type = lambda g: jnp.where(g == 0, 0, jnp.where(g == nb - 1, 2, 1))
    return _call_into(
        functools.partial(_na_kernel, rows=rows), prev, 4,
        grid=(n_seq, nb),
        in_specs=[
            pl.BlockSpec((4, nq, LANES), lambda s, g: (BLK_NA_Q // 4, blk0 + s * nb + g, 0)),
            pl.BlockSpec((4, t, LANES), lambda s, g: (BLK_NA_K // 4, seq0 + s, 0)),
            pl.BlockSpec((4, t, LANES), lambda s, g: (BLK_NA_V // 4, seq0 + s, 0)),
            pl.BlockSpec((1, NA_HEADS // 2, 2 * nq, NA_UROWS * GRID_W), lambda s, g: (group_type(g), 0, 0, 0)),
        ],
        out_specs=pl.BlockSpec((nq, BRANCH_W), lambda s, g: (blk0 + s * nb + g, 0)),
        out_shape=jax.ShapeDtypeStruct((qkv.shape[1], BRANCH_W), BF16),
        compiler_params=_params(("parallel", "arbitrary")),
        name="mixer_a",
    )(qkv, qkv, qkv, bias)


def _na_bias_table(rpb):
    c = np.arange(GRID_W)[:, None]
    kc = np.arange(GRID_W)[None, :]
    cstart = np.clip(c - NA_COLS // 2, 0, GRID_W - NA_COLS)
    ok = (kc >= cstart) & (kc < cstart + NA_COLS)
    pad = GRID_W - NA_COLS
    padded = jnp.pad(rpb.astype(F32) * LOG2E, ((0, 0), (0, 0), (pad, pad)))
    e = jnp.stack([padded[..., GRID_W - 1 - q:2 * GRID_W - 1 - q] for q in range(GRID_W)], axis=2)
    e = jnp.where(ok[None, None], e, NEG_INF)
    neg = jnp.full((NA_HEADS, GRID_W, GRID_W), NEG_INF, F32)
    half = NA_ROWS // 2
    group_types = (
        [(-i, -i) for i in range(RQ_NA)],
        [(-half - i, -half) for i in range(RQ_NA)],
        [(-(NA_UROWS - RQ_NA) - i, -(NA_ROWS - RQ_NA) - i) for i in range(RQ_NA)],
    )
    tables = []
    for rel in group_types:
        q_rows = []
        for u0, w0 in rel:
            blocks = [e[:, u0 + j + NA_ROWS - 1] if w0 <= u0 + j < w0 + NA_ROWS else neg for j in range(NA_UROWS)]
            q_rows.append(jnp.concatenate(blocks, axis=2))
        tables.append(jnp.concatenate(q_rows, axis=1))
    return jnp.stack(tables).reshape(3, NA_HEADS // 2, 2 * RQ_NA * GRID_W, NA_UROWS * GRID_W)


def _diff_kernel(lam_ref, q_ref, k_ref, v_ref, tab_ref, g_ref, o_ref, vext_scr, *, t, tq, out_scale):
    qi = pl.program_id(2)

    @pl.when(qi == 0)
    def _():
        vext_scr[:, :LANES] = v_ref[0]
        vext_scr[:, LANES:] = jnp.ones((t, LANES), BF16)

    lam = lam_ref[0]
    lo = _half_masks(tq)
    k = k_ref[0]
    v_ext = vext_scr[...]

    def scores(b):
        off = pl.multiple_of((t // tq - 1 - (qi * BLOCKS_DIFF + b)) * tq, tq)
        bias = tab_ref[0, :, pl.ds(off, t)]
        q = q_ref[0, b * tq:(b + 1) * tq, :]
        return [_dot_nt(_keep_half(q, lo, half), k) - bias for half in range(2)]

    def probs(ss):
        return [jnp.exp2((s - jnp.max(s, axis=-1, keepdims=True)).astype(BF16)) for s in ss]

    def finish(b, ps):
        outs = [_dot(p, v_ext) for p in ps]
        o1, o2 = (ol[:, :LANES] / ol[:, LANES:] for ol in outs)
        o_ref[b * tq:(b + 1) * tq, :] = (_rms(o1 - lam * o2, g_ref[...], 1e-5) * out_scale).astype(BF16)

    ss = scores(0)
    for b in range(BLOCKS_DIFF):
        ss_next = scores(b + 1) if b + 1 < BLOCKS_DIFF else None
        finish(b, probs(ss))
        ss = ss_next


def _diff_bias_table(t, tq):
    r = lax.broadcasted_iota(jnp.int32, (tq, 2 * t - tq), 0)
    x = lax.broadcasted_iota(jnp.int32, (tq, 2 * t - tq), 1)
    dist = jnp.abs(r - x + (t - tq)).astype(F32)
    return jnp.asarray([s * LOG2E for s in SLOPES_DIFF], F32)[:, None, None] * dist[None]


def _diff(qkv, lam, subln_g, n_seq, t, row_off, lam_init, prev):
    tq = TQ_DIFF
    rows = BLOCKS_DIFF * tq
    nq = t // rows
    blk0 = row_off // rows
    seq0 = row_off // t
    return _call_into(
        functools.partial(_diff_kernel, t=t, tq=tq, out_scale=1.0 - lam_init), prev, 6,
        grid=(DIFF_HEADS, n_seq, nq),
        in_specs=[
            pl.BlockSpec(memory_space=pltpu.SMEM),
            pl.BlockSpec((1, rows, LANES), lambda h, s, i: (BLK_DF_Q + h, blk0 + s * nq + i, 0)),
            pl.BlockSpec((1, t, LANES), lambda h, s, i: (BLK_DF_K + h, seq0 + s, 0)),
            pl.BlockSpec((1, t, LANES), lambda h, s, i: (BLK_DF_V + h, seq0 + s, 0)),
            pl.BlockSpec((1, tq, 2 * t - tq), lambda h, s, i: (h, 0, 0)),
            pl.BlockSpec((1, DIFF_DV), lambda h, s, i: (0, 0)),
        ],
        out_specs=pl.BlockSpec((rows, LANES), lambda h, s, i: (blk0 + s * nq + i, h)),
        out_shape=jax.ShapeDtypeStruct((qkv.shape[1], BRANCH_W), BF16),
        scratch_shapes=[pltpu.VMEM((t, 2 * LANES), BF16)],
        compiler_params=_params(("parallel", "arbitrary", "arbitrary")),
        name="mixer_b",
    )(lam, qkv, qkv, qkv, _diff_bias_table(t, tq), subln_g)


def _band_window(qi, tq, kw, radius, length, segment=None):
    start = 0 if kw == length else pl.multiple_of(jnp.clip(qi * tq - radius, 0, length - kw), 64)
    col = lax.broadcasted_iota(jnp.int32, (tq, kw), 1)
    dist = jnp.abs(lax.broadcasted_iota(jnp.int32, (tq, kw), 0) + (qi * tq - start) - col)
    masked = jnp.where(dist <= radius, dist.astype(F32), MASKED_DIST)
    if segment is not None:
        masked = jnp.where((col + start) // segment == qi, masked, MASKED_DIST)
    return start, masked


def _with_ones(v):
    return jnp.concatenate([v, jnp.ones(v.shape, v.dtype)], axis=1)


def _softmax_pv(s, v_ext, floor=None):
    m = jnp.max(s, axis=-1, keepdims=True)
    if floor is not None:
        m = jnp.maximum(m, floor)
    ol = _dot(jnp.exp2((s - m).astype(BF16)), v_ext)
    return ol[:, :LANES], ol[:, LANES:], m


def _dil_kernel(q_ref, k_ref, v_ref, o_ref, lse_ref, *, n_sub, tq, kw, radius, slopes, scale, nqb, nrb, segment):
    for b in range(nqb):
        start, dist = _band_window(pl.program_id(2) * nqb + b, tq, kw, radius, n_sub, segment)
        rows = slice(b * tq, (b + 1) * tq)
        for rr in range(nrb):
            lanes = slice(rr * LANES, (rr + 1) * LANES)
            for h in range(DIL_HEADS):
                k = k_ref[h, pl.ds(start, kw), lanes]
                v_ext = _with_ones(v_ref[h, pl.ds(start, kw), lanes])
                s = _dot_nt(q_ref[h, rows, lanes], k) * scale - slopes[h] * dist
                o, l, m = _softmax_pv(s, v_ext)
                o_ref[h, rows, lanes] = (o / l).astype(BF16)
                lse_ref[h, rows, lanes] = (m + jnp.log2(l)) * (1.0 / LOG2E)


def _dil_group(view, gi, n_seq, t, row_off, prev):
    dil = DIL_DILATIONS[gi]
    radius = DIL_WINDOWS[gi] // (2 * dil)
    n_sub = t // dil
    segment = None
    if n_sub == TQ_DIL and n_seq % 2 == 0:
        segment, n_sub, n_seq, t = n_sub, 2 * n_sub, n_seq // 2, 2 * t
    tq = min(TQ_DIL, n_sub)
    kw = min(tq + 2 * radius, n_sub)
    nqb = min(n_sub // tq, UNITS_DIL)
    nrb = min(dil, UNITS_DIL // nqb)
    nq = n_sub // (tq * nqb)
    blk0 = row_off // dil // (tq * nqb)
    seq0 = row_off // t
    qb, kb, vb = (BLK_DL_Q // 4, BLK_DL_K // 4, BLK_DL_V // 4) if gi == 0 else (0, 1, 2)
    slopes = tuple(SLOPES_DIL[gi * DIL_HEADS + h] * dil * LOG2E for h in range(DIL_HEADS))
    rows_out = view.shape[1]
    q_idx = lambda s, r, i: (qb, blk0 + s * nq + i, r)
    o_idx = lambda s, r, i: (0, blk0 + s * nq + i, r)
    o, lse = _call_into(
        functools.partial(_dil_kernel, n_sub=n_sub, tq=tq, kw=kw, radius=radius, slopes=slopes,
                          scale=DIL_DH ** -0.5 * LOG2E, nqb=nqb, nrb=nrb, segment=segment), prev, 3,
        grid=(n_seq, dil // nrb, nq),
        in_specs=[
            pl.BlockSpec((4, nqb * tq, nrb * LANES), q_idx),
            pl.BlockSpec((4, n_sub, nrb * LANES), lambda s, r, i: (kb, seq0 + s, r)),
            pl.BlockSpec((4, n_sub, nrb * LANES), lambda s, r, i: (vb, seq0 + s, r)),
        ],
        out_specs=[pl.BlockSpec((4, nqb * tq, nrb * LANES), o_idx),
                   pl.BlockSpec((4, nqb * tq, nrb * LANES), o_idx)],
        out_shape=[jax.ShapeDtypeStruct((DIL_HEADS, rows_out, dil * LANES), BF16),
                   jax.ShapeDtypeStruct((DIL_HEADS, rows_out, dil * LANES), F32)],
        compiler_params=_params(("parallel", "arbitrary", "arbitrary")),
        name=f"mixer_c{gi}",
    )(view, view, view)
    return o, lse


def _dil_combine_kernel(o0, o1, o2, l0, l1, l2, out_ref, o_scr, l_scr, *, tm):
    for h in range(DIL_HEADS):
        for gi, (o_ref, l_ref) in ((1, (o1, l1)), (2, (o2, l2))):
            dil = DIL_DILATIONS[gi]
            for r in range(dil):
                rows = pl.ds(r, tm // dil, stride=dil)
                o_scr[gi - 1, rows, :] = o_ref[h, :, r * LANES:(r + 1) * LANES].astype(F32)
                l_scr[gi - 1, rows, :] = l_ref[h, :, r * LANES:(r + 1) * LANES]
        a0, a1, a2 = l0[h], l_scr[0], l_scr[1]
        mx = jnp.maximum(jnp.maximum(a0, a1), a2)
        e0, e1, e2 = jnp.exp(a0 - mx), jnp.exp(a1 - mx), jnp.exp(a2 - mx)
        num = e0 * o0[h].astype(F32) + e1 * o_scr[0] + e2 * o_scr[1]
        out_ref[:, h * LANES:(h + 1) * LANES] = (num / (e0 + e1 + e2)).astype(BF16)


def _dil_combine(outs, lses):
    m = outs[0].shape[1]
    tm = min(TM_COMBINE, m)
    specs = [pl.BlockSpec((DIL_HEADS, tm // dil, dil * LANES), lambda i: (0, i, 0)) for dil in DIL_DILATIONS]
    return pl.pallas_call(
        functools.partial(_dil_combine_kernel, tm=tm),
        grid=(m // tm,),
        in_specs=specs * 2,
        out_specs=pl.BlockSpec((tm, BRANCH_W), lambda i: (i, 0)),
        out_shape=jax.ShapeDtypeStruct((m, BRANCH_W), BF16),
        scratch_shapes=[pltpu.VMEM((2, tm, LANES), F32), pltpu.VMEM((2, tm, LANES), F32)],
        compiler_params=_params(("parallel",)),
        name="mixer_c_combine",
    )(*outs, *lses)


def _swa_kernel(sink_ref, q_ref, k_ref, v_ref, o_ref, *, t, tq, kw):
    qi = pl.program_id(1)
    start, dist = _band_window(qi, tq, kw, SWA_RADIUS, t)
    k = k_ref[0, pl.ds(start, kw), :]
    v_ext = _with_ones(v_ref[0, pl.ds(start, kw), :])
    lo = _half_masks(tq)
    for g in range(SWA_GROUP):
        q = q_ref[g]
        outs = []
        for hk in range(SWA_KVH):
            head = hk * SWA_GROUP + g
            sink = sink_ref[head] * LOG2E
            s = _dot_nt(_keep_half(q, lo, hk), k) - (SLOPES_SWA[head] * LOG2E) * dist
            o, l, m = _softmax_pv(s, v_ext, floor=sink)
            outs.append(o / (l + jnp.exp2(sink - m)))
        o_ref[:, g * LANES:(g + 1) * LANES] = jnp.where(lo, outs[0], outs[1]).astype(BF16)


def _swa(qkv, sink, n_seq, t, row_off, prev):
    tq = TQ_SWA
    kw = min(tq + 2 * SWA_RADIUS, t)
    nq = t // tq
    blk0 = row_off // tq
    seq0 = row_off // t
    return _call_into(
        functools.partial(_swa_kernel, t=t, tq=tq, kw=kw), prev, 4,
        grid=(n_seq, nq),
        in_specs=[
            pl.BlockSpec(memory_space=pltpu.SMEM),
            pl.BlockSpec((4, tq, LANES), lambda s, i: (BLK_SW_Q // 4, blk0 + s * nq + i, 0)),
            pl.BlockSpec((1, t, LANES), lambda s, i: (BLK_SW_K, seq0 + s, 0)),
            pl.BlockSpec((1, t, LANES), lambda s, i: (BLK_SW_V, seq0 + s, 0)),
        ],
        out_specs=pl.BlockSpec((tq, BRANCH_W), lambda s, i: (blk0 + s * nq + i, 0)),
        out_shape=jax.ShapeDtypeStruct((qkv.shape[1], BRANCH_W), BF16),
        compiler_params=_params(("parallel", "arbitrary")),
        name="mixer_d",
    )(sink, qkv, qkv, qkv)


def _merge_kernel(h_ref, oa_ref, ob_ref, oc_ref, od_ref, wg_ref, bg_ref, wb_ref, wo_ref, xres_ref, o_ref, *, nc, tc):
    j = pl.program_id(1)
    h = h_ref[...]
    acc = None
    for n, b_ref in enumerate((oa_ref, ob_ref, oc_ref, od_ref)):
        gate = jax.nn.sigmoid(_dot(h, wg_ref[n]) + bg_ref[n:n + 1, :])
        term = gate * _dot(b_ref[...], wb_ref[n])
        acc = term if acc is None else acc + term
    merged = acc.astype(BF16)

    def project(first):
        for n in range(D // FFN_TN):
            cols = slice(n * FFN_TN, (n + 1) * FFN_TN)
            part = _dot(merged, wo_ref[:, cols])
            o_ref[:, cols] = part if first else o_ref[:, cols] + part

    pl.when(j == 0)(functools.partial(project, True))
    pl.when(j > 0)(functools.partial(project, False))

    for c in range(nc):
        @pl.when(j == c)
        def _(c=c):
            o_ref[:, c * tc:(c + 1) * tc] += xres_ref[...]


def _merge(x, h, branches, w_gate, b_gate, w_branch, w_out):
    m = x.shape[0]
    tm, tc = TM_MERGE, TC_MERGE
    nc = D // tc
    row_tile = lambda i, j: (i, 0)
    return pl.pallas_call(
        functools.partial(_merge_kernel, nc=nc, tc=tc),
        grid=(m // tm, nc),
        in_specs=[pl.BlockSpec((tm, D), row_tile)]
        + [pl.BlockSpec((tm, BRANCH_W), row_tile)] * 4
        + [
            pl.BlockSpec((None, 4, D, tc), lambda i, j: (j, 0, 0, 0)),
            pl.BlockSpec((4, tc), lambda i, j: (0, j)),
            pl.BlockSpec((4, BRANCH_W, tc), lambda i, j: (0, 0, j)),
            pl.BlockSpec((tc, D), lambda i, j: (j, 0)),
            pl.BlockSpec((tm, tc), lambda i, j: (i, j)),
        ],
        out_specs=pl.BlockSpec((tm, D), row_tile),
        out_shape=jax.ShapeDtypeStruct((m, D), F32),
        compiler_params=_params(("parallel", "arbitrary")),
        name="merge",
    )(h, *branches, w_gate, b_gate, w_branch, w_out, x)


def _cast_ffn_in_kernel(w_ref, o_ref):
    for c in range(FFN_NC1):
        width = min(FFN_TF, D_FF - c * FFN_TF)
        o_ref[c, :, :width] = w_ref[:, c * FFN_TF:c * FFN_TF + width].astype(BF16)
        if width < FFN_TF:
            o_ref[c, :, width:] = jnp.zeros((o_ref.shape[1], FFN_TF - width), BF16)


def _cast_ffn_in(w_in, l):
    tr = 256
    return pl.pallas_call(
        _cast_ffn_in_kernel,
        grid=(2, D // tr),
        in_specs=[pl.BlockSpec((None, tr, D_FF), lambda part, i: (l, i, part))],
        out_specs=pl.BlockSpec((FFN_NC1, None, tr, FFN_TF), lambda part, i: (0, part, i, 0)),
        out_shape=jax.ShapeDtypeStruct((FFN_NC1, 2, D, FFN_TF), BF16),
        compiler_params=_params(("parallel", "parallel")),
        name="cast_ffn_in",
    )(w_in)


def _cast_kernel(w_ref, o_ref):
    o_ref[...] = w_ref[...].astype(BF16)


def _cast_gate(w_gate, l):
    tc = TC_MERGE
    nc = D // tc
    return pl.pallas_call(
        _cast_kernel,
        grid=(nc, 4),
        in_specs=[pl.BlockSpec((None, D, tc), lambda c, n: (l, 0, n * nc + c))],
        out_specs=pl.BlockSpec((None, None, D, tc), lambda c, n: (c, n, 0, 0)),
        out_shape=jax.ShapeDtypeStruct((nc, 4, D, tc), BF16),
        compiler_params=_params(("parallel", "parallel")),
        name="cast_gate",
    )(w_gate)


def _prep_ffn(w_in, w_out, l):
    return _cast_ffn_in(w_in, l), w_out, l


def _swa_pair_heads(w, axis):
    shape = w.shape
    split = shape[:axis] + (SWA_KVH, SWA_GROUP, SWA_DH) + shape[axis + 1:]
    return jnp.swapaxes(w.reshape(split), axis, axis + 1).reshape(shape)


def _proj_columns(w_in):
    scale = np.ones((IN_WIDTH,), np.float32)
    scale[COL_NA:COL_NA + NA_HEADS * NA_DH] = NA_DH ** -0.5 * LOG2E
    scale[COL_DF:COL_DF + DIFF_HEADS * 2 * DIFF_DK] = DIFF_DK ** -0.5 * LOG2E
    scale[COL_SW:COL_SW + SWA_QH * SWA_DH] = SWA_DH ** -0.5 * LOG2E
    w = w_in * scale
    gw = DIL_HEADS * DIL_DH

    def dil_group(gi):
        return [w[:, COL_DL + part * DIL_W + gi * gw:COL_DL + part * DIL_W + (gi + 1) * gw] for part in range(3)]

    sw_q_end = COL_SW + SWA_QH * SWA_DH
    cols = [w[:, :COL_DL]] + dil_group(0) + [_swa_pair_heads(w[:, COL_SW:sw_q_end], 1), w[:, sw_q_end:]] \
        + dil_group(1) + dil_group(2)
    return jnp.concatenate(cols, axis=1)


def _prep_w_in(w_in):
    return _proj_columns(w_in).astype(BF16)


def _encoder_layer(x, l, seqs, p, final_gain=None):
    ffn1 = (p["norm_ffn1"][l][None], *_prep_ffn(p["w_ffn1_in"], p["w_ffn1_out"], l))
    if isinstance(x, tuple):
        m_total = sum(xb.shape[0] for xb in x)
        out, row_off = None, 0
        for xb in x:
            out = _ffn(xb, *ffn1, m_out=m_total, out_off=row_off, prev=None if out is None else (out,))
            row_off += xb.shape[0]
        x = out
    else:
        x = _ffn(x, *ffn1)
    h, qkv, qkv_d4, qkv_d16 = _proj(x, p["norm_mix"][l][None], _prep_w_in(p["w_in"][l]))
    dil_views = (qkv, qkv_d4, qkv_d16)

    lam_init = 0.8 - 0.6 * math.exp(-0.3 * l)
    lv = p["diff_lambda"][l].astype(F32)
    lam = jnp.exp(jnp.sum(lv[0] * lv[1])) - jnp.exp(jnp.sum(lv[2] * lv[3])) + lam_init
    diff_lam = lam.reshape(1)
    na_bias = _na_bias_table(p["na_rpb"][l])
    subln = p["diff_subln"][l].astype(F32)[None]
    sink = p["swa_sink"][l].astype(F32)

    o_a = o_b = o_d = None
    dil = [None] * DIL_GROUPS
    for n_seq, t, row_off in seqs:
        o_a = _na(qkv, na_bias, n_seq, t, row_off, None if o_a is None else (o_a,))
        o_b = _diff(qkv, diff_lam, subln, n_seq, t, row_off, lam_init, None if o_b is None else (o_b,))
        for gi in range(DIL_GROUPS):
            dil[gi] = _dil_group(dil_views[gi], gi, n_seq, t, row_off, dil[gi])
        o_d = _swa(qkv, sink, n_seq, t, row_off, None if o_d is None else (o_d,))
    o_c = _dil_combine([o for o, _ in dil], [lse for _, lse in dil])
    branches = [o_a, o_b, o_c, o_d]

    w_branch = p["w_branch"][l]
    w_branch = jnp.concatenate([w_branch[:3], _swa_pair_heads(w_branch[3:], 1)], axis=0).astype(BF16)
    x = _merge(x, h, branches, _cast_gate(p["w_gate"], l), p["b_gate"][l].astype(F32).reshape(4, D),
               w_branch, p["w_out"][l].astype(BF16))
    ffn2 = (p["norm_ffn2"][l][None], *_prep_ffn(p["w_ffn2_in"], p["w_ffn2_out"], l))
    if final_gain is None:
        return _ffn(x, *ffn2)
    return tuple(_ffn(x, *ffn2, rows=(row_off, n_seq * t), final_gain=final_gain) for n_seq, t, row_off in seqs)


def kernel(x_prompt, x_sample, norm_ffn1, w_ffn1_in, w_ffn1_out, norm_mix, w_in, na_rpb, diff_lambda, diff_subln,
           swa_sink, w_branch, w_gate, b_gate, w_out, norm_ffn2, w_ffn2_in, w_ffn2_out, norm_final):
    p = dict(norm_ffn1=norm_ffn1, w_ffn1_in=w_ffn1_in, w_ffn1_out=w_ffn1_out, norm_mix=norm_mix, w_in=w_in,
             na_rpb=na_rpb, diff_lambda=diff_lambda, diff_subln=diff_subln, swa_sink=swa_sink,
             w_branch=w_branch, w_gate=w_gate, b_gate=b_gate, w_out=w_out, norm_ffn2=norm_ffn2,
             w_ffn2_in=w_ffn2_in, w_ffn2_out=w_ffn2_out)
    bp, tp, _ = x_prompt.shape
    bs, ts, _ = x_sample.shape
    mp = bp * tp
    x = (x_prompt.reshape(mp, D).astype(F32), x_sample.reshape(bs * ts, D).astype(F32))
    seqs = ((bp, tp, 0), (bs, ts, mp))
    for l in range(DEPTH):
        x = _encoder_layer(x, l, seqs, p, final_gain=norm_final.astype(F32)[None] if l == DEPTH - 1 else None)
    y_prompt, y_sample = x
    return y_prompt.reshape(bp, tp, D), y_sample.reshape(bs, ts, D)
```

```python
import functools
import math

import jax
import jax.numpy as jnp
import numpy as np
from jax import lax
from jax.experimental import pallas as pl
from jax.experimental.pallas import tpu as pltpu

F32 = jnp.float32
BF16 = jnp.bfloat16

D = 2048
DEPTH = 2
GRID_W = 64
NA_HEADS, NA_DH, NA_ROWS, NA_COLS = 8, 64, 8, 16
DIFF_HEADS, DIFF_DK, DIFF_DV = 4, 64, 128
DIL_WINDOWS, DIL_DILATIONS = (128, 512, 2048), (1, 4, 16)
DIL_GROUPS, DIL_HEADS, DIL_DH = 3, 4, 128
SWA_QH, SWA_KVH, SWA_DH, SWA_RADIUS = 8, 2, 64, 128
SWA_GROUP = SWA_QH // SWA_KVH
BRANCH_W = 512
D_FF = ((8 * D // 3 + 127) // 128) * 128
N_ALIBI = SWA_QH + DIL_GROUPS * DIL_HEADS + DIFF_HEADS
RMS_EPS = 1e-6
NEG_INF = -1e30
LOG2E = math.log2(math.e)
MASKED_DIST = 1e32

LANES = 128
IN_WIDTH = 3 * NA_HEADS * NA_DH + 2 * DIFF_HEADS * 2 * DIFF_DK + DIFF_HEADS * DIFF_DV \
    + 3 * DIL_GROUPS * DIL_HEADS * DIL_DH + SWA_QH * SWA_DH + 2 * SWA_KVH * SWA_DH
N_BLK = IN_WIDTH // LANES
BLK_NA_Q, BLK_NA_K, BLK_NA_V = 0, 4, 8
BLK_DF_Q, BLK_DF_K, BLK_DF_V = 12, 16, 20
BLK_DL_Q, BLK_DL_K, BLK_DL_V = 24, 28, 32
BLK_SW_Q, BLK_SW_K, BLK_SW_V = 36, 40, 41
N_BLK_MAIN = 42
N_BLK_DIL = 3 * DIL_HEADS
COL_NA, COL_DF, COL_DL, COL_SW = 0, 1536, 3072, 7680
DIL_W = DIL_GROUPS * DIL_HEADS * DIL_DH

FFN_TF = 512
FFN_FP = -(-D_FF // FFN_TF) * FFN_TF
FFN_NC1 = FFN_FP // FFN_TF
FFN_TN = 512
TM_FFN = 1024
TM_PROJ = 1024
NB_PROJ = 6
TN_PROJ = NB_PROJ * LANES
NJ_MAIN = N_BLK_MAIN // NB_PROJ
NJ_DIL = N_BLK_DIL // NB_PROJ
TM_MERGE = 1024
TC_MERGE = 256
TM_COMBINE = 1024
RQ_NA = 4
NA_UROWS = 12
TQ_DIFF = 256
BLOCKS_DIFF = 4
TQ_SWA = 256
TQ_DIL = 128
UNITS_DIL = 8
VMEM_LIMIT = 60 * 1024 * 1024

_ALIBI = [2.0 ** (-8.0 * (i + 1) / N_ALIBI) for i in range(N_ALIBI)]
SLOPES_SWA = _ALIBI[:SWA_QH]
SLOPES_DIL = _ALIBI[SWA_QH:SWA_QH + DIL_GROUPS * DIL_HEADS]
SLOPES_DIFF = _ALIBI[SWA_QH + DIL_GROUPS * DIL_HEADS:]


def _params(sem):
    return pltpu.CompilerParams(dimension_semantics=sem, vmem_limit_bytes=VMEM_LIMIT)


def _call_into(body, prev, n_in, **kw):
    if prev is None:
        return pl.pallas_call(body, **kw)
    prev = tuple(prev)

    def aliased(*refs):
        body(*refs[:n_in], *refs[n_in + len(prev):])

    kw["in_specs"] = list(kw["in_specs"]) + [pl.BlockSpec(memory_space=pl.ANY)] * len(prev)
    call = pl.pallas_call(aliased, input_output_aliases={n_in + i: i for i in range(len(prev))}, **kw)
    return lambda *args: call(*args, *prev)


def _rms(x, g, eps):
    ms = jnp.mean(x * x, axis=-1, keepdims=True)
    return x * lax.rsqrt(ms + eps) * g


def _dot(a, b):
    return jnp.dot(a, b, preferred_element_type=F32)


def _dot_nt(a, b):
    return lax.dot_general(a, b, (((1,), (1,)), ((), ())), preferred_element_type=F32)


def _half_masks(rows):
    lane = lax.broadcasted_iota(jnp.int32, (rows, LANES), 1)
    return lane < (LANES // 2)


def _keep_half(x, lo, half):
    keep = lo if half == 0 else jnp.logical_not(lo)
    return jnp.where(keep, x.astype(F32), 0.0).astype(BF16)


def _ffn_kernel(*refs, final_norm):
    x_ref, g_ref, w1_ref, w2_ref = refs[:4]
    o_ref, h_scr = refs[-2:]
    j = pl.program_id(1)

    @pl.when(j == 0)
    def _():
        x = x_ref[...]
        h_scr[...] = _rms(x, g_ref[...], RMS_EPS).astype(BF16)
        o_ref[...] = x

    h = h_scr[...]
    gate = _dot(h, w1_ref[0])
    up = _dot(h, w1_ref[1])
    act = (gate * jax.nn.sigmoid(gate) * (0.5 * up)).astype(BF16)
    in_range = j * FFN_TF + lax.broadcasted_iota(jnp.int32, (FFN_TF, FFN_TN), 0) < D_FF
    for n in range(D // FFN_TN):
        cols = slice(n * FFN_TN, (n + 1) * FFN_TN)
        o_ref[:, cols] += _dot(act, jnp.where(in_range, w2_ref[:, cols], 0.0).astype(BF16))

    if final_norm:
        @pl.when(j == FFN_NC1 - 1)
        def _():
            o_ref[...] = _rms(o_ref[...], refs[4][...], RMS_EPS)


def _ffn(x, g, w1, w2, layer, rows=None, m_out=None, out_off=0, prev=None, final_gain=None):
    start, m = (0, x.shape[0]) if rows is None else rows
    m_out = m if m_out is None else m_out
    tm = TM_FFN
    in0, out0 = start // tm, out_off // tm
    inputs = [x, g, w1, w2] + ([] if final_gain is None else [final_gain])
    in_specs = [
        pl.BlockSpec((tm, D), lambda i, j: (in0 + i, 0)),
        pl.BlockSpec((1, D), lambda i, j: (0, 0)),
        pl.BlockSpec((None, 2, D, FFN_TF), lambda i, j: (j, 0, 0, 0)),
        pl.BlockSpec((None, FFN_TF, D), lambda i, j: (layer, j, 0)),
    ] + ([] if final_gain is None else [pl.BlockSpec((1, D), lambda i, j: (0, 0))])
    return _call_into(
        functools.partial(_ffn_kernel, final_norm=final_gain is not None), prev, len(inputs),
        grid=(m // tm, FFN_NC1),
        in_specs=in_specs,
        out_specs=pl.BlockSpec((tm, D), lambda i, j: (out0 + i, 0)),
        out_shape=jax.ShapeDtypeStruct((m_out, D), F32),
        scratch_shapes=[pltpu.VMEM((tm, D), BF16)],
        compiler_params=_params(("parallel", "arbitrary")),
        name="ffn",
    )(*inputs)


def _proj_kernel(x_ref, g_ref, w_ref, h_ref, main_ref, d4_ref, d16_ref, res_scr, *, tm):
    j = pl.program_id(1)

    @pl.when(j == 0)
    def _():
        h_ref[...] = _rms(x_ref[...], g_ref[...], RMS_EPS).astype(BF16)

    res = _dot(h_ref[...], w_ref[...])

    for k in range(NB_PROJ):
        main_ref[k] = res[:, k * LANES:(k + 1) * LANES].astype(BF16)
        res_scr[k] = res[:, k * LANES:(k + 1) * LANES]

    for out_ref, dil, first in ((d4_ref, DIL_DILATIONS[1], 0), (d16_ref, DIL_DILATIONS[2], NJ_DIL)):
        @pl.when((j >= first) & (j < first + NJ_DIL))
        def _(out_ref=out_ref, dil=dil):
            for k in range(NB_PROJ):
                for r in range(dil):
                    rows = res_scr[k, pl.ds(r, tm // dil, stride=dil), :]
                    out_ref[k, :, r * LANES:(r + 1) * LANES] = rows.astype(BF16)


def _proj(x, g, w):
    m = x.shape[0]
    tm = TM_PROJ
    d4, d16 = DIL_DILATIONS[1], DIL_DILATIONS[2]
    return pl.pallas_call(
        functools.partial(_proj_kernel, tm=tm),
        grid=(m // tm, NJ_MAIN + 2 * NJ_DIL),
        in_specs=[
            pl.BlockSpec((tm, D), lambda i, j: (i, 0)),
            pl.BlockSpec((1, D), lambda i, j: (0, 0)),
            pl.BlockSpec((D, TN_PROJ), lambda i, j: (0, j)),
        ],
        out_specs=[
            pl.BlockSpec((tm, D), lambda i, j: (i, 0)),
            pl.BlockSpec((NB_PROJ, tm, LANES), lambda i, j: (jnp.maximum(j - 2 * NJ_DIL, 0), i, 0)),
            pl.BlockSpec((NB_PROJ, tm // d4, d4 * LANES), lambda i, j: (jnp.minimum(j, NJ_DIL - 1), i, 0)),
            pl.BlockSpec((NB_PROJ, tm // d16, d16 * LANES),
                         lambda i, j: (jnp.clip(j - NJ_DIL, 0, NJ_DIL - 1), i, 0)),
        ],
        out_shape=[jax.ShapeDtypeStruct((m, D), BF16),
                   jax.ShapeDtypeStruct((N_BLK_MAIN, m, LANES), BF16),
                   jax.ShapeDtypeStruct((N_BLK_DIL, m // d4, d4 * LANES), BF16),
                   jax.ShapeDtypeStruct((N_BLK_DIL, m // d16, d16 * LANES), BF16)],
        scratch_shapes=[pltpu.VMEM((NB_PROJ, tm, LANES), F32)],
        compiler_params=_params(("parallel", "arbitrary")),
        name="proj",
    )(x, g, w)


def _na_kernel(q_ref, k_ref, v_ref, b_ref, o_ref, *, rows):
    nq = RQ_NA * GRID_W
    nk = NA_UROWS * GRID_W
    row0 = jnp.clip(RQ_NA * pl.program_id(1) - NA_ROWS // 2, 0, rows - NA_UROWS)
    start = pl.multiple_of(row0 * GRID_W, GRID_W)
    lo = _half_masks(nq)
    for hp in range(NA_HEADS // 2):
        q = q_ref[hp]
        k = k_ref[hp, pl.ds(start, nk), :]
        v_ext = _with_ones(v_ref[hp, pl.ds(start, nk), :])
        lhs = jnp.concatenate([_keep_half(q, lo, 0), _keep_half(q, lo, 1)], axis=0)
        o, l, _ = _softmax_pv(_dot_nt(lhs, k) + b_ref[0, hp], v_ext)
        o = o / l
        o_ref[:, hp * LANES:(hp + 1) * LANES] = jnp.where(lo, o[:nq], o[nq:]).astype(BF16)


def _na(qkv, bias, n_seq, t, row_off, prev):
    rows = t // GRID_W
    nb = rows // RQ_NA
    nq = RQ_NA * GRID_W
    blk0 = row_off // nq
    seq0 = row_off // t
    group_type = lambda g: jnp.where(g == 0, 0, jnp.where(g == nb - 1, 2, 1))
    return _call_into(
        functools.partial(_na_kernel, rows=rows), prev, 4,
        grid=(n_seq, nb),
        in_specs=[
            pl.BlockSpec((4, nq, LANES), lambda s, g: (BLK_NA_Q // 4, blk0 + s * nb + g, 0)),
            pl.BlockSpec((4, t, LANES), lambda s, g: (BLK_NA_K // 4, seq0 + s, 0)),
            pl.BlockSpec((4, t, LANES), lambda s, g: (BLK_NA_V // 4, seq0 + s, 0)),
            pl.BlockSpec((1, NA_HEADS // 2, 2 * nq, NA_UROWS * GRID_W), lambda s, g: (group_type(g), 0, 0, 0)),
        ],
        out_specs=pl.BlockSpec((nq, BRANCH_W), lambda s, g: (blk0 + s * nb + g, 0)),
        out_shape=jax.ShapeDtypeStruct((qkv.shape[1], BRANCH_W), BF16),
        compiler_params=_params(("parallel", "arbitrary")),
        name="mixer_a",
    )(qkv, qkv, qkv, bias)


def _na_bias_table(rpb):
    c = np.arange(GRID_W)[:, None]
    kc = np.arange(GRID_W)[None, :]
    cstart = np.clip(c - NA_COLS // 2, 0, GRID_W - NA_COLS)
    ok = (kc >= cstart) & (kc < cstart + NA_COLS)
    pad = GRID_W - NA_COLS
    padded = jnp.pad(rpb.astype(F32) * LOG2E, ((0, 0), (0, 0), (pad, pad)))
    e = jnp.stack([padded[..., GRID_W - 1 - q:2 * GRID_W - 1 - q] for q in range(GRID_W)], axis=2)
    e = jnp.where(ok[None, None], e, NEG_INF)
    neg = jnp.full((NA_HEADS, GRID_W, GRID_W), NEG_INF, F32)
    half = NA_ROWS // 2
    group_types = (
        [(-i, -i) for i in range(RQ_NA)],
        [(-half - i, -half) for i in range(RQ_NA)],
        [(-(NA_UROWS - RQ_NA) - i, -(NA_ROWS - RQ_NA) - i) for i in range(RQ_NA)],
    )
    tables = []
    for rel in group_types:
        q_rows = []
        for u0, w0 in rel:
            blocks = [e[:, u0 + j + NA_ROWS - 1] if w0 <= u0 + j < w0 + NA_ROWS else neg for j in range(NA_UROWS)]
            q_rows.append(jnp.concatenate(blocks, axis=2))
        tables.append(jnp.concatenate(q_rows, axis=1))
    return jnp.stack(tables).reshape(3, NA_HEADS // 2, 2 * RQ_NA * GRID_W, NA_UROWS * GRID_W)


def _diff_kernel(lam_ref, q_ref, k_ref, v_ref, tab_ref, g_ref, o_ref, vext_scr, *, t, tq, out_scale):
    qi = pl.program_id(2)

    @pl.when(qi == 0)
    def _():
        vext_scr[:, :LANES] = v_ref[0]
        vext_scr[:, LANES:] = jnp.ones((t, LANES), BF16)

    lam = lam_ref[0]
    lo = _half_masks(tq)
    k = k_ref[0]
    v_ext = vext_scr[...]

    def scores(b):
        off = pl.multiple_of((t // tq - 1 - (qi * BLOCKS_DIFF + b)) * tq, tq)
        bias = tab_ref[0, :, pl.ds(off, t)]
        q = q_ref[0, b * tq:(b + 1) * tq, :]
        return [_dot_nt(_keep_half(q, lo, half), k) - bias for half in range(2)]

    def probs(ss):
        return [jnp.exp2((s - jnp.max(s, axis=-1, keepdims=True)).astype(BF16)) for s in ss]

    def finish(b, ps):
        outs = [_dot(p, v_ext) for p in ps]
        o1, o2 = (ol[:, :LANES] / ol[:, LANES:] for ol in outs)
        o_ref[b * tq:(b + 1) * tq, :] = (_rms(o1 - lam * o2, g_ref[...], 1e-5) * out_scale).astype(BF16)

    ss = scores(0)
    for b in range(BLOCKS_DIFF):
        ss_next = scores(b + 1) if b + 1 < BLOCKS_DIFF else None
        finish(b, probs(ss))
        ss = ss_next


def _diff_bias_table(t, tq):
    r = lax.broadcasted_iota(jnp.int32, (tq, 2 * t - tq), 0)
    x = lax.broadcasted_iota(jnp.int32, (tq, 2 * t - tq), 1)
    dist = jnp.abs(r - x + (t - tq)).astype(F32)
    return jnp.asarray([s * LOG2E for s in SLOPES_DIFF], F32)[:, None, None] * dist[None]


def _diff(qkv, lam, subln_g, n_seq, t, row_off, lam_init, prev):
    tq = TQ_DIFF
    rows = BLOCKS_DIFF * tq
    nq = t // rows
    blk0 = row_off // rows
    seq0 = row_off // t
    return _call_into(
        functools.partial(_diff_kernel, t=t, tq=tq, out_scale=1.0 - lam_init), prev, 6,
        grid=(DIFF_HEADS, n_seq, nq),
        in_specs=[
            pl.BlockSpec(memory_space=pltpu.SMEM),
            pl.BlockSpec((1, rows, LANES), lambda h, s, i: (BLK_DF_Q + h, blk0 + s * nq + i, 0)),
            pl.BlockSpec((1, t, LANES), lambda h, s, i: (BLK_DF_K + h, seq0 + s, 0)),
            pl.BlockSpec((1, t, LANES), lambda h, s, i: (BLK_DF_V + h, seq0 + s, 0)),
            pl.BlockSpec((1, tq, 2 * t - tq), lambda h, s, i: (h, 0, 0)),
            pl.BlockSpec((1, DIFF_DV), lambda h, s, i: (0, 0)),
        ],
        out_specs=pl.BlockSpec((rows, LANES), lambda h, s, i: (blk0 + s * nq + i, h)),
        out_shape=jax.ShapeDtypeStruct((qkv.shape[1], BRANCH_W), BF16),
        scratch_shapes=[pltpu.VMEM((t, 2 * LANES), BF16)],
        compiler_params=_params(("parallel", "arbitrary", "arbitrary")),
        name="mixer_b",
    )(lam, qkv, qkv, qkv, _diff_bias_table(t, tq), subln_g)


def _band_window(qi, tq, kw, radius, length, segment=None):
    start = 0 if kw == length else pl.multiple_of(jnp.clip(qi * tq - radius, 0, length - kw), 64)
    col = lax.broadcasted_iota(jnp.int32, (tq, kw), 1)
    dist = jnp.abs(lax.broadcasted_iota(jnp.int32, (tq, kw), 0) + (qi * tq - start) - col)
    masked = jnp.where(dist <= radius, dist.astype(F32), MASKED_DIST)
    if segment is not None:
        masked = jnp.where((col + start) // segment == qi, masked, MASKED_DIST)
    return start, masked


def _with_ones(v):
    return jnp.concatenate([v, jnp.ones(v.shape, v.dtype)], axis=1)


def _softmax_pv(s, v_ext, floor=None):
    m = jnp.max(s, axis=-1, keepdims=True)
    if floor is not None:
        m = jnp.maximum(m, floor)
    ol = _dot(jnp.exp2((s - m).astype(BF16)), v_ext)
    return ol[:, :LANES], ol[:, LANES:], m


def _dil_kernel(q_ref, k_ref, v_ref, o_ref, lse_ref, *, n_sub, tq, kw, radius, slopes, scale, nqb, nrb, segment):
    for b in range(nqb):
        start, dist = _band_window(pl.program_id(2) * nqb + b, tq, kw, radius, n_sub, segment)
        rows = slice(b * tq, (b + 1) * tq)
        for rr in range(nrb):
            lanes = slice(rr * LANES, (rr + 1) * LANES)
            for h in range(DIL_HEADS):
                k = k_ref[h, pl.ds(start, kw), lanes]
                v_ext = _with_ones(v_ref[h, pl.ds(start, kw), lanes])
                s = _dot_nt(q_ref[h, rows, lanes], k) * scale - slopes[h] * dist
                o, l, m = _softmax_pv(s, v_ext)
                o_ref[h, rows, lanes] = (o / l).astype(BF16)
                lse_ref[h, rows, lanes] = (m + jnp.log2(l)) * (1.0 / LOG2E)


def _dil_group(view, gi, n_seq, t, row_off, prev):
    dil = DIL_DILATIONS[gi]
    radius = DIL_WINDOWS[gi] // (2 * dil)
    n_sub = t // dil
    segment = None
    if n_sub == TQ_DIL and n_seq % 2 == 0:
        segment, n_sub, n_seq, t = n_sub, 2 * n_sub, n_seq // 2, 2 * t
    tq = min(TQ_DIL, n_sub)
    kw = min(tq + 2 * radius, n_sub)
    nqb = min(n_sub // tq, UNITS_DIL)
    nrb = min(dil, UNITS_DIL // nqb)
    nq = n_sub // (tq * nqb)
    blk0 = row_off // dil // (tq * nqb)
    seq0 = row_off // t
    qb, kb, vb = (BLK_DL_Q // 4, BLK_DL_K // 4, BLK_DL_V // 4) if gi == 0 else (0, 1, 2)
    slopes = tuple(SLOPES_DIL[gi * DIL_HEADS + h] * dil * LOG2E for h in range(DIL_HEADS))
    rows_out = view.shape[1]
    q_idx = lambda s, r, i: (qb, blk0 + s * nq + i, r)
    o_idx = lambda s, r, i: (0, blk0 + s * nq + i, r)
    o, lse = _call_into(
        functools.partial(_dil_kernel, n_sub=n_sub, tq=tq, kw=kw, radius=radius, slopes=slopes,
                          scale=DIL_DH ** -0.5 * LOG2E, nqb=nqb, nrb=nrb, segment=segment), prev, 3,
        grid=(n_seq, dil // nrb, nq),
        in_specs=[
            pl.BlockSpec((4, nqb * tq, nrb * LANES), q_idx),
            pl.BlockSpec((4, n_sub, nrb * LANES), lambda s, r, i: (kb, seq0 + s, r)),
            pl.BlockSpec((4, n_sub, nrb * LANES), lambda s, r, i: (vb, seq0 + s, r)),
        ],
        out_specs=[pl.BlockSpec((4, nqb * tq, nrb * LANES), o_idx),
                   pl.BlockSpec((4, nqb * tq, nrb * LANES), o_idx)],
        out_shape=[jax.ShapeDtypeStruct((DIL_HEADS, rows_out, dil * LANES), BF16),
                   jax.ShapeDtypeStruct((DIL_HEADS, rows_out, dil * LANES), F32)],
        compiler_params=_params(("parallel", "arbitrary", "arbitrary")),
        name=f"mixer_c{gi}",
    )(view, view, view)
    return o, lse


def _dil_combine_kernel(o0, o1, o2, l0, l1, l2, out_ref, o_scr, l_scr, *, tm):
    for h in range(DIL_HEADS):
        for gi, (o_ref, l_ref) in ((1, (o1, l1)), (2, (o2, l2))):
            dil = DIL_DILATIONS[gi]
            for r in range(dil):
                rows = pl.ds(r, tm // dil, stride=dil)
                o_scr[gi - 1, rows, :] = o_ref[h, :, r * LANES:(r + 1) * LANES].astype(F32)
                l_scr[gi - 1, rows, :] = l_ref[h, :, r * LANES:(r + 1) * LANES]
        a0, a1, a2 = l0[h], l_scr[0], l_scr[1]
        mx = jnp.maximum(jnp.maximum(a0, a1), a2)
        e0, e1, e2 = jnp.exp(a0 - mx), jnp.exp(a1 - mx), jnp.exp(a2 - mx)
        num = e0 * o0[h].astype(F32) + e1 * o_scr[0] + e2 * o_scr[1]
        out_ref[:, h * LANES:(h + 1) * LANES] = (num / (e0 + e1 + e2)).astype(BF16)


def _dil_combine(outs, lses):
    m = outs[0].shape[1]
    tm = min(TM_COMBINE, m)
    specs = [pl.BlockSpec((DIL_HEADS, tm // dil, dil * LANES), lambda i: (0, i, 0)) for dil in DIL_DILATIONS]
    return pl.pallas_call(
        functools.partial(_dil_combine_kernel, tm=tm),
        grid=(m // tm,),
        in_specs=specs * 2,
        out_specs=pl.BlockSpec((tm, BRANCH_W), lambda i: (i, 0)),
        out_shape=jax.ShapeDtypeStruct((m, BRANCH_W), BF16),
        scratch_shapes=[pltpu.VMEM((2, tm, LANES), F32), pltpu.VMEM((2, tm, LANES), F32)],
        compiler_params=_params(("parallel",)),
        name="mixer_c_combine",
    )(*outs, *lses)


def _swa_kernel(sink_ref, q_ref, k_ref, v_ref, o_ref, *, t, tq, kw):
    qi = pl.program_id(1)
    start, dist = _band_window(qi, tq, kw, SWA_RADIUS, t)
    k = k_ref[0, pl.ds(start, kw), :]
    v_ext = _with_ones(v_ref[0, pl.ds(start, kw), :])
    lo = _half_masks(tq)
    for g in range(SWA_GROUP):
        q = q_ref[g]
        outs = []
        for hk in range(SWA_KVH):
            head = hk * SWA_GROUP + g
            sink = sink_ref[head] * LOG2E
            s = _dot_nt(_keep_half(q, lo, hk), k) - (SLOPES_SWA[head] * LOG2E) * dist
            o, l, m = _softmax_pv(s, v_ext, floor=sink)
            outs.append(o / (l + jnp.exp2(sink - m)))
        o_ref[:, g * LANES:(g + 1) * LANES] = jnp.where(lo, outs[0], outs[1]).astype(BF16)


def _swa(qkv, sink, n_seq, t, row_off, prev):
    tq = TQ_SWA
    kw = min(tq + 2 * SWA_RADIUS, t)
    nq = t // tq
    blk0 = row_off // tq
    seq0 = row_off // t
    return _call_into(
        functools.partial(_swa_kernel, t=t, tq=tq, kw=kw), prev, 4,
        grid=(n_seq, nq),
        in_specs=[
            pl.BlockSpec(memory_space=pltpu.SMEM),
            pl.BlockSpec((4, tq, LANES), lambda s, i: (BLK_SW_Q // 4, blk0 + s * nq + i, 0)),
            pl.BlockSpec((1, t, LANES), lambda s, i: (BLK_SW_K, seq0 + s, 0)),
            pl.BlockSpec((1, t, LANES), lambda s, i: (BLK_SW_V, seq0 + s, 0)),
        ],
        out_specs=pl.BlockSpec((tq, BRANCH_W), lambda s, i: (blk0 + s * nq + i, 0)),
        out_shape=jax.ShapeDtypeStruct((qkv.shape[1], BRANCH_W), BF16),
        compiler_params=_params(("parallel", "arbitrary")),
        name="mixer_d",
    )(sink, qkv, qkv, qkv)


def _merge_kernel(h_ref, oa_ref, ob_ref, oc_ref, od_ref, wg_ref, bg_ref, wb_ref, wo_ref, xres_ref, o_ref, *, nc, tc):
    j = pl.program_id(1)
    h = h_ref[...]
    acc = None
    for n, b_ref in enumerate((oa_ref, ob_ref, oc_ref, od_ref)):
        gate = jax.nn.sigmoid(_dot(h, wg_ref[n]) + bg_ref[n:n + 1, :])
        term = gate * _dot(b_ref[...], wb_ref[n])
        acc = term if acc is None else acc + term
    merged = acc.astype(BF16)

    def project(first):
        for n in range(D // FFN_TN):
            cols = slice(n * FFN_TN, (n + 1) * FFN_TN)
            part = _dot(merged, wo_ref[:, cols])
            o_ref[:, cols] = part if first else o_ref[:, cols] + part

    pl.when(j == 0)(functools.partial(project, True))
    pl.when(j > 0)(functools.partial(project, False))

    for c in range(nc):
        @pl.when(j == c)
        def _(c=c):
            o_ref[:, c * tc:(c + 1) * tc] += xres_ref[...]


def _merge(x, h, branches, w_gate, b_gate, w_branch, w_out):
    m = x.shape[0]
    tm, tc = TM_MERGE, TC_MERGE
    nc = D // tc
    row_tile = lambda i, j: (i, 0)
    return pl.pallas_call(
        functools.partial(_merge_kernel, nc=nc, tc=tc),
        grid=(m // tm, nc),
        in_specs=[pl.BlockSpec((tm, D), row_tile)]
        + [pl.BlockSpec((tm, BRANCH_W), row_tile)] * 4
        + [
            pl.BlockSpec((None, 4, D, tc), lambda i, j: (j, 0, 0, 0)),
            pl.BlockSpec((4, tc), lambda i, j: (0, j)),
            pl.BlockSpec((4, BRANCH_W, tc), lambda i, j: (0, 0, j)),
            pl.BlockSpec((tc, D), lambda i, j: (j, 0)),
            pl.BlockSpec((tm, tc), lambda i, j: (i, j)),
        ],
        out_specs=pl.BlockSpec((tm, D), row_tile),
        out_shape=jax.ShapeDtypeStruct((m, D), F32),
        compiler_params=_params(("parallel", "arbitrary")),
        name="merge",
    )(h, *branches, w_gate, b_gate, w_branch, w_out, x)


def _cast_ffn_in_kernel(w_ref, o_ref):
    for c in range(FFN_NC1):
        width = min(FFN_TF, D_FF - c * FFN_TF)
        o_ref[c, :, :width] = w_ref[:, c * FFN_TF:c * FFN_TF + width].astype(BF16)
        if width < FFN_TF:
            o_ref[c, :, width:] = jnp.zeros((o_ref.shape[1], FFN_TF - width), BF16)


def _cast_ffn_in(w_in, l):
    tr = 256
    return pl.pallas_call(
        _cast_ffn_in_kernel,
        grid=(2, D // tr),
        in_specs=[pl.BlockSpec((None, tr, D_FF), lambda part, i: (l, i, part))],
        out_specs=pl.BlockSpec((FFN_NC1, None, tr, FFN_TF), lambda part, i: (0, part, i, 0)),
        out_shape=jax.ShapeDtypeStruct((FFN_NC1, 2, D, FFN_TF), BF16),
        compiler_params=_params(("parallel", "parallel")),
        name="cast_ffn_in",
    )(w_in)


def _cast_kernel(w_ref, o_ref):
    o_ref[...] = w_ref[...].astype(BF16)


def _cast_gate(w_gate, l):
    tc = TC_MERGE
    nc = D // tc
    return pl.pallas_call(
        _cast_kernel,
        grid=(nc, 4),
        in_specs=[pl.BlockSpec((None, D, tc), lambda c, n: (l, 0, n * nc + c))],
        out_specs=pl.BlockSpec((None, None, D, tc), lambda c, n: (c, n, 0, 0)),
        out_shape=jax.ShapeDtypeStruct((nc, 4, D, tc), BF16),
        compiler_params=_params(("parallel", "parallel")),
        name="cast_gate",
    )(w_gate)


def _prep_ffn(w_in, w_out, l):
    return _cast_ffn_in(w_in, l), w_out, l


def _swa_pair_heads(w, axis):
    shape = w.shape
    split = shape[:axis] + (SWA_KVH, SWA_GROUP, SWA_DH) + shape[axis + 1:]
    return jnp.swapaxes(w.reshape(split), axis, axis + 1).reshape(shape)


def _proj_columns(w_in):
    scale = np.ones((IN_WIDTH,), np.float32)
    scale[COL_NA:COL_NA + NA_HEADS * NA_DH] = NA_DH ** -0.5 * LOG2E
    scale[COL_DF:COL_DF + DIFF_HEADS * 2 * DIFF_DK] = DIFF_DK ** -0.5 * LOG2E
    scale[COL_SW:COL_SW + SWA_QH * SWA_DH] = SWA_DH ** -0.5 * LOG2E
    w = w_in * scale
    gw = DIL_HEADS * DIL_DH

    def dil_group(gi):
        return [w[:, COL_DL + part * DIL_W + gi * gw:COL_DL + part * DIL_W + (gi + 1) * gw] for part in range(3)]

    sw_q_end = COL_SW + SWA_QH * SWA_DH
    cols = dil_group(1) + dil_group(2) + [w[:, :COL_DL]] + dil_group(0) \
        + [_swa_pair_heads(w[:, COL_SW:sw_q_end], 1), w[:, sw_q_end:]]
    return jnp.concatenate(cols, axis=1)


def _prep_w_in(w_in):
    return _proj_columns(w_in).astype(BF16)


def _encoder_layer(x, l, seqs, p, final_gain=None):
    ffn1 = (p["norm_ffn1"][l][None], *_prep_ffn(p["w_ffn1_in"], p["w_ffn1_out"], l))
    if isinstance(x, tuple):
        m_total = sum(xb.shape[0] for xb in x)
        out, row_off = None, 0
        for xb in x:
            out = _ffn(xb, *ffn1, m_out=m_total, out_off=row_off, prev=None if out is None else (out,))
            row_off += xb.shape[0]
        x = out
    else:
        x = _ffn(x, *ffn1)
    h, qkv, qkv_d4, qkv_d16 = _proj(x, p["norm_mix"][l][None], _prep_w_in(p["w_in"][l]))
    dil_views = (qkv, qkv_d4, qkv_d16)

    lam_init = 0.8 - 0.6 * math.exp(-0.3 * l)
    lv = p["diff_lambda"][l].astype(F32)
    lam = jnp.exp(jnp.sum(lv[0] * lv[1])) - jnp.exp(jnp.sum(lv[2] * lv[3])) + lam_init
    diff_lam = lam.reshape(1)
    na_bias = _na_bias_table(p["na_rpb"][l])
    subln = p["diff_subln"][l].astype(F32)[None]
    sink = p["swa_sink"][l].astype(F32)

    o_a = o_b = o_d = None
    dil = [None] * DIL_GROUPS
    for n_seq, t, row_off in seqs:
        o_a = _na(qkv, na_bias, n_seq, t, row_off, None if o_a is None else (o_a,))
        o_b = _diff(qkv, diff_lam, subln, n_seq, t, row_off, lam_init, None if o_b is None else (o_b,))
        for gi in range(DIL_GROUPS):
            dil[gi] = _dil_group(dil_views[gi], gi, n_seq, t, row_off, dil[gi])
        o_d = _swa(qkv, sink, n_seq, t, row_off, None if o_d is None else (o_d,))
    o_c = _dil_combine([o for o, _ in dil], [lse for _, lse in dil])
    branches = [o_a, o_b, o_c, o_d]

    w_branch = p["w_branch"][l]
    w_branch = jnp.concatenate([w_branch[:3], _swa_pair_heads(w_branch[3:], 1)], axis=0).astype(BF16)
    x = _merge(x, h, branches, _cast_gate(p["w_gate"], l), p["b_gate"][l].astype(F32).reshape(4, D),
               w_branch, p["w_out"][l].astype(BF16))
    ffn2 = (p["norm_ffn2"][l][None], *_prep_ffn(p["w_ffn2_in"], p["w_ffn2_out"], l))
    if final_gain is None:
        return _ffn(x, *ffn2)
    return tuple(_ffn(x, *ffn2, rows=(row_off, n_seq * t), final_gain=final_gain) for n_seq, t, row_off in seqs)


def kernel(x_prompt, x_sample, norm_ffn1, w_ffn1_in, w_ffn1_out, norm_mix, w_in, na_rpb, diff_lambda, diff_subln,
           swa_sink, w_branch, w_gate, b_gate, w_out, norm_ffn2, w_ffn2_in, w_ffn2_out, norm_final):
    p = dict(norm_ffn1=norm_ffn1, w_ffn1_in=w_ffn1_in, w_ffn1_out=w_ffn1_out, norm_mix=norm_mix, w_in=w_in,
             na_rpb=na_rpb, diff_lambda=diff_lambda, diff_subln=diff_subln, swa_sink=swa_sink,
             w_branch=w_branch, w_gate=w_gate, b_gate=b_gate, w_out=w_out, norm_ffn2=norm_ffn2,
             w_ffn2_in=w_ffn2_in, w_ffn2_out=w_ffn2_out)
    bp, tp, _ = x_prompt.shape
    bs, ts, _ = x_sample.shape
    mp = bp * tp
    x = (x_prompt.reshape(mp, D).astype(F32), x_sample.reshape(bs * ts, D).astype(F32))
    seqs = ((bp, tp, 0), (bs, ts, mp))
    for l in range(DEPTH):
        x = _encoder_layer(x, l, seqs, p, final_gain=norm_final.astype(F32)[None] if l == DEPTH - 1 else None)
    y_prompt, y_sample = x
    return y_prompt.reshape(bp, tp, D), y_sample.reshape(bs, ts, D)
```

```python
import functools
import math

import jax
import jax.numpy as jnp
import numpy as np
from jax import lax
from jax.experimental import pallas as pl
from jax.experimental.pallas import tpu as pltpu

F32 = jnp.float32
BF16 = jnp.bfloat16

D = 2048
DEPTH = 2
GRID_W = 64
NA_HEADS, NA_DH, NA_ROWS, NA_COLS = 8, 64, 8, 16
DIFF_HEADS, DIFF_DK, DIFF_DV = 4, 64, 128
DIL_WINDOWS, DIL_DILATIONS = (128, 512, 2048), (1, 4, 16)
DIL_GROUPS, DIL_HEADS, DIL_DH = 3, 4, 128
SWA_QH, SWA_KVH, SWA_DH, SWA_RADIUS = 8, 2, 64, 128
SWA_GROUP = SWA_QH // SWA_KVH
BRANCH_W = 512
D_FF = ((8 * D // 3 + 127) // 128) * 128
N_ALIBI = SWA_QH + DIL_GROUPS * DIL_HEADS + DIFF_HEADS
RMS_EPS = 1e-6
NEG_INF = -1e30
LOG2E = math.log2(math.e)
MASKED_DIST = 1e32

LANES = 128
IN_WIDTH = 3 * NA_HEADS * NA_DH + 2 * DIFF_HEADS * 2 * DIFF_DK + DIFF_HEADS * DIFF_DV \
    + 3 * DIL_GROUPS * DIL_HEADS * DIL_DH + SWA_QH * SWA_DH + 2 * SWA_KVH * SWA_DH
BLK_NA_Q, BLK_NA_K, BLK_NA_V = 0, 4, 8
BLK_DF_Q, BLK_DF_K, BLK_DF_V = 12, 16, 20
BLK_DL_Q, BLK_DL_K, BLK_DL_V = 24, 28, 32
BLK_SW_Q, BLK_SW_K, BLK_SW_V = 36, 40, 41
N_BLK_MAIN = 42
N_BLK_DIL = 3 * DIL_HEADS
COL_NA, COL_DF, COL_DL, COL_SW = 0, 1536, 3072, 7680
DIL_W = DIL_GROUPS * DIL_HEADS * DIL_DH

FFN_TF = 512
FFN_FP = -(-D_FF // FFN_TF) * FFN_TF
FFN_NC1 = FFN_FP // FFN_TF
FFN_TN = 1024
TM_FFN = 1024
TM_PROJ = 1024
NB_PROJ = 6
TN_PROJ = NB_PROJ * LANES
NJ_MAIN = N_BLK_MAIN // NB_PROJ
NJ_DIL = N_BLK_DIL // NB_PROJ
TM_MERGE = 1024
TC_MERGE = 256
TM_COMBINE = 1024
RQ_NA = 4
NA_UROWS = 12
TQ_DIFF = 256
BLOCKS_DIFF = 4
TQ_SWA = 256
TQ_DIL = 128
UNITS_DIL = 8
V7X_VMEM_BYTES = 64 * 1024 * 1024
VMEM_LIMIT = V7X_VMEM_BYTES - 4 * 1024 * 1024

_ALIBI = [2.0 ** (-8.0 * (i + 1) / N_ALIBI) for i in range(N_ALIBI)]
SLOPES_SWA = _ALIBI[:SWA_QH]
SLOPES_DIL = _ALIBI[SWA_QH:SWA_QH + DIL_GROUPS * DIL_HEADS]
SLOPES_DIFF = _ALIBI[SWA_QH + DIL_GROUPS * DIL_HEADS:]


def _params(sem):
    return pltpu.CompilerParams(dimension_semantics=sem, vmem_limit_bytes=VMEM_LIMIT)


def _call_into(body, prev, n_in, **kw):
    if prev is None:
        return pl.pallas_call(body, **kw)
    prev = tuple(prev)

    def aliased(*refs):
        body(*refs[:n_in], *refs[n_in + len(prev):])

    kw["in_specs"] = list(kw["in_specs"]) + [pl.BlockSpec(memory_space=pl.ANY)] * len(prev)
    call = pl.pallas_call(aliased, input_output_aliases={n_in + i: i for i in range(len(prev))}, **kw)
    return lambda *args: call(*args, *prev)


def _rms(x, g, eps):
    ms = jnp.mean(x * x, axis=-1, keepdims=True)
    return x * lax.rsqrt(ms + eps) * g


def _dot(a, b):
    return jnp.dot(a, b, preferred_element_type=F32)


def _dot_nt(a, b):
    return lax.dot_general(a, b, (((1,), (1,)), ((), ())), preferred_element_type=F32)


def _half_masks(rows):
    lane = lax.broadcasted_iota(jnp.int32, (rows, LANES), 1)
    return lane < (LANES // 2)


def _keep_half(x, lo, half):
    keep = lo if half == 0 else jnp.logical_not(lo)
    return jnp.where(keep, x.astype(F32), 0.0).astype(BF16)


def _ffn_kernel(*refs, final_norm):
    x_ref, g_ref, w1_ref, w2_ref = refs[:4]
    o_ref, h_scr = refs[-2:]
    j = pl.program_id(1)

    @pl.when(j == 0)
    def _():
        x = x_ref[...]
        h_scr[...] = _rms(x, g_ref[...], RMS_EPS).astype(BF16)
        o_ref[...] = x

    h = h_scr[...]
    gate = _dot(h, w1_ref[0])
    up = _dot(h, w1_ref[1])
    act = (gate * jax.nn.sigmoid(gate) * (0.5 * up)).astype(BF16)
    in_range = j * FFN_TF + lax.broadcasted_iota(jnp.int32, (FFN_TF, FFN_TN), 0) < D_FF
    for n in range(D // FFN_TN):
        cols = slice(n * FFN_TN, (n + 1) * FFN_TN)
        o_ref[:, cols] += _dot(act, jnp.where(in_range, w2_ref[:, cols], 0.0).astype(BF16))

    if final_norm:
        @pl.when(j == FFN_NC1 - 1)
        def _():
            o_ref[...] = _rms(o_ref[...], refs[4][...], RMS_EPS)


def _ffn(x, g, w1, w2, layer, rows=None, m_out=None, out_off=0, prev=None, final_gain=None):
    start, m = (0, x.shape[0]) if rows is None else rows
    m_out = m if m_out is None else m_out
    tm = TM_FFN
    in0, out0 = start // tm, out_off // tm
    inputs = [x, g, w1, w2] + ([] if final_gain is None else [final_gain])
    in_specs = [
        pl.BlockSpec((tm, D), lambda i, j: (in0 + i, 0)),
        pl.BlockSpec((1, D), lambda i, j: (0, 0)),
        pl.BlockSpec((None, 2, D, FFN_TF), lambda i, j: (j, 0, 0, 0)),
        pl.BlockSpec((None, FFN_TF, D), lambda i, j: (layer, j, 0)),
    ] + ([] if final_gain is None else [pl.BlockSpec((1, D), lambda i, j: (0, 0))])
    return _call_into(
        functools.partial(_ffn_kernel, final_norm=final_gain is not None), prev, len(inputs),
        grid=(m // tm, FFN_NC1),
        in_specs=in_specs,
        out_specs=pl.BlockSpec((tm, D), lambda i, j: (out0 + i, 0)),
        out_shape=jax.ShapeDtypeStruct((m_out, D), F32),
        scratch_shapes=[pltpu.VMEM((tm, D), BF16)],
        compiler_params=_params(("parallel", "arbitrary")),
        name="ffn",
    )(*inputs)


def _proj_kernel(x_ref, g_ref, w_ref, h_ref, main_ref, d4_ref, d16_ref, res_scr, stage_scr, *, tm):
    j = pl.program_id(1)

    @pl.when(j == 0)
    def _():
        h_ref[...] = _rms(x_ref[...], g_ref[...], RMS_EPS).astype(BF16)

    res = _dot(h_ref[...], w_ref[...])

    for k in range(NB_PROJ):
        main_ref[k] = res[:, k * LANES:(k + 1) * LANES].astype(BF16)
        res_scr[k] = res[:, k * LANES:(k + 1) * LANES]

    @pl.when(j < NJ_DIL)
    def _():
        dil = DIL_DILATIONS[1]
        for k in range(NB_PROJ):
            for r in range(dil):
                rows = res_scr[k, pl.ds(r, tm // dil, stride=dil), :]
                d4_ref[k, :, r * LANES:(r + 1) * LANES] = rows.astype(BF16)

    @pl.when((j >= NJ_DIL) & (j < 2 * NJ_DIL))
    def _():
        quarter = tm // 4
        for k in range(NB_PROJ):
            for r4 in range(4):
                stage_scr[k, r4 * quarter:(r4 + 1) * quarter, :] = res_scr[k, pl.ds(r4, quarter, stride=4), :]
            for r4 in range(4):
                for s4 in range(4):
                    rows = stage_scr[k, pl.ds(r4 * quarter + s4, tm // 16, stride=4), :]
                    r = 4 * s4 + r4
                    d16_ref[k, :, r * LANES:(r + 1) * LANES] = rows.astype(BF16)


def _proj(x, g, w):
    m = x.shape[0]
    tm = TM_PROJ
    d4, d16 = DIL_DILATIONS[1], DIL_DILATIONS[2]
    return pl.pallas_call(
        functools.partial(_proj_kernel, tm=tm),
        grid=(m // tm, NJ_MAIN + 2 * NJ_DIL),
        in_specs=[
            pl.BlockSpec((tm, D), lambda i, j: (i, 0)),
            pl.BlockSpec((1, D), lambda i, j: (0, 0)),
            pl.BlockSpec((D, TN_PROJ), lambda i, j: (0, j)),
        ],
        out_specs=[
            pl.BlockSpec((tm, D), lambda i, j: (i, 0)),
            pl.BlockSpec((NB_PROJ, tm, LANES), lambda i, j: (jnp.maximum(j - 2 * NJ_DIL, 0), i, 0)),
            pl.BlockSpec((NB_PROJ, tm // d4, d4 * LANES), lambda i, j: (jnp.minimum(j, NJ_DIL - 1), i, 0)),
            pl.BlockSpec((NB_PROJ, tm // d16, d16 * LANES),
                         lambda i, j: (jnp.clip(j - NJ_DIL, 0, NJ_DIL - 1), i, 0)),
        ],
        out_shape=[jax.ShapeDtypeStruct((m, D), BF16),
                   jax.ShapeDtypeStruct((N_BLK_MAIN, m, LANES), BF16),
                   jax.ShapeDtypeStruct((N_BLK_DIL, m // d4, d4 * LANES), BF16),
                   jax.ShapeDtypeStruct((N_BLK_DIL, m // d16, d16 * LANES), BF16)],
        scratch_shapes=[pltpu.VMEM((NB_PROJ, tm, LANES), F32)] * 2,
        compiler_params=_params(("parallel", "arbitrary")),
        name="proj",
    )(x, g, w)


def _na_kernel(q_ref, k_ref, v_ref, b_ref, o_ref, *, rows):
    nq = RQ_NA * GRID_W
    nk = NA_UROWS * GRID_W
    row0 = jnp.clip(RQ_NA * pl.program_id(1) - NA_ROWS // 2, 0, rows - NA_UROWS)
    start = pl.multiple_of(row0 * GRID_W, GRID_W)
    lo = _half_masks(nq)
    for hp in range(NA_HEADS // 2):
        q = q_ref[hp]
        k = k_ref[hp, pl.ds(start, nk), :]
        v_ext = _with_ones(v_ref[hp, pl.ds(start, nk), :])
        lhs = jnp.concatenate([_keep_half(q, lo, 0), _keep_half(q, lo, 1)], axis=0)
        o, l, _ = _softmax_pv(_dot_nt(lhs, k) + b_ref[0, hp], v_ext)
        o = o / l
        o_ref[:, hp * LANES:(hp + 1) * LANES] = jnp.where(lo, o[:nq], o[nq:]).astype(BF16)


def _na(qkv, bias, layer, n_seq, t, row_off, prev):
    rows = t // GRID_W
    nb = rows // RQ_NA
    nq = RQ_NA * GRID_W
    blk0 = row_off // nq
    seq0 = row_off // t
    group_type = lambda g: 3 * layer + jnp.where(g == 0, 0, jnp.where(g == nb - 1, 2, 1))
    return _call_into(
        functools.partial(_na_kernel, rows=rows), prev, 4,
        grid=(n_seq, nb),
        in_specs=[
            pl.BlockSpec((4, nq, LANES), lambda s, g: (BLK_NA_Q // 4, blk0 + s * nb + g, 0)),
            pl.BlockSpec((4, t, LANES), lambda s, g: (BLK_NA_K // 4, seq0 + s, 0)),
            pl.BlockSpec((4, t, LANES), lambda s, g: (BLK_NA_V // 4, seq0 + s, 0)),
            pl.BlockSpec((1, NA_HEADS // 2, 2 * nq, NA_UROWS * GRID_W), lambda s, g: (group_type(g), 0, 0, 0)),
        ],
        out_specs=pl.BlockSpec((nq, BRANCH_W), lambda s, g: (blk0 + s * nb + g, 0)),
        out_shape=jax.ShapeDtypeStruct((qkv.shape[1], BRANCH_W), BF16),
        compiler_params=_params(("parallel", "arbitrary")),
        name="mixer_a",
    )(qkv, qkv, qkv, bias)


def _na_bias_table(rpb):
    c = np.arange(GRID_W)[:, None]
    kc = np.arange(GRID_W)[None, :]
    cstart = np.clip(c - NA_COLS // 2, 0, GRID_W - NA_COLS)
    ok = (kc >= cstart) & (kc < cstart + NA_COLS)
    pad = GRID_W - NA_COLS
    padded = jnp.pad(rpb.astype(F32) * LOG2E, ((0, 0), (0, 0), (0, 0), (pad, pad)))
    e = jnp.stack([padded[..., GRID_W - 1 - q:2 * GRID_W - 1 - q] for q in range(GRID_W)], axis=3)
    e = jnp.where(ok, e, NEG_INF)
    neg = jnp.full(e.shape[:2] + (GRID_W, GRID_W), NEG_INF, F32)
    half = NA_ROWS // 2
    group_types = (
        [(-i, -i) for i in range(RQ_NA)],
        [(-half - i, -half) for i in range(RQ_NA)],
        [(-(NA_UROWS - RQ_NA) - i, -(NA_ROWS - RQ_NA) - i) for i in range(RQ_NA)],
    )
    tables = []
    for rel in group_types:
        q_rows = []
        for u0, w0 in rel:
            blocks = [e[:, :, u0 + j + NA_ROWS - 1] if w0 <= u0 + j < w0 + NA_ROWS else neg
                      for j in range(NA_UROWS)]
            q_rows.append(jnp.concatenate(blocks, axis=3))
        tables.append(jnp.concatenate(q_rows, axis=2))
    return jnp.stack(tables, axis=1).reshape(-1, NA_HEADS // 2, 2 * RQ_NA * GRID_W, NA_UROWS * GRID_W)


def _diff_kernel(lam_ref, q_ref, k_ref, v_ref, tab_ref, g_ref, o_ref, vext_scr, *, t, tq, out_scale):
    qi = pl.program_id(2)

    @pl.when(qi == 0)
    def _():
        vext_scr[:, :LANES] = v_ref[0]
        vext_scr[:, LANES:] = jnp.ones((t, LANES), BF16)

    lam = lam_ref[0]
    lo = _half_masks(tq)
    k = k_ref[0]
    v_ext = vext_scr[...]

    def scores(b):
        off = pl.multiple_of((t // tq - 1 - (qi * BLOCKS_DIFF + b)) * tq, tq)
        bias = tab_ref[0, :, pl.ds(off, t)]
        q = q_ref[0, b * tq:(b + 1) * tq, :]
        return [_dot_nt(_keep_half(q, lo, half), k) - bias for half in range(2)]

    def probs(ss):
        return [jnp.exp2((s - jnp.max(s, axis=-1, keepdims=True)).astype(BF16)) for s in ss]

    def finish(b, ps):
        outs = [_dot(p, v_ext) for p in ps]
        o1, o2 = (ol[:, :LANES] / ol[:, LANES:] for ol in outs)
        o_ref[b * tq:(b + 1) * tq, :] = (_rms(o1 - lam * o2, g_ref[...], 1e-5) * out_scale).astype(BF16)

    ss = scores(0)
    for b in range(BLOCKS_DIFF):
        ss_next = scores(b + 1) if b + 1 < BLOCKS_DIFF else None
        finish(b, probs(ss))
        ss = ss_next


def _diff_bias_table(t, tq):
    r = lax.broadcasted_iota(jnp.int32, (tq, 2 * t - tq), 0)
    x = lax.broadcasted_iota(jnp.int32, (tq, 2 * t - tq), 1)
    dist = jnp.abs(r - x + (t - tq)).astype(F32)
    return jnp.asarray([s * LOG2E for s in SLOPES_DIFF], F32)[:, None, None] * dist[None]


def _diff(qkv, lam, subln_g, n_seq, t, row_off, lam_init, prev):
    tq = TQ_DIFF
    rows = BLOCKS_DIFF * tq
    nq = t // rows
    blk0 = row_off // rows
    seq0 = row_off // t
    return _call_into(
        functools.partial(_diff_kernel, t=t, tq=tq, out_scale=1.0 - lam_init), prev, 6,
        grid=(DIFF_HEADS, n_seq, nq),
        in_specs=[
            pl.BlockSpec(memory_space=pltpu.SMEM),
            pl.BlockSpec((1, rows, LANES), lambda h, s, i: (BLK_DF_Q + h, blk0 + s * nq + i, 0)),
            pl.BlockSpec((1, t, LANES), lambda h, s, i: (BLK_DF_K + h, seq0 + s, 0)),
            pl.BlockSpec((1, t, LANES), lambda h, s, i: (BLK_DF_V + h, seq0 + s, 0)),
            pl.BlockSpec((1, tq, 2 * t - tq), lambda h, s, i: (h, 0, 0)),
            pl.BlockSpec((1, DIFF_DV), lambda h, s, i: (0, 0)),
        ],
        out_specs=pl.BlockSpec((rows, LANES), lambda h, s, i: (blk0 + s * nq + i, h)),
        out_shape=jax.ShapeDtypeStruct((qkv.shape[1], BRANCH_W), BF16),
        scratch_shapes=[pltpu.VMEM((t, 2 * LANES), BF16)],
        compiler_params=_params(("parallel", "arbitrary", "arbitrary")),
        name="mixer_b",
    )(lam, qkv, qkv, qkv, _diff_bias_table(t, tq), subln_g)


def _band_window(qi, tq, kw, radius, length, segment=None):
    start = 0 if kw == length else pl.multiple_of(jnp.clip(qi * tq - radius, 0, length - kw), 64)
    col = lax.broadcasted_iota(jnp.int32, (tq, kw), 1)
    dist = jnp.abs(lax.broadcasted_iota(jnp.int32, (tq, kw), 0) + (qi * tq - start) - col)
    masked = jnp.where(dist <= radius, dist.astype(F32), MASKED_DIST)
    if segment is not None:
        masked = jnp.where((col + start) // segment == qi, masked, MASKED_DIST)
    return start, masked


def _with_ones(v):
    return jnp.concatenate([v, jnp.ones(v.shape, v.dtype)], axis=1)


def _softmax_pv(s, v_ext, floor=None):
    m = jnp.max(s, axis=-1, keepdims=True)
    if floor is not None:
        m = jnp.maximum(m, floor)
    ol = _dot(jnp.exp2((s - m).astype(BF16)), v_ext)
    return ol[:, :LANES], ol[:, LANES:], m


def _dil_kernel(q_ref, k_ref, v_ref, o_ref, lse_ref, *, n_sub, tq, kw, radius, slopes, scale, nqb, nrb, segment):
    for b in range(nqb):
        start, dist = _band_window(pl.program_id(2) * nqb + b, tq, kw, radius, n_sub, segment)
        rows = slice(b * tq, (b + 1) * tq)
        for rr in range(nrb):
            lanes = slice(rr * LANES, (rr + 1) * LANES)
            for h in range(DIL_HEADS):
                k = k_ref[h, pl.ds(start, kw), lanes]
                v_ext = _with_ones(v_ref[h, pl.ds(start, kw), lanes])
                s = _dot_nt(q_ref[h, rows, lanes], k) * scale - slopes[h] * dist
                o, l, m = _softmax_pv(s, v_ext)
                o_ref[h, rows, lanes] = (o / l).astype(BF16)
                lse_ref[h, rows, lanes] = (m + jnp.log2(l)) * (1.0 / LOG2E)


def _dil_group(view, gi, n_seq, t, row_off, prev):
    dil = DIL_DILATIONS[gi]
    radius = DIL_WINDOWS[gi] // (2 * dil)
    n_sub = t // dil
    segment = None
    if n_sub == TQ_DIL and n_seq % 2 == 0:
        segment, n_sub, n_seq, t = n_sub, 2 * n_sub, n_seq // 2, 2 * t
    tq = min(TQ_DIL, n_sub)
    kw = min(tq + 2 * radius, n_sub)
    nqb = min(n_sub // tq, UNITS_DIL)
    nrb = min(dil, UNITS_DIL // nqb)
    nq = n_sub // (tq * nqb)
    blk0 = row_off // dil // (tq * nqb)
    seq0 = row_off // t
    qb, kb, vb = (BLK_DL_Q // 4, BLK_DL_K // 4, BLK_DL_V // 4) if gi == 0 else (0, 1, 2)
    slopes = tuple(SLOPES_DIL[gi * DIL_HEADS + h] * dil * LOG2E for h in range(DIL_HEADS))
    rows_out = view.shape[1]
    q_idx = lambda s, r, i: (qb, blk0 + s * nq + i, r)
    o_idx = lambda s, r, i: (0, blk0 + s * nq + i, r)
    o, lse = _call_into(
        functools.partial(_dil_kernel, n_sub=n_sub, tq=tq, kw=kw, radius=radius, slopes=slopes,
                          scale=DIL_DH ** -0.5 * LOG2E, nqb=nqb, nrb=nrb, segment=segment), prev, 3,
        grid=(n_seq, dil // nrb, nq),
        in_specs=[
            pl.BlockSpec((4, nqb * tq, nrb * LANES), q_idx),
            pl.BlockSpec((4, n_sub, nrb * LANES), lambda s, r, i: (kb, seq0 + s, r)),
            pl.BlockSpec((4, n_sub, nrb * LANES), lambda s, r, i: (vb, seq0 + s, r)),
        ],
        out_specs=[pl.BlockSpec((4, nqb * tq, nrb * LANES), o_idx),
                   pl.BlockSpec((4, nqb * tq, nrb * LANES), o_idx)],
        out_shape=[jax.ShapeDtypeStruct((DIL_HEADS, rows_out, dil * LANES), BF16),
                   jax.ShapeDtypeStruct((DIL_HEADS, rows_out, dil * LANES), F32)],
        compiler_params=_params(("parallel", "arbitrary", "arbitrary")),
        name=f"mixer_c{gi}",
    )(view, view, view)
    return o, lse


def _dil_combine_kernel(o0, o1, o2, l0, l1, l2, out_ref, o_scr, l_scr, *, tm):
    for h in range(DIL_HEADS):
        for gi, (o_ref, l_ref) in ((1, (o1, l1)), (2, (o2, l2))):
            dil = DIL_DILATIONS[gi]
            for r in range(dil):
                rows = pl.ds(r, tm // dil, stride=dil)
                o_scr[gi - 1, rows, :] = o_ref[h, :, r * LANES:(r + 1) * LANES].astype(F32)
                l_scr[gi - 1, rows, :] = l_ref[h, :, r * LANES:(r + 1) * LANES]
        a0, a1, a2 = l0[h], l_scr[0], l_scr[1]
        mx = jnp.maximum(jnp.maximum(a0, a1), a2)
        e0, e1, e2 = jnp.exp(a0 - mx), jnp.exp(a1 - mx), jnp.exp(a2 - mx)
        num = e0 * o0[h].astype(F32) + e1 * o_scr[0] + e2 * o_scr[1]
        out_ref[:, h * LANES:(h + 1) * LANES] = (num / (e0 + e1 + e2)).astype(BF16)


def _dil_combine(outs, lses):
    m = outs[0].shape[1]
    tm = min(TM_COMBINE, m)
    specs = [pl.BlockSpec((DIL_HEADS, tm // dil, dil * LANES), lambda i: (0, i, 0)) for dil in DIL_DILATIONS]
    return pl.pallas_call(
        functools.partial(_dil_combine_kernel, tm=tm),
        grid=(m // tm,),
        in_specs=specs * 2,
        out_specs=pl.BlockSpec((tm, BRANCH_W), lambda i: (i, 0)),
        out_shape=jax.ShapeDtypeStruct((m, BRANCH_W), BF16),
        scratch_shapes=[pltpu.VMEM((2, tm, LANES), F32), pltpu.VMEM((2, tm, LANES), F32)],
        compiler_params=_params(("parallel",)),
        name="mixer_c_combine",
    )(*outs, *lses)


def _swa_kernel(sink_ref, q_ref, k_ref, v_ref, o_ref, *, t, tq, kw):
    qi = pl.program_id(1)
    start, dist = _band_window(qi, tq, kw, SWA_RADIUS, t)
    k = k_ref[0, pl.ds(start, kw), :]
    v_ext = _with_ones(v_ref[0, pl.ds(start, kw), :])
    lo = _half_masks(tq)
    for g in range(SWA_GROUP):
        q = q_ref[g]
        outs = []
        for hk in range(SWA_KVH):
            head = hk * SWA_GROUP + g
            sink = sink_ref[head] * LOG2E
            s = _dot_nt(_keep_half(q, lo, hk), k) - (SLOPES_SWA[head] * LOG2E) * dist
            o, l, m = _softmax_pv(s, v_ext, floor=sink)
            outs.append(o / (l + jnp.exp2(sink - m)))
        o_ref[:, g * LANES:(g + 1) * LANES] = jnp.where(lo, outs[0], outs[1]).astype(BF16)


def _swa(qkv, sink, n_seq, t, row_off, prev):
    tq = TQ_SWA
    kw = min(tq + 2 * SWA_RADIUS, t)
    nq = t // tq
    blk0 = row_off // tq
    seq0 = row_off // t
    return _call_into(
        functools.partial(_swa_kernel, t=t, tq=tq, kw=kw), prev, 4,
        grid=(n_seq, nq),
        in_specs=[
            pl.BlockSpec(memory_space=pltpu.SMEM),
            pl.BlockSpec((4, tq, LANES), lambda s, i: (BLK_SW_Q // 4, blk0 + s * nq + i, 0)),
            pl.BlockSpec((1, t, LANES), lambda s, i: (BLK_SW_K, seq0 + s, 0)),
            pl.BlockSpec((1, t, LANES), lambda s, i: (BLK_SW_V, seq0 + s, 0)),
        ],
        out_specs=pl.BlockSpec((tq, BRANCH_W), lambda s, i: (blk0 + s * nq + i, 0)),
        out_shape=jax.ShapeDtypeStruct((qkv.shape[1], BRANCH_W), BF16),
        compiler_params=_params(("parallel", "arbitrary")),
        name="mixer_d",
    )(sink, qkv, qkv, qkv)


def _merge_kernel(h_ref, oa_ref, ob_ref, oc_ref, od_ref, wg_ref, bg_ref, wb_ref, wo_ref, xres_ref, o_ref, *, nc, tc):
    j = pl.program_id(1)

    @pl.when(j == 0)
    def _():
        o_ref[...] = jnp.zeros(o_ref.shape, F32)

    h = h_ref[...]
    acc = None
    for n, b_ref in enumerate((oa_ref, ob_ref, oc_ref, od_ref)):
        gate = jax.nn.sigmoid(_dot(h, wg_ref[n]) + bg_ref[n:n + 1, :])
        term = gate * _dot(b_ref[...], wb_ref[n])
        acc = term if acc is None else acc + term
    merged = acc.astype(BF16)
    for n in range(D // FFN_TN):
        cols = slice(n * FFN_TN, (n + 1) * FFN_TN)
        o_ref[:, cols] += _dot(merged, wo_ref[:, cols])
    res_cols = pl.ds(pl.multiple_of(j * tc, tc), tc)
    o_ref[:, res_cols] += xres_ref[...]


def _merge(x, h, branches, w_gate, b_gate, w_branch, w_out):
    m = x.shape[0]
    tm, tc = TM_MERGE, TC_MERGE
    nc = D // tc
    row_tile = lambda i, j: (i, 0)
    return pl.pallas_call(
        functools.partial(_merge_kernel, nc=nc, tc=tc),
        grid=(m // tm, nc),
        in_specs=[pl.BlockSpec((tm, D), row_tile)]
        + [pl.BlockSpec((tm, BRANCH_W), row_tile)] * 4
        + [
            pl.BlockSpec((None, 4, D, tc), lambda i, j: (j, 0, 0, 0)),
            pl.BlockSpec((4, tc), lambda i, j: (0, j)),
            pl.BlockSpec((4, BRANCH_W, tc), lambda i, j: (0, 0, j)),
            pl.BlockSpec((tc, D), lambda i, j: (j, 0)),
            pl.BlockSpec((tm, tc), lambda i, j: (i, j)),
        ],
        out_specs=pl.BlockSpec((tm, D), row_tile),
        out_shape=jax.ShapeDtypeStruct((m, D), F32),
        compiler_params=_params(("parallel", "arbitrary")),
        name="merge",
    )(h, *branches, w_gate, b_gate, w_branch, w_out, x)


def _cast_ffn_in_kernel(w_ref, o_ref):
    for c in range(FFN_NC1):
        width = min(FFN_TF, D_FF - c * FFN_TF)
        o_ref[c, :, :width] = w_ref[:, c * FFN_TF:c * FFN_TF + width].astype(BF16)
        if width < FFN_TF:
            o_ref[c, :, width:] = jnp.zeros((o_ref.shape[1], FFN_TF - width), BF16)


def _cast_ffn_in(w_in, l):
    tr = 256
    return pl.pallas_call(
        _cast_ffn_in_kernel,
        grid=(2, D // tr),
        in_specs=[pl.BlockSpec((None, tr, D_FF), lambda part, i: (l, i, part))],
        out_specs=pl.BlockSpec((FFN_NC1, None, tr, FFN_TF), lambda part, i: (0, part, i, 0)),
        out_shape=jax.ShapeDtypeStruct((FFN_NC1, 2, D, FFN_TF), BF16),
        compiler_params=_params(("parallel", "parallel")),
        name="cast_ffn_in",
    )(w_in)


def _cast_kernel(w_ref, o_ref):
    o_ref[...] = w_ref[...].astype(BF16)


def _cast_gate(w_gate, l):
    tc = TC_MERGE
    nc = D // tc
    return pl.pallas_call(
        _cast_kernel,
        grid=(nc, 4),
        in_specs=[pl.BlockSpec((None, D, tc), lambda c, n: (l, 0, n * nc + c))],
        out_specs=pl.BlockSpec((None, None, D, tc), lambda c, n: (c, n, 0, 0)),
        out_shape=jax.ShapeDtypeStruct((nc, 4, D, tc), BF16),
        compiler_params=_params(("parallel", "parallel")),
        name="cast_gate",
    )(w_gate)


def _prep_ffn(w_in, w_out, l):
    return _cast_ffn_in(w_in, l), w_out, l


def _swa_pair_heads(w, axis):
    shape = w.shape
    split = shape[:axis] + (SWA_KVH, SWA_GROUP, SWA_DH) + shape[axis + 1:]
    return jnp.swapaxes(w.reshape(split), axis, axis + 1).reshape(shape)


def _proj_columns(w_in):
    scale = np.ones((IN_WIDTH,), np.float32)
    scale[COL_NA:COL_NA + NA_HEADS * NA_DH] = NA_DH ** -0.5 * LOG2E
    scale[COL_DF:COL_DF + DIFF_HEADS * 2 * DIFF_DK] = DIFF_DK ** -0.5 * LOG2E
    scale[COL_SW:COL_SW + SWA_QH * SWA_DH] = SWA_DH ** -0.5 * LOG2E
    w = w_in * scale
    gw = DIL_HEADS * DIL_DH

    def dil_group(gi):
        return [w[:, COL_DL + part * DIL_W + gi * gw:COL_DL + part * DIL_W + (gi + 1) * gw] for part in range(3)]

    sw_q_end = COL_SW + SWA_QH * SWA_DH
    cols = dil_group(1) + dil_group(2) + [w[:, :COL_DL]] + dil_group(0) \
        + [_swa_pair_heads(w[:, COL_SW:sw_q_end], 1), w[:, sw_q_end:]]
    return jnp.concatenate(cols, axis=1)


def _prep_w_in(w_in):
    return _proj_columns(w_in).astype(BF16)


def _encoder_layer(x, l, seqs, p, final_gain=None):
    ffn1 = (p["norm_ffn1"][l][None], *_prep_ffn(p["w_ffn1_in"], p["w_ffn1_out"], l))
    if isinstance(x, tuple):
        m_total = sum(xb.shape[0] for xb in x)
        out, row_off = None, 0
        for xb in x:
            out = _ffn(xb, *ffn1, m_out=m_total, out_off=row_off, prev=None if out is None else (out,))
            row_off += xb.shape[0]
        x = out
    else:
        x = _ffn(x, *ffn1)
    h, qkv, qkv_d4, qkv_d16 = _proj(x, p["norm_mix"][l][None], _prep_w_in(p["w_in"][l]))
    dil_views = (qkv, qkv_d4, qkv_d16)

    lam_init = 0.8 - 0.6 * math.exp(-0.3 * l)
    lv = p["diff_lambda"][l].astype(F32)
    lam = jnp.exp(jnp.sum(lv[0] * lv[1])) - jnp.exp(jnp.sum(lv[2] * lv[3])) + lam_init
    diff_lam = lam.reshape(1)
    subln = p["diff_subln"][l].astype(F32)[None]
    sink = p["swa_sink"][l].astype(F32)

    o_a = o_b = o_d = None
    dil = [None] * DIL_GROUPS
    for n_seq, t, row_off in seqs:
        o_a = _na(qkv, p["na_bias"], l, n_seq, t, row_off, None if o_a is None else (o_a,))
        o_b = _diff(qkv, diff_lam, subln, n_seq, t, row_off, lam_init, None if o_b is None else (o_b,))
        for gi in range(DIL_GROUPS):
            dil[gi] = _dil_group(dil_views[gi], gi, n_seq, t, row_off, dil[gi])
        o_d = _swa(qkv, sink, n_seq, t, row_off, None if o_d is None else (o_d,))
    o_c = _dil_combine([o for o, _ in dil], [lse for _, lse in dil])
    branches = [o_a, o_b, o_c, o_d]

    w_branch = p["w_branch"][l]
    w_branch = jnp.concatenate([w_branch[:3], _swa_pair_heads(w_branch[3:], 1)], axis=0).astype(BF16)
    x = _merge(x, h, branches, _cast_gate(p["w_gate"], l), p["b_gate"][l].astype(F32).reshape(4, D),
               w_branch, p["w_out"][l].astype(BF16))
    ffn2 = (p["norm_ffn2"][l][None], *_prep_ffn(p["w_ffn2_in"], p["w_ffn2_out"], l))
    if final_gain is None:
        return _ffn(x, *ffn2)
    return tuple(_ffn(x, *ffn2, rows=(row_off, n_seq * t), final_gain=final_gain) for n_seq, t, row_off in seqs)


def kernel(x_prompt, x_sample, norm_ffn1, w_ffn1_in, w_ffn1_out, norm_mix, w_in, na_rpb, diff_lambda, diff_subln,
           swa_sink, w_branch, w_gate, b_gate, w_out, norm_ffn2, w_ffn2_in, w_ffn2_out, norm_final):
    p = dict(norm_ffn1=norm_ffn1, w_ffn1_in=w_ffn1_in, w_ffn1_out=w_ffn1_out, norm_mix=norm_mix, w_in=w_in,
             na_bias=_na_bias_table(na_rpb), diff_lambda=diff_lambda, diff_subln=diff_subln, swa_sink=swa_sink,
             w_branch=w_branch, w_gate=w_gate, b_gate=b_gate, w_out=w_out, norm_ffn2=norm_ffn2,
             w_ffn2_in=w_ffn2_in, w_ffn2_out=w_ffn2_out)
    bp, tp, _ = x_prompt.shape
    bs, ts, _ = x_sample.shape
    mp = bp * tp
    x = (x_prompt.reshape(mp, D).astype(F32), x_sample.reshape(bs * ts, D).astype(F32))
    seqs = ((bp, tp, 0), (bs, ts, mp))
    for l in range(DEPTH):
        x = _encoder_layer(x, l, seqs, p, final_gain=norm_final.astype(F32)[None] if l == DEPTH - 1 else None)
    y_prompt, y_sample = x
    return y_prompt.reshape(bp, tp, D), y_sample.reshape(bs, ts, D)
```

```python
import functools
import math

import jax
import jax.numpy as jnp
import numpy as np
from jax import lax
from jax.experimental import pallas as pl
from jax.experimental.pallas import tpu as pltpu

F32 = jnp.float32
BF16 = jnp.bfloat16

D = 2048
DEPTH = 2
GRID_W = 64
NA_HEADS, NA_DH, NA_ROWS, NA_COLS = 8, 64, 8, 16
DIFF_HEADS, DIFF_DK, DIFF_DV = 4, 64, 128
DIL_WINDOWS, DIL_DILATIONS = (128, 512, 2048), (1, 4, 16)
DIL_GROUPS, DIL_HEADS, DIL_DH = 3, 4, 128
SWA_QH, SWA_KVH, SWA_DH, SWA_RADIUS = 8, 2, 64, 128
SWA_GROUP = SWA_QH // SWA_KVH
BRANCH_W = 512
D_FF = ((8 * D // 3 + 127) // 128) * 128
N_ALIBI = SWA_QH + DIL_GROUPS * DIL_HEADS + DIFF_HEADS
RMS_EPS = 1e-6
NEG_INF = -1e30
LOG2E = math.log2(math.e)
MASKED_DIST = 1e32

LANES = 128
IN_WIDTH = 3 * NA_HEADS * NA_DH + 2 * DIFF_HEADS * 2 * DIFF_DK + DIFF_HEADS * DIFF_DV \
    + 3 * DIL_GROUPS * DIL_HEADS * DIL_DH + SWA_QH * SWA_DH + 2 * SWA_KVH * SWA_DH
BLK_NA_Q, BLK_NA_K, BLK_NA_V = 0, 4, 8
BLK_DF_Q, BLK_DF_K, BLK_DF_V = 12, 16, 20
BLK_DL_Q, BLK_DL_K, BLK_DL_V = 24, 28, 32
BLK_SW_Q, BLK_SW_K, BLK_SW_V = 36, 40, 41
N_BLK_MAIN = 42
N_BLK_DIL = 3 * DIL_HEADS
COL_NA, COL_DF, COL_DL, COL_SW = 0, 1536, 3072, 7680
DIL_W = DIL_GROUPS * DIL_HEADS * DIL_DH

FFN_TF = 512
FFN_FP = -(-D_FF // FFN_TF) * FFN_TF
FFN_NC1 = FFN_FP // FFN_TF
FFN_TN = 1024
TM_FFN = 1024
TM_PROJ = 1024
NB_PROJ = 6
TN_PROJ = NB_PROJ * LANES
NJ_MAIN = N_BLK_MAIN // NB_PROJ
NJ_DIL = N_BLK_DIL // NB_PROJ
TM_MERGE = 1024
TC_MERGE = 256
TM_COMBINE = 1024
RQ_NA = 4
NA_UROWS = 12
TQ_DIFF = 256
BLOCKS_DIFF = 4
TQ_SWA = 256
TQ_DIL = 128
UNITS_DIL = 8
V7X_VMEM_BYTES = 64 * 1024 * 1024
VMEM_LIMIT = V7X_VMEM_BYTES - 4 * 1024 * 1024

_ALIBI = [2.0 ** (-8.0 * (i + 1) / N_ALIBI) for i in range(N_ALIBI)]
SLOPES_SWA = _ALIBI[:SWA_QH]
SLOPES_DIL = _ALIBI[SWA_QH:SWA_QH + DIL_GROUPS * DIL_HEADS]
SLOPES_DIFF = _ALIBI[SWA_QH + DIL_GROUPS * DIL_HEADS:]


def _params(sem):
    return pltpu.CompilerParams(dimension_semantics=sem, vmem_limit_bytes=VMEM_LIMIT)


def _call_into(body, prev, n_in, **kw):
    if prev is None:
        return pl.pallas_call(body, **kw)
    prev = tuple(prev)

    def aliased(*refs):
        body(*refs[:n_in], *refs[n_in + len(prev):])

    kw["in_specs"] = list(kw["in_specs"]) + [pl.BlockSpec(memory_space=pl.ANY)] * len(prev)
    call = pl.pallas_call(aliased, input_output_aliases={n_in + i: i for i in range(len(prev))}, **kw)
    return lambda *args: call(*args, *prev)


def _rms(x, g, eps):
    ms = jnp.mean(x * x, axis=-1, keepdims=True)
    return x * lax.rsqrt(ms + eps) * g


def _dot(a, b):
    return jnp.dot(a, b, preferred_element_type=F32)


def _dot_nt(a, b):
    return lax.dot_general(a, b, (((1,), (1,)), ((), ())), preferred_element_type=F32)


def _half_masks(rows):
    lane = lax.broadcasted_iota(jnp.int32, (rows, LANES), 1)
    return lane < (LANES // 2)


def _keep_half(x, lo, half):
    keep = lo if half == 0 else jnp.logical_not(lo)
    return jnp.where(keep, x.astype(F32), 0.0).astype(BF16)


def _ffn_kernel(*refs, final_norm):
    x_ref, g_ref, w1_ref, w2_ref = refs[:4]
    o_ref, h_scr = refs[-2:]
    j = pl.program_id(1)

    @pl.when(j == 0)
    def _():
        x = x_ref[...]
        h_scr[...] = _rms(x, g_ref[...], RMS_EPS).astype(BF16)
        o_ref[...] = x

    h = h_scr[...]
    gate = _dot(h, w1_ref[0])
    up = _dot(h, w1_ref[1])
    act = (gate * jax.nn.sigmoid(gate) * (0.5 * up)).astype(BF16)
    in_range = j * FFN_TF + lax.broadcasted_iota(jnp.int32, (FFN_TF, FFN_TN), 0) < D_FF
    for n in range(D // FFN_TN):
        cols = slice(n * FFN_TN, (n + 1) * FFN_TN)
        o_ref[:, cols] += _dot(act, jnp.where(in_range, w2_ref[:, cols], 0.0).astype(BF16))

    if final_norm:
        @pl.when(j == FFN_NC1 - 1)
        def _():
            o_ref[...] = _rms(o_ref[...], refs[4][...], RMS_EPS)


def _ffn(x, g, w1, w2, layer, rows=None, m_out=None, out_off=0, prev=None, final_gain=None):
    start, m = (0, x.shape[0]) if rows is None else rows
    m_out = m if m_out is None else m_out
    tm = TM_FFN
    in0, out0 = start // tm, out_off // tm
    inputs = [x, g, w1, w2] + ([] if final_gain is None else [final_gain])
    in_specs = [
        pl.BlockSpec((tm, D), lambda i, j: (in0 + i, 0)),
        pl.BlockSpec((1, D), lambda i, j: (0, 0)),
        pl.BlockSpec((None, 2, D, FFN_TF), lambda i, j: (j, 0, 0, 0)),
        pl.BlockSpec((None, FFN_TF, D), lambda i, j: (layer, j, 0)),
    ] + ([] if final_gain is None else [pl.BlockSpec((1, D), lambda i, j: (0, 0))])
    return _call_into(
        functools.partial(_ffn_kernel, final_norm=final_gain is not None), prev, len(inputs),
        grid=(m // tm, FFN_NC1),
        in_specs=in_specs,
        out_specs=pl.BlockSpec((tm, D), lambda i, j: (out0 + i, 0)),
        out_shape=jax.ShapeDtypeStruct((m_out, D), F32),
        scratch_shapes=[pltpu.VMEM((tm, D), BF16)],
        compiler_params=_params(("parallel", "arbitrary")),
        name="ffn",
    )(*inputs)


def _proj_kernel(x_ref, g_ref, w_ref, h_ref, main_ref, d4_ref, d16_ref, res_scr, stage_scr, *, tm):
    j = pl.program_id(1)

    @pl.when(j == 0)
    def _():
        h_ref[...] = _rms(x_ref[...], g_ref[...], RMS_EPS).astype(BF16)

    res = _dot(h_ref[...], w_ref[...])

    for k in range(NB_PROJ):
        main_ref[k] = res[:, k * LANES:(k + 1) * LANES].astype(BF16)
        res_scr[k] = res[:, k * LANES:(k + 1) * LANES]

    @pl.when(j < NJ_DIL)
    def _():
        dil = DIL_DILATIONS[1]
        for k in range(NB_PROJ):
            for r in range(dil):
                rows = res_scr[k, pl.ds(r, tm // dil, stride=dil), :]
                d4_ref[k, :, r * LANES:(r + 1) * LANES] = rows.astype(BF16)

    @pl.when((j >= NJ_DIL) & (j < 2 * NJ_DIL))
    def _():
        quarter = tm // 4
        for k in range(NB_PROJ):
            for r4 in range(4):
                stage_scr[k, r4 * quarter:(r4 + 1) * quarter, :] = res_scr[k, pl.ds(r4, quarter, stride=4), :]
            for r4 in range(4):
                for s4 in range(4):
                    rows = stage_scr[k, pl.ds(r4 * quarter + s4, tm // 16, stride=4), :]
                    r = 4 * s4 + r4
                    d16_ref[k, :, r * LANES:(r + 1) * LANES] = rows.astype(BF16)


def _proj(x, g, w):
    m = x.shape[0]
    tm = TM_PROJ
    d4, d16 = DIL_DILATIONS[1], DIL_DILATIONS[2]
    return pl.pallas_call(
        functools.partial(_proj_kernel, tm=tm),
        grid=(m // tm, NJ_MAIN + 2 * NJ_DIL),
        in_specs=[
            pl.BlockSpec((tm, D), lambda i, j: (i, 0)),
            pl.BlockSpec((1, D), lambda i, j: (0, 0)),
            pl.BlockSpec((D, TN_PROJ), lambda i, j: (0, j)),
        ],
        out_specs=[
            pl.BlockSpec((tm, D), lambda i, j: (i, 0)),
            pl.BlockSpec((NB_PROJ, tm, LANES), lambda i, j: (jnp.maximum(j - 2 * NJ_DIL, 0), i, 0)),
            pl.BlockSpec((NB_PROJ, tm // d4, d4 * LANES), lambda i, j: (jnp.minimum(j, NJ_DIL - 1), i, 0)),
            pl.BlockSpec((NB_PROJ, tm // d16, d16 * LANES),
                         lambda i, j: (jnp.clip(j - NJ_DIL, 0, NJ_DIL - 1), i, 0)),
        ],
        out_shape=[jax.ShapeDtypeStruct((m, D), BF16),
                   jax.ShapeDtypeStruct((N_BLK_MAIN, m, LANES), BF16),
                   jax.ShapeDtypeStruct((N_BLK_DIL, m // d4, d4 * LANES), BF16),
                   jax.ShapeDtypeStruct((N_BLK_DIL, m // d16, d16 * LANES), BF16)],
        scratch_shapes=[pltpu.VMEM((NB_PROJ, tm, LANES), F32)] * 2,
        compiler_params=_params(("parallel", "arbitrary")),
        name="proj",
    )(x, g, w)


def _na_kernel(q_ref, k_ref, v_ref, b_ref, o_ref, *, rows):
    nq = RQ_NA * GRID_W
    nk = NA_UROWS * GRID_W
    row0 = jnp.clip(RQ_NA * pl.program_id(1) - NA_ROWS // 2, 0, rows - NA_UROWS)
    start = pl.multiple_of(row0 * GRID_W, GRID_W)
    lo = _half_masks(nq)
    for hp in range(NA_HEADS // 2):
        q = q_ref[hp]
        k = k_ref[hp, pl.ds(start, nk), :]
        v_ext = _with_ones(v_ref[hp, pl.ds(start, nk), :])
        lhs = jnp.concatenate([_keep_half(q, lo, 0), _keep_half(q, lo, 1)], axis=0)
        o, l, _ = _softmax_pv(_dot_nt(lhs, k) + b_ref[0, hp], v_ext)
        o = o / l
        o_ref[:, hp * LANES:(hp + 1) * LANES] = jnp.where(lo, o[:nq], o[nq:]).astype(BF16)


def _na(qkv, bias, layer, n_seq, t, row_off, prev):
    rows = t // GRID_W
    nb = rows // RQ_NA
    nq = RQ_NA * GRID_W
    blk0 = row_off // nq
    seq0 = row_off // t
    group_type = lambda g: 3 * layer + jnp.where(g == 0, 0, jnp.where(g == nb - 1, 2, 1))
    return _call_into(
        functools.partial(_na_kernel, rows=rows), prev, 4,
        grid=(n_seq, nb),
        in_specs=[
            pl.BlockSpec((4, nq, LANES), lambda s, g: (BLK_NA_Q // 4, blk0 + s * nb + g, 0)),
            pl.BlockSpec((4, t, LANES), lambda s, g: (BLK_NA_K // 4, seq0 + s, 0)),
            pl.BlockSpec((4, t, LANES), lambda s, g: (BLK_NA_V // 4, seq0 + s, 0)),
            pl.BlockSpec((1, NA_HEADS // 2, 2 * nq, NA_UROWS * GRID_W), lambda s, g: (group_type(g), 0, 0, 0)),
        ],
        out_specs=pl.BlockSpec((nq, BRANCH_W), lambda s, g: (blk0 + s * nb + g, 0)),
        out_shape=jax.ShapeDtypeStruct((qkv.shape[1], BRANCH_W), BF16),
        compiler_params=_params(("parallel", "arbitrary")),
        name="mixer_a",
    )(qkv, qkv, qkv, bias)


def _na_bias_table(rpb):
    c = np.arange(GRID_W)[:, None]
    kc = np.arange(GRID_W)[None, :]
    cstart = np.clip(c - NA_COLS // 2, 0, GRID_W - NA_COLS)
    ok = (kc >= cstart) & (kc < cstart + NA_COLS)
    pad = GRID_W - NA_COLS
    padded = jnp.pad(rpb.astype(F32) * LOG2E, ((0, 0), (0, 0), (0, 0), (pad, pad)))
    e = jnp.stack([padded[..., GRID_W - 1 - q:2 * GRID_W - 1 - q] for q in range(GRID_W)], axis=3)
    e = jnp.where(ok, e, NEG_INF)
    neg = jnp.full(e.shape[:2] + (GRID_W, GRID_W), NEG_INF, F32)
    half = NA_ROWS // 2
    group_types = (
        [(-i, -i) for i in range(RQ_NA)],
        [(-half - i, -half) for i in range(RQ_NA)],
        [(-(NA_UROWS - RQ_NA) - i, -(NA_ROWS - RQ_NA) - i) for i in range(RQ_NA)],
    )
    tables = []
    for rel in group_types:
        q_rows = []
        for u0, w0 in rel:
            blocks = [e[:, :, u0 + j + NA_ROWS - 1] if w0 <= u0 + j < w0 + NA_ROWS else neg
                      for j in range(NA_UROWS)]
            q_rows.append(jnp.concatenate(blocks, axis=3))
        tables.append(jnp.concatenate(q_rows, axis=2))
    return jnp.stack(tables, axis=1).reshape(-1, NA_HEADS // 2, 2 * RQ_NA * GRID_W, NA_UROWS * GRID_W)


def _diff_kernel(lam_ref, q_ref, k_ref, v_ref, tab_ref, g_ref, o_ref, vext_scr, *, t, tq, out_scale):
    qi = pl.program_id(2)

    @pl.when(qi == 0)
    def _():
        vext_scr[:, :LANES] = v_ref[0]
        vext_scr[:, LANES:] = jnp.ones((t, LANES), BF16)

    lam = lam_ref[0]
    lo = _half_masks(tq)
    k = k_ref[0]
    v_ext = vext_scr[...]

    def scores(b):
        off = pl.multiple_of((t // tq - 1 - (qi * BLOCKS_DIFF + b)) * tq, tq)
        bias = tab_ref[0, :, pl.ds(off, t)]
        q = q_ref[0, b * tq:(b + 1) * tq, :]
        return [_dot_nt(_keep_half(q, lo, half), k) - bias for half in range(2)]

    def probs(ss):
        return [jnp.exp2((s - jnp.max(s, axis=-1, keepdims=True)).astype(BF16)) for s in ss]

    def finish(b, ps):
        outs = [_dot(p, v_ext) for p in ps]
        o1, o2 = (ol[:, :LANES] / ol[:, LANES:] for ol in outs)
        o_ref[b * tq:(b + 1) * tq, :] = (_rms(o1 - lam * o2, g_ref[...], 1e-5) * out_scale).astype(BF16)

    ss = scores(0)
    for b in range(BLOCKS_DIFF):
        ss_next = scores(b + 1) if b + 1 < BLOCKS_DIFF else None
        finish(b, probs(ss))
        ss = ss_next


def _diff_bias_table(t, tq):
    r = lax.broadcasted_iota(jnp.int32, (tq, 2 * t - tq), 0)
    x = lax.broadcasted_iota(jnp.int32, (tq, 2 * t - tq), 1)
    dist = jnp.abs(r - x + (t - tq)).astype(F32)
    return jnp.asarray([s * LOG2E for s in SLOPES_DIFF], F32)[:, None, None] * dist[None]


def _diff(qkv, lam, subln_g, n_seq, t, row_off, lam_init, prev):
    tq = TQ_DIFF
    rows = BLOCKS_DIFF * tq
    nq = t // rows
    blk0 = row_off // rows
    seq0 = row_off // t
    return _call_into(
        functools.partial(_diff_kernel, t=t, tq=tq, out_scale=1.0 - lam_init), prev, 6,
        grid=(DIFF_HEADS, n_seq, nq),
        in_specs=[
            pl.BlockSpec(memory_space=pltpu.SMEM),
            pl.BlockSpec((1, rows, LANES), lambda h, s, i: (BLK_DF_Q + h, blk0 + s * nq + i, 0)),
            pl.BlockSpec((1, t, LANES), lambda h, s, i: (BLK_DF_K + h, seq0 + s, 0)),
            pl.BlockSpec((1, t, LANES), lambda h, s, i: (BLK_DF_V + h, seq0 + s, 0)),
            pl.BlockSpec((1, tq, 2 * t - tq), lambda h, s, i: (h, 0, 0)),
            pl.BlockSpec((1, DIFF_DV), lambda h, s, i: (0, 0)),
        ],
        out_specs=pl.BlockSpec((rows, LANES), lambda h, s, i: (blk0 + s * nq + i, h)),
        out_shape=jax.ShapeDtypeStruct((qkv.shape[1], BRANCH_W), BF16),
        scratch_shapes=[pltpu.VMEM((t, 2 * LANES), BF16)],
        compiler_params=_params(("parallel", "arbitrary", "arbitrary")),
        name="mixer_b",
    )(lam, qkv, qkv, qkv, _diff_bias_table(t, tq), subln_g)


def _band_window(qi, tq, kw, radius, length, segment=None):
    start = 0 if kw == length else pl.multiple_of(jnp.clip(qi * tq - radius, 0, length - kw), 64)
    col = lax.broadcasted_iota(jnp.int32, (tq, kw), 1)
    dist = jnp.abs(lax.broadcasted_iota(jnp.int32, (tq, kw), 0) + (qi * tq - start) - col)
    masked = jnp.where(dist <= radius, dist.astype(F32), MASKED_DIST)
    if segment is not None:
        masked = jnp.where((col + start) // segment == qi, masked, MASKED_DIST)
    return start, masked


def _with_ones(v):
    return jnp.concatenate([v, jnp.ones(v.shape, v.dtype)], axis=1)


def _softmax_pv(s, v_ext, floor=None):
    m = jnp.max(s, axis=-1, keepdims=True)
    if floor is not None:
        m = jnp.maximum(m, floor)
    ol = _dot(jnp.exp2((s - m).astype(BF16)), v_ext)
    return ol[:, :LANES], ol[:, LANES:], m


def _dil_kernel(q_ref, k_ref, v_ref, o_ref, lse_ref, *, n_sub, tq, kw, radius, slopes, scale, nqb, nrb, segment):
    for b in range(nqb):
        start, dist = _band_window(pl.program_id(2) * nqb + b, tq, kw, radius, n_sub, segment)
        rows = slice(b * tq, (b + 1) * tq)
        for rr in range(nrb):
            lanes = slice(rr * LANES, (rr + 1) * LANES)
            for h in range(DIL_HEADS):
                k = k_ref[h, pl.ds(start, kw), lanes]
                v_ext = _with_ones(v_ref[h, pl.ds(start, kw), lanes])
                s = _dot_nt(q_ref[h, rows, lanes], k) * scale - slopes[h] * dist
                o, l, m = _softmax_pv(s, v_ext)
                o_ref[h, rows, lanes] = (o / l).astype(BF16)
                lse_ref[h, rows, lanes] = (m + jnp.log2(l)) * (1.0 / LOG2E)


def _dil_group(view, gi, n_seq, t, row_off, prev):
    dil = DIL_DILATIONS[gi]
    radius = DIL_WINDOWS[gi] // (2 * dil)
    n_sub = t // dil
    segment = None
    if n_sub == TQ_DIL and n_seq % 2 == 0:
        segment, n_sub, n_seq, t = n_sub, 2 * n_sub, n_seq // 2, 2 * t
    tq = min(TQ_DIL, n_sub)
    kw = min(tq + 2 * radius, n_sub)
    nqb = min(n_sub // tq, UNITS_DIL)
    nrb = min(dil, UNITS_DIL // nqb)
    nq = n_sub // (tq * nqb)
    blk0 = row_off // dil // (tq * nqb)
    seq0 = row_off // t
    qb, kb, vb = (BLK_DL_Q // 4, BLK_DL_K // 4, BLK_DL_V // 4) if gi == 0 else (0, 1, 2)
    slopes = tuple(SLOPES_DIL[gi * DIL_HEADS + h] * dil * LOG2E for h in range(DIL_HEADS))
    rows_out = view.shape[1]
    q_idx = lambda s, r, i: (qb, blk0 + s * nq + i, r)
    o_idx = lambda s, r, i: (0, blk0 + s * nq + i, r)
    o, lse = _call_into(
        functools.partial(_dil_kernel, n_sub=n_sub, tq=tq, kw=kw, radius=radius, slopes=slopes,
                          scale=DIL_DH ** -0.5 * LOG2E, nqb=nqb, nrb=nrb, segment=segment), prev, 3,
        grid=(n_seq, dil // nrb, nq),
        in_specs=[
            pl.BlockSpec((4, nqb * tq, nrb * LANES), q_idx),
            pl.BlockSpec((4, n_sub, nrb * LANES), lambda s, r, i: (kb, seq0 + s, r)),
            pl.BlockSpec((4, n_sub, nrb * LANES), lambda s, r, i: (vb, seq0 + s, r)),
        ],
        out_specs=[pl.BlockSpec((4, nqb * tq, nrb * LANES), o_idx),
                   pl.BlockSpec((4, nqb * tq, nrb * LANES), o_idx)],
        out_shape=[jax.ShapeDtypeStruct((DIL_HEADS, rows_out, dil * LANES), BF16),
                   jax.ShapeDtypeStruct((DIL_HEADS, rows_out, dil * LANES), F32)],
        compiler_params=_params(("parallel", "arbitrary", "arbitrary")),
        name=f"mixer_c{gi}",
    )(view, view, view)
    return o, lse


def _dil_combine_kernel(o0, o1, o2, l0, l1, l2, out_ref, o_scr, l_scr, stage_scr, *, tm):
    quarter = tm // 4
    for h in range(DIL_HEADS):
        for r in range(DIL_DILATIONS[1]):
            rows = pl.ds(r, quarter, stride=DIL_DILATIONS[1])
            o_scr[0, rows, :] = o1[h, :, r * LANES:(r + 1) * LANES].astype(F32)
            l_scr[0, rows, :] = l1[h, :, r * LANES:(r + 1) * LANES]
        for r4 in range(4):
            for s4 in range(4):
                r = 4 * s4 + r4
                rows = pl.ds(r4 * quarter + s4, tm // 16, stride=4)
                stage_scr[0, rows, :] = o2[h, :, r * LANES:(r + 1) * LANES].astype(F32)
                stage_scr[1, rows, :] = l2[h, :, r * LANES:(r + 1) * LANES]
        for r4 in range(4):
            rows = pl.ds(r4, quarter, stride=4)
            o_scr[1, rows, :] = stage_scr[0, r4 * quarter:(r4 + 1) * quarter, :]
            l_scr[1, rows, :] = stage_scr[1, r4 * quarter:(r4 + 1) * quarter, :]
        a0, a1, a2 = l0[h], l_scr[0], l_scr[1]
        mx = jnp.maximum(jnp.maximum(a0, a1), a2)
        e0, e1, e2 = jnp.exp(a0 - mx), jnp.exp(a1 - mx), jnp.exp(a2 - mx)
        num = e0 * o0[h].astype(F32) + e1 * o_scr[0] + e2 * o_scr[1]
        out_ref[:, h * LANES:(h + 1) * LANES] = (num / (e0 + e1 + e2)).astype(BF16)


def _dil_combine(outs, lses):
    m = outs[0].shape[1]
    tm = min(TM_COMBINE, m)
    specs = [pl.BlockSpec((DIL_HEADS, tm // dil, dil * LANES), lambda i: (0, i, 0)) for dil in DIL_DILATIONS]
    return pl.pallas_call(
        functools.partial(_dil_combine_kernel, tm=tm),
        grid=(m // tm,),
        in_specs=specs * 2,
        out_specs=pl.BlockSpec((tm, BRANCH_W), lambda i: (i, 0)),
        out_shape=jax.ShapeDtypeStruct((m, BRANCH_W), BF16),
        scratch_shapes=[pltpu.VMEM((2, tm, LANES), F32)] * 3,
        compiler_params=_params(("parallel",)),
        name="mixer_c_combine",
    )(*outs, *lses)


def _swa_kernel(sink_ref, q_ref, k_ref, v_ref, o_ref, *, t, tq, kw):
    qi = pl.program_id(1)
    start, dist = _band_window(qi, tq, kw, SWA_RADIUS, t)
    k = k_ref[0, pl.ds(start, kw), :]
    v_ext = _with_ones(v_ref[0, pl.ds(start, kw), :])
    lo = _half_masks(tq)
    for g in range(SWA_GROUP):
        q = q_ref[g]
        outs = []
        for hk in range(SWA_KVH):
            head = hk * SWA_GROUP + g
            sink = sink_ref[head] * LOG2E
            s = _dot_nt(_keep_half(q, lo, hk), k) - (SLOPES_SWA[head] * LOG2E) * dist
            o, l, m = _softmax_pv(s, v_ext, floor=sink)
            outs.append(o / (l + jnp.exp2(sink - m)))
        o_ref[:, g * LANES:(g + 1) * LANES] = jnp.where(lo, outs[0], outs[1]).astype(BF16)


def _swa(qkv, sink, n_seq, t, row_off, prev):
    tq = TQ_SWA
    kw = min(tq + 2 * SWA_RADIUS, t)
    nq = t // tq
    blk0 = row_off // tq
    seq0 = row_off // t
    return _call_into(
        functools.partial(_swa_kernel, t=t, tq=tq, kw=kw), prev, 4,
        grid=(n_seq, nq),
        in_specs=[
            pl.BlockSpec(memory_space=pltpu.SMEM),
            pl.BlockSpec((4, tq, LANES), lambda s, i: (BLK_SW_Q // 4, blk0 + s * nq + i, 0)),
            pl.BlockSpec((1, t, LANES), lambda s, i: (BLK_SW_K, seq0 + s, 0)),
            pl.BlockSpec((1, t, LANES), lambda s, i: (BLK_SW_V, seq0 + s, 0)),
        ],
        out_specs=pl.BlockSpec((tq, BRANCH_W), lambda s, i: (blk0 + s * nq + i, 0)),
        out_shape=jax.ShapeDtypeStruct((qkv.shape[1], BRANCH_W), BF16),
        compiler_params=_params(("parallel", "arbitrary")),
        name="mixer_d",
    )(sink, qkv, qkv, qkv)


def _merge_kernel(h_ref, oa_ref, ob_ref, oc_ref, od_ref, wg_ref, bg_ref, wb_ref, wo_ref, xres_ref, o_ref, *, nc, tc):
    j = pl.program_id(1)

    @pl.when(j == 0)
    def _():
        o_ref[...] = jnp.zeros(o_ref.shape, F32)

    h = h_ref[...]
    acc = None
    for n, b_ref in enumerate((oa_ref, ob_ref, oc_ref, od_ref)):
        gate = jax.nn.sigmoid(_dot(h, wg_ref[n]) + bg_ref[n:n + 1, :])
        term = gate * _dot(b_ref[...], wb_ref[n])
        acc = term if acc is None else acc + term
    merged = acc.astype(BF16)
    for n in range(D // FFN_TN):
        cols = slice(n * FFN_TN, (n + 1) * FFN_TN)
        o_ref[:, cols] += _dot(merged, wo_ref[:, cols])
    res_cols = pl.ds(pl.multiple_of(j * tc, tc), tc)
    o_ref[:, res_cols] += xres_ref[...]


def _merge(x, h, branches, w_gate, b_gate, w_branch, w_out):
    m = x.shape[0]
    tm, tc = TM_MERGE, TC_MERGE
    nc = D // tc
    row_tile = lambda i, j: (i, 0)
    return pl.pallas_call(
        functools.partial(_merge_kernel, nc=nc, tc=tc),
        grid=(m // tm, nc),
        in_specs=[pl.BlockSpec((tm, D), row_tile)]
        + [pl.BlockSpec((tm, BRANCH_W), row_tile)] * 4
        + [
            pl.BlockSpec((None, 4, D, tc), lambda i, j: (j, 0, 0, 0)),
            pl.BlockSpec((4, tc), lambda i, j: (0, j)),
            pl.BlockSpec((4, BRANCH_W, tc), lambda i, j: (0, 0, j)),
            pl.BlockSpec((tc, D), lambda i, j: (j, 0)),
            pl.BlockSpec((tm, tc), lambda i, j: (i, j)),
        ],
        out_specs=pl.BlockSpec((tm, D), row_tile),
        out_shape=jax.ShapeDtypeStruct((m, D), F32),
        compiler_params=_params(("parallel", "arbitrary")),
        name="merge",
    )(h, *branches, w_gate, b_gate, w_branch, w_out, x)


def _cast_ffn_in_kernel(w_ref, o_ref):
    for c in range(FFN_NC1):
        width = min(FFN_TF, D_FF - c * FFN_TF)
        o_ref[c, :, :width] = w_ref[:, c * FFN_TF:c * FFN_TF + width].astype(BF16)
        if width < FFN_TF:
            o_ref[c, :, width:] = jnp.zeros((o_ref.shape[1], FFN_TF - width), BF16)


def _cast_ffn_in(w_in, l):
    tr = 256
    return pl.pallas_call(
        _cast_ffn_in_kernel,
        grid=(2, D // tr),
        in_specs=[pl.BlockSpec((None, tr, D_FF), lambda part, i: (l, i, part))],
        out_specs=pl.BlockSpec((FFN_NC1, None, tr, FFN_TF), lambda part, i: (0, part, i, 0)),
        out_shape=jax.ShapeDtypeStruct((FFN_NC1, 2, D, FFN_TF), BF16),
        compiler_params=_params(("parallel", "parallel")),
        name="cast_ffn_in",
    )(w_in)


def _cast_kernel(w_ref, o_ref):
    o_ref[...] = w_ref[...].astype(BF16)


def _cast_gate(w_gate, l):
    tc = TC_MERGE
    nc = D // tc
    return pl.pallas_call(
        _cast_kernel,
        grid=(nc, 4),
        in_specs=[pl.BlockSpec((None, D, tc), lambda c, n: (l, 0, n * nc + c))],
        out_specs=pl.BlockSpec((None, None, D, tc), lambda c, n: (c, n, 0, 0)),
        out_shape=jax.ShapeDtypeStruct((nc, 4, D, tc), BF16),
        compiler_params=_params(("parallel", "parallel")),
        name="cast_gate",
    )(w_gate)


def _prep_ffn(w_in, w_out, l):
    return _cast_ffn_in(w_in, l), w_out, l


def _swa_pair_heads(w, axis):
    shape = w.shape
    split = shape[:axis] + (SWA_KVH, SWA_GROUP, SWA_DH) + shape[axis + 1:]
    return jnp.swapaxes(w.reshape(split), axis, axis + 1).reshape(shape)


def _proj_columns(w_in):
    scale = np.ones((IN_WIDTH,), np.float32)
    scale[COL_NA:COL_NA + NA_HEADS * NA_DH] = NA_DH ** -0.5 * LOG2E
    scale[COL_DF:COL_DF + DIFF_HEADS * 2 * DIFF_DK] = DIFF_DK ** -0.5 * LOG2E
    scale[COL_SW:COL_SW + SWA_QH * SWA_DH] = SWA_DH ** -0.5 * LOG2E
    w = w_in * scale
    gw = DIL_HEADS * DIL_DH

    def dil_group(gi):
        return [w[:, COL_DL + part * DIL_W + gi * gw:COL_DL + part * DIL_W + (gi + 1) * gw] for part in range(3)]

    sw_q_end = COL_SW + SWA_QH * SWA_DH
    cols = dil_group(1) + dil_group(2) + [w[:, :COL_DL]] + dil_group(0) \
        + [_swa_pair_heads(w[:, COL_SW:sw_q_end], 1), w[:, sw_q_end:]]
    return jnp.concatenate(cols, axis=1)


def _prep_w_in(w_in):
    return _proj_columns(w_in).astype(BF16)


def _encoder_layer(x, l, seqs, p, final_gain=None):
    ffn1 = (p["norm_ffn1"][l][None], *_prep_ffn(p["w_ffn1_in"], p["w_ffn1_out"], l))
    if isinstance(x, tuple):
        m_total = sum(xb.shape[0] for xb in x)
        out, row_off = None, 0
        for xb in x:
            out = _ffn(xb, *ffn1, m_out=m_total, out_off=row_off, prev=None if out is None else (out,))
            row_off += xb.shape[0]
        x = out
    else:
        x = _ffn(x, *ffn1)
    h, qkv, qkv_d4, qkv_d16 = _proj(x, p["norm_mix"][l][None], _prep_w_in(p["w_in"][l]))
    dil_views = (qkv, qkv_d4, qkv_d16)

    lam_init = 0.8 - 0.6 * math.exp(-0.3 * l)
    lv = p["diff_lambda"][l].astype(F32)
    lam = jnp.exp(jnp.sum(lv[0] * lv[1])) - jnp.exp(jnp.sum(lv[2] * lv[3])) + lam_init
    diff_lam = lam.reshape(1)
    subln = p["diff_subln"][l].astype(F32)[None]
    sink = p["swa_sink"][l].astype(F32)

    o_a = o_b = o_d = None
    dil = [None] * DIL_GROUPS
    for n_seq, t, row_off in seqs:
        o_a = _na(qkv, p["na_bias"], l, n_seq, t, row_off, None if o_a is None else (o_a,))
        o_b = _diff(qkv, diff_lam, subln, n_seq, t, row_off, lam_init, None if o_b is None else (o_b,))
        for gi in range(DIL_GROUPS):
            dil[gi] = _dil_group(dil_views[gi], gi, n_seq, t, row_off, dil[gi])
        o_d = _swa(qkv, sink, n_seq, t, row_off, None if o_d is None else (o_d,))
    o_c = _dil_combine([o for o, _ in dil], [lse for _, lse in dil])
    branches = [o_a, o_b, o_c, o_d]

    w_branch = p["w_branch"][l]
    w_branch = jnp.concatenate([w_branch[:3], _swa_pair_heads(w_branch[3:], 1)], axis=0).astype(BF16)
    x = _merge(x, h, branches, _cast_gate(p["w_gate"], l), p["b_gate"][l].astype(F32).reshape(4, D),
               w_branch, p["w_out"][l].astype(BF16))
    ffn2 = (p["norm_ffn2"][l][None], *_prep_ffn(p["w_ffn2_in"], p["w_ffn2_out"], l))
    if final_gain is None:
        return _ffn(x, *ffn2)
    return tuple(_ffn(x, *ffn2, rows=(row_off, n_seq * t), final_gain=final_gain) for n_seq, t, row_off in seqs)


def kernel(x_prompt, x_sample, norm_ffn1, w_ffn1_in, w_ffn1_out, norm_mix, w_in, na_rpb, diff_lambda, diff_subln,
           swa_sink, w_branch, w_gate, b_gate, w_out, norm_ffn2, w_ffn2_in, w_ffn2_out, norm_final):
    p = dict(norm_ffn1=norm_ffn1, w_ffn1_in=w_ffn1_in, w_ffn1_out=w_ffn1_out, norm_mix=norm_mix, w_in=w_in,
             na_bias=_na_bias_table(na_rpb), diff_lambda=diff_lambda, diff_subln=diff_subln, swa_sink=swa_sink,
             w_branch=w_branch, w_gate=w_gate, b_gate=b_gate, w_out=w_out, norm_ffn2=norm_ffn2,
             w_ffn2_in=w_ffn2_in, w_ffn2_out=w_ffn2_out)
    bp, tp, _ = x_prompt.shape
    bs, ts, _ = x_sample.shape
    mp = bp * tp
    x = (x_prompt.reshape(mp, D).astype(F32), x_sample.reshape(bs * ts, D).astype(F32))
    seqs = ((bp, tp, 0), (bs, ts, mp))
    for l in range(DEPTH):
        x = _encoder_layer(x, l, seqs, p, final_gain=norm_final.astype(F32)[None] if l == DEPTH - 1 else None)
    y_prompt, y_sample = x
    return y_prompt.reshape(bp, tp, D), y_sample.reshape(bs, ts, D)
```
